```python
import jax, jax.numpy as jnp
from jax import lax
import numpy as np

D_MODEL = 2048
BATCH = 8
SEQ = 4096
DEPTH = 4

MIX_W = D_MODEL
ATT_W = MIX_W // 2
HEAD_DIM = 128
N_ATT_HEADS = ATT_W // HEAD_DIM
CONV_W = MIX_W // 4
CONV_TAPS = 3
POOL_W = MIX_W - ATT_W - CONV_W
POOL_WINDOWS = (2, 4, 8, 16)
POOL_GROUP = POOL_W // len(POOL_WINDOWS)
Q_BLOCK = 128
LN_EPS = 1e-5
DEEPNORM_ALPHA = (2 * DEPTH) ** 0.25
DEEPNORM_BETA = (8 * DEPTH) ** -0.25

IN_WIDTHS = (ATT_W, ATT_W, ATT_W, ATT_W, N_ATT_HEADS,
             CONV_W, CONV_W, CONV_W, CONV_W, POOL_W, POOL_W)
IN_W = sum(IN_WIDTHS)
IN_SPLITS = tuple(sum(IN_WIDTHS[:i + 1]) for i in range(len(IN_WIDTHS) - 1))

kernel_name = "hybrid_fox_shortconv_pool_deepnorm"


def layer_norm(x, g, b):
    x32 = x.astype(jnp.float32)
    mu = jnp.mean(x32, axis=-1, keepdims=True)
    var = jnp.mean(jnp.square(x32 - mu), axis=-1, keepdims=True)
    return ((x32 - mu) * lax.rsqrt(var + LN_EPS) * g + b).astype(x.dtype)


def forgetting_attention(q, k, v, fg_logit, b_f):
    b, s, h, dh = q.shape
    nb = s // Q_BLOCK
    log_f = jax.nn.log_sigmoid(fg_logit.astype(jnp.float32) + b_f.astype(jnp.float32))
    cum = jnp.cumsum(log_f, axis=1)
    cum_k = jnp.transpose(cum, (0, 2, 1))
    q_blocks = q.reshape(b, nb, Q_BLOCK, h, dh).transpose(1, 0, 2, 3, 4)
    c_blocks = cum.reshape(b, nb, Q_BLOCK, h).transpose(1, 0, 3, 2)
    k_pos = jnp.arange(s)
    scale = HEAD_DIM ** -0.5

    def block(args):
        qb, cb, i = args
        logits = jnp.einsum('bqhd,bkhd->bhqk', qb, k).astype(jnp.float32) * scale
        logits = logits + cb[..., None] - cum_k[:, :, None, :]
        q_pos = i * Q_BLOCK + jnp.arange(Q_BLOCK)
        causal = k_pos[None, :] <= q_pos[:, None]
        logits = jnp.where(causal, logits, -jnp.inf)
        p = jax.nn.softmax(logits, axis=-1).astype(v.dtype)
        return jnp.einsum('bhqk,bkhd->bqhd', p, v)

    out = lax.map(block, (q_blocks, c_blocks, jnp.arange(nb)))
    return out.transpose(1, 0, 2, 3, 4).reshape(b, s, h * dh)


def short_conv_mixer(gate_b, gate_c, h, conv_w):
    s = h.shape[1]
    u = gate_c * h
    up = jnp.pad(u, ((0, 0), (CONV_TAPS - 1, 0), (0, 0)))
    y = conv_w[0] * up[:, 0:s]
    for j in range(1, CONV_TAPS):
        y = y + conv_w[j] * up[:, j:j + s]
    return gate_b * y


def multiscale_pool_mixer(u, pool_w, pool_scale):
    s = u.shape[1]
    u32 = u.astype(jnp.float32)
    cs = jnp.pad(jnp.cumsum(u32, axis=1), ((0, 0), (1, 0), (0, 0)))
    t1 = jnp.arange(1, s + 1, dtype=jnp.float32)
    outs = []
    for g, w in enumerate(POOL_WINDOWS):
        sl = slice(g * POOL_GROUP, (g + 1) * POOL_GROUP)
        csg = cs[:, :, sl]
        lagged = jnp.pad(csg[:, :s - w + 1], ((0, 0), (w - 1, 0), (0, 0)))
        count = jnp.minimum(t1, float(w))
        mean = (csg[:, 1:] - lagged) / count[None, :, None]
        z = (mean - u32[:, :, sl]).astype(u.dtype)
        outs.append(jnp.einsum('bsc,cd->bsd', z, pool_w[g]))
    return jnp.concatenate(outs, axis=-1) * pool_scale


def _fwd_setup_inputs(seed: int = 0) -> dict:
    key = jax.random.key(seed)
    ks = jax.random.split(key, 10)
    x = jax.random.normal(ks[0], (BATCH, SEQ, D_MODEL), jnp.float32)
    w_in = jax.random.normal(ks[1], (DEPTH, D_MODEL, IN_W), jnp.float32) * D_MODEL ** -0.5
    b_f = jax.random.uniform(ks[2], (DEPTH, N_ATT_HEADS), jnp.float32, 1.0, 4.0)
    conv_w = jax.random.normal(ks[3], (DEPTH, CONV_TAPS, CONV_W), jnp.float32) * CONV_TAPS ** -0.5
    pool_w = jax.random.normal(ks[4], (DEPTH, len(POOL_WINDOWS), POOL_GROUP, POOL_GROUP),
                               jnp.float32) * POOL_GROUP ** -0.5
    pool_scale = 1.0 + 0.1 * jax.random.normal(ks[5], (DEPTH, POOL_W), jnp.float32)
    w_out = jax.random.normal(ks[6], (DEPTH, MIX_W, D_MODEL), jnp.float32) * (
        MIX_W ** -0.5 * DEEPNORM_BETA)
    ln_g = 1.0 + 0.1 * jax.random.normal(ks[7], (DEPTH, D_MODEL), jnp.float32)
    ln_b = 0.02 * jax.random.normal(ks[8], (DEPTH, D_MODEL), jnp.float32)
    return {"x": x, "w_in": w_in, "b_f": b_f, "conv_w": conv_w, "pool_w": pool_w,
            "pool_scale": pool_scale, "w_out": w_out, "ln_g": ln_g, "ln_b": ln_b}


def _fwd_reference(x, w_in, b_f, conv_w, pool_w, pool_scale, w_out, ln_g, ln_b):
    b, s, _ = x.shape
    for l in range(DEPTH):
        proj = jnp.einsum('bsd,de->bse', x, w_in[l])
        (q, k, v, g_att, fg, c_b, c_c, c_h, g_conv, p_u, g_pool) = jnp.split(
            proj, IN_SPLITS, axis=-1)
        heads = (b, s, N_ATT_HEADS, HEAD_DIM)
        y_att = forgetting_attention(q.reshape(heads), k.reshape(heads), v.reshape(heads),
                                     fg, b_f[l]) * jax.nn.silu(g_att)
        y_conv = short_conv_mixer(c_b, c_c, c_h, conv_w[l]) * jax.nn.silu(g_conv)
        y_pool = multiscale_pool_mixer(p_u, pool_w[l], pool_scale[l]) * jax.nn.silu(g_pool)
        y = jnp.concatenate([y_att, y_conv, y_pool], axis=-1)
        y = jnp.einsum('bse,ed->bsd', y, w_out[l])
        x = layer_norm(DEEPNORM_ALPHA * x + y, ln_g[l], ln_b[l])
    return x


import jax as _jax
import jax.numpy as _jnp

TWIN_FORMAT = 'train_step'
FWD_PARAMS = ['x', 'w_in', 'b_f', 'conv_w', 'pool_w', 'pool_scale', 'w_out', 'ln_g', 'ln_b']
TWIN_WEIGHTS = ['w_in', 'b_f', 'conv_w', 'pool_w', 'pool_scale', 'w_out', 'ln_g', 'ln_b']
TWIN_DIFF_INPUT = 'x'
TWIN_INPUTS = ['x', 'w_in', 'b_f', 'conv_w', 'pool_w', 'pool_scale', 'w_out', 'ln_g', 'ln_b', 'loss_target', 'm_w_in', 'm_b_f', 'm_conv_w', 'm_pool_w', 'm_pool_scale', 'm_w_out', 'm_ln_g', 'm_ln_b', 'v_w_in', 'v_b_f', 'v_conv_w', 'v_pool_w', 'v_pool_scale', 'v_w_out', 'v_ln_g', 'v_ln_b']
TWIN_OUTPUTS = ['loss', 'grad_x', 'grad_w_in', 'grad_b_f', 'grad_conv_w', 'grad_pool_w', 'grad_pool_scale', 'grad_w_out', 'grad_ln_g', 'grad_ln_b', 'delta_w_in', 'delta_b_f', 'delta_conv_w', 'delta_pool_w', 'delta_pool_scale', 'delta_w_out', 'delta_ln_g', 'delta_ln_b', 'new_m_w_in', 'new_m_b_f', 'new_m_conv_w', 'new_m_pool_w', 'new_m_pool_scale', 'new_m_w_out', 'new_m_ln_g', 'new_m_ln_b', 'new_v_w_in', 'new_v_b_f', 'new_v_conv_w', 'new_v_pool_w', 'new_v_pool_scale', 'new_v_w_out', 'new_v_ln_g', 'new_v_ln_b']
TWIN_LEAF_KINDS = {'loss': 'loss', 'grad_x': 'grad_x', 'grad_w_in': 'grad_w', 'grad_b_f': 'grad_w', 'grad_conv_w': 'grad_w', 'grad_pool_w': 'grad_w', 'grad_pool_scale': 'grad_w', 'grad_w_out': 'grad_w', 'grad_ln_g': 'grad_w', 'grad_ln_b': 'grad_w', 'delta_w_in': 'delta_w', 'delta_b_f': 'delta_w', 'delta_conv_w': 'delta_w', 'delta_pool_w': 'delta_w', 'delta_pool_scale': 'delta_w', 'delta_w_out': 'delta_w', 'delta_ln_g': 'delta_w', 'delta_ln_b': 'delta_w', 'new_m_w_in': 'new_m', 'new_m_b_f': 'new_m', 'new_m_conv_w': 'new_m', 'new_m_pool_w': 'new_m', 'new_m_pool_scale': 'new_m', 'new_m_w_out': 'new_m', 'new_m_ln_g': 'new_m', 'new_m_ln_b': 'new_m', 'new_v_w_in': 'new_v', 'new_v_b_f': 'new_v', 'new_v_conv_w': 'new_v', 'new_v_pool_w': 'new_v', 'new_v_pool_scale': 'new_v', 'new_v_w_out': 'new_v', 'new_v_ln_g': 'new_v', 'new_v_ln_b': 'new_v'}


def _forward(args):
    return _fwd_reference(*[args[k] for k in FWD_PARAMS])


def _output_shape():
    out = _jax.eval_shape(lambda: _forward(_fwd_setup_inputs(0)))
    return out.shape, out.dtype

N_MICROBATCH = 1
ADAM_LR = 0.001
ADAM_B1 = 0.9
ADAM_B2 = 0.999
ADAM_EPS = 1e-08
ADAM_WD = 0.01
ADAM_STEP = 10
PER_EXAMPLE_BATCH_AXIS = {'x': 0, 'loss_target': 0}
SHARED_INPUTS = []
_WEIGHT_DTYPES = {'w_in': _jnp.float32, 'b_f': _jnp.float32, 'conv_w': _jnp.float32, 'pool_w': _jnp.float32, 'pool_scale': _jnp.float32, 'w_out': _jnp.float32, 'ln_g': _jnp.float32, 'ln_b': _jnp.float32}
MOMENT_SCALE = {'w_in': 9.524399e-03, 'b_f': 4.154667e-02, 'conv_w': 1.407248e-02, 'pool_w': 1.218713e-02, 'pool_scale': 1.218430e-02, 'w_out': 2.363351e-02, 'ln_g': 8.953442e+00, 'ln_b': 3.245290e-01}


def _to_microbatches(a, axis):
    t = _jnp.moveaxis(a, axis, 0)
    t = t.reshape((N_MICROBATCH, t.shape[0] // N_MICROBATCH) + t.shape[1:])
    return _jnp.moveaxis(t, 1, axis + 1)


def setup_inputs(seed: int = 0) -> dict:
    inp = _fwd_setup_inputs(seed)
    key = _jax.random.fold_in(_jax.random.key(seed), 7919)
    shape, _ = _output_shape()
    out = dict(inp)
    out["loss_target"] = _jax.random.normal(_jax.random.fold_in(key, 0), shape, _jnp.float32)
    for i, name in enumerate(TWIN_WEIGHTS):
        w = inp[name].astype(_jnp.float32)
        if MOMENT_SCALE is None:
            s = _jnp.sqrt(_jnp.mean(_jnp.square(w)) + 1e-30)
        else:
            s = MOMENT_SCALE[name]
        km, kv = _jax.random.split(_jax.random.fold_in(key, i + 1))
        out[name] = w
        out["m_" + name] = s * _jax.random.normal(km, w.shape, _jnp.float32)
        out["v_" + name] = (s * s) * _jax.random.uniform(kv, w.shape, _jnp.float32, 0.5, 1.5)
    if N_MICROBATCH > 1:
        for name, axis in PER_EXAMPLE_BATCH_AXIS.items():
            out[name] = _to_microbatches(out[name], axis)
    return {'x': out['x'], 'w_in': out['w_in'], 'b_f': out['b_f'], 'conv_w': out['conv_w'], 'pool_w': out['pool_w'], 'pool_scale': out['pool_scale'], 'w_out': out['w_out'], 'ln_g': out['ln_g'], 'ln_b': out['ln_b'], 'loss_target': out['loss_target'], 'm_w_in': out['m_w_in'], 'm_b_f': out['m_b_f'], 'm_conv_w': out['m_conv_w'], 'm_pool_w': out['m_pool_w'], 'm_pool_scale': out['m_pool_scale'], 'm_w_out': out['m_w_out'], 'm_ln_g': out['m_ln_g'], 'm_ln_b': out['m_ln_b'], 'v_w_in': out['v_w_in'], 'v_b_f': out['v_b_f'], 'v_conv_w': out['v_conv_w'], 'v_pool_w': out['v_pool_w'], 'v_pool_scale': out['v_pool_scale'], 'v_w_out': out['v_w_out'], 'v_ln_g': out['v_ln_g'], 'v_ln_b': out['v_ln_b']}


def _loss(weights, diff, rest, loss_target):
    with _jax.named_scope("forward"):
        args = {**rest, TWIN_DIFF_INPUT: diff, **{k: w.astype(_WEIGHT_DTYPES[k]) for k, w in weights.items()}}
        y = _forward(args)
    with _jax.named_scope("loss_head"):
        err = _jnp.square(y.astype(_jnp.float32) - loss_target)
        return 0.5 * _jnp.sum(_jnp.mean(err, axis=-1)) if err.ndim else 0.5 * err


def _adamw(w, g, m, v):
    m = ADAM_B1 * m + (1.0 - ADAM_B1) * g
    v = ADAM_B2 * v + (1.0 - ADAM_B2) * _jnp.square(g)
    m_hat = m / (1.0 - ADAM_B1 ** ADAM_STEP)
    v_hat = v / (1.0 - ADAM_B2 ** ADAM_STEP)
    delta = -ADAM_LR * (m_hat / (_jnp.sqrt(v_hat) + ADAM_EPS) + ADAM_WD * w)
    return delta, m, v


def reference(x, w_in, b_f, conv_w, pool_w, pool_scale, w_out, ln_g, ln_b, loss_target, m_w_in, m_b_f, m_conv_w, m_pool_w, m_pool_scale, m_w_out, m_ln_g, m_ln_b, v_w_in, v_b_f, v_conv_w, v_pool_w, v_pool_scale, v_w_out, v_ln_g, v_ln_b):
    given = dict(x=x, w_in=w_in, b_f=b_f, conv_w=conv_w, pool_w=pool_w, pool_scale=pool_scale, w_out=w_out, ln_g=ln_g, ln_b=ln_b, loss_target=loss_target, m_w_in=m_w_in, m_b_f=m_b_f, m_conv_w=m_conv_w, m_pool_w=m_pool_w, m_pool_scale=m_pool_scale, m_w_out=m_w_out, m_ln_g=m_ln_g, m_ln_b=m_ln_b, v_w_in=v_w_in, v_b_f=v_b_f, v_conv_w=v_conv_w, v_pool_w=v_pool_w, v_pool_scale=v_pool_scale, v_w_out=v_w_out, v_ln_g=v_ln_g, v_ln_b=v_ln_b)
    weights = {n: given[n] for n in TWIN_WEIGHTS}
    shared = {n: given[n] for n in SHARED_INPUTS}
    per_example = {n: given[n] for n in ['x']}
    grad_fn = _jax.value_and_grad(_loss, argnums=(0, 1))

    def one_microbatch(ex, loss_target):
        ex = dict(ex)
        diff = ex.pop(TWIN_DIFF_INPUT)
        return grad_fn(weights, diff, {**shared, **ex}, loss_target)

    if N_MICROBATCH == 1:
        loss, (grad_w, grad_x) = one_microbatch(per_example, given["loss_target"])
    else:
        def body(carry, xs):
            loss_sum, grad_sum = carry
            l_k, (gw_k, gx_k) = one_microbatch(xs[0], xs[1])
            with _jax.named_scope("update"):
                return (loss_sum + l_k, _jax.tree.map(_jnp.add, grad_sum, gw_k)), gx_k

        init = (_jnp.zeros((), _jnp.float32), _jax.tree.map(_jnp.zeros_like, weights))
        (loss, grad_w), grad_x = _jax.lax.scan(body, init, (per_example, given["loss_target"]))
    with _jax.named_scope("update"):
        delta_w, new_m, new_v = {}, {}, {}
        for n in TWIN_WEIGHTS:
            delta_w[n], new_m[n], new_v[n] = _adamw(weights[n], grad_w[n], given["m_" + n], given["v_" + n])
    return (loss, grad_x, *[grad_w[n] for n in TWIN_WEIGHTS], *[delta_w[n] for n in TWIN_WEIGHTS],
            *[new_m[n] for n in TWIN_WEIGHTS], *[new_v[n] for n in TWIN_WEIGHTS])
```

```python
import functools

import jax
import jax.numpy as jnp
from jax import lax
from jax.experimental import pallas as pl
from jax.experimental.pallas import tpu as pltpu

F32 = jnp.float32
BF16 = jnp.bfloat16
MESH = pl.DeviceIdType.MESH

HEAD_DIM = 128
POOL_WINDOWS = (2, 4, 8, 16)
HALO = 16
LN_EPS = 1e-5
ADAM_LR = 0.001
ADAM_B1 = 0.9
ADAM_B2 = 0.999
ADAM_EPS = 1e-08
ADAM_WD = 0.01
ADAM_STEP = 10
VMEM_LIMIT = 56 * 1024 * 1024


def _params(sem, vmem=VMEM_LIMIT):
    return pltpu.CompilerParams(dimension_semantics=sem, vmem_limit_bytes=vmem)


def _tile(n, prefs):
    for t in prefs:
        if n % t == 0:
            return t
    return n


def _sigmoid(x):
    return 1.0 / (1.0 + jnp.exp(-x))


_DIMS = {"nn": ((1,), (0,)), "nt": ((1,), (1,)), "tn": ((0,), (0,))}


def _matmul(a, b, form, out_dtype, name, add=None, add_scale=1.0, tm=None, tn=None, tk=None):
    if form == "nn":
        (m, k), (_, n) = a.shape, b.shape
    elif form == "nt":
        (m, k), (n, _) = a.shape, b.shape
    else:
        (k, m), (_, n) = a.shape, b.shape
    tm = tm or _tile(m, (1024, 512, 256, 128))
    tn = tn or _tile(n, (1024, 512, 256, 128))
    tk = tk or _tile(k, (512, 256, 128))
    nk = k // tk
    if form == "tn":
        a_spec = pl.BlockSpec((tk, tm), lambda i, j, kk: (kk, i))
    else:
        a_spec = pl.BlockSpec((tm, tk), lambda i, j, kk: (i, kk))
    if form == "nt":
        b_spec = pl.BlockSpec((tn, tk), lambda i, j, kk: (j, kk))
    else:
        b_spec = pl.BlockSpec((tk, tn), lambda i, j, kk: (kk, j))
    o_spec = pl.BlockSpec((tm, tn), lambda i, j, kk: (i, j))
    dims = (_DIMS[form], ((), ()))
    has_add = add is not None

    def body(a_ref, b_ref, *rest):
        if has_add:
            add_ref, o_ref, acc = rest
        else:
            o_ref, acc = rest
        kk = pl.program_id(2)

        @pl.when(kk == 0)
        def _():
            acc[...] = jnp.zeros_like(acc)

        acc[...] += lax.dot_general(a_ref[...], b_ref[...], dims, preferred_element_type=F32)

        @pl.when(kk == nk - 1)
        def _():
            r = acc[...]
            if has_add:
                r = r + add_scale * add_ref[...]
            o_ref[...] = r.astype(out_dtype)

    in_specs = [a_spec, b_spec] + ([o_spec] if has_add else [])
    args = (a, b) + ((add,) if has_add else ())
    return pl.pallas_call(
        body, name=name, grid=(m // tm, n // tn, nk),
        in_specs=in_specs, out_specs=o_spec,
        out_shape=jax.ShapeDtypeStruct((m, n), out_dtype),
        scratch_shapes=[pltpu.VMEM((tm, tn), F32)],
        compiler_params=_params(("parallel", "parallel", "arbitrary")),
    )(*args)


def _tri(n, upper):
    r = lax.broadcasted_iota(jnp.int32, (n, n), 0)
    c = lax.broadcasted_iota(jnp.int32, (n, n), 1)
    return jnp.where((r <= c) if upper else (r >= c), 1.0, 0.0).astype(F32)


def _gate_fwd(fg_t, b_col):
    h, s = fg_t.shape
    nb = s // 128

    def body(fg_ref, b_ref, cum_ref):
        u = _tri(128, True)
        carry = jnp.zeros((h, 128), F32)
        for j in range(nb):
            z = fg_ref[:, j * 128:(j + 1) * 128] + b_ref[...]
            logf = -(jnp.maximum(-z, 0.0) + jnp.log(1.0 + jnp.exp(-jnp.abs(z))))
            c = jnp.dot(logf, u, precision=lax.Precision.HIGHEST, preferred_element_type=F32) + carry
            cum_ref[:, j * 128:(j + 1) * 128] = c
            carry = jnp.broadcast_to(c[:, 127:128], (h, 128))

    return pl.pallas_call(
        body, name="gate_fwd", out_shape=jax.ShapeDtypeStruct((h, s), F32),
        in_specs=[pl.BlockSpec(memory_space=pltpu.VMEM)] * 2,
        out_specs=pl.BlockSpec(memory_space=pltpu.VMEM),
    )(fg_t, b_col)


def _gate_bwd(drow_t, dcol_t, fg_t, b_col):
    h, s = fg_t.shape
    nb = s // 128

    def body(dr_ref, dc_ref, fg_ref, b_ref, dfg_ref, db_ref):
        low = _tri(128, False)
        carry = jnp.zeros((h, 128), F32)
        db = jnp.zeros((h, 128), F32)
        for j in reversed(range(nb)):
            sl = slice(j * 128, (j + 1) * 128)
            r = jnp.dot(dr_ref[:, sl] - dc_ref[:, sl], low, precision=lax.Precision.HIGHEST,
                        preferred_element_type=F32) + carry
            carry = jnp.broadcast_to(r[:, 0:1], (h, 128))
            z = fg_ref[:, sl] + b_ref[...]
            dfg = r / (1.0 + jnp.exp(z))
            dfg_ref[:, sl] = dfg
            db = db + dfg
        db_ref[...] = jnp.broadcast_to(jnp.sum(db, axis=1, keepdims=True), (h, 128))

    return pl.pallas_call(
        body, name="gate_bwd",
        out_shape=(jax.ShapeDtypeStruct((h, s), F32), jax.ShapeDtypeStruct((h, 128), F32)),
        in_specs=[pl.BlockSpec(memory_space=pltpu.VMEM)] * 4,
        out_specs=(pl.BlockSpec(memory_space=pltpu.VMEM),) * 2,
    )(drow_t, dcol_t, fg_t, b_col)


def _attn_fwd(p_main, cum3, n_heads, t):
    s = p_main.shape[0]
    nq = s // t
    scale = HEAD_DIM ** -0.5
    rep = t // 128

    def body(q_ref, k_ref, v_ref, c_ref, o_ref, lse_ref, m_s, l_s, acc_s):
        qi = pl.program_id(1)
        ki = pl.program_id(2)

        @pl.when(ki == 0)
        def _():
            m_s[...] = jnp.full_like(m_s, -jnp.inf)
            l_s[...] = jnp.zeros_like(l_s)
            acc_s[...] = jnp.zeros_like(acc_s)

        def step(masked):
            sc = lax.dot_general(q_ref[...], k_ref[...], (_DIMS["nt"], ((), ())),
                                 preferred_element_type=F32) * scale - c_ref[...]
            if masked:
                row = lax.broadcasted_iota(jnp.int32, (t, t), 0)
                col = lax.broadcasted_iota(jnp.int32, (t, t), 1)
                sc = jnp.where(col <= row, sc, -jnp.inf)
            m_prev = m_s[...]
            m_new = jnp.maximum(m_prev, jnp.max(sc, axis=1, keepdims=True))
            alpha = jnp.exp(m_prev - m_new)
            p = jnp.exp(sc - jnp.tile(m_new, (1, rep)))
            l_s[...] = alpha * l_s[...] + jnp.sum(p, axis=1, keepdims=True)
            acc_s[...] = alpha * acc_s[...] + jnp.dot(p.astype(BF16), v_ref[...], preferred_element_type=F32)
            m_s[...] = m_new

        @pl.when(ki < qi)
        def _():
            step(False)

        @pl.when(ki == qi)
        def _():
            step(True)
            o_ref[...] = (acc_s[...] / l_s[...]).astype(BF16)
            lse_ref[...] = m_s[...] + jnp.log(l_s[...])

    blk = lambda off: pl.BlockSpec((t, 128), lambda h, qi, ki: (jnp.minimum(ki, qi), off + h))
    return pl.pallas_call(
        body, name="attn_fwd", grid=(n_heads, nq, nq),
        in_specs=[pl.BlockSpec((t, 128), lambda h, qi, ki: (qi, h)), blk(n_heads), blk(2 * n_heads),
                  pl.BlockSpec((None, 1, t), lambda h, qi, ki: (h, 0, jnp.minimum(ki, qi)))],
        out_specs=(pl.BlockSpec((t, 128), lambda h, qi, ki: (qi, h)),) * 2,
        out_shape=(jax.ShapeDtypeStruct((s, n_heads * 128), BF16), jax.ShapeDtypeStruct((s, n_heads * 128), F32)),
        scratch_shapes=[pltpu.VMEM((t, 128), F32)] * 3,
        compiler_params=_params(("parallel", "parallel", "arbitrary")),
    )(p_main, p_main, p_main, cum3)


def _attn_bwd(p_main, do, lse_rep, delta_rep, cum3, n_heads, t):
    s = p_main.shape[0]
    nq = s // t
    scale = HEAD_DIM ** -0.5
    rep = t // 128

    def body(q_ref, k_ref, v_ref, do_ref, lse_ref, dl_ref, c_ref,
             dq_ref, dk_ref, dv_ref, drow_ref, dcol_ref, dq_acc, dk_acc, dv_acc):
        kb = pl.program_id(1)
        qb = pl.program_id(2)
        rows = pl.ds(pl.multiple_of(qb * t, t), t)

        @pl.when(qb == kb)
        def _():
            dk_acc[...] = jnp.zeros_like(dk_acc)
            dv_acc[...] = jnp.zeros_like(dv_acc)

        def step(masked):
            q = q_ref[...]
            k = k_ref[...]
            dout = do_ref[...]
            ones = jnp.ones((t, 128), BF16)
            sc = lax.dot_general(q, k, (_DIMS["nt"], ((), ())), preferred_element_type=F32) * scale - c_ref[...]
            if masked:
                row = lax.broadcasted_iota(jnp.int32, (t, t), 0)
                col = lax.broadcasted_iota(jnp.int32, (t, t), 1)
                sc = jnp.where(col <= row, sc, -jnp.inf)
            p = jnp.exp(sc - jnp.tile(lse_ref[...], (1, rep)))
            dp = lax.dot_general(dout, v_ref[...], (_DIMS["nt"], ((), ())), preferred_element_type=F32)
            dsb = (p * (dp - jnp.tile(dl_ref[...], (1, rep)))).astype(BF16)
            dv_acc[...] += lax.dot_general(p.astype(BF16), dout, (_DIMS["tn"], ((), ())),
                                           preferred_element_type=F32)
            dk_acc[...] += lax.dot_general(dsb, jnp.concatenate([q, ones], axis=1), (_DIMS["tn"], ((), ())),
                                           preferred_element_type=F32)
            dq_c = jnp.dot(dsb, jnp.concatenate([k, ones], axis=1), preferred_element_type=F32)

            @pl.when(kb == 0)
            def _():
                dq_acc[rows, :] = dq_c

            @pl.when(kb > 0)
            def _():
                dq_acc[rows, :] += dq_c

        @pl.when(qb > kb)
        def _():
            step(False)

        @pl.when(qb == kb)
        def _():
            step(True)
            dq_ref[rows, :] = (dq_acc[rows, 0:128] * scale).astype(BF16)
            drow_ref[rows, :] = dq_acc[rows, 128:256]

        @pl.when(qb == nq - 1)
        def _():
            dk_ref[...] = (dk_acc[:, 0:128] * scale).astype(BF16)
            dcol_ref[...] = dk_acc[:, 128:256]
            dv_ref[...] = dv_acc[...].astype(BF16)

    qside = pl.BlockSpec((t, 128), lambda h, kb, qb: (jnp.maximum(qb, kb), h))
    kside = lambda off: pl.BlockSpec((t, 128), lambda h, kb, qb: (kb, off + h))
    whole = pl.BlockSpec((s, 128), lambda h, kb, qb: (0, h))
    hw = n_heads * 128
    return pl.pallas_call(
        body, name="attn_bwd", grid=(n_heads, nq, nq),
        in_specs=[qside, kside(n_heads), kside(2 * n_heads), qside, qside, qside,
                  pl.BlockSpec((None, 1, t), lambda h, kb, qb: (h, 0, kb))],
        out_specs=(whole, kside(0), kside(0), whole, kside(0)),
        out_shape=(jax.ShapeDtypeStruct((s, hw), BF16), jax.ShapeDtypeStruct((s, hw), BF16),
                   jax.ShapeDtypeStruct((s, hw), BF16), jax.ShapeDtypeStruct((s, hw), F32),
                   jax.ShapeDtypeStruct((s, hw), F32)),
        scratch_shapes=[pltpu.VMEM((s, 256), F32), pltpu.VMEM((t, 256), F32), pltpu.VMEM((t, 128), F32)],
        compiler_params=_params(("parallel", "arbitrary", "arbitrary")),
    )(p_main, p_main, p_main, do, lse_rep, delta_rep, cum3)


def _conv_fwd(u_ext, cw_ref):
    r1 = pltpu.roll(u_ext, 1, 0)
    r2 = pltpu.roll(u_ext, 2, 0)
    conv = cw_ref[2:3, :] * u_ext + cw_ref[1:2, :] * r1 + cw_ref[0:1, :] * r2
    return conv[HALO:], r1[HALO:], r2[HALO:]


def _pool_z(pu, pu_halo, row0, tm, pg):
    ext = jnp.concatenate([pu_halo, pu], axis=0)
    t1 = (row0 + lax.broadcasted_iota(jnp.int32, (tm, pg), 0) + 1).astype(F32)
    zs = []
    for g, w in enumerate(POOL_WINDOWS):
        sm = ext[:, g * pg:(g + 1) * pg]
        sh = 1
        while sh < w:
            sm = sm + pltpu.roll(sm, sh, 0)
            sh *= 2
        mean = sm[HALO:] / jnp.minimum(t1, float(w))
        zs.append(mean - pu[:, g * pg:(g + 1) * pg])
    return zs


def _mix_specs(tm, s):
    per = tm // HALO
    last = s // HALO - 1
    cur = lambda w, j: pl.BlockSpec((tm, w), lambda i: (i, j))
    prev = lambda j: pl.BlockSpec((HALO, 512), lambda i: (jnp.maximum(i * per - 1, 0), j))
    nxt = lambda j: pl.BlockSpec((HALO, 512), lambda i: (jnp.minimum((i + 1) * per, last), j))
    full = lambda shape: pl.BlockSpec(shape, lambda i: (0,) * len(shape))
    return cur, prev, nxt, full


def _mix_fwd(p_main, o, conv_w, pool_w, pool_scale, tm):
    s = p_main.shape[0]
    d = 2048
    pg = 128
    cur, prev, nxt, full = _mix_specs(tm, s)

    def body(ga_ref, cb_ref, cc_ref, ch_ref, gc_ref, pu_ref, gp_ref, cch_ref, chh_ref, puh_ref,
             o_ref, cw_ref, pw_ref, ps_ref, y_ref):
        i = pl.program_id(0)
        first = i == 0
        ga = ga_ref[...].astype(F32)
        y_ref[:, 0:1024] = (o_ref[...].astype(F32) * ga * _sigmoid(ga)).astype(BF16)

        u = cc_ref[...].astype(F32) * ch_ref[...].astype(F32)
        uh = jnp.where(first, 0.0, cch_ref[...].astype(F32) * chh_ref[...].astype(F32))
        conv, _, _ = _conv_fwd(jnp.concatenate([uh, u], axis=0), cw_ref)
        gc = gc_ref[...].astype(F32)
        y_ref[:, 1024:1536] = (cb_ref[...].astype(F32) * conv * gc * _sigmoid(gc)).astype(BF16)

        pu = pu_ref[...].astype(F32)
        puh = jnp.where(first, 0.0, puh_ref[...].astype(F32))
        zs = _pool_z(pu, puh, i * tm, tm, pg)
        gp = gp_ref[...].astype(F32)
        gate = gp * _sigmoid(gp) * ps_ref[...]
        for g in range(4):
            r = jnp.dot(zs[g].astype(BF16), pw_ref[g], preferred_element_type=F32)
            y_ref[:, 1536 + g * pg:1536 + (g + 1) * pg] = (r * gate[:, g * pg:(g + 1) * pg]).astype(BF16)

    return pl.pallas_call(
        body, name="mix_fwd", grid=(s // tm,),
        in_specs=[cur(1024, 3), cur(512, 8), cur(512, 9), cur(512, 10), cur(512, 11), cur(512, 12), cur(512, 13),
                  prev(9), prev(10), prev(12),
                  cur(1024, 0), full((3, 512)), full((4, pg, pg)), full((1, 512))],
        out_specs=pl.BlockSpec((tm, d), lambda i: (i, 0)),
        out_shape=jax.ShapeDtypeStruct((s, d), BF16),
        compiler_params=_params(("parallel",)),
    )(*([p_main] * 10), o, conv_w, pool_w, pool_scale)


def _mix_bwd(p_main, o, dy, conv_w, pool_w, pool_scale, tm):
    s = p_main.shape[0]
    pg = 128
    n_heads = 8
    cur, prev, nxt, full = _mix_specs(tm, s)
    nblk = s // tm
    n_ext = tm + HALO

    def silu_and_grad(x):
        sg = _sigmoid(x)
        return x * sg, sg * (1.0 + x * (1.0 - sg))

    def body(ga_ref, cb_ref, cc_ref, ch_ref, gc_ref, pu_ref, gp_ref, cch_ref, chh_ref, puh_ref,
             cbn_ref, gcn_ref, gpn_ref, o_ref, dy_ref, dycn_ref, dypn_ref, cw_ref, pw_ref, ps_ref,
             do_ref, dl_ref, dp_ref, dsm_ref, dpw_ref):
        i = pl.program_id(0)
        first = i == 0
        last = i == nblk - 1

        @pl.when(first)
        def _():
            dsm_ref[...] = jnp.zeros_like(dsm_ref)
            dpw_ref[...] = jnp.zeros_like(dpw_ref)

        ga = ga_ref[...].astype(F32)
        of = o_ref[...].astype(F32)
        dya = dy_ref[:, 0:1024]
        sa, dsa = silu_and_grad(ga)
        dout = dya * sa
        do_ref[...] = dout.astype(BF16)
        dp_ref[:, 0:1024] = (dya * of * dsa).astype(BF16)
        prod = dout * of
        for h in range(n_heads):
            dsum = jnp.sum(prod[:, h * 128:(h + 1) * 128], axis=1, keepdims=True)
            dl_ref[:, h * 128:(h + 1) * 128] = jnp.broadcast_to(dsum, (tm, 128))

        cb = cb_ref[...].astype(F32)
        cc = cc_ref[...].astype(F32)
        ch = ch_ref[...].astype(F32)
        gc = gc_ref[...].astype(F32)
        u = cc * ch
        uh = jnp.where(first, 0.0, cch_ref[...].astype(F32) * chh_ref[...].astype(F32))
        conv, u1, u2 = _conv_fwd(jnp.concatenate([uh, u], axis=0), cw_ref)
        sc_, dsc = silu_and_grad(gc)
        dyc = dy_ref[:, 1024:1536]
        dp_ref[:, 1024:1536] = (dyc * conv * sc_).astype(BF16)
        dp_ref[:, 2560:3072] = (dyc * cb * conv * dsc).astype(BF16)
        dconv = dyc * cb * sc_
        gcn = gcn_ref[...].astype(F32)
        dconv_n = jnp.where(last, 0.0, dycn_ref[...] * cbn_ref[...].astype(F32) * gcn * _sigmoid(gcn))
        dext = jnp.concatenate([dconv, dconv_n], axis=0)
        du = (cw_ref[2:3, :] * dext + cw_ref[1:2, :] * pltpu.roll(dext, n_ext - 1, 0)
              + cw_ref[0:1, :] * pltpu.roll(dext, n_ext - 2, 0))[:tm]
        dp_ref[:, 1536:2048] = (du * ch).astype(BF16)
        dp_ref[:, 2048:2560] = (du * cc).astype(BF16)
        dsm_ref[0:1, :] += jnp.sum(dconv * u2, axis=0, keepdims=True)
        dsm_ref[1:2, :] += jnp.sum(dconv * u1, axis=0, keepdims=True)
        dsm_ref[2:3, :] += jnp.sum(dconv * u, axis=0, keepdims=True)

        pu = pu_ref[...].astype(F32)
        puh = jnp.where(first, 0.0, puh_ref[...].astype(F32))
        zs = _pool_z(pu, puh, i * tm, tm, pg)
        gp = gp_ref[...].astype(F32)
        sp, dsp = silu_and_grad(gp)
        dyp = dy_ref[:, 1536:2048]
        gpn = gpn_ref[...].astype(F32)
        dr_n = jnp.where(last, 0.0, dypn_ref[...] * gpn * _sigmoid(gpn) * ps_ref[...])
        drs = dyp * sp
        dr = drs * ps_ref[...]
        t1 = (i * tm + lax.broadcasted_iota(jnp.int32, (tm, pg), 0) + 1).astype(F32)
        r_parts, dpu_parts = [], []
        for g, w in enumerate(POOL_WINDOWS):
            cols = slice(g * pg, (g + 1) * pg)
            zb = zs[g].astype(BF16)
            r_parts.append(jnp.dot(zb, pw_ref[g], preferred_element_type=F32))
            drb = dr[:, cols].astype(BF16)
            dpw_ref[g] += lax.dot_general(zb, drb, (_DIMS["tn"], ((), ())), preferred_element_type=F32)
            dz = lax.dot_general(drb, pw_ref[g], (_DIMS["nt"], ((), ())), preferred_element_type=F32)
            dz_n = lax.dot_general(dr_n[:, cols].astype(BF16), pw_ref[g], (_DIMS["nt"], ((), ())),
                                   preferred_element_type=F32)
            sm = jnp.concatenate([dz / jnp.minimum(t1, float(w)), dz_n / float(w)], axis=0)
            sh = 1
            while sh < w:
                sm = sm + pltpu.roll(sm, n_ext - sh, 0)
                sh *= 2
            dpu_parts.append(sm[:tm] - dz)
        r = jnp.concatenate(r_parts, axis=1)
        dp_ref[:, 3072:3584] = jnp.concatenate(dpu_parts, axis=1).astype(BF16)
        dp_ref[:, 3584:4096] = (dyp * r * ps_ref[...] * dsp).astype(BF16)
        dsm_ref[3:4, :] += jnp.sum(drs * r, axis=0, keepdims=True)

    dy_next = lambda j: pl.BlockSpec((HALO, 512), lambda i: (jnp.minimum((i + 1) * (tm // HALO), s // HALO - 1), j))
    return pl.pallas_call(
        body, name="mix_bwd", grid=(nblk,),
        in_specs=[cur(1024, 3), cur(512, 8), cur(512, 9), cur(512, 10), cur(512, 11), cur(512, 12), cur(512, 13),
                  prev(9), prev(10), prev(12), nxt(8), nxt(11), nxt(13),
                  cur(1024, 0), pl.BlockSpec((tm, 2048), lambda i: (i, 0)), dy_next(2), dy_next(3),
                  full((3, 512)), full((4, pg, pg)), full((1, 512))],
        out_specs=(pl.BlockSpec((tm, 1024), lambda i: (i, 0)), pl.BlockSpec((tm, 1024), lambda i: (i, 0)),
                   pl.BlockSpec((tm, 4096), lambda i: (i, 0)), full((8, 512)), full((4, pg, pg))),
        out_shape=(jax.ShapeDtypeStruct((s, 1024), BF16), jax.ShapeDtypeStruct((s, 1024), F32),
                   jax.ShapeDtypeStruct((s, 4096), BF16), jax.ShapeDtypeStruct((8, 512), F32),
                   jax.ShapeDtypeStruct((4, pg, pg), F32)),
        compiler_params=_params(("arbitrary",)),
    )(*([p_main] * 13), o, dy, dy, dy, conv_w, pool_w, pool_scale)


def _outproj_ln(y, w_out, x, ln_g, ln_b, alpha, tm):
    s, d = x.shape

    def body(y_ref, w_ref, x_ref, g_ref, b_ref, xn_ref, xb_ref, pre_ref):
        pre = alpha * x_ref[...] + jnp.dot(y_ref[...], w_ref[...], preferred_element_type=F32)
        mu = jnp.mean(pre, axis=1, keepdims=True)
        cen = pre - mu
        var = jnp.mean(cen * cen, axis=1, keepdims=True)
        xn = cen * lax.rsqrt(var + LN_EPS) * g_ref[...] + b_ref[...]
        pre_ref[...] = pre
        xn_ref[...] = xn
        xb_ref[...] = xn.astype(BF16)

    row = pl.BlockSpec((tm, d), lambda i: (i, 0))
    vec = pl.BlockSpec((1, d), lambda i: (0, 0))
    return pl.pallas_call(
        body, name="outproj_ln", grid=(s // tm,),
        in_specs=[row, pl.BlockSpec((d, d), lambda i: (0, 0)), row, vec, vec],
        out_specs=(row, row, row),
        out_shape=(jax.ShapeDtypeStruct((s, d), F32), jax.ShapeDtypeStruct((s, d), BF16),
                   jax.ShapeDtypeStruct((s, d), F32)),
        compiler_params=_params(("parallel",)),
    )(y, w_out, x, ln_g, ln_b)


def _ln_bwd(dxn, pre, ln_g, tm):
    s, d = pre.shape

    def body(dx_ref, pre_ref, g_ref, dpre_ref, dpb_ref, dgb_ref):
        @pl.when(pl.program_id(0) == 0)
        def _():
            dgb_ref[...] = jnp.zeros_like(dgb_ref)

        pre_ = pre_ref[...]
        dx = dx_ref[...]
        mu = jnp.mean(pre_, axis=1, keepdims=True)
        cen = pre_ - mu
        var = jnp.mean(cen * cen, axis=1, keepdims=True)
        rstd = lax.rsqrt(var + LN_EPS)
        xhat = cen * rstd
        dxh = dx * g_ref[...]
        dpre = rstd * (dxh - jnp.mean(dxh, axis=1, keepdims=True)
                       - xhat * jnp.mean(dxh * xhat, axis=1, keepdims=True))
        dpre_ref[...] = dpre
        dpb_ref[...] = dpre.astype(BF16)
        dgb_ref[0:1, :] += jnp.sum(dx * xhat, axis=0, keepdims=True)
        dgb_ref[1:2, :] += jnp.sum(dx, axis=0, keepdims=True)

    row = pl.BlockSpec((tm, d), lambda i: (i, 0))
    return pl.pallas_call(
        body, name="ln_bwd", grid=(s // tm,),
        in_specs=[row, row, pl.BlockSpec((1, d), lambda i: (0, 0))],
        out_specs=(row, row, pl.BlockSpec((8, d), lambda i: (0, 0))),
        out_shape=(jax.ShapeDtypeStruct((s, d), F32), jax.ShapeDtypeStruct((s, d), BF16),
                   jax.ShapeDtypeStruct((8, d), F32)),
        compiler_params=_params(("arbitrary",)),
    )(dxn, pre, ln_g)


def _loss_grad(y, target, tm):
    s, d = y.shape

    def body(y_ref, t_ref, dy_ref, loss_ref):
        @pl.when(pl.program_id(0) == 0)
        def _():
            loss_ref[...] = jnp.zeros_like(loss_ref)

        diff = y_ref[...] - t_ref[...]
        dy_ref[...] = diff / d
        loss_ref[...] += 0.5 * jnp.sum(jnp.sum(diff * diff, axis=1, keepdims=True) / d)

    row = pl.BlockSpec((tm, d), lambda i: (i, 0))
    return pl.pallas_call(
        body, name="loss_grad", grid=(s // tm,),
        in_specs=[row, row],
        out_specs=(row, pl.BlockSpec((8, 128), lambda i: (0, 0))),
        out_shape=(jax.ShapeDtypeStruct((s, d), F32), jax.ShapeDtypeStruct((8, 128), F32)),
        compiler_params=_params(("arbitrary",)),
    )(y, target)


def _adamw(w, g, m, v, name):
    a, r, c = w.shape
    tr = _tile(r, (256, 128, 64, 32, 16, 8))
    c1 = 1.0 - ADAM_B1 ** ADAM_STEP
    c2 = 1.0 - ADAM_B2 ** ADAM_STEP

    def body(w_ref, g_ref, m_ref, v_ref, d_ref, mo_ref, vo_ref):
        g_ = g_ref[...]
        m_new = ADAM_B1 * m_ref[...] + (1.0 - ADAM_B1) * g_
        v_new = ADAM_B2 * v_ref[...] + (1.0 - ADAM_B2) * (g_ * g_)
        d_ref[...] = -ADAM_LR * ((m_new / c1) / (jnp.sqrt(v_new / c2) + ADAM_EPS) + ADAM_WD * w_ref[...])
        mo_ref[...] = m_new
        vo_ref[...] = v_new

    spec = pl.BlockSpec((1, tr, c), lambda i, j: (i, j, 0))
    shp = jax.ShapeDtypeStruct(w.shape, F32)
    return pl.pallas_call(
        body, name=name, grid=(a, r // tr), in_specs=[spec] * 4, out_specs=(spec,) * 3, out_shape=(shp,) * 3,
        compiler_params=_params(("parallel", "parallel")),
    )(w, g, m, v)


def _place():
    x, y, c = lax.axis_index("x"), lax.axis_index("y"), lax.axis_index("c")
    return x, y, c, [(1 - x, y), (x, 1 - y), (1 - x, 1 - y)]


def _hbm_spec():
    return pl.BlockSpec(memory_space=pltpu.HBM)


def _gather_chip_shards(shard, name):
    _, nl, r, cdim = shard.shape
    half = nl // 2

    def body(s_ref, out_ref, send_sems, recv_sems, local_sem):
        x, y, c, chips = _place()
        sib = (x, y, 1 - c)

        def slot(px, py, pc):
            return out_ref.at[pl.ds(2 * px + py, 1), pl.ds(pc * half, half)]

        def copy(k, dst, to, src=None):
            return pltpu.make_async_remote_copy(
                src_ref=dst if src is None else src, dst_ref=dst, send_sem=send_sems.at[k],
                recv_sem=recv_sems.at[k], device_id=to, device_id_type=MESH)

        mine = pltpu.make_async_copy(s_ref, out_ref.at[pl.ds(2 * x + y, 1)], local_sem)
        mine.start()
        first = [copy(k, slot(x, y, c), (*chip, c), src=s_ref.at[:, pl.ds(c * half, half)])
                 for k, chip in enumerate(chips)]
        for cp in first:
            cp.start()
        passed = [copy(3 + k, slot(*chip, c), sib) for k, chip in enumerate(chips)]
        for k, chip in enumerate(chips):
            copy(k, slot(*chip, c), (x, y, c)).wait_recv()
            passed[k].start()
        for k, chip in enumerate(chips):
            copy(3 + k, slot(*chip, 1 - c), (x, y, c)).wait_recv()
        for cp in first + passed:
            cp.wait_send()
        mine.wait()

    return pl.pallas_call(
        body, name=name, out_shape=jax.ShapeDtypeStruct((4, nl, r, cdim), shard.dtype),
        in_specs=[_hbm_spec()], out_specs=_hbm_spec(),
        scratch_shapes=[pltpu.SemaphoreType.DMA((6,)), pltpu.SemaphoreType.DMA((6,)), pltpu.SemaphoreType.DMA],
    )(shard)


def _swap_other_half(g, name):
    nchip, nl, r, cdim = g.shape
    half = nl // 2

    def body(g_ref, out_ref, send_sem, recv_sem):
        x, y, c, _ = _place()
        cp = pltpu.make_async_remote_copy(
            src_ref=g_ref.at[:, pl.ds((1 - c) * half, half)], dst_ref=out_ref, send_sem=send_sem,
            recv_sem=recv_sem, device_id=(x, y, 1 - c), device_id_type=MESH)
        cp.start()
        cp.wait()

    return pl.pallas_call(
        body, name=name, out_shape=jax.ShapeDtypeStruct((nchip, half, r, cdim), g.dtype),
        in_specs=[_hbm_spec()], out_specs=_hbm_spec(),
        scratch_shapes=[pltpu.SemaphoreType.DMA, pltpu.SemaphoreType.DMA],
    )(g)


def _exchange_chip_partials(hsum, name):
    nchip, half, r, cdim = hsum.shape

    def body(h_ref, out_ref, send_sems, recv_sems):
        x, y, c, chips = _place()
        cps = [pltpu.make_async_remote_copy(
            src_ref=h_ref.at[pl.ds(2 * px + py, 1)], dst_ref=out_ref.at[pl.ds(k, 1)], send_sem=send_sems.at[k],
            recv_sem=recv_sems.at[k], device_id=(px, py, c), device_id_type=MESH)
            for k, (px, py) in enumerate(chips)]
        for cp in cps:
            cp.start()
        for cp in cps:
            cp.wait()

    return pl.pallas_call(
        body, name=name, out_shape=jax.ShapeDtypeStruct((3, half, r, cdim), hsum.dtype),
        in_specs=[_hbm_spec()], out_specs=_hbm_spec(),
        scratch_shapes=[pltpu.SemaphoreType.DMA((3,)), pltpu.SemaphoreType.DMA((3,))],
    )(hsum)


def _share_final_half(f, name):
    half, r, cdim = f.shape

    def body(f_ref, out_ref, send_sem, recv_sem, local_sem):
        x, y, c, _ = _place()
        dst = out_ref.at[pl.ds(c * half, half)]
        mine = pltpu.make_async_copy(f_ref, dst, local_sem)
        mine.start()
        cp = pltpu.make_async_remote_copy(
            src_ref=f_ref, dst_ref=dst, send_sem=send_sem, recv_sem=recv_sem,
            device_id=(x, y, 1 - c), device_id_type=MESH)
        cp.start()
        cp.wait_send()
        pltpu.make_async_remote_copy(
            src_ref=f_ref, dst_ref=out_ref.at[pl.ds((1 - c) * half, half)], send_sem=send_sem,
            recv_sem=recv_sem, device_id=(x, y, 1 - c), device_id_type=MESH).wait_recv()
        mine.wait()

    return pl.pallas_call(
        body, name=name, out_shape=jax.ShapeDtypeStruct((2 * half, r, cdim), f.dtype),
        in_specs=[_hbm_spec()], out_specs=_hbm_spec(),
        scratch_shapes=[pltpu.SemaphoreType.DMA, pltpu.SemaphoreType.DMA, pltpu.SemaphoreType.DMA],
    )(f)


def _add_halves(g, recv, core, name):
    nchip, nl, r, cdim = g.shape
    half = nl // 2
    tr = _tile(r, (256, 128))

    def body(c_ref, g_ref, r_ref, o_ref):
        o_ref[...] = (g_ref[...].astype(F32) + r_ref[...].astype(F32)).astype(BF16)

    blk = (1, 1, tr, cdim)
    return pl.pallas_call(
        body, name=name,
        grid_spec=pltpu.PrefetchScalarGridSpec(
            num_scalar_prefetch=1, grid=(nchip, half, r // tr),
            in_specs=[pl.BlockSpec(blk, lambda j, l, i, c_ref: (j, c_ref[0] * half + l, i, 0)),
                      pl.BlockSpec(blk, lambda j, l, i, c_ref: (j, l, i, 0))],
            out_specs=pl.BlockSpec(blk, lambda j, l, i, c_ref: (j, l, i, 0))),
        out_shape=jax.ShapeDtypeStruct((nchip, half, r, cdim), BF16),
        compiler_params=_params(("parallel", "parallel", "parallel")),
    )(core, g, recv)


def _add_partials(hsum, others, chip, name):
    nchip, half, r, cdim = hsum.shape
    tr = _tile(r, (256, 128))

    def body(i_ref, h_ref, a_ref, b_ref, c_ref, o_ref):
        o_ref[...] = ((h_ref[...].astype(F32) + a_ref[...].astype(F32))
                      + (b_ref[...].astype(F32) + c_ref[...].astype(F32)))[0]

    blk = (1, 1, tr, cdim)
    other = lambda k: pl.BlockSpec(blk, lambda l, i, i_ref: (k, l, i, 0))
    return pl.pallas_call(
        body, name=name,
        grid_spec=pltpu.PrefetchScalarGridSpec(
            num_scalar_prefetch=1, grid=(half, r // tr),
            in_specs=[pl.BlockSpec(blk, lambda l, i, i_ref: (i_ref[0], l, i, 0)), other(0), other(1), other(2)],
            out_specs=pl.BlockSpec((1, tr, cdim), lambda l, i, i_ref: (l, i, 0))),
        out_shape=jax.ShapeDtypeStruct((half, r, cdim), F32),
        compiler_params=_params(("parallel", "parallel")),
    )(chip, hsum, others, others, others)


def _reduce_scatter(g, core, chip, tag):
    recv = _swap_other_half(g, f"rs_swap_{tag}")
    hsum = _add_halves(g, recv, core, f"rs_add_halves_{tag}")
    others = _exchange_chip_partials(hsum, f"rs_exchange_{tag}")
    final_half = _add_partials(hsum, others, chip, f"rs_add_partials_{tag}")
    return _share_final_half(final_half, f"rs_share_{tag}")


def _all_gather_small(v, name):
    m_per, n = v.shape

    def body(x_ref, out_ref, send_sems, recv_sems, local_sem):
        x, y, c, chips = _place()
        me, sib = (x, y, c), (x, y, 1 - c)

        def rows(px, py, pc):
            return out_ref.at[pl.ds((4 * px + 2 * py + pc) * m_per, m_per), :]

        def copy(k, block, to, src=None):
            return pltpu.make_async_remote_copy(
                src_ref=rows(*block) if src is None else src, dst_ref=rows(*block), send_sem=send_sems.at[k],
                recv_sem=recv_sems.at[k], device_id=to, device_id_type=MESH)

        mine = pltpu.make_async_copy(x_ref, rows(*me), local_sem)
        mine.start()
        first = [copy(0, me, sib, src=x_ref)]
        first += [copy(1 + j, me, (*chip, c), src=x_ref) for j, chip in enumerate(chips)]
        for cp in first:
            cp.start()
        passed = [copy(4 + j, (*chip, c), sib) for j, chip in enumerate(chips)]
        for j, chip in enumerate(chips):
            copy(1 + j, (*chip, c), me).wait_recv()
            passed[j].start()
        copy(0, sib, me).wait_recv()
        for j, chip in enumerate(chips):
            copy(4 + j, (*chip, 1 - c), me).wait_recv()
        for cp in first + passed:
            cp.wait_send()
        mine.wait()

    return pl.pallas_call(
        body, name=name, out_shape=jax.ShapeDtypeStruct((8 * m_per, n), v.dtype),
        in_specs=[pl.BlockSpec(memory_space=pltpu.VMEM)], out_specs=pl.BlockSpec(memory_space=pltpu.VMEM),
        scratch_shapes=[pltpu.SemaphoreType.DMA((7,)), pltpu.SemaphoreType.DMA((7,)), pltpu.SemaphoreType.DMA],
        compiler_params=pltpu.CompilerParams(vmem_limit_bytes=VMEM_LIMIT),
    )(v)


def _sum_blocks(v, nblk, name):
    m_per = v.shape[0] // nblk

    def body(v_ref, o_ref):
        acc = v_ref[0:m_per, :]
        for j in range(1, nblk):
            acc = acc + v_ref[j * m_per:(j + 1) * m_per, :]
        o_ref[...] = acc

    return pl.pallas_call(
        body, name=name, out_shape=jax.ShapeDtypeStruct((m_per, 128), F32),
        in_specs=[pl.BlockSpec(memory_space=pltpu.VMEM)], out_specs=pl.BlockSpec(memory_space=pltpu.VMEM),
        compiler_params=pltpu.CompilerParams(vmem_limit_bytes=VMEM_LIMIT),
    )(v)


def _rows128(a, rows=None):
    flat = a.reshape(-1).astype(F32)
    need = -(-flat.shape[0] // 128)
    rows = rows or -(-need // 8) * 8
    return jnp.pad(flat, (0, rows * 128 - flat.shape[0])).reshape(rows, 128)


def kernel(x, w_in, b_f, conv_w, pool_w, pool_scale, w_out, ln_g, ln_b, loss_target, m_w_in, m_b_f, m_conv_w, m_pool_w, m_pool_scale, m_w_out, m_ln_g, m_ln_b, v_w_in, v_b_f, v_conv_w, v_pool_w, v_pool_scale, v_w_out, v_ln_g, v_ln_b):
    depth, d, in_shard = w_in.shape
    s = x.shape[1]
    att = d // 2
    n_heads = att // HEAD_DIM
    cw = d // 4
    pw = d - att - cw
    in_w = 4 * att + n_heads + 4 * cw + 2 * pw
    assert in_shard * 4 == in_w and (d, n_heads, cw, pw) == (2048, 8, 512, 512) and s % 512 == 0
    main_w = in_w - n_heads
    fg0 = 4 * att
    alpha = (2 * depth) ** 0.25
    t_att = 512
    tm_mix = 256

    cx, cy, cc = lax.axis_index("x"), lax.axis_index("y"), lax.axis_index("c")
    chip = 2 * cx + cy
    core_arr = jnp.reshape(cc, (1,)).astype(jnp.int32)
    chip_arr = jnp.reshape(chip, (1,)).astype(jnp.int32)

    g_in = _gather_chip_shards(w_in.astype(BF16)[None], "gather_w_in")
    w_full = jnp.moveaxis(g_in, 0, 2).reshape(depth, d, in_w)
    w_main = jnp.concatenate([w_full[:, :, :fg0], w_full[:, :, fg0 + n_heads:]], axis=2)
    w_fg = jnp.pad(w_full[:, :, fg0:fg0 + n_heads], ((0, 0), (0, 0), (0, 128 - n_heads)))
    g_out = _gather_chip_shards(w_out.astype(BF16)[None], "gather_w_out")
    w_out_full = jnp.moveaxis(g_out, 0, 1).reshape(depth, d, d)
    cw_rows = _rows128(conv_w)
    cw_all = _all_gather_small(cw_rows, "gather_conv_w").reshape(4, 2, cw_rows.shape[0] * 128)
    conv_full = cw_all[:, 0, :depth * 3 * (cw // 4)].reshape(4, depth, 3, cw // 4)
    conv_full = jnp.moveaxis(conv_full, 0, 2).reshape(depth, 3, cw)
    pool_b = pool_w.astype(BF16)

    xf = x[0]
    xb = xf.astype(BF16)
    saved = []
    for l in range(depth):
        p_main = _matmul(xb, w_main[l], "nn", BF16, "proj_main")
        fg = _matmul(xb, w_fg[l], "nn", F32, "proj_fg")
        fg_t = fg[:, :n_heads].T
        b_col = b_f[l].reshape(n_heads, 1)
        cum3 = _gate_fwd(fg_t, b_col).reshape(n_heads, 1, s)
        o, lse = _attn_fwd(p_main, cum3, n_heads, t_att)
        y = _mix_fwd(p_main, o, conv_full[l], pool_b[l], pool_scale[l].reshape(1, pw), tm_mix)
        x_new, xb_new, pre = _outproj_ln(y, w_out_full[l], xf, ln_g[l].reshape(1, d), ln_b[l].reshape(1, d),
                                         alpha, tm_mix)
        saved.append((xb, p_main, fg_t, b_col, cum3, o, lse, y, pre))
        xf, xb = x_new, xb_new

    dx, loss_blk = _loss_grad(xf, loss_target[0], tm_mix)

    gw_main, gw_fg, gw_out, g_bf, g_sm, g_pw, g_ln = ([None] * depth for _ in range(7))
    for l in reversed(range(depth)):
        xb, p_main, fg_t, b_col, cum3, o, lse, y, pre = saved[l]
        dpre, dpre_b, g_ln[l] = _ln_bwd(dx, pre, ln_g[l].reshape(1, d), tm_mix)
        dy = _matmul(dpre_b, w_out_full[l], "nt", F32, "d_y")
        gw_out[l] = _matmul(y, dpre_b, "tn", F32, "d_w_out")
        do, delta, dp_rest, g_sm[l], g_pw[l] = _mix_bwd(p_main, o, dy, conv_full[l], pool_b[l],
                                                        pool_scale[l].reshape(1, pw), tm_mix)
        dq, dk, dv, drow, dcol = _attn_bwd(p_main, do, lse, delta, cum3, n_heads, t_att)
        dfg_t, g_bf[l] = _gate_bwd(drow[:, ::128].T, dcol[:, ::128].T, fg_t, b_col)
        dfg_b = jnp.pad(dfg_t.T, ((0, 0), (0, 128 - n_heads))).astype(BF16)
        dp_main = jnp.concatenate([dq, dk, dv, dp_rest], axis=1)
        gw_main[l] = _matmul(xb, dp_main, "tn", F32, "d_w_main")
        gw_fg[l] = _matmul(xb, dfg_b, "tn", F32, "d_w_fg")
        dx = _matmul(dp_main, w_main[l], "nt", F32, "d_x_main", add=dpre, add_scale=alpha)
        dx = _matmul(dfg_b, w_fg[l], "nt", F32, "d_x_fg", add=dx)

    gwm = jnp.stack(gw_main)
    gw_in_full = jnp.concatenate([gwm[:, :, :fg0], jnp.stack(gw_fg)[:, :, :n_heads], gwm[:, :, fg0:]], axis=2)
    gin_by_chip = jnp.moveaxis(gw_in_full.astype(BF16).reshape(depth, d, 4, in_shard), 2, 0)
    grad_w_in = _reduce_scatter(gin_by_chip, core_arr, chip_arr, "w_in")
    gout_by_chip = jnp.moveaxis(jnp.stack(gw_out).astype(BF16).reshape(depth, 4, d // 4, d), 1, 0)
    grad_w_out = _reduce_scatter(gout_by_chip, core_arr, chip_arr, "w_out")

    g_sm = jnp.stack(g_sm)
    g_ln = jnp.stack(g_ln)
    parts = [jnp.stack(g_pw), g_sm[:, 0:3, :], g_sm[:, 3, :], g_ln[:, 0, :], g_ln[:, 1, :],
             jnp.stack(g_bf)[:, :, 0], loss_blk[0, 0]]
    packed = jnp.concatenate([_rows128(p) for p in parts], axis=0)
    summed = _sum_blocks(_all_gather_small(packed, "gather_small_grads"), 8, "sum_small_grads")
    outs, off = [], 0
    for p in parts:
        nr = _rows128(p).shape[0]
        outs.append(summed[off:off + nr].reshape(-1)[:p.size].reshape(p.shape))
        off += nr
    grad_pool_w, gconv_full, grad_pool_scale, grad_ln_g, grad_ln_b, grad_b_f, loss = outs
    grad_conv_w = lax.dynamic_slice_in_dim(gconv_full, chip * (cw // 4), cw // 4, axis=2)

    d_w_in, nm_w_in, nv_w_in = _adamw(w_in, grad_w_in, m_w_in, v_w_in, "adamw_w_in")
    d_w_out, nm_w_out, nv_w_out = _adamw(w_out, grad_w_out, m_w_out, v_w_out, "adamw_w_out")
    small = [(b_f, grad_b_f, m_b_f, v_b_f), (conv_w, grad_conv_w, m_conv_w, v_conv_w),
             (pool_w, grad_pool_w, m_pool_w, v_pool_w), (pool_scale, grad_pool_scale, m_pool_scale, v_pool_scale),
             (ln_g, grad_ln_g, m_ln_g, v_ln_g), (ln_b, grad_ln_b, m_ln_b, v_ln_b)]
    pk = [jnp.concatenate([_rows128(t[j]) for t in small], axis=0)[None] for j in range(4)]
    sm_out = _adamw(pk[0], pk[1], pk[2], pk[3], "adamw_small")
    res = {j: [] for j in range(3)}
    off = 0
    for t in small:
        nr = _rows128(t[0]).shape[0]
        for j in range(3):
            res[j].append(sm_out[j][0, off:off + nr].reshape(-1)[:t[0].size].reshape(t[0].shape))
        off += nr
    (d_b_f, d_conv_w, d_pool_w, d_pool_scale, d_ln_g, d_ln_b) = res[0]
    (nm_b_f, nm_conv_w, nm_pool_w, nm_pool_scale, nm_ln_g, nm_ln_b) = res[1]
    (nv_b_f, nv_conv_w, nv_pool_w, nv_pool_scale, nv_ln_g, nv_ln_b) = res[2]

    return (loss, dx[None], grad_w_in, grad_b_f, grad_conv_w, grad_pool_w, grad_pool_scale, grad_w_out,
            grad_ln_g, grad_ln_b,
            d_w_in, d_b_f, d_conv_w, d_pool_w, d_pool_scale, d_w_out, d_ln_g, d_ln_b,
            nm_w_in, nm_b_f, nm_conv_w, nm_pool_w, nm_pool_scale, nm_w_out, nm_ln_g, nm_ln_b,
            nv_w_in, nv_b_f, nv_conv_w, nv_pool_w, nv_pool_scale, nv_w_out, nv_ln_g, nv_ln_b)
```

```python
import functools

import jax
import jax.numpy as jnp
from jax import lax
from jax.experimental import pallas as pl
from jax.experimental.pallas import tpu as pltpu

F32 = jnp.float32
BF16 = jnp.bfloat16
MESH = pl.DeviceIdType.MESH

HEAD_DIM = 128
POOL_WINDOWS = (2, 4, 8, 16)
HALO = 16
LN_EPS = 1e-5
ADAM_LR = 0.001
ADAM_B1 = 0.9
ADAM_B2 = 0.999
ADAM_EPS = 1e-08
ADAM_WD = 0.01
ADAM_STEP = 10
VMEM_LIMIT = 56 * 1024 * 1024


def _params(sem, vmem=VMEM_LIMIT):
    return pltpu.CompilerParams(dimension_semantics=sem, vmem_limit_bytes=vmem)


def _tile(n, prefs):
    for t in prefs:
        if n % t == 0:
            return t
    return n


def _sigmoid(x):
    return 1.0 / (1.0 + jnp.exp(-x))


_DIMS = {"nn": ((1,), (0,)), "nt": ((1,), (1,)), "tn": ((0,), (0,))}


def _matmul(a, b, form, out_dtype, name, add=None, add_scale=1.0, tm=None, tn=None, tk=None):
    if form == "nn":
        (m, k), (_, n) = a.shape, b.shape
    elif form == "nt":
        (m, k), (n, _) = a.shape, b.shape
    else:
        (k, m), (_, n) = a.shape, b.shape
    tm = tm or _tile(m, (1024, 512, 256, 128))
    tn = tn or _tile(n, (1024, 512, 256, 128))
    tk = tk or _tile(k, (2048, 1792, 1024, 512, 256, 128))
    nk = k // tk
    if form == "tn":
        a_spec = pl.BlockSpec((tk, tm), lambda i, j, kk: (kk, i))
    else:
        a_spec = pl.BlockSpec((tm, tk), lambda i, j, kk: (i, kk))
    if form == "nt":
        b_spec = pl.BlockSpec((tn, tk), lambda i, j, kk: (j, kk))
    else:
        b_spec = pl.BlockSpec((tk, tn), lambda i, j, kk: (kk, j))
    o_spec = pl.BlockSpec((tm, tn), lambda i, j, kk: (i, j))
    dims = (_DIMS[form], ((), ()))
    has_add = add is not None

    def finish(r, add_ref, o_ref):
        if has_add:
            r = r + add_scale * add_ref[...]
        o_ref[...] = r.astype(out_dtype)

    def body_one_pass(a_ref, b_ref, *rest):
        add_ref, o_ref = rest if has_add else (None,) + rest
        finish(lax.dot_general(a_ref[...], b_ref[...], dims, preferred_element_type=F32), add_ref, o_ref)

    def body_accumulate(a_ref, b_ref, *rest):
        add_ref, o_ref, acc = rest if has_add else (None,) + rest
        kk = pl.program_id(2)
        part = lax.dot_general(a_ref[...], b_ref[...], dims, preferred_element_type=F32)

        @pl.when(kk == 0)
        def _():
            acc[...] = part

        @pl.when(jnp.logical_and(kk > 0, kk < nk - 1))
        def _():
            acc[...] += part

        @pl.when(kk == nk - 1)
        def _():
            finish(acc[...] + part, add_ref, o_ref)

    in_specs = [a_spec, b_spec] + ([o_spec] if has_add else [])
    args = (a, b) + ((add,) if has_add else ())
    return pl.pallas_call(
        body_one_pass if nk == 1 else body_accumulate, name=name, grid=(m // tm, n // tn, nk),
        in_specs=in_specs, out_specs=o_spec,
        out_shape=jax.ShapeDtypeStruct((m, n), out_dtype),
        scratch_shapes=[] if nk == 1 else [pltpu.VMEM((tm, tn), F32)],
        compiler_params=_params(("parallel", "parallel", "arbitrary")),
    )(*args)


def _tri(n, upper):
    r = lax.broadcasted_iota(jnp.int32, (n, n), 0)
    c = lax.broadcasted_iota(jnp.int32, (n, n), 1)
    return jnp.where((r <= c) if upper else (r >= c), 1.0, 0.0).astype(F32)


def _gate_fwd(fg_t, b_col):
    h, s = fg_t.shape
    nb = s // 128

    def body(fg_ref, b_ref, cum_ref):
        u = _tri(128, True)
        carry = jnp.zeros((h, 128), F32)
        for j in range(nb):
            z = fg_ref[:, j * 128:(j + 1) * 128] + b_ref[...]
            logf = -(jnp.maximum(-z, 0.0) + jnp.log(1.0 + jnp.exp(-jnp.abs(z))))
            c = jnp.dot(logf, u, precision=lax.Precision.HIGHEST, preferred_element_type=F32) + carry
            cum_ref[:, j * 128:(j + 1) * 128] = c
            carry = jnp.broadcast_to(c[:, 127:128], (h, 128))

    return pl.pallas_call(
        body, name="gate_fwd", out_shape=jax.ShapeDtypeStruct((h, s), F32),
        in_specs=[pl.BlockSpec(memory_space=pltpu.VMEM)] * 2,
        out_specs=pl.BlockSpec(memory_space=pltpu.VMEM),
    )(fg_t, b_col)


def _gate_bwd(drow_t, dcol_t, fg_t, b_col):
    h, s = fg_t.shape
    nb = s // 128

    def body(dr_ref, dc_ref, fg_ref, b_ref, dfg_ref, db_ref):
        low = _tri(128, False)
        carry = jnp.zeros((h, 128), F32)
        db = jnp.zeros((h, 128), F32)
        for j in reversed(range(nb)):
            sl = slice(j * 128, (j + 1) * 128)
            r = jnp.dot(dr_ref[:, 0, sl] - dc_ref[:, 0, sl], low, precision=lax.Precision.HIGHEST,
                        preferred_element_type=F32) + carry
            carry = jnp.broadcast_to(r[:, 0:1], (h, 128))
            z = fg_ref[:, sl] + b_ref[...]
            dfg = r / (1.0 + jnp.exp(z))
            dfg_ref[:, sl] = dfg
            db = db + dfg
        db_ref[...] = jnp.broadcast_to(jnp.sum(db, axis=1, keepdims=True), (h, 128))

    return pl.pallas_call(
        body, name="gate_bwd",
        out_shape=(jax.ShapeDtypeStruct((h, s), F32), jax.ShapeDtypeStruct((h, 128), F32)),
        in_specs=[pl.BlockSpec(memory_space=pltpu.VMEM)] * 4,
        out_specs=(pl.BlockSpec(memory_space=pltpu.VMEM),) * 2,
    )(drow_t, dcol_t, fg_t, b_col)


def _attn_fwd(p_main, cum3, n_heads, t):
    s = p_main.shape[0]
    nq = s // t
    scale = HEAD_DIM ** -0.5
    rep = t // 128

    def body(q_ref, k_ref, v_ref, c_ref, o_ref, lse_ref, m_s, l_s, acc_s):
        qi = pl.program_id(1)
        ki = pl.program_id(2)

        @pl.when(ki == 0)
        def _():
            m_s[...] = jnp.full_like(m_s, -jnp.inf)
            l_s[...] = jnp.zeros_like(l_s)
            acc_s[...] = jnp.zeros_like(acc_s)

        def step(masked):
            sc = lax.dot_general(q_ref[...], k_ref[...], (_DIMS["nt"], ((), ())),
                                 preferred_element_type=F32) * scale - c_ref[...]
            if masked:
                row = lax.broadcasted_iota(jnp.int32, (t, t), 0)
                col = lax.broadcasted_iota(jnp.int32, (t, t), 1)
                sc = jnp.where(col <= row, sc, -jnp.inf)
            m_prev = m_s[...]
            m_new = jnp.maximum(m_prev, jnp.max(sc, axis=1, keepdims=True))
            alpha = jnp.exp(m_prev - m_new)
            p = jnp.exp(sc - jnp.tile(m_new, (1, rep)))
            l_s[...] = alpha * l_s[...] + jnp.sum(p, axis=1, keepdims=True)
            acc_s[...] = alpha * acc_s[...] + jnp.dot(p.astype(BF16), v_ref[...], preferred_element_type=F32)
            m_s[...] = m_new

        @pl.when(ki < qi)
        def _():
            step(False)

        @pl.when(ki == qi)
        def _():
            step(True)
            o_ref[...] = (acc_s[...] / l_s[...]).astype(BF16)
            lse_ref[...] = m_s[...] + jnp.log(l_s[...])

    blk = lambda off: pl.BlockSpec((t, 128), lambda h, qi, ki: (jnp.minimum(ki, qi), off + h))
    return pl.pallas_call(
        body, name="attn_fwd", grid=(n_heads, nq, nq),
        in_specs=[pl.BlockSpec((t, 128), lambda h, qi, ki: (qi, h)), blk(n_heads), blk(2 * n_heads),
                  pl.BlockSpec((None, 1, t), lambda h, qi, ki: (h, 0, jnp.minimum(ki, qi)))],
        out_specs=(pl.BlockSpec((t, 128), lambda h, qi, ki: (qi, h)),) * 2,
        out_shape=(jax.ShapeDtypeStruct((s, n_heads * 128), BF16), jax.ShapeDtypeStruct((s, n_heads * 128), F32)),
        scratch_shapes=[pltpu.VMEM((t, 128), F32)] * 3,
        compiler_params=_params(("parallel", "parallel", "arbitrary")),
    )(p_main, p_main, p_main, cum3)


def _attn_bwd(p_main, do, lse_rep, delta_rep, cum3, n_heads, t):
    s = p_main.shape[0]
    nq = s // t
    scale = HEAD_DIM ** -0.5
    rep = t // 128

    def body(q_ref, k_ref, v_ref, do_ref, lse_ref, dl_ref, c_ref,
             dq_ref, dk_ref, dv_ref, drow_ref, dcol_ref, dq_acc, dk_acc, dv_acc):
        kb = pl.program_id(1)
        qb = pl.program_id(2)
        rows = pl.ds(pl.multiple_of(qb * t, t), t)

        @pl.when(qb == kb)
        def _():
            dk_acc[...] = jnp.zeros_like(dk_acc)
            dv_acc[...] = jnp.zeros_like(dv_acc)

        def step(masked):
            q = q_ref[...]
            k = k_ref[...]
            dout = do_ref[...]
            ones = jnp.ones((t, 128), BF16)
            sc = lax.dot_general(q, k, (_DIMS["nt"], ((), ())), preferred_element_type=F32) * scale - c_ref[...]
            if masked:
                row = lax.broadcasted_iota(jnp.int32, (t, t), 0)
                col = lax.broadcasted_iota(jnp.int32, (t, t), 1)
                sc = jnp.where(col <= row, sc, -jnp.inf)
            p = jnp.exp(sc - jnp.tile(lse_ref[...], (1, rep)))
            dp = lax.dot_general(dout, v_ref[...], (_DIMS["nt"], ((), ())), preferred_element_type=F32)
            dsb = (p * (dp - jnp.tile(dl_ref[...], (1, rep)))).astype(BF16)
            dv_acc[...] += lax.dot_general(p.astype(BF16), dout, (_DIMS["tn"], ((), ())),
                                           preferred_element_type=F32)
            dk_acc[...] += lax.dot_general(dsb, jnp.concatenate([q, ones], axis=1), (_DIMS["tn"], ((), ())),
                                           preferred_element_type=F32)
            dq_c = jnp.dot(dsb, jnp.concatenate([k, ones], axis=1), preferred_element_type=F32)

            @pl.when(kb == 0)
            def _():
                dq_acc[rows, :] = dq_c

            @pl.when(kb > 0)
            def _():
                dq_acc[rows, :] += dq_c

        @pl.when(qb > kb)
        def _():
            step(False)

        @pl.when(qb == kb)
        def _():
            step(True)
            dq_ref[rows, :] = (dq_acc[rows, 0:128] * scale).astype(BF16)
            drow_ref[:, rows] = dq_acc[rows, 128:256].T[0:8, :]

        @pl.when(qb == nq - 1)
        def _():
            dk_ref[...] = (dk_acc[:, 0:128] * scale).astype(BF16)
            dcol_ref[...] = dk_acc[:, 128:256].T[0:8, :]
            dv_ref[...] = dv_acc[...].astype(BF16)

    qside = pl.BlockSpec((t, 128), lambda h, kb, qb: (jnp.maximum(qb, kb), h))
    kside = lambda off: pl.BlockSpec((t, 128), lambda h, kb, qb: (kb, off + h))
    whole = pl.BlockSpec((s, 128), lambda h, kb, qb: (0, h))
    hw = n_heads * 128
    return pl.pallas_call(
        body, name="attn_bwd", grid=(n_heads, nq, nq),
        in_specs=[qside, kside(n_heads), kside(2 * n_heads), qside, qside, qside,
                  pl.BlockSpec((None, 1, t), lambda h, kb, qb: (h, 0, kb))],
        out_specs=(whole, kside(0), kside(0), pl.BlockSpec((None, 8, s), lambda h, kb, qb: (h, 0, 0)),
                   pl.BlockSpec((None, 8, t), lambda h, kb, qb: (h, 0, kb))),
        out_shape=(jax.ShapeDtypeStruct((s, hw), BF16), jax.ShapeDtypeStruct((s, hw), BF16),
                   jax.ShapeDtypeStruct((s, hw), BF16), jax.ShapeDtypeStruct((n_heads, 8, s), F32),
                   jax.ShapeDtypeStruct((n_heads, 8, s), F32)),
        scratch_shapes=[pltpu.VMEM((s, 256), F32), pltpu.VMEM((t, 256), F32), pltpu.VMEM((t, 128), F32)],
        compiler_params=_params(("parallel", "arbitrary", "arbitrary")),
    )(p_main, p_main, p_main, do, lse_rep, delta_rep, cum3)


def _conv_fwd(u_ext, cw_ref):
    r1 = pltpu.roll(u_ext, 1, 0)
    r2 = pltpu.roll(u_ext, 2, 0)
    conv = cw_ref[2:3, :] * u_ext + cw_ref[1:2, :] * r1 + cw_ref[0:1, :] * r2
    return conv[HALO:], r1[HALO:], r2[HALO:]


def _pool_z(pu, pu_halo, row0, tm, pg):
    ext = jnp.concatenate([pu_halo, pu], axis=0)
    t1 = (row0 + lax.broadcasted_iota(jnp.int32, (tm, pg), 0) + 1).astype(F32)
    zs = []
    for g, w in enumerate(POOL_WINDOWS):
        sm = ext[:, g * pg:(g + 1) * pg]
        sh = 1
        while sh < w:
            sm = sm + pltpu.roll(sm, sh, 0)
            sh *= 2
        mean = sm[HALO:] / jnp.minimum(t1, float(w))
        zs.append(mean - pu[:, g * pg:(g + 1) * pg])
    return zs


def _mix_specs(tm, s):
    per = tm // HALO
    last = s // HALO - 1
    cur = lambda w, j: pl.BlockSpec((tm, w), lambda i: (i, j))
    prev = lambda j: pl.BlockSpec((HALO, 512), lambda i: (jnp.maximum(i * per - 1, 0), j))
    nxt = lambda j: pl.BlockSpec((HALO, 512), lambda i: (jnp.minimum((i + 1) * per, last), j))
    full = lambda shape: pl.BlockSpec(shape, lambda i: (0,) * len(shape))
    return cur, prev, nxt, full


def _mix_fwd(p_main, o, conv_w, pool_w, pool_scale, tm):
    s = p_main.shape[0]
    d = 2048
    pg = 128
    cur, prev, nxt, full = _mix_specs(tm, s)

    def body(ga_ref, cb_ref, cc_ref, ch_ref, gc_ref, pu_ref, gp_ref, cch_ref, chh_ref, puh_ref,
             o_ref, cw_ref, pw_ref, ps_ref, y_ref):
        i = pl.program_id(0)
        first = i == 0
        ga = ga_ref[...].astype(F32)
        y_ref[:, 0:1024] = (o_ref[...].astype(F32) * ga * _sigmoid(ga)).astype(BF16)

        u = cc_ref[...].astype(F32) * ch_ref[...].astype(F32)
        uh = jnp.where(first, 0.0, cch_ref[...].astype(F32) * chh_ref[...].astype(F32))
        conv, _, _ = _conv_fwd(jnp.concatenate([uh, u], axis=0), cw_ref)
        gc = gc_ref[...].astype(F32)
        y_ref[:, 1024:1536] = (cb_ref[...].astype(F32) * conv * gc * _sigmoid(gc)).astype(BF16)

        pu = pu_ref[...].astype(F32)
        puh = jnp.where(first, 0.0, puh_ref[...].astype(F32))
        zs = _pool_z(pu, puh, i * tm, tm, pg)
        gp = gp_ref[...].astype(F32)
        gate = gp * _sigmoid(gp) * ps_ref[...]
        for g in range(4):
            r = jnp.dot(zs[g].astype(BF16), pw_ref[g], preferred_element_type=F32)
            y_ref[:, 1536 + g * pg:1536 + (g + 1) * pg] = (r * gate[:, g * pg:(g + 1) * pg]).astype(BF16)

    return pl.pallas_call(
        body, name="mix_fwd", grid=(s // tm,),
        in_specs=[cur(1024, 3), cur(512, 8), cur(512, 9), cur(512, 10), cur(512, 11), cur(512, 12), cur(512, 13),
                  prev(9), prev(10), prev(12),
                  cur(1024, 0), full((3, 512)), full((4, pg, pg)), full((1, 512))],
        out_specs=pl.BlockSpec((tm, d), lambda i: (i, 0)),
        out_shape=jax.ShapeDtypeStruct((s, d), BF16),
        compiler_params=_params(("parallel",)),
    )(*([p_main] * 10), o, conv_w, pool_w, pool_scale)


def _mix_bwd(p_main, o, dy, conv_w, pool_w, pool_scale, tm):
    s = p_main.shape[0]
    pg = 128
    n_heads = 8
    cur, prev, nxt, full = _mix_specs(tm, s)
    nblk = s // tm
    n_ext = tm + HALO

    def silu_and_grad(x):
        sg = _sigmoid(x)
        return x * sg, sg * (1.0 + x * (1.0 - sg))

    def body(ga_ref, cb_ref, cc_ref, ch_ref, gc_ref, pu_ref, gp_ref, cch_ref, chh_ref, puh_ref,
             cbn_ref, gcn_ref, gpn_ref, o_ref, dy_ref, dycn_ref, dypn_ref, cw_ref, pw_ref, ps_ref,
             do_ref, dl_ref, dp_ref, dsm_ref, dpw_ref):
        i = pl.program_id(0)
        first = i == 0
        last = i == nblk - 1

        @pl.when(first)
        def _():
            dsm_ref[...] = jnp.zeros_like(dsm_ref)
            dpw_ref[...] = jnp.zeros_like(dpw_ref)

        ga = ga_ref[...].astype(F32)
        of = o_ref[...].astype(F32)
        dya = dy_ref[:, 0:1024]
        sa, dsa = silu_and_grad(ga)
        dout = dya * sa
        do_ref[...] = dout.astype(BF16)
        dp_ref[:, 0:1024] = (dya * of * dsa).astype(BF16)
        prod = dout * of
        for h in range(n_heads):
            dsum = jnp.sum(prod[:, h * 128:(h + 1) * 128], axis=1, keepdims=True)
            dl_ref[:, h * 128:(h + 1) * 128] = jnp.broadcast_to(dsum, (tm, 128))

        cb = cb_ref[...].astype(F32)
        cc = cc_ref[...].astype(F32)
        ch = ch_ref[...].astype(F32)
        gc = gc_ref[...].astype(F32)
        u = cc * ch
        uh = jnp.where(first, 0.0, cch_ref[...].astype(F32) * chh_ref[...].astype(F32))
        conv, u1, u2 = _conv_fwd(jnp.concatenate([uh, u], axis=0), cw_ref)
        sc_, dsc = silu_and_grad(gc)
        dyc = dy_ref[:, 1024:1536]
        dp_ref[:, 1024:1536] = (dyc * conv * sc_).astype(BF16)
        dp_ref[:, 2560:3072] = (dyc * cb * conv * dsc).astype(BF16)
        dconv = dyc * cb * sc_
        gcn = gcn_ref[...].astype(F32)
        dconv_n = jnp.where(last, 0.0, dycn_ref[...] * cbn_ref[...].astype(F32) * gcn * _sigmoid(gcn))
        dext = jnp.concatenate([dconv, dconv_n], axis=0)
        du = (cw_ref[2:3, :] * dext + cw_ref[1:2, :] * pltpu.roll(dext, n_ext - 1, 0)
              + cw_ref[0:1, :] * pltpu.roll(dext, n_ext - 2, 0))[:tm]
        dp_ref[:, 1536:2048] = (du * ch).astype(BF16)
        dp_ref[:, 2048:2560] = (du * cc).astype(BF16)
        dsm_ref[0:1, :] += jnp.sum(dconv * u2, axis=0, keepdims=True)
        dsm_ref[1:2, :] += jnp.sum(dconv * u1, axis=0, keepdims=True)
        dsm_ref[2:3, :] += jnp.sum(dconv * u, axis=0, keepdims=True)

        pu = pu_ref[...].astype(F32)
        puh = jnp.where(first, 0.0, puh_ref[...].astype(F32))
        zs = _pool_z(pu, puh, i * tm, tm, pg)
        gp = gp_ref[...].astype(F32)
        sp, dsp = silu_and_grad(gp)
        dyp = dy_ref[:, 1536:2048]
        gpn = gpn_ref[...].astype(F32)
        dr_n = jnp.where(last, 0.0, dypn_ref[...] * gpn * _sigmoid(gpn) * ps_ref[...])
        drs = dyp * sp
        dr = drs * ps_ref[...]
        t1 = (i * tm + lax.broadcasted_iota(jnp.int32, (tm, pg), 0) + 1).astype(F32)
        r_parts, dpu_parts = [], []
        for g, w in enumerate(POOL_WINDOWS):
            cols = slice(g * pg, (g + 1) * pg)
            zb = zs[g].astype(BF16)
            r_parts.append(jnp.dot(zb, pw_ref[g], preferred_element_type=F32))
            drb = dr[:, cols].astype(BF16)
            dpw_ref[g] += lax.dot_general(zb, drb, (_DIMS["tn"], ((), ())), preferred_element_type=F32)
            dz = lax.dot_general(drb, pw_ref[g], (_DIMS["nt"], ((), ())), preferred_element_type=F32)
            dz_n = lax.dot_general(dr_n[:, cols].astype(BF16), pw_ref[g], (_DIMS["nt"], ((), ())),
                                   preferred_element_type=F32)
            sm = jnp.concatenate([dz / jnp.minimum(t1, float(w)), dz_n / float(w)], axis=0)
            sh = 1
            while sh < w:
                sm = sm + pltpu.roll(sm, n_ext - sh, 0)
                sh *= 2
            dpu_parts.append(sm[:tm] - dz)
        r = jnp.concatenate(r_parts, axis=1)
        dp_ref[:, 3072:3584] = jnp.concatenate(dpu_parts, axis=1).astype(BF16)
        dp_ref[:, 3584:4096] = (dyp * r * ps_ref[...] * dsp).astype(BF16)
        dsm_ref[3:4, :] += jnp.sum(drs * r, axis=0, keepdims=True)

    dy_next = lambda j: pl.BlockSpec((HALO, 512), lambda i: (jnp.minimum((i + 1) * (tm // HALO), s // HALO - 1), j))
    return pl.pallas_call(
        body, name="mix_bwd", grid=(nblk,),
        in_specs=[cur(1024, 3), cur(512, 8), cur(512, 9), cur(512, 10), cur(512, 11), cur(512, 12), cur(512, 13),
                  prev(9), prev(10), prev(12), nxt(8), nxt(11), nxt(13),
                  cur(1024, 0), pl.BlockSpec((tm, 2048), lambda i: (i, 0)), dy_next(2), dy_next(3),
                  full((3, 512)), full((4, pg, pg)), full((1, 512))],
        out_specs=(pl.BlockSpec((tm, 1024), lambda i: (i, 0)), pl.BlockSpec((tm, 1024), lambda i: (i, 0)),
                   pl.BlockSpec((tm, 4096), lambda i: (i, 0)), full((8, 512)), full((4, pg, pg))),
        out_shape=(jax.ShapeDtypeStruct((s, 1024), BF16), jax.ShapeDtypeStruct((s, 1024), F32),
                   jax.ShapeDtypeStruct((s, 4096), BF16), jax.ShapeDtypeStruct((8, 512), F32),
                   jax.ShapeDtypeStruct((4, pg, pg), F32)),
        compiler_params=_params(("arbitrary",)),
    )(*([p_main] * 13), o, dy, dy, dy, conv_w, pool_w, pool_scale)


def _outproj_ln(y, w_out, x, ln_g, ln_b, alpha, tm):
    s, d = x.shape

    def body(y_ref, w_ref, x_ref, g_ref, b_ref, xn_ref, xb_ref, pre_ref):
        pre = alpha * x_ref[...] + jnp.dot(y_ref[...], w_ref[...], preferred_element_type=F32)
        mu = jnp.mean(pre, axis=1, keepdims=True)
        cen = pre - mu
        var = jnp.mean(cen * cen, axis=1, keepdims=True)
        xn = cen * lax.rsqrt(var + LN_EPS) * g_ref[...] + b_ref[...]
        pre_ref[...] = pre
        xn_ref[...] = xn
        xb_ref[...] = xn.astype(BF16)

    row = pl.BlockSpec((tm, d), lambda i: (i, 0))
    vec = pl.BlockSpec((1, d), lambda i: (0, 0))
    return pl.pallas_call(
        body, name="outproj_ln", grid=(s // tm,),
        in_specs=[row, pl.BlockSpec((d, d), lambda i: (0, 0)), row, vec, vec],
        out_specs=(row, row, row),
        out_shape=(jax.ShapeDtypeStruct((s, d), F32), jax.ShapeDtypeStruct((s, d), BF16),
                   jax.ShapeDtypeStruct((s, d), F32)),
        compiler_params=_params(("parallel",)),
    )(y, w_out, x, ln_g, ln_b)


def _ln_bwd(dxn, pre, ln_g, tm):
    s, d = pre.shape

    def body(dx_ref, pre_ref, g_ref, dpre_ref, dpb_ref, dgb_ref):
        @pl.when(pl.program_id(0) == 0)
        def _():
            dgb_ref[...] = jnp.zeros_like(dgb_ref)

        pre_ = pre_ref[...]
        dx = dx_ref[...]
        mu = jnp.mean(pre_, axis=1, keepdims=True)
        cen = pre_ - mu
        var = jnp.mean(cen * cen, axis=1, keepdims=True)
        rstd = lax.rsqrt(var + LN_EPS)
        xhat = cen * rstd
        dxh = dx * g_ref[...]
        dpre = rstd * (dxh - jnp.mean(dxh, axis=1, keepdims=True)
                       - xhat * jnp.mean(dxh * xhat, axis=1, keepdims=True))
        dpre_ref[...] = dpre
        dpb_ref[...] = dpre.astype(BF16)
        dgb_ref[0:1, :] += jnp.sum(dx * xhat, axis=0, keepdims=True)
        dgb_ref[1:2, :] += jnp.sum(dx, axis=0, keepdims=True)

    row = pl.BlockSpec((tm, d), lambda i: (i, 0))
    return pl.pallas_call(
        body, name="ln_bwd", grid=(s // tm,),
        in_specs=[row, row, pl.BlockSpec((1, d), lambda i: (0, 0))],
        out_specs=(row, row, pl.BlockSpec((8, d), lambda i: (0, 0))),
        out_shape=(jax.ShapeDtypeStruct((s, d), F32), jax.ShapeDtypeStruct((s, d), BF16),
                   jax.ShapeDtypeStruct((8, d), F32)),
        compiler_params=_params(("arbitrary",)),
    )(dxn, pre, ln_g)


def _loss_grad(y, target, tm):
    s, d = y.shape

    def body(y_ref, t_ref, dy_ref, loss_ref):
        @pl.when(pl.program_id(0) == 0)
        def _():
            loss_ref[...] = jnp.zeros_like(loss_ref)

        diff = y_ref[...] - t_ref[...]
        dy_ref[...] = diff / d
        loss_ref[...] += 0.5 * jnp.sum(jnp.sum(diff * diff, axis=1, keepdims=True) / d)

    row = pl.BlockSpec((tm, d), lambda i: (i, 0))
    return pl.pallas_call(
        body, name="loss_grad", grid=(s // tm,),
        in_specs=[row, row],
        out_specs=(row, pl.BlockSpec((8, 128), lambda i: (0, 0))),
        out_shape=(jax.ShapeDtypeStruct((s, d), F32), jax.ShapeDtypeStruct((8, 128), F32)),
        compiler_params=_params(("arbitrary",)),
    )(y, target)


def _adamw(w, g, m, v, name):
    a, r, c = w.shape
    tr = _tile(r, (256, 128, 64, 32, 16, 8))
    c1 = 1.0 - ADAM_B1 ** ADAM_STEP
    c2 = 1.0 - ADAM_B2 ** ADAM_STEP

    def body(w_ref, g_ref, m_ref, v_ref, d_ref, mo_ref, vo_ref):
        g_ = g_ref[...]
        m_new = ADAM_B1 * m_ref[...] + (1.0 - ADAM_B1) * g_
        v_new = ADAM_B2 * v_ref[...] + (1.0 - ADAM_B2) * (g_ * g_)
        d_ref[...] = -ADAM_LR * ((m_new / c1) / (jnp.sqrt(v_new / c2) + ADAM_EPS) + ADAM_WD * w_ref[...])
        mo_ref[...] = m_new
        vo_ref[...] = v_new

    spec = pl.BlockSpec((1, tr, c), lambda i, j: (i, j, 0))
    shp = jax.ShapeDtypeStruct(w.shape, F32)
    return pl.pallas_call(
        body, name=name, grid=(a, r // tr), in_specs=[spec] * 4, out_specs=(spec,) * 3, out_shape=(shp,) * 3,
        compiler_params=_params(("parallel", "parallel")),
    )(w, g, m, v)


def _place():
    x, y, c = lax.axis_index("x"), lax.axis_index("y"), lax.axis_index("c")
    return x, y, c, [(1 - x, y), (x, 1 - y), (1 - x, 1 - y)]


def _hbm_spec():
    return pl.BlockSpec(memory_space=pltpu.HBM)


def _gather_chip_shards(shard, name):
    _, nl, r, cdim = shard.shape
    half = nl // 2

    def body(s_ref, out_ref, send_sems, recv_sems):
        x, y, c, chips = _place()
        sib = (x, y, 1 - c)

        def slot(px, py, pc):
            return out_ref.at[pl.ds(2 * px + py, 1), pl.ds(pc * half, half)]

        def copy(k, dst, to, src=None):
            return pltpu.make_async_remote_copy(
                src_ref=dst if src is None else src, dst_ref=dst, send_sem=send_sems.at[k],
                recv_sem=recv_sems.at[k], device_id=to, device_id_type=MESH)

        first = [copy(k, slot(x, y, c), (*chip, c), src=s_ref.at[:, pl.ds(c * half, half)])
                 for k, chip in enumerate(chips)]
        for cp in first:
            cp.start()
        passed = [copy(3 + k, slot(*chip, c), sib) for k, chip in enumerate(chips)]
        for k, chip in enumerate(chips):
            copy(k, slot(*chip, c), (x, y, c)).wait_recv()
            passed[k].start()
        for k, chip in enumerate(chips):
            copy(3 + k, slot(*chip, 1 - c), (x, y, c)).wait_recv()
        for cp in first + passed:
            cp.wait_send()

    gathered = pl.pallas_call(
        body, name=name, out_shape=jax.ShapeDtypeStruct((4, nl, r, cdim), shard.dtype),
        in_specs=[_hbm_spec()], out_specs=_hbm_spec(),
        scratch_shapes=[pltpu.SemaphoreType.DMA((6,)), pltpu.SemaphoreType.DMA((6,))],
    )(shard)
    chip = 2 * lax.axis_index("x") + lax.axis_index("y")
    return lax.dynamic_update_slice(gathered, shard, (chip, 0, 0, 0))


def _swap_other_half(g, name):
    nchip, nl, r, cdim = g.shape
    half = nl // 2

    def body(g_ref, out_ref, send_sem, recv_sem):
        x, y, c, _ = _place()
        cp = pltpu.make_async_remote_copy(
            src_ref=g_ref.at[:, pl.ds((1 - c) * half, half)], dst_ref=out_ref, send_sem=send_sem,
            recv_sem=recv_sem, device_id=(x, y, 1 - c), device_id_type=MESH)
        cp.start()
        cp.wait()

    return pl.pallas_call(
        body, name=name, out_shape=jax.ShapeDtypeStruct((nchip, half, r, cdim), g.dtype),
        in_specs=[_hbm_spec()], out_specs=_hbm_spec(),
        scratch_shapes=[pltpu.SemaphoreType.DMA, pltpu.SemaphoreType.DMA],
    )(g)


def _exchange_chip_partials(hsum, name):
    nchip, half, r, cdim = hsum.shape

    def body(h_ref, out_ref, send_sems, recv_sems):
        x, y, c, chips = _place()
        cps = [pltpu.make_async_remote_copy(
            src_ref=h_ref.at[pl.ds(2 * px + py, 1)], dst_ref=out_ref.at[pl.ds(k, 1)], send_sem=send_sems.at[k],
            recv_sem=recv_sems.at[k], device_id=(px, py, c), device_id_type=MESH)
            for k, (px, py) in enumerate(chips)]
        for cp in cps:
            cp.start()
        for cp in cps:
            cp.wait()

    return pl.pallas_call(
        body, name=name, out_shape=jax.ShapeDtypeStruct((3, half, r, cdim), hsum.dtype),
        in_specs=[_hbm_spec()], out_specs=_hbm_spec(),
        scratch_shapes=[pltpu.SemaphoreType.DMA((3,)), pltpu.SemaphoreType.DMA((3,))],
    )(hsum)


def _share_final_half(f, name):
    half, r, cdim = f.shape

    def body(f_ref, out_ref, send_sem, recv_sem):
        x, y, c, _ = _place()
        cp = pltpu.make_async_remote_copy(
            src_ref=f_ref, dst_ref=out_ref, send_sem=send_sem, recv_sem=recv_sem,
            device_id=(x, y, 1 - c), device_id_type=MESH)
        cp.start()
        cp.wait()

    theirs = pl.pallas_call(
        body, name=name, out_shape=jax.ShapeDtypeStruct((half, r, cdim), f.dtype),
        in_specs=[_hbm_spec()], out_specs=_hbm_spec(),
        scratch_shapes=[pltpu.SemaphoreType.DMA, pltpu.SemaphoreType.DMA],
    )(f)
    mine_first = lax.axis_index("c") == 0
    return jnp.concatenate([jnp.where(mine_first, f, theirs), jnp.where(mine_first, theirs, f)], axis=0)


def _add_halves(g, recv, core, name):
    nchip, nl, r, cdim = g.shape
    half = nl // 2
    tr = _tile(r, (256, 128))

    def body(c_ref, g_ref, r_ref, o_ref):
        o_ref[...] = (g_ref[...].astype(F32) + r_ref[...].astype(F32)).astype(BF16)

    blk = (1, 1, tr, cdim)
    return pl.pallas_call(
        body, name=name,
        grid_spec=pltpu.PrefetchScalarGridSpec(
            num_scalar_prefetch=1, grid=(nchip, half, r // tr),
            in_specs=[pl.BlockSpec(blk, lambda j, l, i, c_ref: (j, c_ref[0] * half + l, i, 0)),
                      pl.BlockSpec(blk, lambda j, l, i, c_ref: (j, l, i, 0))],
            out_specs=pl.BlockSpec(blk, lambda j, l, i, c_ref: (j, l, i, 0))),
        out_shape=jax.ShapeDtypeStruct((nchip, half, r, cdim), BF16),
        compiler_params=_params(("parallel", "parallel", "parallel")),
    )(core, g, recv)


def _add_partials(hsum, others, chip, name):
    nchip, half, r, cdim = hsum.shape
    tr = _tile(r, (256, 128))

    def body(i_ref, h_ref, a_ref, b_ref, c_ref, o_ref):
        o_ref[...] = ((h_ref[...].astype(F32) + a_ref[...].astype(F32))
                      + (b_ref[...].astype(F32) + c_ref[...].astype(F32)))[0]

    blk = (1, 1, tr, cdim)
    other = lambda k: pl.BlockSpec(blk, lambda l, i, i_ref: (k, l, i, 0))
    return pl.pallas_call(
        body, name=name,
        grid_spec=pltpu.PrefetchScalarGridSpec(
            num_scalar_prefetch=1, grid=(half, r // tr),
            in_specs=[pl.BlockSpec(blk, lambda l, i, i_ref: (i_ref[0], l, i, 0)), other(0), other(1), other(2)],
            out_specs=pl.BlockSpec((1, tr, cdim), lambda l, i, i_ref: (l, i, 0))),
        out_shape=jax.ShapeDtypeStruct((half, r, cdim), F32),
        compiler_params=_params(("parallel", "parallel")),
    )(chip, hsum, others, others, others)


def _reduce_scatter(g, core, chip, tag):
    recv = _swap_other_half(g, f"rs_swap_{tag}")
    hsum = _add_halves(g, recv, core, f"rs_add_halves_{tag}")
    others = _exchange_chip_partials(hsum, f"rs_exchange_{tag}")
    final_half = _add_partials(hsum, others, chip, f"rs_add_partials_{tag}")
    return _share_final_half(final_half, f"rs_share_{tag}")


def _all_gather_small(v, name):
    m_per, n = v.shape

    def body(x_ref, out_ref, send_sems, recv_sems, local_sem):
        x, y, c, chips = _place()
        me, sib = (x, y, c), (x, y, 1 - c)

        def rows(px, py, pc):
            return out_ref.at[pl.ds((4 * px + 2 * py + pc) * m_per, m_per), :]

        def copy(k, block, to, src=None):
            return pltpu.make_async_remote_copy(
                src_ref=rows(*block) if src is None else src, dst_ref=rows(*block), send_sem=send_sems.at[k],
                recv_sem=recv_sems.at[k], device_id=to, device_id_type=MESH)

        mine = pltpu.make_async_copy(x_ref, rows(*me), local_sem)
        mine.start()
        first = [copy(0, me, sib, src=x_ref)]
        first += [copy(1 + j, me, (*chip, c), src=x_ref) for j, chip in enumerate(chips)]
        for cp in first:
            cp.start()
        passed = [copy(4 + j, (*chip, c), sib) for j, chip in enumerate(chips)]
        for j, chip in enumerate(chips):
            copy(1 + j, (*chip, c), me).wait_recv()
            passed[j].start()
        copy(0, sib, me).wait_recv()
        for j, chip in enumerate(chips):
            copy(4 + j, (*chip, 1 - c), me).wait_recv()
        for cp in first + passed:
            cp.wait_send()
        mine.wait()

    return pl.pallas_call(
        body, name=name, out_shape=jax.ShapeDtypeStruct((8 * m_per, n), v.dtype),
        in_specs=[pl.BlockSpec(memory_space=pltpu.VMEM)], out_specs=pl.BlockSpec(memory_space=pltpu.VMEM),
        scratch_shapes=[pltpu.SemaphoreType.DMA((7,)), pltpu.SemaphoreType.DMA((7,)), pltpu.SemaphoreType.DMA],
        compiler_params=pltpu.CompilerParams(vmem_limit_bytes=VMEM_LIMIT),
    )(v)


def _sum_blocks(v, nblk, name):
    m_per = v.shape[0] // nblk

    def body(v_ref, o_ref):
        acc = v_ref[0:m_per, :]
        for j in range(1, nblk):
            acc = acc + v_ref[j * m_per:(j + 1) * m_per, :]
        o_ref[...] = acc

    return pl.pallas_call(
        body, name=name, out_shape=jax.ShapeDtypeStruct((m_per, 128), F32),
        in_specs=[pl.BlockSpec(memory_space=pltpu.VMEM)], out_specs=pl.BlockSpec(memory_space=pltpu.VMEM),
        compiler_params=pltpu.CompilerParams(vmem_limit_bytes=VMEM_LIMIT),
    )(v)


def _rows128(a, rows=None):
    flat = a.reshape(-1).astype(F32)
    need = -(-flat.shape[0] // 128)
    rows = rows or -(-need // 8) * 8
    return jnp.pad(flat, (0, rows * 128 - flat.shape[0])).reshape(rows, 128)


def kernel(x, w_in, b_f, conv_w, pool_w, pool_scale, w_out, ln_g, ln_b, loss_target, m_w_in, m_b_f, m_conv_w, m_pool_w, m_pool_scale, m_w_out, m_ln_g, m_ln_b, v_w_in, v_b_f, v_conv_w, v_pool_w, v_pool_scale, v_w_out, v_ln_g, v_ln_b):
    depth, d, in_shard = w_in.shape
    s = x.shape[1]
    att = d // 2
    n_heads = att // HEAD_DIM
    cw = d // 4
    pw = d - att - cw
    in_w = 4 * att + n_heads + 4 * cw + 2 * pw
    assert in_shard * 4 == in_w and (d, n_heads, cw, pw) == (2048, 8, 512, 512) and s % 512 == 0
    main_w = in_w - n_heads
    fg0 = 4 * att
    alpha = (2 * depth) ** 0.25
    t_att = 512
    tm_mix = 256

    cx, cy, cc = lax.axis_index("x"), lax.axis_index("y"), lax.axis_index("c")
    chip = 2 * cx + cy
    core_arr = jnp.reshape(cc, (1,)).astype(jnp.int32)
    chip_arr = jnp.reshape(chip, (1,)).astype(jnp.int32)

    g_in = _gather_chip_shards(w_in.astype(BF16)[None], "gather_w_in")
    w_full = jnp.moveaxis(g_in, 0, 2).reshape(depth, d, in_w)
    w_main = jnp.concatenate([w_full[:, :, :fg0], w_full[:, :, fg0 + n_heads:]], axis=2)
    w_fg = jnp.pad(w_full[:, :, fg0:fg0 + n_heads], ((0, 0), (0, 0), (0, 128 - n_heads)))
    g_out = _gather_chip_shards(w_out.astype(BF16)[None], "gather_w_out")
    w_out_full = jnp.moveaxis(g_out, 0, 1).reshape(depth, d, d)
    cw_rows = _rows128(conv_w)
    cw_all = _all_gather_small(cw_rows, "gather_conv_w").reshape(4, 2, cw_rows.shape[0] * 128)
    conv_full = cw_all[:, 0, :depth * 3 * (cw // 4)].reshape(4, depth, 3, cw // 4)
    conv_full = jnp.moveaxis(conv_full, 0, 2).reshape(depth, 3, cw)
    pool_b = pool_w.astype(BF16)

    xf = x[0]
    xb = xf.astype(BF16)
    saved = []
    for l in range(depth):
        p_main = _matmul(xb, w_main[l], "nn", BF16, "proj_main")
        fg = _matmul(xb, w_fg[l], "nn", F32, "proj_fg")
        fg_t = fg[:, :n_heads].T
        b_col = b_f[l].reshape(n_heads, 1)
        cum3 = _gate_fwd(fg_t, b_col).reshape(n_heads, 1, s)
        o, lse = _attn_fwd(p_main, cum3, n_heads, t_att)
        y = _mix_fwd(p_main, o, conv_full[l], pool_b[l], pool_scale[l].reshape(1, pw), tm_mix)
        x_new, xb_new, pre = _outproj_ln(y, w_out_full[l], xf, ln_g[l].reshape(1, d), ln_b[l].reshape(1, d),
                                         alpha, tm_mix)
        saved.append((xb, p_main, fg_t, b_col, cum3, o, lse, y, pre))
        xf, xb = x_new, xb_new

    dx, loss_blk = _loss_grad(xf, loss_target[0], tm_mix)

    gw_main, gw_fg, gw_out, g_bf, g_sm, g_pw, g_ln = ([None] * depth for _ in range(7))
    for l in reversed(range(depth)):
        xb, p_main, fg_t, b_col, cum3, o, lse, y, pre = saved[l]
        dpre, dpre_b, g_ln[l] = _ln_bwd(dx, pre, ln_g[l].reshape(1, d), tm_mix)
        dy = _matmul(dpre_b, w_out_full[l], "nt", F32, "d_y")
        gw_out[l] = _matmul(y, dpre_b, "tn", F32, "d_w_out")
        do, delta, dp_rest, g_sm[l], g_pw[l] = _mix_bwd(p_main, o, dy, conv_full[l], pool_b[l],
                                                        pool_scale[l].reshape(1, pw), tm_mix)
        dq, dk, dv, drow, dcol = _attn_bwd(p_main, do, lse, delta, cum3, n_heads, t_att)
        dfg_t, g_bf[l] = _gate_bwd(drow, dcol, fg_t, b_col)
        dfg_b = jnp.pad(dfg_t.T, ((0, 0), (0, 128 - n_heads))).astype(BF16)
        dp_main = jnp.concatenate([dq, dk, dv, dp_rest], axis=1)
        gw_main[l] = _matmul(xb, dp_main, "tn", F32, "d_w_main")
        gw_fg[l] = _matmul(xb, dfg_b, "tn", F32, "d_w_fg")
        dx = _matmul(dp_main, w_main[l], "nt", F32, "d_x_main", add=dpre, add_scale=alpha)
        dx = _matmul(dfg_b, w_fg[l], "nt", F32, "d_x_fg", add=dx)

    gwm = jnp.stack(gw_main)
    gw_in_full = jnp.concatenate([gwm[:, :, :fg0], jnp.stack(gw_fg)[:, :, :n_heads], gwm[:, :, fg0:]], axis=2)
    gin_by_chip = jnp.moveaxis(gw_in_full.astype(BF16).reshape(depth, d, 4, in_shard), 2, 0)
    grad_w_in = _reduce_scatter(gin_by_chip, core_arr, chip_arr, "w_in")
    gout_by_chip = jnp.moveaxis(jnp.stack(gw_out).astype(BF16).reshape(depth, 4, d // 4, d), 1, 0)
    grad_w_out = _reduce_scatter(gout_by_chip, core_arr, chip_arr, "w_out")

    g_sm = jnp.stack(g_sm)
    g_ln = jnp.stack(g_ln)
    parts = [jnp.stack(g_pw), g_sm[:, 0:3, :], g_sm[:, 3, :], g_ln[:, 0, :], g_ln[:, 1, :],
             jnp.stack(g_bf)[:, :, 0], loss_blk[0, 0]]
    packed = jnp.concatenate([_rows128(p) for p in parts], axis=0)
    summed = _sum_blocks(_all_gather_small(packed, "gather_small_grads"), 8, "sum_small_grads")
    outs, off = [], 0
    for p in parts:
        nr = _rows128(p).shape[0]
        outs.append(summed[off:off + nr].reshape(-1)[:p.size].reshape(p.shape))
        off += nr
    grad_pool_w, gconv_full, grad_pool_scale, grad_ln_g, grad_ln_b, grad_b_f, loss = outs
    grad_conv_w = lax.dynamic_slice_in_dim(gconv_full, chip * (cw // 4), cw // 4, axis=2)

    d_w_in, nm_w_in, nv_w_in = _adamw(w_in, grad_w_in, m_w_in, v_w_in, "adamw_w_in")
    d_w_out, nm_w_out, nv_w_out = _adamw(w_out, grad_w_out, m_w_out, v_w_out, "adamw_w_out")
    small = [(b_f, grad_b_f, m_b_f, v_b_f), (conv_w, grad_conv_w, m_conv_w, v_conv_w),
             (pool_w, grad_pool_w, m_pool_w, v_pool_w), (pool_scale, grad_pool_scale, m_pool_scale, v_pool_scale),
             (ln_g, grad_ln_g, m_ln_g, v_ln_g), (ln_b, grad_ln_b, m_ln_b, v_ln_b)]
    pk = [jnp.concatenate([_rows128(t[j]) for t in small], axis=0)[None] for j in range(4)]
    sm_out = _adamw(pk[0], pk[1], pk[2], pk[3], "adamw_small")
    res = {j: [] for j in range(3)}
    off = 0
    for t in small:
        nr = _rows128(t[0]).shape[0]
        for j in range(3):
            res[j].append(sm_out[j][0, off:off + nr].reshape(-1)[:t[0].size].reshape(t[0].shape))
        off += nr
    (d_b_f, d_conv_w, d_pool_w, d_pool_scale, d_ln_g, d_ln_b) = res[0]
    (nm_b_f, nm_conv_w, nm_pool_w, nm_pool_scale, nm_ln_g, nm_ln_b) = res[1]
    (nv_b_f, nv_conv_w, nv_pool_w, nv_pool_scale, nv_ln_g, nv_ln_b) = res[2]

    return (loss, dx[None], grad_w_in, grad_b_f, grad_conv_w, grad_pool_w, grad_pool_scale, grad_w_out,
            grad_ln_g, grad_ln_b,
            d_w_in, d_b_f, d_conv_w, d_pool_w, d_pool_scale, d_w_out, d_ln_g, d_ln_b,
            nm_w_in, nm_b_f, nm_conv_w, nm_pool_w, nm_pool_scale, nm_w_out, nm_ln_g, nm_ln_b,
            nv_w_in, nv_b_f, nv_conv_w, nv_pool_w, nv_pool_scale, nv_w_out, nv_ln_g, nv_ln_b)
```

```python
import functools

import jax
import jax.numpy as jnp
from jax import lax
from jax.experimental import pallas as pl
from jax.experimental.pallas import tpu as pltpu

F32 = jnp.float32
BF16 = jnp.bfloat16
MESH = pl.DeviceIdType.MESH

HEAD_DIM = 128
POOL_WINDOWS = (2, 4, 8, 16)
HALO = 16
LN_EPS = 1e-5
ADAM_LR = 0.001
ADAM_B1 = 0.9
ADAM_B2 = 0.999
ADAM_EPS = 1e-08
ADAM_WD = 0.01
ADAM_STEP = 10
VMEM_LIMIT = 56 * 1024 * 1024


def _params(sem, vmem=VMEM_LIMIT):
    return pltpu.CompilerParams(dimension_semantics=sem, vmem_limit_bytes=vmem)


def _tile(n, prefs):
    for t in prefs:
        if n % t == 0:
            return t
    return n


def _sigmoid(x):
    return 1.0 / (1.0 + jnp.exp(-x))


_DIMS = {"nn": ((1,), (0,)), "nt": ((1,), (1,)), "tn": ((0,), (0,))}


def _matmul(a, b, form, out_dtype, name, add=None, add_scale=1.0, tm=None, tn=None, tk=None):
    if form == "nn":
        (m, k), (_, n) = a.shape, b.shape
    elif form == "nt":
        (m, k), (n, _) = a.shape, b.shape
    else:
        (k, m), (_, n) = a.shape, b.shape
    tm = tm or _tile(m, (1024, 512, 256, 128))
    tn = tn or _tile(n, (1024, 512, 256, 128))
    tk = tk or _tile(k, (2048, 1792, 1024, 512, 256, 128))
    nk = k // tk
    if form == "tn":
        a_spec = pl.BlockSpec((tk, tm), lambda i, j, kk: (kk, i))
    else:
        a_spec = pl.BlockSpec((tm, tk), lambda i, j, kk: (i, kk))
    if form == "nt":
        b_spec = pl.BlockSpec((tn, tk), lambda i, j, kk: (j, kk))
    else:
        b_spec = pl.BlockSpec((tk, tn), lambda i, j, kk: (kk, j))
    o_spec = pl.BlockSpec((tm, tn), lambda i, j, kk: (i, j))
    dims = (_DIMS[form], ((), ()))
    has_add = add is not None

    def finish(r, add_ref, o_ref):
        if has_add:
            r = r + add_scale * add_ref[...]
        o_ref[...] = r.astype(out_dtype)

    def body_one_pass(a_ref, b_ref, *rest):
        add_ref, o_ref = rest if has_add else (None,) + rest
        finish(lax.dot_general(a_ref[...], b_ref[...], dims, preferred_element_type=F32), add_ref, o_ref)

    def body_accumulate(a_ref, b_ref, *rest):
        add_ref, o_ref, acc = rest if has_add else (None,) + rest
        kk = pl.program_id(2)
        part = lax.dot_general(a_ref[...], b_ref[...], dims, preferred_element_type=F32)

        @pl.when(kk == 0)
        def _():
            acc[...] = part

        @pl.when(jnp.logical_and(kk > 0, kk < nk - 1))
        def _():
            acc[...] += part

        @pl.when(kk == nk - 1)
        def _():
            finish(acc[...] + part, add_ref, o_ref)

    in_specs = [a_spec, b_spec] + ([o_spec] if has_add else [])
    args = (a, b) + ((add,) if has_add else ())
    return pl.pallas_call(
        body_one_pass if nk == 1 else body_accumulate, name=name, grid=(m // tm, n // tn, nk),
        in_specs=in_specs, out_specs=o_spec,
        out_shape=jax.ShapeDtypeStruct((m, n), out_dtype),
        scratch_shapes=[] if nk == 1 else [pltpu.VMEM((tm, tn), F32)],
        compiler_params=_params(("parallel", "parallel", "arbitrary")),
    )(*args)


def _tri(n, upper):
    r = lax.broadcasted_iota(jnp.int32, (n, n), 0)
    c = lax.broadcasted_iota(jnp.int32, (n, n), 1)
    return jnp.where((r <= c) if upper else (r >= c), 1.0, 0.0).astype(F32)


def _gate_fwd(fg_t, b_col):
    h, s = fg_t.shape
    nb = s // 128

    def body(fg_ref, b_ref, cum_ref):
        u = _tri(128, True)
        carry = jnp.zeros((h, 128), F32)
        for j in range(nb):
            z = fg_ref[:, j * 128:(j + 1) * 128] + b_ref[...]
            logf = -(jnp.maximum(-z, 0.0) + jnp.log(1.0 + jnp.exp(-jnp.abs(z))))
            c = jnp.dot(logf, u, precision=lax.Precision.HIGHEST, preferred_element_type=F32) + carry
            cum_ref[:, j * 128:(j + 1) * 128] = c
            carry = jnp.broadcast_to(c[:, 127:128], (h, 128))

    return pl.pallas_call(
        body, name="gate_fwd", out_shape=jax.ShapeDtypeStruct((h, s), F32),
        in_specs=[pl.BlockSpec(memory_space=pltpu.VMEM)] * 2,
        out_specs=pl.BlockSpec(memory_space=pltpu.VMEM),
    )(fg_t, b_col)


def _gate_bwd(drow_t, dcol_t, fg_t, b_col):
    h, s = fg_t.shape
    nb = s // 128

    def body(dr_ref, dc_ref, fg_ref, b_ref, dfg_ref, db_ref):
        low = _tri(128, False)
        carry = jnp.zeros((h, 128), F32)
        db = jnp.zeros((h, 128), F32)
        for j in reversed(range(nb)):
            sl = slice(j * 128, (j + 1) * 128)
            r = jnp.dot(dr_ref[:, 0, sl] - dc_ref[:, 0, sl], low, precision=lax.Precision.HIGHEST,
                        preferred_element_type=F32) + carry
            carry = jnp.broadcast_to(r[:, 0:1], (h, 128))
            z = fg_ref[:, sl] + b_ref[...]
            dfg = r / (1.0 + jnp.exp(z))
            dfg_ref[:, sl] = dfg
            db = db + dfg
        db_ref[...] = jnp.broadcast_to(jnp.sum(db, axis=1, keepdims=True), (h, 128))

    return pl.pallas_call(
        body, name="gate_bwd",
        out_shape=(jax.ShapeDtypeStruct((h, s), F32), jax.ShapeDtypeStruct((h, 128), F32)),
        in_specs=[pl.BlockSpec(memory_space=pltpu.VMEM)] * 4,
        out_specs=(pl.BlockSpec(memory_space=pltpu.VMEM),) * 2,
    )(drow_t, dcol_t, fg_t, b_col)


LOG2E = 1.4426950408889634
ATT_ROW_CHAINS = 1


def _attn_fwd(p_main, cum4, n_heads, t):
    s = p_main.shape[0]
    nq = s // t
    scale2 = HEAD_DIM ** -0.5 * LOG2E
    rep = t // 128
    tr = t // ATT_ROW_CHAINS

    def body(q_ref, k_ref, v_ref, c_ref, o_ref, lse_ref, m_s, l_s, acc_s):
        qi = pl.program_id(1)
        m_s[...] = jnp.full_like(m_s, -jnp.inf)
        l_s[...] = jnp.zeros_like(l_s)
        acc_s[...] = jnp.zeros_like(acc_s)

        def block(ki, masked):
            keys = pl.ds(pl.multiple_of(ki * t, t), t)
            k = k_ref[keys, :]
            v = v_ref[keys, :]
            c2 = c_ref[ki] * LOG2E
            for r in range(ATT_ROW_CHAINS):
                rs = slice(r * tr, (r + 1) * tr)
                sc = lax.dot_general(q_ref[rs, :], k, (_DIMS["nt"], ((), ())),
                                     preferred_element_type=F32) * scale2 - c2
                if masked:
                    row = lax.broadcasted_iota(jnp.int32, (tr, t), 0) + r * tr
                    col = lax.broadcasted_iota(jnp.int32, (tr, t), 1)
                    sc = jnp.where(col <= row, sc, -jnp.inf)
                m_prev = m_s[rs, :]
                m_new = jnp.maximum(m_prev, jnp.max(sc, axis=1, keepdims=True))
                alpha = jnp.exp2(m_prev - m_new)
                p = jnp.exp2(sc - jnp.tile(m_new, (1, rep)))
                l_s[rs, :] = alpha * l_s[rs, :] + jnp.sum(p, axis=1, keepdims=True)
                acc_s[rs, :] = alpha * acc_s[rs, :] + jnp.dot(p.astype(BF16), v, preferred_element_type=F32)
                m_s[rs, :] = m_new

        def full_block(ki, carry):
            block(ki, False)
            return carry

        lax.fori_loop(0, qi, full_block, 0)
        block(qi, True)
        o_ref[...] = (acc_s[...] / l_s[...]).astype(BF16)
        lse_ref[...] = m_s[...] + jnp.log(l_s[...]) * LOG2E

    head = lambda off: pl.BlockSpec((s, 128), lambda h, qi: (0, off + h))
    return pl.pallas_call(
        body, name="attn_fwd", grid=(n_heads, nq),
        in_specs=[pl.BlockSpec((t, 128), lambda h, qi: (qi, h)), head(n_heads), head(2 * n_heads),
                  pl.BlockSpec((None, nq, 1, t), lambda h, qi: (h, 0, 0, 0))],
        out_specs=(pl.BlockSpec((t, 128), lambda h, qi: (qi, h)),) * 2,
        out_shape=(jax.ShapeDtypeStruct((s, n_heads * 128), BF16), jax.ShapeDtypeStruct((s, n_heads * 128), F32)),
        scratch_shapes=[pltpu.VMEM((t, 128), F32)] * 3,
        compiler_params=_params(("parallel", "arbitrary")),
    )(p_main, p_main, p_main, cum4)


def _attn_bwd(p_main, do, lse_rep, delta_rep, cum4, n_heads, t):
    s = p_main.shape[0]
    nq = s // t
    scale = HEAD_DIM ** -0.5
    rep = t // 128
    tr = t // ATT_ROW_CHAINS

    def body(q_ref, k_ref, v_ref, do_ref, lse_ref, dl_ref, c_ref,
             dq_ref, dk_ref, dv_ref, drow_ref, dcol_ref, dq_acc, dk_acc, dv_acc):
        kb = pl.program_id(1)

        @pl.when(kb == 0)
        def _():
            dq_acc[...] = jnp.zeros_like(dq_acc)

        dk_acc[...] = jnp.zeros_like(dk_acc)
        dv_acc[...] = jnp.zeros_like(dv_acc)
        k = k_ref[...]
        v = v_ref[...]
        c2 = c_ref[...] * LOG2E
        k_ones = jnp.concatenate([k, jnp.ones((t, 128), BF16)], axis=1)
        q_ones = jnp.ones((tr, 128), BF16)

        def block(qb, masked):
            for r in range(ATT_ROW_CHAINS):
                rows = pl.ds(pl.multiple_of(qb * t + r * tr, tr), tr)
                q = q_ref[rows, :]
                dout = do_ref[rows, :]
                sc = lax.dot_general(q, k, (_DIMS["nt"], ((), ())), preferred_element_type=F32) * (scale * LOG2E) - c2
                if masked:
                    row = lax.broadcasted_iota(jnp.int32, (tr, t), 0) + r * tr
                    col = lax.broadcasted_iota(jnp.int32, (tr, t), 1)
                    sc = jnp.where(col <= row, sc, -jnp.inf)
                p = jnp.exp2(sc - jnp.tile(lse_ref[rows, :], (1, rep)))
                dp = lax.dot_general(dout, v, (_DIMS["nt"], ((), ())), preferred_element_type=F32)
                dsb = (p * (dp - jnp.tile(dl_ref[rows, :], (1, rep)))).astype(BF16)
                dv_acc[...] += lax.dot_general(p.astype(BF16), dout, (_DIMS["tn"], ((), ())),
                                               preferred_element_type=F32)
                dk_acc[...] += lax.dot_general(dsb, jnp.concatenate([q, q_ones], axis=1), (_DIMS["tn"], ((), ())),
                                               preferred_element_type=F32)
                dq_acc[rows, :] += jnp.dot(dsb, k_ones, preferred_element_type=F32)

        def full_block(qb, carry):
            block(qb, False)
            return carry

        block(kb, True)
        diag = pl.ds(pl.multiple_of(kb * t, t), t)
        dq_ref[diag, :] = (dq_acc[diag, 0:128] * scale).astype(BF16)
        drow_ref[:, diag] = dq_acc[diag, 128:256].T[0:8, :]
        lax.fori_loop(kb + 1, nq, full_block, 0)
        dk_ref[...] = (dk_acc[:, 0:128] * scale).astype(BF16)
        dcol_ref[...] = dk_acc[:, 128:256].T[0:8, :]
        dv_ref[...] = dv_acc[...].astype(BF16)

    kside = lambda off: pl.BlockSpec((t, 128), lambda h, kb: (kb, off + h))
    whole = pl.BlockSpec((s, 128), lambda h, kb: (0, h))
    hw = n_heads * 128
    return pl.pallas_call(
        body, name="attn_bwd", grid=(n_heads, nq),
        in_specs=[whole, kside(n_heads), kside(2 * n_heads), whole, whole, whole,
                  pl.BlockSpec((None, None, 1, t), lambda h, kb: (h, kb, 0, 0))],
        out_specs=(whole, kside(0), kside(0), pl.BlockSpec((None, 8, s), lambda h, kb: (h, 0, 0)),
                   pl.BlockSpec((None, 8, t), lambda h, kb: (h, 0, kb))),
        out_shape=(jax.ShapeDtypeStruct((s, hw), BF16), jax.ShapeDtypeStruct((s, hw), BF16),
                   jax.ShapeDtypeStruct((s, hw), BF16), jax.ShapeDtypeStruct((n_heads, 8, s), F32),
                   jax.ShapeDtypeStruct((n_heads, 8, s), F32)),
        scratch_shapes=[pltpu.VMEM((s, 256), F32), pltpu.VMEM((t, 256), F32), pltpu.VMEM((t, 128), F32)],
        compiler_params=_params(("parallel", "arbitrary")),
    )(p_main, p_main, p_main, do, lse_rep, delta_rep, cum4)


def _conv_fwd(u_ext, cw_ref):
    r1 = pltpu.roll(u_ext, 1, 0)
    r2 = pltpu.roll(u_ext, 2, 0)
    conv = cw_ref[2:3, :] * u_ext + cw_ref[1:2, :] * r1 + cw_ref[0:1, :] * r2
    return conv[HALO:], r1[HALO:], r2[HALO:]


def _pool_z(pu, pu_halo, row0, tm, pg):
    ext = jnp.concatenate([pu_halo, pu], axis=0)
    t1 = (row0 + lax.broadcasted_iota(jnp.int32, (tm, pg), 0) + 1).astype(F32)
    zs = []
    for g, w in enumerate(POOL_WINDOWS):
        sm = ext[:, g * pg:(g + 1) * pg]
        sh = 1
        while sh < w:
            sm = sm + pltpu.roll(sm, sh, 0)
            sh *= 2
        mean = sm[HALO:] / jnp.minimum(t1, float(w))
        zs.append(mean - pu[:, g * pg:(g + 1) * pg])
    return zs


def _mix_specs(tm, s):
    per = tm // HALO
    last = s // HALO - 1
    cur = lambda w, j: pl.BlockSpec((tm, w), lambda i: (i, j))
    prev = lambda j: pl.BlockSpec((HALO, 512), lambda i: (jnp.maximum(i * per - 1, 0), j))
    nxt = lambda j: pl.BlockSpec((HALO, 512), lambda i: (jnp.minimum((i + 1) * per, last), j))
    full = lambda shape: pl.BlockSpec(shape, lambda i: (0,) * len(shape))
    return cur, prev, nxt, full


def _mix_fwd(p_main, o, conv_w, pool_w, pool_scale, tm):
    s = p_main.shape[0]
    d = 2048
    pg = 128
    cur, prev, nxt, full = _mix_specs(tm, s)

    def body(ga_ref, cb_ref, cc_ref, ch_ref, gc_ref, pu_ref, gp_ref, cch_ref, chh_ref, puh_ref,
             o_ref, cw_ref, pw_ref, ps_ref, y_ref):
        i = pl.program_id(0)
        first = i == 0
        ga = ga_ref[...].astype(F32)
        y_ref[:, 0:1024] = (o_ref[...].astype(F32) * ga * _sigmoid(ga)).astype(BF16)

        u = cc_ref[...].astype(F32) * ch_ref[...].astype(F32)
        uh = jnp.where(first, 0.0, cch_ref[...].astype(F32) * chh_ref[...].astype(F32))
        conv, _, _ = _conv_fwd(jnp.concatenate([uh, u], axis=0), cw_ref)
        gc = gc_ref[...].astype(F32)
        y_ref[:, 1024:1536] = (cb_ref[...].astype(F32) * conv * gc * _sigmoid(gc)).astype(BF16)

        pu = pu_ref[...].astype(F32)
        puh = jnp.where(first, 0.0, puh_ref[...].astype(F32))
        zs = _pool_z(pu, puh, i * tm, tm, pg)
        gp = gp_ref[...].astype(F32)
        gate = gp * _sigmoid(gp) * ps_ref[...]
        for g in range(4):
            r = jnp.dot(zs[g].astype(BF16), pw_ref[g], preferred_element_type=F32)
            y_ref[:, 1536 + g * pg:1536 + (g + 1) * pg] = (r * gate[:, g * pg:(g + 1) * pg]).astype(BF16)

    return pl.pallas_call(
        body, name="mix_fwd", grid=(s // tm,),
        in_specs=[cur(1024, 3), cur(512, 8), cur(512, 9), cur(512, 10), cur(512, 11), cur(512, 12), cur(512, 13),
                  prev(9), prev(10), prev(12),
                  cur(1024, 0), full((3, 512)), full((4, pg, pg)), full((1, 512))],
        out_specs=pl.BlockSpec((tm, d), lambda i: (i, 0)),
        out_shape=jax.ShapeDtypeStruct((s, d), BF16),
        compiler_params=_params(("parallel",)),
    )(*([p_main] * 10), o, conv_w, pool_w, pool_scale)


def _mix_bwd(p_main, o, dy, conv_w, pool_w, pool_scale, tm):
    s = p_main.shape[0]
    pg = 128
    n_heads = 8
    cur, prev, nxt, full = _mix_specs(tm, s)
    nblk = s // tm
    n_ext = tm + HALO

    def silu_and_grad(x):
        sg = _sigmoid(x)
        return x * sg, sg * (1.0 + x * (1.0 - sg))

    def body(ga_ref, cb_ref, cc_ref, ch_ref, gc_ref, pu_ref, gp_ref, cch_ref, chh_ref, puh_ref,
             cbn_ref, gcn_ref, gpn_ref, o_ref, dy_ref, dycn_ref, dypn_ref, cw_ref, pw_ref, ps_ref,
             do_ref, dl_ref, dp_ref, dsm_ref, dpw_ref):
        i = pl.program_id(0)
        first = i == 0
        last = i == nblk - 1

        @pl.when(first)
        def _():
            dsm_ref[...] = jnp.zeros_like(dsm_ref)
            dpw_ref[...] = jnp.zeros_like(dpw_ref)

        ga = ga_ref[...].astype(F32)
        of = o_ref[...].astype(F32)
        dya = dy_ref[:, 0:1024]
        sa, dsa = silu_and_grad(ga)
        dout = dya * sa
        do_ref[...] = dout.astype(BF16)
        dp_ref[:, 0:1024] = (dya * of * dsa).astype(BF16)
        prod = dout * of
        for h in range(n_heads):
            dsum = jnp.sum(prod[:, h * 128:(h + 1) * 128], axis=1, keepdims=True)
            dl_ref[:, h * 128:(h + 1) * 128] = jnp.broadcast_to(dsum, (tm, 128))

        cb = cb_ref[...].astype(F32)
        cc = cc_ref[...].astype(F32)
        ch = ch_ref[...].astype(F32)
        gc = gc_ref[...].astype(F32)
        u = cc * ch
        uh = jnp.where(first, 0.0, cch_ref[...].astype(F32) * chh_ref[...].astype(F32))
        conv, u1, u2 = _conv_fwd(jnp.concatenate([uh, u], axis=0), cw_ref)
        sc_, dsc = silu_and_grad(gc)
        dyc = dy_ref[:, 1024:1536]
        dp_ref[:, 1024:1536] = (dyc * conv * sc_).astype(BF16)
        dp_ref[:, 2560:3072] = (dyc * cb * conv * dsc).astype(BF16)
        dconv = dyc * cb * sc_
        gcn = gcn_ref[...].astype(F32)
        dconv_n = jnp.where(last, 0.0, dycn_ref[...] * cbn_ref[...].astype(F32) * gcn * _sigmoid(gcn))
        dext = jnp.concatenate([dconv, dconv_n], axis=0)
        du = (cw_ref[2:3, :] * dext + cw_ref[1:2, :] * pltpu.roll(dext, n_ext - 1, 0)
              + cw_ref[0:1, :] * pltpu.roll(dext, n_ext - 2, 0))[:tm]
        dp_ref[:, 1536:2048] = (du * ch).astype(BF16)
        dp_ref[:, 2048:2560] = (du * cc).astype(BF16)
        dsm_ref[0:1, :] += jnp.sum(dconv * u2, axis=0, keepdims=True)
        dsm_ref[1:2, :] += jnp.sum(dconv * u1, axis=0, keepdims=True)
        dsm_ref[2:3, :] += jnp.sum(dconv * u, axis=0, keepdims=True)

        pu = pu_ref[...].astype(F32)
        puh = jnp.where(first, 0.0, puh_ref[...].astype(F32))
        zs = _pool_z(pu, puh, i * tm, tm, pg)
        gp = gp_ref[...].astype(F32)
        sp, dsp = silu_and_grad(gp)
        dyp = dy_ref[:, 1536:2048]
        gpn = gpn_ref[...].astype(F32)
        dr_n = jnp.where(last, 0.0, dypn_ref[...] * gpn * _sigmoid(gpn) * ps_ref[...])
        drs = dyp * sp
        dr = drs * ps_ref[...]
        t1 = (i * tm + lax.broadcasted_iota(jnp.int32, (tm, pg), 0) + 1).astype(F32)
        r_parts, dpu_parts = [], []
        for g, w in enumerate(POOL_WINDOWS):
            cols = slice(g * pg, (g + 1) * pg)
            zb = zs[g].astype(BF16)
            r_parts.append(jnp.dot(zb, pw_ref[g], preferred_element_type=F32))
            drb = dr[:, cols].astype(BF16)
            dpw_ref[g] += lax.dot_general(zb, drb, (_DIMS["tn"], ((), ())), preferred_element_type=F32)
            dz = lax.dot_general(drb, pw_ref[g], (_DIMS["nt"], ((), ())), preferred_element_type=F32)
            dz_n = lax.dot_general(dr_n[:, cols].astype(BF16), pw_ref[g], (_DIMS["nt"], ((), ())),
                                   preferred_element_type=F32)
            sm = jnp.concatenate([dz / jnp.minimum(t1, float(w)), dz_n / float(w)], axis=0)
            sh = 1
            while sh < w:
                sm = sm + pltpu.roll(sm, n_ext - sh, 0)
                sh *= 2
            dpu_parts.append(sm[:tm] - dz)
        r = jnp.concatenate(r_parts, axis=1)
        dp_ref[:, 3072:3584] = jnp.concatenate(dpu_parts, axis=1).astype(BF16)
        dp_ref[:, 3584:4096] = (dyp * r * ps_ref[...] * dsp).astype(BF16)
        dsm_ref[3:4, :] += jnp.sum(drs * r, axis=0, keepdims=True)

    dy_next = lambda j: pl.BlockSpec((HALO, 512), lambda i: (jnp.minimum((i + 1) * (tm // HALO), s // HALO - 1), j))
    return pl.pallas_call(
        body, name="mix_bwd", grid=(nblk,),
        in_specs=[cur(1024, 3), cur(512, 8), cur(512, 9), cur(512, 10), cur(512, 11), cur(512, 12), cur(512, 13),
                  prev(9), prev(10), prev(12), nxt(8), nxt(11), nxt(13),
                  cur(1024, 0), pl.BlockSpec((tm, 2048), lambda i: (i, 0)), dy_next(2), dy_next(3),
                  full((3, 512)), full((4, pg, pg)), full((1, 512))],
        out_specs=(pl.BlockSpec((tm, 1024), lambda i: (i, 0)), pl.BlockSpec((tm, 1024), lambda i: (i, 0)),
                   pl.BlockSpec((tm, 4096), lambda i: (i, 0)), full((8, 512)), full((4, pg, pg))),
        out_shape=(jax.ShapeDtypeStruct((s, 1024), BF16), jax.ShapeDtypeStruct((s, 1024), F32),
                   jax.ShapeDtypeStruct((s, 4096), BF16), jax.ShapeDtypeStruct((8, 512), F32),
                   jax.ShapeDtypeStruct((4, pg, pg), F32)),
        compiler_params=_params(("arbitrary",)),
    )(*([p_main] * 13), o, dy, dy, dy, conv_w, pool_w, pool_scale)


def _outproj_ln(y, w_out, x, ln_g, ln_b, alpha, tm):
    s, d = x.shape

    def body(y_ref, w_ref, x_ref, g_ref, b_ref, xn_ref, xb_ref, pre_ref):
        pre = alpha * x_ref[...] + jnp.dot(y_ref[...], w_ref[...], preferred_element_type=F32)
        mu = jnp.mean(pre, axis=1, keepdims=True)
        cen = pre - mu
        var = jnp.mean(cen * cen, axis=1, keepdims=True)
        xn = cen * lax.rsqrt(var + LN_EPS) * g_ref[...] + b_ref[...]
        pre_ref[...] = pre
        xn_ref[...] = xn
        xb_ref[...] = xn.astype(BF16)

    row = pl.BlockSpec((tm, d), lambda i: (i, 0))
    vec = pl.BlockSpec((1, d), lambda i: (0, 0))
    return pl.pallas_call(
        body, name="outproj_ln", grid=(s // tm,),
        in_specs=[row, pl.BlockSpec((d, d), lambda i: (0, 0)), row, vec, vec],
        out_specs=(row, row, row),
        out_shape=(jax.ShapeDtypeStruct((s, d), F32), jax.ShapeDtypeStruct((s, d), BF16),
                   jax.ShapeDtypeStruct((s, d), F32)),
        compiler_params=_params(("parallel",)),
    )(y, w_out, x, ln_g, ln_b)


def _ln_bwd(dxn, pre, ln_g, tm):
    s, d = pre.shape

    def body(dx_ref, pre_ref, g_ref, dpre_ref, dpb_ref, dgb_ref):
        @pl.when(pl.program_id(0) == 0)
        def _():
            dgb_ref[...] = jnp.zeros_like(dgb_ref)

        pre_ = pre_ref[...]
        dx = dx_ref[...]
        mu = jnp.mean(pre_, axis=1, keepdims=True)
        cen = pre_ - mu
        var = jnp.mean(cen * cen, axis=1, keepdims=True)
        rstd = lax.rsqrt(var + LN_EPS)
        xhat = cen * rstd
        dxh = dx * g_ref[...]
        dpre = rstd * (dxh - jnp.mean(dxh, axis=1, keepdims=True)
                       - xhat * jnp.mean(dxh * xhat, axis=1, keepdims=True))
        dpre_ref[...] = dpre
        dpb_ref[...] = dpre.astype(BF16)
        dgb_ref[0:1, :] += jnp.sum(dx * xhat, axis=0, keepdims=True)
        dgb_ref[1:2, :] += jnp.sum(dx, axis=0, keepdims=True)

    row = pl.BlockSpec((tm, d), lambda i: (i, 0))
    return pl.pallas_call(
        body, name="ln_bwd", grid=(s // tm,),
        in_specs=[row, row, pl.BlockSpec((1, d), lambda i: (0, 0))],
        out_specs=(row, row, pl.BlockSpec((8, d), lambda i: (0, 0))),
        out_shape=(jax.ShapeDtypeStruct((s, d), F32), jax.ShapeDtypeStruct((s, d), BF16),
                   jax.ShapeDtypeStruct((8, d), F32)),
        compiler_params=_params(("arbitrary",)),
    )(dxn, pre, ln_g)


def _loss_grad(y, target, tm):
    s, d = y.shape

    def body(y_ref, t_ref, dy_ref, loss_ref):
        @pl.when(pl.program_id(0) == 0)
        def _():
            loss_ref[...] = jnp.zeros_like(loss_ref)

        diff = y_ref[...] - t_ref[...]
        dy_ref[...] = diff / d
        loss_ref[...] += 0.5 * jnp.sum(jnp.sum(diff * diff, axis=1, keepdims=True) / d)

    row = pl.BlockSpec((tm, d), lambda i: (i, 0))
    return pl.pallas_call(
        body, name="loss_grad", grid=(s // tm,),
        in_specs=[row, row],
        out_specs=(row, pl.BlockSpec((8, 128), lambda i: (0, 0))),
        out_shape=(jax.ShapeDtypeStruct((s, d), F32), jax.ShapeDtypeStruct((8, 128), F32)),
        compiler_params=_params(("arbitrary",)),
    )(y, target)


def _adamw(w, g, m, v, name):
    a, r, c = w.shape
    tr = _tile(r, (256, 128, 64, 32, 16, 8))
    c1 = 1.0 - ADAM_B1 ** ADAM_STEP
    c2 = 1.0 - ADAM_B2 ** ADAM_STEP

    def body(w_ref, g_ref, m_ref, v_ref, d_ref, mo_ref, vo_ref):
        g_ = g_ref[...]
        m_new = ADAM_B1 * m_ref[...] + (1.0 - ADAM_B1) * g_
        v_new = ADAM_B2 * v_ref[...] + (1.0 - ADAM_B2) * (g_ * g_)
        d_ref[...] = -ADAM_LR * ((m_new / c1) / (jnp.sqrt(v_new / c2) + ADAM_EPS) + ADAM_WD * w_ref[...])
        mo_ref[...] = m_new
        vo_ref[...] = v_new

    spec = pl.BlockSpec((1, tr, c), lambda i, j: (i, j, 0))
    shp = jax.ShapeDtypeStruct(w.shape, F32)
    return pl.pallas_call(
        body, name=name, grid=(a, r // tr), in_specs=[spec] * 4, out_specs=(spec,) * 3, out_shape=(shp,) * 3,
        compiler_params=_params(("parallel", "parallel")),
    )(w, g, m, v)


def _place():
    x, y, c = lax.axis_index("x"), lax.axis_index("y"), lax.axis_index("c")
    return x, y, c, [(1 - x, y), (x, 1 - y), (1 - x, 1 - y)]


def _hbm_spec():
    return pl.BlockSpec(memory_space=pltpu.HBM)


def _gather_chip_shards(shard, name):
    _, nl, r, cdim = shard.shape
    half = nl // 2

    def body(s_ref, out_ref, send_sems, recv_sems):
        x, y, c, chips = _place()
        sib = (x, y, 1 - c)

        def slot(px, py, pc):
            return out_ref.at[pl.ds(2 * px + py, 1), pl.ds(pc * half, half)]

        def copy(k, dst, to, src=None):
            return pltpu.make_async_remote_copy(
                src_ref=dst if src is None else src, dst_ref=dst, send_sem=send_sems.at[k],
                recv_sem=recv_sems.at[k], device_id=to, device_id_type=MESH)

        first = [copy(k, slot(x, y, c), (*chip, c), src=s_ref.at[:, pl.ds(c * half, half)])
                 for k, chip in enumerate(chips)]
        for cp in first:
            cp.start()
        passed = [copy(3 + k, slot(*chip, c), sib) for k, chip in enumerate(chips)]
        for k, chip in enumerate(chips):
            copy(k, slot(*chip, c), (x, y, c)).wait_recv()
            passed[k].start()
        for k, chip in enumerate(chips):
            copy(3 + k, slot(*chip, 1 - c), (x, y, c)).wait_recv()
        for cp in first + passed:
            cp.wait_send()

    gathered = pl.pallas_call(
        body, name=name, out_shape=jax.ShapeDtypeStruct((4, nl, r, cdim), shard.dtype),
        in_specs=[_hbm_spec()], out_specs=_hbm_spec(),
        scratch_shapes=[pltpu.SemaphoreType.DMA((6,)), pltpu.SemaphoreType.DMA((6,))],
    )(shard)
    chip = 2 * lax.axis_index("x") + lax.axis_index("y")
    return lax.dynamic_update_slice(gathered, shard, (chip, 0, 0, 0))


def _swap_other_half(g, name):
    nchip, nl, r, cdim = g.shape
    half = nl // 2

    def body(g_ref, out_ref, send_sem, recv_sem):
        x, y, c, _ = _place()
        cp = pltpu.make_async_remote_copy(
            src_ref=g_ref.at[:, pl.ds((1 - c) * half, half)], dst_ref=out_ref, send_sem=send_sem,
            recv_sem=recv_sem, device_id=(x, y, 1 - c), device_id_type=MESH)
        cp.start()
        cp.wait()

    return pl.pallas_call(
        body, name=name, out_shape=jax.ShapeDtypeStruct((nchip, half, r, cdim), g.dtype),
        in_specs=[_hbm_spec()], out_specs=_hbm_spec(),
        scratch_shapes=[pltpu.SemaphoreType.DMA, pltpu.SemaphoreType.DMA],
    )(g)


def _exchange_chip_partials(hsum, name):
    nchip, half, r, cdim = hsum.shape

    def body(h_ref, out_ref, send_sems, recv_sems):
        x, y, c, chips = _place()
        cps = [pltpu.make_async_remote_copy(
            src_ref=h_ref.at[pl.ds(2 * px + py, 1)], dst_ref=out_ref.at[pl.ds(k, 1)], send_sem=send_sems.at[k],
            recv_sem=recv_sems.at[k], device_id=(px, py, c), device_id_type=MESH)
            for k, (px, py) in enumerate(chips)]
        for cp in cps:
            cp.start()
        for cp in cps:
            cp.wait()

    return pl.pallas_call(
        body, name=name, out_shape=jax.ShapeDtypeStruct((3, half, r, cdim), hsum.dtype),
        in_specs=[_hbm_spec()], out_specs=_hbm_spec(),
        scratch_shapes=[pltpu.SemaphoreType.DMA((3,)), pltpu.SemaphoreType.DMA((3,))],
    )(hsum)


def _share_final_half(f, name):
    half, r, cdim = f.shape

    def body(f_ref, out_ref, send_sem, recv_sem):
        x, y, c, _ = _place()
        cp = pltpu.make_async_remote_copy(
            src_ref=f_ref, dst_ref=out_ref, send_sem=send_sem, recv_sem=recv_sem,
            device_id=(x, y, 1 - c), device_id_type=MESH)
        cp.start()
        cp.wait()

    theirs = pl.pallas_call(
        body, name=name, out_shape=jax.ShapeDtypeStruct((half, r, cdim), f.dtype),
        in_specs=[_hbm_spec()], out_specs=_hbm_spec(),
        scratch_shapes=[pltpu.SemaphoreType.DMA, pltpu.SemaphoreType.DMA],
    )(f)
    mine_first = lax.axis_index("c") == 0
    return jnp.concatenate([jnp.where(mine_first, f, theirs), jnp.where(mine_first, theirs, f)], axis=0)


def _add_halves(g, recv, core, name):
    nchip, nl, r, cdim = g.shape
    half = nl // 2
    tr = _tile(r, (256, 128))

    def body(c_ref, g_ref, r_ref, o_ref):
        o_ref[...] = (g_ref[...].astype(F32) + r_ref[...].astype(F32)).astype(BF16)

    blk = (1, 1, tr, cdim)
    return pl.pallas_call(
        body, name=name,
        grid_spec=pltpu.PrefetchScalarGridSpec(
            num_scalar_prefetch=1, grid=(nchip, half, r // tr),
            in_specs=[pl.BlockSpec(blk, lambda j, l, i, c_ref: (j, c_ref[0] * half + l, i, 0)),
                      pl.BlockSpec(blk, lambda j, l, i, c_ref: (j, l, i, 0))],
            out_specs=pl.BlockSpec(blk, lambda j, l, i, c_ref: (j, l, i, 0))),
        out_shape=jax.ShapeDtypeStruct((nchip, half, r, cdim), BF16),
        compiler_params=_params(("parallel", "parallel", "parallel")),
    )(core, g, recv)


def _add_partials(hsum, others, chip, name):
    nchip, half, r, cdim = hsum.shape
    tr = _tile(r, (256, 128))

    def body(i_ref, h_ref, a_ref, b_ref, c_ref, o_ref):
        o_ref[...] = ((h_ref[...].astype(F32) + a_ref[...].astype(F32))
                      + (b_ref[...].astype(F32) + c_ref[...].astype(F32)))[0]

    blk = (1, 1, tr, cdim)
    other = lambda k: pl.BlockSpec(blk, lambda l, i, i_ref: (k, l, i, 0))
    return pl.pallas_call(
        body, name=name,
        grid_spec=pltpu.PrefetchScalarGridSpec(
            num_scalar_prefetch=1, grid=(half, r // tr),
            in_specs=[pl.BlockSpec(blk, lambda l, i, i_ref: (i_ref[0], l, i, 0)), other(0), other(1), other(2)],
            out_specs=pl.BlockSpec((1, tr, cdim), lambda l, i, i_ref: (l, i, 0))),
        out_shape=jax.ShapeDtypeStruct((half, r, cdim), F32),
        compiler_params=_params(("parallel", "parallel")),
    )(chip, hsum, others, others, others)


def _reduce_scatter(g, core, chip, tag):
    recv = _swap_other_half(g, f"rs_swap_{tag}")
    hsum = _add_halves(g, recv, core, f"rs_add_halves_{tag}")
    others = _exchange_chip_partials(hsum, f"rs_exchange_{tag}")
    final_half = _add_partials(hsum, others, chip, f"rs_add_partials_{tag}")
    return _share_final_half(final_half, f"rs_share_{tag}")


def _all_gather_small(v, name):
    m_per, n = v.shape

    def body(x_ref, out_ref, send_sems, recv_sems, local_sem):
        x, y, c, chips = _place()
        me, sib = (x, y, c), (x, y, 1 - c)

        def rows(px, py, pc):
            return out_ref.at[pl.ds((4 * px + 2 * py + pc) * m_per, m_per), :]

        def copy(k, block, to, src=None):
            return pltpu.make_async_remote_copy(
                src_ref=rows(*block) if src is None else src, dst_ref=rows(*block), send_sem=send_sems.at[k],
                recv_sem=recv_sems.at[k], device_id=to, device_id_type=MESH)

        mine = pltpu.make_async_copy(x_ref, rows(*me), local_sem)
        mine.start()
        first = [copy(0, me, sib, src=x_ref)]
        first += [copy(1 + j, me, (*chip, c), src=x_ref) for j, chip in enumerate(chips)]
        for cp in first:
            cp.start()
        passed = [copy(4 + j, (*chip, c), sib) for j, chip in enumerate(chips)]
        for j, chip in enumerate(chips):
            copy(1 + j, (*chip, c), me).wait_recv()
            passed[j].start()
        copy(0, sib, me).wait_recv()
        for j, chip in enumerate(chips):
            copy(4 + j, (*chip, 1 - c), me).wait_recv()
        for cp in first + passed:
            cp.wait_send()
        mine.wait()

    return pl.pallas_call(
        body, name=name, out_shape=jax.ShapeDtypeStruct((8 * m_per, n), v.dtype),
        in_specs=[pl.BlockSpec(memory_space=pltpu.VMEM)], out_specs=pl.BlockSpec(memory_space=pltpu.VMEM),
        scratch_shapes=[pltpu.SemaphoreType.DMA((7,)), pltpu.SemaphoreType.DMA((7,)), pltpu.SemaphoreType.DMA],
        compiler_params=pltpu.CompilerParams(vmem_limit_bytes=VMEM_LIMIT),
    )(v)


def _sum_blocks(v, nblk, name):
    m_per = v.shape[0] // nblk

    def body(v_ref, o_ref):
        acc = v_ref[0:m_per, :]
        for j in range(1, nblk):
            acc = acc + v_ref[j * m_per:(j + 1) * m_per, :]
        o_ref[...] = acc

    return pl.pallas_call(
        body, name=name, out_shape=jax.ShapeDtypeStruct((m_per, 128), F32),
        in_specs=[pl.BlockSpec(memory_space=pltpu.VMEM)], out_specs=pl.BlockSpec(memory_space=pltpu.VMEM),
        compiler_params=pltpu.CompilerParams(vmem_limit_bytes=VMEM_LIMIT),
    )(v)


def _rows128(a, rows=None):
    flat = a.reshape(-1).astype(F32)
    need = -(-flat.shape[0] // 128)
    rows = rows or -(-need // 8) * 8
    return jnp.pad(flat, (0, rows * 128 - flat.shape[0])).reshape(rows, 128)


def kernel(x, w_in, b_f, conv_w, pool_w, pool_scale, w_out, ln_g, ln_b, loss_target, m_w_in, m_b_f, m_conv_w, m_pool_w, m_pool_scale, m_w_out, m_ln_g, m_ln_b, v_w_in, v_b_f, v_conv_w, v_pool_w, v_pool_scale, v_w_out, v_ln_g, v_ln_b):
    depth, d, in_shard = w_in.shape
    s = x.shape[1]
    att = d // 2
    n_heads = att // HEAD_DIM
    cw = d // 4
    pw = d - att - cw
    in_w = 4 * att + n_heads + 4 * cw + 2 * pw
    assert in_shard * 4 == in_w and (d, n_heads, cw, pw) == (2048, 8, 512, 512) and s % 512 == 0
    main_w = in_w - n_heads
    fg0 = 4 * att
    alpha = (2 * depth) ** 0.25
    t_att = 512
    tm_mix = 256

    cx, cy, cc = lax.axis_index("x"), lax.axis_index("y"), lax.axis_index("c")
    chip = 2 * cx + cy
    core_arr = jnp.reshape(cc, (1,)).astype(jnp.int32)
    chip_arr = jnp.reshape(chip, (1,)).astype(jnp.int32)

    g_in = _gather_chip_shards(w_in.astype(BF16)[None], "gather_w_in")
    w_full = jnp.moveaxis(g_in, 0, 2).reshape(depth, d, in_w)
    w_main = jnp.concatenate([w_full[:, :, :fg0], w_full[:, :, fg0 + n_heads:]], axis=2)
    w_fg = jnp.pad(w_full[:, :, fg0:fg0 + n_heads], ((0, 0), (0, 0), (0, 128 - n_heads)))
    g_out = _gather_chip_shards(w_out.astype(BF16)[None], "gather_w_out")
    w_out_full = jnp.moveaxis(g_out, 0, 1).reshape(depth, d, d)
    cw_rows = _rows128(conv_w)
    cw_all = _all_gather_small(cw_rows, "gather_conv_w").reshape(4, 2, cw_rows.shape[0] * 128)
    conv_full = cw_all[:, 0, :depth * 3 * (cw // 4)].reshape(4, depth, 3, cw // 4)
    conv_full = jnp.moveaxis(conv_full, 0, 2).reshape(depth, 3, cw)
    pool_b = pool_w.astype(BF16)

    xf = x[0]
    xb = xf.astype(BF16)
    saved = []
    for l in range(depth):
        p_main = _matmul(xb, w_main[l], "nn", BF16, "proj_main")
        fg = _matmul(xb, w_fg[l], "nn", F32, "proj_fg")
        fg_t = fg[:, :n_heads].T
        b_col = b_f[l].reshape(n_heads, 1)
        cum4 = _gate_fwd(fg_t, b_col).reshape(n_heads, s // t_att, 1, t_att)
        o, lse = _attn_fwd(p_main, cum4, n_heads, t_att)
        y = _mix_fwd(p_main, o, conv_full[l], pool_b[l], pool_scale[l].reshape(1, pw), tm_mix)
        x_new, xb_new, pre = _outproj_ln(y, w_out_full[l], xf, ln_g[l].reshape(1, d), ln_b[l].reshape(1, d),
                                         alpha, tm_mix)
        saved.append((xb, p_main, fg_t, b_col, cum4, o, lse, y, pre))
        xf, xb = x_new, xb_new

    dx, loss_blk = _loss_grad(xf, loss_target[0], tm_mix)

    gw_main, gw_fg, gw_out, g_bf, g_sm, g_pw, g_ln = ([None] * depth for _ in range(7))
    for l in reversed(range(depth)):
        xb, p_main, fg_t, b_col, cum4, o, lse, y, pre = saved[l]
        dpre, dpre_b, g_ln[l] = _ln_bwd(dx, pre, ln_g[l].reshape(1, d), tm_mix)
        dy = _matmul(dpre_b, w_out_full[l], "nt", F32, "d_y")
        gw_out[l] = _matmul(y, dpre_b, "tn", F32, "d_w_out")
        do, delta, dp_rest, g_sm[l], g_pw[l] = _mix_bwd(p_main, o, dy, conv_full[l], pool_b[l],
                                                        pool_scale[l].reshape(1, pw), tm_mix)
        dq, dk, dv, drow, dcol = _attn_bwd(p_main, do, lse, delta, cum4, n_heads, t_att)
        dfg_t, g_bf[l] = _gate_bwd(drow, dcol, fg_t, b_col)
        dfg_b = jnp.pad(dfg_t.T, ((0, 0), (0, 128 - n_heads))).astype(BF16)
        dp_main = jnp.concatenate([dq, dk, dv, dp_rest], axis=1)
        gw_main[l] = _matmul(xb, dp_main, "tn", F32, "d_w_main")
        gw_fg[l] = _matmul(xb, dfg_b, "tn", F32, "d_w_fg")
        dx = _matmul(dp_main, w_main[l], "nt", F32, "d_x_main", add=dpre, add_scale=alpha)
        dx = _matmul(dfg_b, w_fg[l], "nt", F32, "d_x_fg", add=dx)

    gwm = jnp.stack(gw_main)
    gw_in_full = jnp.concatenate([gwm[:, :, :fg0], jnp.stack(gw_fg)[:, :, :n_heads], gwm[:, :, fg0:]], axis=2)
    gin_by_chip = jnp.moveaxis(gw_in_full.astype(BF16).reshape(depth, d, 4, in_shard), 2, 0)
    grad_w_in = _reduce_scatter(gin_by_chip, core_arr, chip_arr, "w_in")
    gout_by_chip = jnp.moveaxis(jnp.stack(gw_out).astype(BF16).reshape(depth, 4, d // 4, d), 1, 0)
    grad_w_out = _reduce_scatter(gout_by_chip, core_arr, chip_arr, "w_out")

    g_sm = jnp.stack(g_sm)
    g_ln = jnp.stack(g_ln)
    parts = [jnp.stack(g_pw), g_sm[:, 0:3, :], g_sm[:, 3, :], g_ln[:, 0, :], g_ln[:, 1, :],
             jnp.stack(g_bf)[:, :, 0], loss_blk[0, 0]]
    packed = jnp.concatenate([_rows128(p) for p in parts], axis=0)
    summed = _sum_blocks(_all_gather_small(packed, "gather_small_grads"), 8, "sum_small_grads")
    outs, off = [], 0
    for p in parts:
        nr = _rows128(p).shape[0]
        outs.append(summed[off:off + nr].reshape(-1)[:p.size].reshape(p.shape))
        off += nr
    grad_pool_w, gconv_full, grad_pool_scale, grad_ln_g, grad_ln_b, grad_b_f, loss = outs
    grad_conv_w = lax.dynamic_slice_in_dim(gconv_full, chip * (cw // 4), cw // 4, axis=2)

    d_w_in, nm_w_in, nv_w_in = _adamw(w_in, grad_w_in, m_w_in, v_w_in, "adamw_w_in")
    d_w_out, nm_w_out, nv_w_out = _adamw(w_out, grad_w_out, m_w_out, v_w_out, "adamw_w_out")
    small = [(b_f, grad_b_f, m_b_f, v_b_f), (conv_w, grad_conv_w, m_conv_w, v_conv_w),
             (pool_w, grad_pool_w, m_pool_w, v_pool_w), (pool_scale, grad_pool_scale, m_pool_scale, v_pool_scale),
             (ln_g, grad_ln_g, m_ln_g, v_ln_g), (ln_b, grad_ln_b, m_ln_b, v_ln_b)]
    pk = [jnp.concatenate([_rows128(t[j]) for t in small], axis=0)[None] for j in range(4)]
    sm_out = _adamw(pk[0], pk[1], pk[2], pk[3], "adamw_small")
    res = {j: [] for j in range(3)}
    off = 0
    for t in small:
        nr = _rows128(t[0]).shape[0]
        for j in range(3):
            res[j].append(sm_out[j][0, off:off + nr].reshape(-1)[:t[0].size].reshape(t[0].shape))
        off += nr
    (d_b_f, d_conv_w, d_pool_w, d_pool_scale, d_ln_g, d_ln_b) = res[0]
    (nm_b_f, nm_conv_w, nm_pool_w, nm_pool_scale, nm_ln_g, nm_ln_b) = res[1]
    (nv_b_f, nv_conv_w, nv_pool_w, nv_pool_scale, nv_ln_g, nv_ln_b) = res[2]

    return (loss, dx[None], grad_w_in, grad_b_f, grad_conv_w, grad_pool_w, grad_pool_scale, grad_w_out,
            grad_ln_g, grad_ln_b,
            d_w_in, d_b_f, d_conv_w, d_pool_w, d_pool_scale, d_w_out, d_ln_g, d_ln_b,
            nm_w_in, nm_b_f, nm_conv_w, nm_pool_w, nm_pool_scale, nm_w_out, nm_ln_g, nm_ln_b,
            nv_w_in, nv_b_f, nv_conv_w, nv_pool_w, nv_pool_scale, nv_w_out, nv_ln_g, nv_ln_b)
```

```python
import functools

import jax
import jax.numpy as jnp
from jax import lax
from jax.experimental import pallas as pl
from jax.experimental.pallas import tpu as pltpu

F32 = jnp.float32
BF16 = jnp.bfloat16
MESH = pl.DeviceIdType.MESH

HEAD_DIM = 128
POOL_WINDOWS = (2, 4, 8, 16)
HALO = 16
LN_EPS = 1e-5
ADAM_LR = 0.001
ADAM_B1 = 0.9
ADAM_B2 = 0.999
ADAM_EPS = 1e-08
ADAM_WD = 0.01
ADAM_STEP = 10
VMEM_LIMIT = 56 * 1024 * 1024


def _params(sem, vmem=VMEM_LIMIT):
    return pltpu.CompilerParams(dimension_semantics=sem, vmem_limit_bytes=vmem)


def _tile(n, prefs):
    for t in prefs:
        if n % t == 0:
            return t
    return n


def _sigmoid(x):
    return 1.0 / (1.0 + jnp.exp(-x))


_DIMS = {"nn": ((1,), (0,)), "nt": ((1,), (1,)), "tn": ((0,), (0,))}


def _matmul(a, b, form, out_dtype, name, add=None, add_scale=1.0, tm=None, tn=None, tk=None):
    if form == "nn":
        (m, k), (_, n) = a.shape, b.shape
    elif form == "nt":
        (m, k), (n, _) = a.shape, b.shape
    else:
        (k, m), (_, n) = a.shape, b.shape
    tm = tm or _tile(m, (1024, 512, 256, 128))
    tn = tn or _tile(n, (1024, 512, 256, 128))
    tk = tk or _tile(k, (2048, 1792, 1024, 512, 256, 128))
    nk = k // tk
    if form == "tn":
        a_spec = pl.BlockSpec((tk, tm), lambda i, j, kk: (kk, i))
    else:
        a_spec = pl.BlockSpec((tm, tk), lambda i, j, kk: (i, kk))
    if form == "nt":
        b_spec = pl.BlockSpec((tn, tk), lambda i, j, kk: (j, kk))
    else:
        b_spec = pl.BlockSpec((tk, tn), lambda i, j, kk: (kk, j))
    o_spec = pl.BlockSpec((tm, tn), lambda i, j, kk: (i, j))
    dims = (_DIMS[form], ((), ()))
    has_add = add is not None

    def finish(r, add_ref, o_ref):
        if has_add:
            r = r + add_scale * add_ref[...]
        o_ref[...] = r.astype(out_dtype)

    def body_one_pass(a_ref, b_ref, *rest):
        add_ref, o_ref = rest if has_add else (None,) + rest
        finish(lax.dot_general(a_ref[...], b_ref[...], dims, preferred_element_type=F32), add_ref, o_ref)

    def body_accumulate(a_ref, b_ref, *rest):
        add_ref, o_ref, acc = rest if has_add else (None,) + rest
        kk = pl.program_id(2)
        part = lax.dot_general(a_ref[...], b_ref[...], dims, preferred_element_type=F32)

        @pl.when(kk == 0)
        def _():
            acc[...] = part

        @pl.when(jnp.logical_and(kk > 0, kk < nk - 1))
        def _():
            acc[...] += part

        @pl.when(kk == nk - 1)
        def _():
            finish(acc[...] + part, add_ref, o_ref)

    in_specs = [a_spec, b_spec] + ([o_spec] if has_add else [])
    args = (a, b) + ((add,) if has_add else ())
    return pl.pallas_call(
        body_one_pass if nk == 1 else body_accumulate, name=name, grid=(m // tm, n // tn, nk),
        in_specs=in_specs, out_specs=o_spec,
        out_shape=jax.ShapeDtypeStruct((m, n), out_dtype),
        scratch_shapes=[] if nk == 1 else [pltpu.VMEM((tm, tn), F32)],
        compiler_params=_params(("parallel", "parallel", "arbitrary")),
    )(*args)


def _tri(n, upper):
    r = lax.broadcasted_iota(jnp.int32, (n, n), 0)
    c = lax.broadcasted_iota(jnp.int32, (n, n), 1)
    return jnp.where((r <= c) if upper else (r >= c), 1.0, 0.0).astype(F32)


def _gate_fwd(fg_t, b_col):
    h, s = fg_t.shape
    nb = s // 128

    def body(fg_ref, b_ref, cum_ref):
        u = _tri(128, True)
        carry = jnp.zeros((h, 128), F32)
        for j in range(nb):
            z = fg_ref[:, j * 128:(j + 1) * 128] + b_ref[...]
            logf = -(jnp.maximum(-z, 0.0) + jnp.log(1.0 + jnp.exp(-jnp.abs(z))))
            c = jnp.dot(logf, u, precision=lax.Precision.HIGHEST, preferred_element_type=F32) + carry
            cum_ref[:, j * 128:(j + 1) * 128] = c
            carry = jnp.broadcast_to(c[:, 127:128], (h, 128))

    return pl.pallas_call(
        body, name="gate_fwd", out_shape=jax.ShapeDtypeStruct((h, s), F32),
        in_specs=[pl.BlockSpec(memory_space=pltpu.VMEM)] * 2,
        out_specs=pl.BlockSpec(memory_space=pltpu.VMEM),
    )(fg_t, b_col)


def _gate_bwd(drow_t, dcol_t, fg_t, b_col):
    h, s = fg_t.shape
    nb = s // 128

    def body(dr_ref, dc_ref, fg_ref, b_ref, dfg_ref, db_ref):
        low = _tri(128, False)
        carry = jnp.zeros((h, 128), F32)
        db = jnp.zeros((h, 128), F32)
        for j in reversed(range(nb)):
            sl = slice(j * 128, (j + 1) * 128)
            r = jnp.dot(dr_ref[:, 0, sl] - dc_ref[:, 0, sl], low, precision=lax.Precision.HIGHEST,
                        preferred_element_type=F32) + carry
            carry = jnp.broadcast_to(r[:, 0:1], (h, 128))
            z = fg_ref[:, sl] + b_ref[...]
            dfg = r / (1.0 + jnp.exp(z))
            dfg_ref[:, sl] = dfg
            db = db + dfg
        db_ref[...] = jnp.broadcast_to(jnp.sum(db, axis=1, keepdims=True), (h, 128))

    return pl.pallas_call(
        body, name="gate_bwd",
        out_shape=(jax.ShapeDtypeStruct((h, s), F32), jax.ShapeDtypeStruct((h, 128), F32)),
        in_specs=[pl.BlockSpec(memory_space=pltpu.VMEM)] * 4,
        out_specs=(pl.BlockSpec(memory_space=pltpu.VMEM),) * 2,
    )(drow_t, dcol_t, fg_t, b_col)


LOG2E = 1.4426950408889634
ATT_ROW_CHAINS = 1


def _attn_fwd(p_main, cum4, n_heads, t):
    s = p_main.shape[0]
    nq = s // t
    scale2 = HEAD_DIM ** -0.5 * LOG2E
    rep = t // 128
    tr = t // ATT_ROW_CHAINS

    def body(q_ref, k_ref, v_ref, c_ref, o_ref, lse_ref, m_s, l_s, acc_s):
        qi = pl.program_id(1)
        m_s[...] = jnp.full_like(m_s, -jnp.inf)
        l_s[...] = jnp.zeros_like(l_s)
        acc_s[...] = jnp.zeros_like(acc_s)

        def block(ki, masked):
            keys = pl.ds(pl.multiple_of(ki * t, t), t)
            k = k_ref[keys, :]
            v = v_ref[keys, :]
            c2 = c_ref[ki] * LOG2E
            for r in range(ATT_ROW_CHAINS):
                rs = slice(r * tr, (r + 1) * tr)
                sc = lax.dot_general(q_ref[rs, :], k, (_DIMS["nt"], ((), ())),
                                     preferred_element_type=F32) * scale2 - c2
                if masked:
                    row = lax.broadcasted_iota(jnp.int32, (tr, t), 0) + r * tr
                    col = lax.broadcasted_iota(jnp.int32, (tr, t), 1)
                    sc = jnp.where(col <= row, sc, -jnp.inf)
                m_prev = m_s[rs, :]
                m_new = jnp.maximum(m_prev, jnp.max(sc, axis=1, keepdims=True))
                alpha = jnp.exp2(m_prev - m_new)
                p = jnp.exp2(sc - jnp.tile(m_new, (1, rep)))
                l_s[rs, :] = alpha * l_s[rs, :] + jnp.sum(p, axis=1, keepdims=True)
                acc_s[rs, :] = alpha * acc_s[rs, :] + jnp.dot(p.astype(BF16), v, preferred_element_type=F32)
                m_s[rs, :] = m_new

        def full_block(ki, carry):
            block(ki, False)
            return carry

        lax.fori_loop(0, qi, full_block, 0)
        block(qi, True)
        o_ref[...] = (acc_s[...] / l_s[...]).astype(BF16)
        lse_ref[...] = m_s[...] + jnp.log(l_s[...]) * LOG2E

    head = lambda off: pl.BlockSpec((s, 128), lambda h, qi: (0, off + h))
    return pl.pallas_call(
        body, name="attn_fwd", grid=(n_heads, nq),
        in_specs=[pl.BlockSpec((t, 128), lambda h, qi: (qi, h)), head(n_heads), head(2 * n_heads),
                  pl.BlockSpec((None, nq, 1, t), lambda h, qi: (h, 0, 0, 0))],
        out_specs=(pl.BlockSpec((t, 128), lambda h, qi: (qi, h)),) * 2,
        out_shape=(jax.ShapeDtypeStruct((s, n_heads * 128), BF16), jax.ShapeDtypeStruct((s, n_heads * 128), F32)),
        scratch_shapes=[pltpu.VMEM((t, 128), F32)] * 3,
        compiler_params=_params(("parallel", "arbitrary")),
    )(p_main, p_main, p_main, cum4)


def _attn_bwd(p_main, do, lse_rep, delta_rep, cum4, n_heads, t):
    s = p_main.shape[0]
    nq = s // t
    scale = HEAD_DIM ** -0.5
    rep = t // 128
    tr = t // ATT_ROW_CHAINS

    def body(q_ref, k_ref, v_ref, do_ref, lse_ref, dl_ref, c_ref,
             dq_ref, dk_ref, dv_ref, drow_ref, dcol_ref, dq_acc, dk_acc, dv_acc):
        kb = pl.program_id(1)

        @pl.when(kb == 0)
        def _():
            dq_acc[...] = jnp.zeros_like(dq_acc)

        dk_acc[...] = jnp.zeros_like(dk_acc)
        dv_acc[...] = jnp.zeros_like(dv_acc)
        k = k_ref[...]
        v = v_ref[...]
        c2 = c_ref[...] * LOG2E
        k_ones = jnp.concatenate([k, jnp.ones((t, 128), BF16)], axis=1)
        q_ones = jnp.ones((tr, 128), BF16)

        def block(qb, masked):
            for r in range(ATT_ROW_CHAINS):
                rows = pl.ds(pl.multiple_of(qb * t + r * tr, tr), tr)
                q = q_ref[rows, :]
                dout = do_ref[rows, :]
                sc = lax.dot_general(q, k, (_DIMS["nt"], ((), ())), preferred_element_type=F32) * (scale * LOG2E) - c2
                if masked:
                    row = lax.broadcasted_iota(jnp.int32, (tr, t), 0) + r * tr
                    col = lax.broadcasted_iota(jnp.int32, (tr, t), 1)
                    sc = jnp.where(col <= row, sc, -jnp.inf)
                p = jnp.exp2(sc - jnp.tile(lse_ref[rows, :], (1, rep)))
                dp = lax.dot_general(dout, v, (_DIMS["nt"], ((), ())), preferred_element_type=F32)
                dsb = (p * (dp - jnp.tile(dl_ref[rows, :], (1, rep)))).astype(BF16)
                dv_acc[...] += lax.dot_general(p.astype(BF16), dout, (_DIMS["tn"], ((), ())),
                                               preferred_element_type=F32)
                dk_acc[...] += lax.dot_general(dsb, jnp.concatenate([q, q_ones], axis=1), (_DIMS["tn"], ((), ())),
                                               preferred_element_type=F32)
                dq_acc[rows, :] += jnp.dot(dsb, k_ones, preferred_element_type=F32)

        def full_block(qb, carry):
            block(qb, False)
            return carry

        block(kb, True)
        diag = pl.ds(pl.multiple_of(kb * t, t), t)
        dq_ref[diag, :] = (dq_acc[diag, 0:128] * scale).astype(BF16)
        drow_ref[:, diag] = dq_acc[diag, 128:256].T[0:8, :]
        lax.fori_loop(kb + 1, nq, full_block, 0)
        dk_ref[...] = (dk_acc[:, 0:128] * scale).astype(BF16)
        dcol_ref[...] = dk_acc[:, 128:256].T[0:8, :]
        dv_ref[...] = dv_acc[...].astype(BF16)

    kside = lambda off: pl.BlockSpec((t, 128), lambda h, kb: (kb, off + h))
    whole = pl.BlockSpec((s, 128), lambda h, kb: (0, h))
    hw = n_heads * 128
    return pl.pallas_call(
        body, name="attn_bwd", grid=(n_heads, nq),
        in_specs=[whole, kside(n_heads), kside(2 * n_heads), whole, whole, whole,
                  pl.BlockSpec((None, None, 1, t), lambda h, kb: (h, kb, 0, 0))],
        out_specs=(whole, kside(0), kside(0), pl.BlockSpec((None, 8, s), lambda h, kb: (h, 0, 0)),
                   pl.BlockSpec((None, 8, t), lambda h, kb: (h, 0, kb))),
        out_shape=(jax.ShapeDtypeStruct((s, hw), BF16), jax.ShapeDtypeStruct((s, hw), BF16),
                   jax.ShapeDtypeStruct((s, hw), BF16), jax.ShapeDtypeStruct((n_heads, 8, s), F32),
                   jax.ShapeDtypeStruct((n_heads, 8, s), F32)),
        scratch_shapes=[pltpu.VMEM((s, 256), F32), pltpu.VMEM((t, 256), F32), pltpu.VMEM((t, 128), F32)],
        compiler_params=_params(("parallel", "arbitrary")),
    )(p_main, p_main, p_main, do, lse_rep, delta_rep, cum4)


def _conv_fwd(u_ext, cw_ref):
    r1 = pltpu.roll(u_ext, 1, 0)
    r2 = pltpu.roll(u_ext, 2, 0)
    conv = cw_ref[2:3, :] * u_ext + cw_ref[1:2, :] * r1 + cw_ref[0:1, :] * r2
    return conv[HALO:], r1[HALO:], r2[HALO:]


def _pool_z(pu, pu_halo, row0, tm, pg):
    ext = jnp.concatenate([pu_halo, pu], axis=0)
    t1 = (row0 + lax.broadcasted_iota(jnp.int32, (tm, pg), 0) + 1).astype(F32)
    zs = []
    for g, w in enumerate(POOL_WINDOWS):
        sm = ext[:, g * pg:(g + 1) * pg]
        sh = 1
        while sh < w:
            sm = sm + pltpu.roll(sm, sh, 0)
            sh *= 2
        mean = sm[HALO:] / jnp.minimum(t1, float(w))
        zs.append(mean - pu[:, g * pg:(g + 1) * pg])
    return zs


def _mix_specs(tm, s):
    per = tm // HALO
    last = s // HALO - 1
    cur = lambda w, j: pl.BlockSpec((tm, w), lambda i: (i, j))
    prev = lambda j: pl.BlockSpec((HALO, 512), lambda i: (jnp.maximum(i * per - 1, 0), j))
    nxt = lambda j: pl.BlockSpec((HALO, 512), lambda i: (jnp.minimum((i + 1) * per, last), j))
    full = lambda shape: pl.BlockSpec(shape, lambda i: (0,) * len(shape))
    return cur, prev, nxt, full


def _mix_fwd(p_main, o, conv_w, pool_w, pool_scale, tm):
    s = p_main.shape[0]
    d = 2048
    pg = 128
    cur, prev, nxt, full = _mix_specs(tm, s)

    def body(ga_ref, cb_ref, cc_ref, ch_ref, gc_ref, pu_ref, gp_ref, cch_ref, chh_ref, puh_ref,
             o_ref, cw_ref, pw_ref, ps_ref, y_ref):
        i = pl.program_id(0)
        first = i == 0
        ga = ga_ref[...].astype(F32)
        y_ref[:, 0:1024] = (o_ref[...].astype(F32) * ga * _sigmoid(ga)).astype(BF16)

        u = cc_ref[...].astype(F32) * ch_ref[...].astype(F32)
        uh = jnp.where(first, 0.0, cch_ref[...].astype(F32) * chh_ref[...].astype(F32))
        conv, _, _ = _conv_fwd(jnp.concatenate([uh, u], axis=0), cw_ref)
        gc = gc_ref[...].astype(F32)
        y_ref[:, 1024:1536] = (cb_ref[...].astype(F32) * conv * gc * _sigmoid(gc)).astype(BF16)

        pu = pu_ref[...].astype(F32)
        puh = jnp.where(first, 0.0, puh_ref[...].astype(F32))
        zs = _pool_z(pu, puh, i * tm, tm, pg)
        gp = gp_ref[...].astype(F32)
        gate = gp * _sigmoid(gp) * ps_ref[...]
        for g in range(4):
            r = jnp.dot(zs[g].astype(BF16), pw_ref[g], preferred_element_type=F32)
            y_ref[:, 1536 + g * pg:1536 + (g + 1) * pg] = (r * gate[:, g * pg:(g + 1) * pg]).astype(BF16)

    return pl.pallas_call(
        body, name="mix_fwd", grid=(s // tm,),
        in_specs=[cur(1024, 3), cur(512, 8), cur(512, 9), cur(512, 10), cur(512, 11), cur(512, 12), cur(512, 13),
                  prev(9), prev(10), prev(12),
                  cur(1024, 0), full((3, 512)), full((4, pg, pg)), full((1, 512))],
        out_specs=pl.BlockSpec((tm, d), lambda i: (i, 0)),
        out_shape=jax.ShapeDtypeStruct((s, d), BF16),
        compiler_params=_params(("parallel",)),
    )(*([p_main] * 10), o, conv_w, pool_w, pool_scale)


def _mix_bwd(p_main, o, dy, conv_w, pool_w, pool_scale, tm):
    s = p_main.shape[0]
    pg = 128
    n_heads = 8
    cur, prev, nxt, full = _mix_specs(tm, s)
    nblk = s // tm
    n_ext = tm + HALO

    def silu_and_grad(x):
        sg = _sigmoid(x)
        return x * sg, sg * (1.0 + x * (1.0 - sg))

    def body(ga_ref, cb_ref, cc_ref, ch_ref, gc_ref, pu_ref, gp_ref, cch_ref, chh_ref, puh_ref,
             cbn_ref, gcn_ref, gpn_ref, o_ref, dy_ref, dycn_ref, dypn_ref, cw_ref, pw_ref, ps_ref,
             do_ref, dl_ref, dp_ref, dsm_ref, dpw_ref):
        i = pl.program_id(0)
        first = i == 0
        last = i == nblk - 1

        @pl.when(first)
        def _():
            dsm_ref[...] = jnp.zeros_like(dsm_ref)
            dpw_ref[...] = jnp.zeros_like(dpw_ref)

        ga = ga_ref[...].astype(F32)
        of = o_ref[...].astype(F32)
        dya = dy_ref[:, 0:1024]
        sa, dsa = silu_and_grad(ga)
        dout = dya * sa
        do_ref[...] = dout.astype(BF16)
        dp_ref[:, 0:1024] = (dya * of * dsa).astype(BF16)
        prod = dout * of
        for h in range(n_heads):
            dsum = jnp.sum(prod[:, h * 128:(h + 1) * 128], axis=1, keepdims=True)
            dl_ref[:, h * 128:(h + 1) * 128] = jnp.broadcast_to(dsum, (tm, 128))

        cb = cb_ref[...].astype(F32)
        cc = cc_ref[...].astype(F32)
        ch = ch_ref[...].astype(F32)
        gc = gc_ref[...].astype(F32)
        u = cc * ch
        uh = jnp.where(first, 0.0, cch_ref[...].astype(F32) * chh_ref[...].astype(F32))
        conv, u1, u2 = _conv_fwd(jnp.concatenate([uh, u], axis=0), cw_ref)
        sc_, dsc = silu_and_grad(gc)
        dyc = dy_ref[:, 1024:1536]
        dp_ref[:, 1024:1536] = (dyc * conv * sc_).astype(BF16)
        dp_ref[:, 2560:3072] = (dyc * cb * conv * dsc).astype(BF16)
        dconv = dyc * cb * sc_
        gcn = gcn_ref[...].astype(F32)
        dconv_n = jnp.where(last, 0.0, dycn_ref[...] * cbn_ref[...].astype(F32) * gcn * _sigmoid(gcn))
        dext = jnp.concatenate([dconv, dconv_n], axis=0)
        du = (cw_ref[2:3, :] * dext + cw_ref[1:2, :] * pltpu.roll(dext, n_ext - 1, 0)
              + cw_ref[0:1, :] * pltpu.roll(dext, n_ext - 2, 0))[:tm]
        dp_ref[:, 1536:2048] = (du * ch).astype(BF16)
        dp_ref[:, 2048:2560] = (du * cc).astype(BF16)
        dsm_ref[0:1, :] += jnp.sum(dconv * u2, axis=0, keepdims=True)
        dsm_ref[1:2, :] += jnp.sum(dconv * u1, axis=0, keepdims=True)
        dsm_ref[2:3, :] += jnp.sum(dconv * u, axis=0, keepdims=True)

        pu = pu_ref[...].astype(F32)
        puh = jnp.where(first, 0.0, puh_ref[...].astype(F32))
        zs = _pool_z(pu, puh, i * tm, tm, pg)
        gp = gp_ref[...].astype(F32)
        sp, dsp = silu_and_grad(gp)
        dyp = dy_ref[:, 1536:2048]
        gpn = gpn_ref[...].astype(F32)
        dr_n = jnp.where(last, 0.0, dypn_ref[...] * gpn * _sigmoid(gpn) * ps_ref[...])
        drs = dyp * sp
        dr = drs * ps_ref[...]
        t1 = (i * tm + lax.broadcasted_iota(jnp.int32, (tm, pg), 0) + 1).astype(F32)
        r_parts, dpu_parts = [], []
        for g, w in enumerate(POOL_WINDOWS):
            cols = slice(g * pg, (g + 1) * pg)
            zb = zs[g].astype(BF16)
            r_parts.append(jnp.dot(zb, pw_ref[g], preferred_element_type=F32))
            drb = dr[:, cols].astype(BF16)
            dpw_ref[g] += lax.dot_general(zb, drb, (_DIMS["tn"], ((), ())), preferred_element_type=F32)
            dz = lax.dot_general(drb, pw_ref[g], (_DIMS["nt"], ((), ())), preferred_element_type=F32)
            dz_n = lax.dot_general(dr_n[:, cols].astype(BF16), pw_ref[g], (_DIMS["nt"], ((), ())),
                                   preferred_element_type=F32)
            sm = jnp.concatenate([dz / jnp.minimum(t1, float(w)), dz_n / float(w)], axis=0)
            sh = 1
            while sh < w:
                sm = sm + pltpu.roll(sm, n_ext - sh, 0)
                sh *= 2
            dpu_parts.append(sm[:tm] - dz)
        r = jnp.concatenate(r_parts, axis=1)
        dp_ref[:, 3072:3584] = jnp.concatenate(dpu_parts, axis=1).astype(BF16)
        dp_ref[:, 3584:4096] = (dyp * r * ps_ref[...] * dsp).astype(BF16)
        dsm_ref[3:4, :] += jnp.sum(drs * r, axis=0, keepdims=True)

    dy_next = lambda j: pl.BlockSpec((HALO, 512), lambda i: (jnp.minimum((i + 1) * (tm // HALO), s // HALO - 1), j))
    return pl.pallas_call(
        body, name="mix_bwd", grid=(nblk,),
        in_specs=[cur(1024, 3), cur(512, 8), cur(512, 9), cur(512, 10), cur(512, 11), cur(512, 12), cur(512, 13),
                  prev(9), prev(10), prev(12), nxt(8), nxt(11), nxt(13),
                  cur(1024, 0), pl.BlockSpec((tm, 2048), lambda i: (i, 0)), dy_next(2), dy_next(3),
                  full((3, 512)), full((4, pg, pg)), full((1, 512))],
        out_specs=(pl.BlockSpec((tm, 1024), lambda i: (i, 0)), pl.BlockSpec((tm, 1024), lambda i: (i, 0)),
                   pl.BlockSpec((tm, 4096), lambda i: (i, 0)), full((8, 512)), full((4, pg, pg))),
        out_shape=(jax.ShapeDtypeStruct((s, 1024), BF16), jax.ShapeDtypeStruct((s, 1024), F32),
                   jax.ShapeDtypeStruct((s, 4096), BF16), jax.ShapeDtypeStruct((8, 512), F32),
                   jax.ShapeDtypeStruct((4, pg, pg), F32)),
        compiler_params=_params(("arbitrary",)),
    )(*([p_main] * 13), o, dy, dy, dy, conv_w, pool_w, pool_scale)


def _outproj_ln(y, w_out, x, ln_g, ln_b, alpha, tm):
    s, d = x.shape

    def body(y_ref, w_ref, x_ref, g_ref, b_ref, xn_ref, xb_ref, pre_ref):
        pre = alpha * x_ref[...] + jnp.dot(y_ref[...], w_ref[...], preferred_element_type=F32)
        mu = jnp.mean(pre, axis=1, keepdims=True)
        cen = pre - mu
        var = jnp.mean(cen * cen, axis=1, keepdims=True)
        xn = cen * lax.rsqrt(var + LN_EPS) * g_ref[...] + b_ref[...]
        pre_ref[...] = pre
        xn_ref[...] = xn
        xb_ref[...] = xn.astype(BF16)

    row = pl.BlockSpec((tm, d), lambda i: (i, 0))
    vec = pl.BlockSpec((1, d), lambda i: (0, 0))
    return pl.pallas_call(
        body, name="outproj_ln", grid=(s // tm,),
        in_specs=[row, pl.BlockSpec((d, d), lambda i: (0, 0)), row, vec, vec],
        out_specs=(row, row, row),
        out_shape=(jax.ShapeDtypeStruct((s, d), F32), jax.ShapeDtypeStruct((s, d), BF16),
                   jax.ShapeDtypeStruct((s, d), F32)),
        compiler_params=_params(("parallel",)),
    )(y, w_out, x, ln_g, ln_b)


def _ln_bwd(dxn, pre, ln_g, tm):
    s, d = pre.shape

    def body(dx_ref, pre_ref, g_ref, dpre_ref, dpb_ref, dgb_ref):
        @pl.when(pl.program_id(0) == 0)
        def _():
            dgb_ref[...] = jnp.zeros_like(dgb_ref)

        pre_ = pre_ref[...]
        dx = dx_ref[...]
        mu = jnp.mean(pre_, axis=1, keepdims=True)
        cen = pre_ - mu
        var = jnp.mean(cen * cen, axis=1, keepdims=True)
        rstd = lax.rsqrt(var + LN_EPS)
        xhat = cen * rstd
        dxh = dx * g_ref[...]
        dpre = rstd * (dxh - jnp.mean(dxh, axis=1, keepdims=True)
                       - xhat * jnp.mean(dxh * xhat, axis=1, keepdims=True))
        dpre_ref[...] = dpre
        dpb_ref[...] = dpre.astype(BF16)
        dgb_ref[0:1, :] += jnp.sum(dx * xhat, axis=0, keepdims=True)
        dgb_ref[1:2, :] += jnp.sum(dx, axis=0, keepdims=True)

    row = pl.BlockSpec((tm, d), lambda i: (i, 0))
    return pl.pallas_call(
        body, name="ln_bwd", grid=(s // tm,),
        in_specs=[row, row, pl.BlockSpec((1, d), lambda i: (0, 0))],
        out_specs=(row, row, pl.BlockSpec((8, d), lambda i: (0, 0))),
        out_shape=(jax.ShapeDtypeStruct((s, d), F32), jax.ShapeDtypeStruct((s, d), BF16),
                   jax.ShapeDtypeStruct((8, d), F32)),
        compiler_params=_params(("arbitrary",)),
    )(dxn, pre, ln_g)


def _loss_grad(y, target, tm):
    s, d = y.shape

    def body(y_ref, t_ref, dy_ref, loss_ref):
        @pl.when(pl.program_id(0) == 0)
        def _():
            loss_ref[...] = jnp.zeros_like(loss_ref)

        diff = y_ref[...] - t_ref[...]
        dy_ref[...] = diff / d
        loss_ref[...] += 0.5 * jnp.sum(jnp.sum(diff * diff, axis=1, keepdims=True) / d)

    row = pl.BlockSpec((tm, d), lambda i: (i, 0))
    return pl.pallas_call(
        body, name="loss_grad", grid=(s // tm,),
        in_specs=[row, row],
        out_specs=(row, pl.BlockSpec((8, 128), lambda i: (0, 0))),
        out_shape=(jax.ShapeDtypeStruct((s, d), F32), jax.ShapeDtypeStruct((8, 128), F32)),
        compiler_params=_params(("arbitrary",)),
    )(y, target)


def _adamw(w, g, m, v, name, lead_block=None):
    a, r, c = w.shape
    tr = r if lead_block else _tile(r, (256, 128, 64, 32, 16, 8))
    c1 = 1.0 - ADAM_B1 ** ADAM_STEP
    c2 = 1.0 - ADAM_B2 ** ADAM_STEP

    def body(w_ref, g_ref, m_ref, v_ref, d_ref, mo_ref, vo_ref):
        g_ = g_ref[...]
        m_new = ADAM_B1 * m_ref[...] + (1.0 - ADAM_B1) * g_
        v_new = ADAM_B2 * v_ref[...] + (1.0 - ADAM_B2) * (g_ * g_)
        d_ref[...] = -ADAM_LR * ((m_new / c1) / (jnp.sqrt(v_new / c2) + ADAM_EPS) + ADAM_WD * w_ref[...])
        mo_ref[...] = m_new
        vo_ref[...] = v_new

    ta = lead_block or 1
    spec = pl.BlockSpec((ta, tr, c), lambda i, j: (i, j, 0))
    shp = jax.ShapeDtypeStruct(w.shape, F32)
    return pl.pallas_call(
        body, name=name, grid=(a // ta, r // tr), in_specs=[spec] * 4, out_specs=(spec,) * 3, out_shape=(shp,) * 3,
        compiler_params=_params(("parallel", "parallel")),
    )(w, g, m, v)


def _place():
    x, y, c = lax.axis_index("x"), lax.axis_index("y"), lax.axis_index("c")
    return x, y, c, [(1 - x, y), (x, 1 - y), (1 - x, 1 - y)]


def _hbm_spec():
    return pl.BlockSpec(memory_space=pltpu.HBM)


def _gather_chip_shards(shard, name):
    _, nl, r, cdim = shard.shape
    half = nl // 2

    def body(s_ref, out_ref, send_sems, recv_sems):
        x, y, c, chips = _place()
        sib = (x, y, 1 - c)

        def slot(px, py, pc):
            return out_ref.at[pl.ds(2 * px + py, 1), pl.ds(pc * half, half)]

        def copy(k, dst, to, src=None):
            return pltpu.make_async_remote_copy(
                src_ref=dst if src is None else src, dst_ref=dst, send_sem=send_sems.at[k],
                recv_sem=recv_sems.at[k], device_id=to, device_id_type=MESH)

        first = [copy(k, slot(x, y, c), (*chip, c), src=s_ref.at[:, pl.ds(c * half, half)])
                 for k, chip in enumerate(chips)]
        for cp in first:
            cp.start()
        passed = [copy(3 + k, slot(*chip, c), sib) for k, chip in enumerate(chips)]
        for k, chip in enumerate(chips):
            copy(k, slot(*chip, c), (x, y, c)).wait_recv()
            passed[k].start()
        for k, chip in enumerate(chips):
            copy(3 + k, slot(*chip, 1 - c), (x, y, c)).wait_recv()
        for cp in first + passed:
            cp.wait_send()

    gathered = pl.pallas_call(
        body, name=name, out_shape=jax.ShapeDtypeStruct((4, nl, r, cdim), shard.dtype),
        in_specs=[_hbm_spec()], out_specs=_hbm_spec(),
        scratch_shapes=[pltpu.SemaphoreType.DMA((6,)), pltpu.SemaphoreType.DMA((6,))],
    )(shard)
    chip = 2 * lax.axis_index("x") + lax.axis_index("y")
    return lax.dynamic_update_slice(gathered, shard, (chip, 0, 0, 0))


def _swap_other_half(g, name):
    nchip, nl, r, cdim = g.shape
    half = nl // 2

    def body(g_ref, out_ref, send_sem, recv_sem):
        x, y, c, _ = _place()
        cp = pltpu.make_async_remote_copy(
            src_ref=g_ref.at[:, pl.ds((1 - c) * half, half)], dst_ref=out_ref, send_sem=send_sem,
            recv_sem=recv_sem, device_id=(x, y, 1 - c), device_id_type=MESH)
        cp.start()
        cp.wait()

    return pl.pallas_call(
        body, name=name, out_shape=jax.ShapeDtypeStruct((nchip, half, r, cdim), g.dtype),
        in_specs=[_hbm_spec()], out_specs=_hbm_spec(),
        scratch_shapes=[pltpu.SemaphoreType.DMA, pltpu.SemaphoreType.DMA],
    )(g)


def _exchange_chip_partials(hsum, name):
    nchip, half, r, cdim = hsum.shape

    def body(h_ref, out_ref, send_sems, recv_sems):
        x, y, c, chips = _place()
        cps = [pltpu.make_async_remote_copy(
            src_ref=h_ref.at[pl.ds(2 * px + py, 1)], dst_ref=out_ref.at[pl.ds(k, 1)], send_sem=send_sems.at[k],
            recv_sem=recv_sems.at[k], device_id=(px, py, c), device_id_type=MESH)
            for k, (px, py) in enumerate(chips)]
        for cp in cps:
            cp.start()
        for cp in cps:
            cp.wait()

    return pl.pallas_call(
        body, name=name, out_shape=jax.ShapeDtypeStruct((3, half, r, cdim), hsum.dtype),
        in_specs=[_hbm_spec()], out_specs=_hbm_spec(),
        scratch_shapes=[pltpu.SemaphoreType.DMA((3,)), pltpu.SemaphoreType.DMA((3,))],
    )(hsum)


def _share_final_half(f, name):
    half, r, cdim = f.shape

    def body(f_ref, out_ref, send_sem, recv_sem):
        x, y, c, _ = _place()
        cp = pltpu.make_async_remote_copy(
            src_ref=f_ref, dst_ref=out_ref, send_sem=send_sem, recv_sem=recv_sem,
            device_id=(x, y, 1 - c), device_id_type=MESH)
        cp.start()
        cp.wait()

    theirs = pl.pallas_call(
        body, name=name, out_shape=jax.ShapeDtypeStruct((half, r, cdim), f.dtype),
        in_specs=[_hbm_spec()], out_specs=_hbm_spec(),
        scratch_shapes=[pltpu.SemaphoreType.DMA, pltpu.SemaphoreType.DMA],
    )(f)
    mine_first = lax.axis_index("c") == 0
    return jnp.concatenate([jnp.where(mine_first, f, theirs), jnp.where(mine_first, theirs, f)], axis=0)


def _plane_tile(r, cdim):
    if r % 8 == 0:
        return _tile(r, (256, 128, 64, 32, 16, 8)), cdim
    return r, _tile(cdim, (256, 128))


def _add_halves(g, recv, core, name):
    nchip, nl, r, cdim = g.shape
    half = nl // 2
    tr, tc = _plane_tile(r, cdim)

    def body(c_ref, g_ref, r_ref, o_ref):
        o_ref[...] = (g_ref[...].astype(F32) + r_ref[...].astype(F32)).astype(BF16)

    blk = (1, 1, tr, tc)
    return pl.pallas_call(
        body, name=name,
        grid_spec=pltpu.PrefetchScalarGridSpec(
            num_scalar_prefetch=1, grid=(nchip, half, r // tr, cdim // tc),
            in_specs=[pl.BlockSpec(blk, lambda j, l, i, k, c_ref: (j, c_ref[0] * half + l, i, k)),
                      pl.BlockSpec(blk, lambda j, l, i, k, c_ref: (j, l, i, k))],
            out_specs=pl.BlockSpec(blk, lambda j, l, i, k, c_ref: (j, l, i, k))),
        out_shape=jax.ShapeDtypeStruct((nchip, half, r, cdim), BF16),
        compiler_params=_params(("parallel",) * 4),
    )(core, g, recv)


def _add_partials(hsum, others, chip, name):
    nchip, half, r, cdim = hsum.shape
    tr, tc = _plane_tile(r, cdim)

    def body(i_ref, h_ref, a_ref, b_ref, c_ref, o_ref):
        o_ref[...] = ((h_ref[...].astype(F32) + a_ref[...].astype(F32))
                      + (b_ref[...].astype(F32) + c_ref[...].astype(F32)))[0]

    blk = (1, 1, tr, tc)
    other = lambda n: pl.BlockSpec(blk, lambda l, i, k, i_ref: (n, l, i, k))
    return pl.pallas_call(
        body, name=name,
        grid_spec=pltpu.PrefetchScalarGridSpec(
            num_scalar_prefetch=1, grid=(half, r // tr, cdim // tc),
            in_specs=[pl.BlockSpec(blk, lambda l, i, k, i_ref: (i_ref[0], l, i, k)), other(0), other(1), other(2)],
            out_specs=pl.BlockSpec((1, tr, tc), lambda l, i, k, i_ref: (l, i, k))),
        out_shape=jax.ShapeDtypeStruct((half, r, cdim), F32),
        compiler_params=_params(("parallel",) * 3),
    )(chip, hsum, others, others, others)


def _reduce_scatter(g, core, chip, tag):
    recv = _swap_other_half(g, f"rs_swap_{tag}")
    hsum = _add_halves(g, recv, core, f"rs_add_halves_{tag}")
    others = _exchange_chip_partials(hsum, f"rs_exchange_{tag}")
    final_half = _add_partials(hsum, others, chip, f"rs_add_partials_{tag}")
    return _share_final_half(final_half, f"rs_share_{tag}")


def _all_gather_small(v, name):
    m_per, n = v.shape

    def body(x_ref, out_ref, send_sems, recv_sems, local_sem):
        x, y, c, chips = _place()
        me, sib = (x, y, c), (x, y, 1 - c)

        def rows(px, py, pc):
            return out_ref.at[pl.ds((4 * px + 2 * py + pc) * m_per, m_per), :]

        def copy(k, block, to, src=None):
            return pltpu.make_async_remote_copy(
                src_ref=rows(*block) if src is None else src, dst_ref=rows(*block), send_sem=send_sems.at[k],
                recv_sem=recv_sems.at[k], device_id=to, device_id_type=MESH)

        mine = pltpu.make_async_copy(x_ref, rows(*me), local_sem)
        mine.start()
        first = [copy(0, me, sib, src=x_ref)]
        first += [copy(1 + j, me, (*chip, c), src=x_ref) for j, chip in enumerate(chips)]
        for cp in first:
            cp.start()
        passed = [copy(4 + j, (*chip, c), sib) for j, chip in enumerate(chips)]
        for j, chip in enumerate(chips):
            copy(1 + j, (*chip, c), me).wait_recv()
            passed[j].start()
        copy(0, sib, me).wait_recv()
        for j, chip in enumerate(chips):
            copy(4 + j, (*chip, 1 - c), me).wait_recv()
        for cp in first + passed:
            cp.wait_send()
        mine.wait()

    return pl.pallas_call(
        body, name=name, out_shape=jax.ShapeDtypeStruct((8 * m_per, n), v.dtype),
        in_specs=[pl.BlockSpec(memory_space=pltpu.VMEM)], out_specs=pl.BlockSpec(memory_space=pltpu.VMEM),
        scratch_shapes=[pltpu.SemaphoreType.DMA((7,)), pltpu.SemaphoreType.DMA((7,)), pltpu.SemaphoreType.DMA],
        compiler_params=pltpu.CompilerParams(vmem_limit_bytes=VMEM_LIMIT),
    )(v)


def _sum_blocks(v, nblk, name):
    m_per = v.shape[0] // nblk

    def body(v_ref, o_ref):
        acc = v_ref[0:m_per, :]
        for j in range(1, nblk):
            acc = acc + v_ref[j * m_per:(j + 1) * m_per, :]
        o_ref[...] = acc

    return pl.pallas_call(
        body, name=name, out_shape=jax.ShapeDtypeStruct((m_per, 128), F32),
        in_specs=[pl.BlockSpec(memory_space=pltpu.VMEM)], out_specs=pl.BlockSpec(memory_space=pltpu.VMEM),
        compiler_params=pltpu.CompilerParams(vmem_limit_bytes=VMEM_LIMIT),
    )(v)


def _rows128(a, rows=None):
    flat = a.reshape(-1).astype(F32)
    need = -(-flat.shape[0] // 128)
    rows = rows or -(-need // 8) * 8
    return jnp.pad(flat, (0, rows * 128 - flat.shape[0])).reshape(rows, 128)


def kernel(x, w_in, b_f, conv_w, pool_w, pool_scale, w_out, ln_g, ln_b, loss_target, m_w_in, m_b_f, m_conv_w, m_pool_w, m_pool_scale, m_w_out, m_ln_g, m_ln_b, v_w_in, v_b_f, v_conv_w, v_pool_w, v_pool_scale, v_w_out, v_ln_g, v_ln_b):
    depth, d, in_shard = w_in.shape
    s = x.shape[1]
    att = d // 2
    n_heads = att // HEAD_DIM
    cw = d // 4
    pw = d - att - cw
    in_w = 4 * att + n_heads + 4 * cw + 2 * pw
    assert in_shard * 4 == in_w and (d, n_heads, cw, pw) == (2048, 8, 512, 512) and s % 512 == 0
    main_w = in_w - n_heads
    fg0 = 4 * att
    alpha = (2 * depth) ** 0.25
    t_att = 512
    tm_mix = 256

    cx, cy, cc = lax.axis_index("x"), lax.axis_index("y"), lax.axis_index("c")
    chip = 2 * cx + cy
    core_arr = jnp.reshape(cc, (1,)).astype(jnp.int32)
    chip_arr = jnp.reshape(chip, (1,)).astype(jnp.int32)

    to_rows = lambda a: jnp.transpose(a, (2, 0, 1))
    from_rows = lambda a: jnp.transpose(a, (1, 2, 0))
    shard_b = jnp.transpose(to_rows(w_in).astype(BF16), (1, 0, 2))
    g_in = _gather_chip_shards(shard_b[None], "gather_w_in")
    wt_full = jnp.moveaxis(g_in, 0, 1).reshape(depth, in_w, d)
    wt_main = jnp.concatenate([wt_full[:, :fg0], wt_full[:, fg0 + n_heads:]], axis=1)
    wt_fg = jnp.pad(wt_full[:, fg0:fg0 + n_heads], ((0, 0), (0, 128 - n_heads), (0, 0)))
    g_out = _gather_chip_shards(w_out.astype(BF16)[None], "gather_w_out")
    w_out_full = jnp.moveaxis(g_out, 0, 1).reshape(depth, d, d)
    cw_rows = _rows128(conv_w)
    cw_all = _all_gather_small(cw_rows, "gather_conv_w").reshape(4, 2, cw_rows.shape[0] * 128)
    conv_full = cw_all[:, 0, :depth * 3 * (cw // 4)].reshape(4, depth, 3, cw // 4)
    conv_full = jnp.moveaxis(conv_full, 0, 2).reshape(depth, 3, cw)
    pool_b = pool_w.astype(BF16)

    xf = x[0]
    xb = xf.astype(BF16)
    saved = []
    for l in range(depth):
        p_main = _matmul(xb, wt_main[l], "nt", BF16, "proj_main")
        fg = _matmul(xb, wt_fg[l], "nt", F32, "proj_fg")
        fg_t = fg[:, :n_heads].T
        b_col = b_f[l].reshape(n_heads, 1)
        cum4 = _gate_fwd(fg_t, b_col).reshape(n_heads, s // t_att, 1, t_att)
        o, lse = _attn_fwd(p_main, cum4, n_heads, t_att)
        y = _mix_fwd(p_main, o, conv_full[l], pool_b[l], pool_scale[l].reshape(1, pw), tm_mix)
        x_new, xb_new, pre = _outproj_ln(y, w_out_full[l], xf, ln_g[l].reshape(1, d), ln_b[l].reshape(1, d),
                                         alpha, tm_mix)
        saved.append((xb, p_main, fg_t, b_col, cum4, o, lse, y, pre))
        xf, xb = x_new, xb_new

    dx, loss_blk = _loss_grad(xf, loss_target[0], tm_mix)

    gw_main, gw_fg, gw_out, g_bf, g_sm, g_pw, g_ln = ([None] * depth for _ in range(7))
    for l in reversed(range(depth)):
        xb, p_main, fg_t, b_col, cum4, o, lse, y, pre = saved[l]
        dpre, dpre_b, g_ln[l] = _ln_bwd(dx, pre, ln_g[l].reshape(1, d), tm_mix)
        dy = _matmul(dpre_b, w_out_full[l], "nt", F32, "d_y")
        gw_out[l] = _matmul(y, dpre_b, "tn", F32, "d_w_out")
        do, delta, dp_rest, g_sm[l], g_pw[l] = _mix_bwd(p_main, o, dy, conv_full[l], pool_b[l],
                                                        pool_scale[l].reshape(1, pw), tm_mix)
        dq, dk, dv, drow, dcol = _attn_bwd(p_main, do, lse, delta, cum4, n_heads, t_att)
        dfg_t, g_bf[l] = _gate_bwd(drow, dcol, fg_t, b_col)
        dfg_b = jnp.pad(dfg_t.T, ((0, 0), (0, 128 - n_heads))).astype(BF16)
        dp_main = jnp.concatenate([dq, dk, dv, dp_rest], axis=1)
        gw_main[l] = _matmul(dp_main, xb, "tn", F32, "d_w_main")
        gw_fg[l] = _matmul(dfg_b, xb, "tn", F32, "d_w_fg")
        dx = _matmul(dp_main, wt_main[l], "nn", F32, "d_x_main", add=dpre, add_scale=alpha)
        dx = _matmul(dfg_b, wt_fg[l], "nn", F32, "d_x_fg", add=dx)

    gwm = jnp.stack(gw_main)
    gw_in_full = jnp.concatenate([gwm[:, :fg0], jnp.stack(gw_fg)[:, :n_heads], gwm[:, fg0:]], axis=1)
    gin_by_chip = jnp.moveaxis(gw_in_full.astype(BF16).reshape(depth, 4, in_shard, d), 1, 0)
    grad_rows = jnp.transpose(_reduce_scatter(gin_by_chip, core_arr, chip_arr, "w_in"), (1, 0, 2))
    grad_w_in = from_rows(grad_rows)
    gout_by_chip = jnp.moveaxis(jnp.stack(gw_out).astype(BF16).reshape(depth, 4, d // 4, d), 1, 0)
    grad_w_out = _reduce_scatter(gout_by_chip, core_arr, chip_arr, "w_out")

    g_sm = jnp.stack(g_sm)
    g_ln = jnp.stack(g_ln)
    parts = [jnp.stack(g_pw), g_sm[:, 0:3, :], g_sm[:, 3, :], g_ln[:, 0, :], g_ln[:, 1, :],
             jnp.stack(g_bf)[:, :, 0], loss_blk[0, 0]]
    packed = jnp.concatenate([_rows128(p) for p in parts], axis=0)
    summed = _sum_blocks(_all_gather_small(packed, "gather_small_grads"), 8, "sum_small_grads")
    outs, off = [], 0
    for p in parts:
        nr = _rows128(p).shape[0]
        outs.append(summed[off:off + nr].reshape(-1)[:p.size].reshape(p.shape))
        off += nr
    grad_pool_w, gconv_full, grad_pool_scale, grad_ln_g, grad_ln_b, grad_b_f, loss = outs
    grad_conv_w = lax.dynamic_slice_in_dim(gconv_full, chip * (cw // 4), cw // 4, axis=2)

    d_w_in, nm_w_in, nv_w_in = (from_rows(a) for a in _adamw(
        to_rows(w_in), grad_rows, to_rows(m_w_in), to_rows(v_w_in), "adamw_w_in", lead_block=46))
    d_w_out, nm_w_out, nv_w_out = _adamw(w_out, grad_w_out, m_w_out, v_w_out, "adamw_w_out")
    small = [(b_f, grad_b_f, m_b_f, v_b_f), (conv_w, grad_conv_w, m_conv_w, v_conv_w),
             (pool_w, grad_pool_w, m_pool_w, v_pool_w), (pool_scale, grad_pool_scale, m_pool_scale, v_pool_scale),
             (ln_g, grad_ln_g, m_ln_g, v_ln_g), (ln_b, grad_ln_b, m_ln_b, v_ln_b)]
    pk = [jnp.concatenate([_rows128(t[j]) for t in small], axis=0)[None] for j in range(4)]
    sm_out = _adamw(pk[0], pk[1], pk[2], pk[3], "adamw_small")
    res = {j: [] for j in range(3)}
    off = 0
    for t in small:
        nr = _rows128(t[0]).shape[0]
        for j in range(3):
            res[j].append(sm_out[j][0, off:off + nr].reshape(-1)[:t[0].size].reshape(t[0].shape))
        off += nr
    (d_b_f, d_conv_w, d_pool_w, d_pool_scale, d_ln_g, d_ln_b) = res[0]
    (nm_b_f, nm_conv_w, nm_pool_w, nm_pool_scale, nm_ln_g, nm_ln_b) = res[1]
    (nv_b_f, nv_conv_w, nv_pool_w, nv_pool_scale, nv_ln_g, nv_ln_b) = res[2]

    return (loss, dx[None], grad_w_in, grad_b_f, grad_conv_w, grad_pool_w, grad_pool_scale, grad_w_out,
            grad_ln_g, grad_ln_b,
            d_w_in, d_b_f, d_conv_w, d_pool_w, d_pool_scale, d_w_out, d_ln_g, d_ln_b,
            nm_w_in, nm_b_f, nm_conv_w, nm_pool_w, nm_pool_scale, nm_w_out, nm_ln_g, nm_ln_b,
            nv_w_in, nv_b_f, nv_conv_w, nv_pool_w, nv_pool_scale, nv_w_out, nv_ln_g, nv_ln_b)
```

```python
import functools

import jax
import jax.numpy as jnp
from jax import lax
from jax.experimental import pallas as pl
from jax.experimental.pallas import tpu as pltpu

F32 = jnp.float32
BF16 = jnp.bfloat16
MESH = pl.DeviceIdType.MESH

HEAD_DIM = 128
POOL_WINDOWS = (2, 4, 8, 16)
HALO = 16
LN_EPS = 1e-5
ADAM_LR = 0.001
ADAM_B1 = 0.9
ADAM_B2 = 0.999
ADAM_EPS = 1e-08
ADAM_WD = 0.01
ADAM_STEP = 10
VMEM_LIMIT = 56 * 1024 * 1024


def _params(sem, vmem=VMEM_LIMIT):
    return pltpu.CompilerParams(dimension_semantics=sem, vmem_limit_bytes=vmem)


def _tile(n, prefs):
    for t in prefs:
        if n % t == 0:
            return t
    return n


def _sigmoid(x):
    return 1.0 / (1.0 + jnp.exp(-x))


def _hbm_spec():
    return pl.BlockSpec(memory_space=pltpu.HBM)


class _Side:
    def __init__(self, inputs, out_shapes, n_sems, start, wait, aliases=None):
        self.inputs, self.out_shapes, self.n_sems = list(inputs), list(out_shapes), n_sems
        self.start, self.wait, self.aliases = start, wait, dict(aliases or {})


def _pcall(body, *, name, grid, in_specs, out_specs, out_shape, scratch_shapes, semantics, args, side=None):
    if side is None:
        return pl.pallas_call(
            body, name=name, grid=grid, in_specs=in_specs, out_specs=out_specs, out_shape=out_shape,
            scratch_shapes=scratch_shapes, compiler_params=_params(semantics))(*args)
    single = not isinstance(out_shape, (tuple, list))
    out_specs_l = [out_specs] if single else list(out_specs)
    out_shape_l = [out_shape] if single else list(out_shape)
    n_i, n_si, n_o, n_so, n_s = len(in_specs), len(side.inputs), len(out_shape_l), len(side.out_shapes), len(scratch_shapes)

    def wrapped(*refs):
        ins, sins = refs[:n_i], refs[n_i:n_i + n_si]
        o0 = n_i + n_si
        outs, souts = refs[o0:o0 + n_o], refs[o0 + n_o:o0 + n_o + n_so]
        s0 = o0 + n_o + n_so
        scr, (send, recv) = refs[s0:s0 + n_s], refs[s0 + n_s:]
        first = pl.program_id(0) == 0
        last = pl.program_id(0) == grid[0] - 1
        for ax in range(1, len(grid)):
            first = jnp.logical_and(first, pl.program_id(ax) == 0)
            last = jnp.logical_and(last, pl.program_id(ax) == grid[ax] - 1)

        @pl.when(first)
        def _():
            side.start(sins, souts, send, recv)

        body(*ins, *outs, *scr)

        @pl.when(last)
        def _():
            side.wait(sins, souts, send, recv)

    res = pl.pallas_call(
        wrapped, name=name, grid=grid,
        in_specs=list(in_specs) + [_hbm_spec()] * n_si,
        out_specs=tuple(out_specs_l + [_hbm_spec()] * n_so),
        out_shape=tuple(out_shape_l + side.out_shapes),
        scratch_shapes=list(scratch_shapes) + [pltpu.SemaphoreType.DMA((side.n_sems,))] * 2,
        input_output_aliases={n_i + i: n_o + o for i, o in side.aliases.items()},
        compiler_params=_params(("arbitrary",) * len(grid)),
    )(*args, *side.inputs)
    main = res[:n_o]
    return (main[0] if single else tuple(main)), tuple(res[n_o:])


def _run_side(side, name):
    n_si, n_so = len(side.inputs), len(side.out_shapes)

    def body(*refs):
        sins, souts = refs[:n_si], refs[n_si:n_si + n_so]
        send, recv = refs[n_si + n_so:]
        side.start(sins, souts, send, recv)
        side.wait(sins, souts, send, recv)

    return pl.pallas_call(
        body, name=name, out_shape=tuple(side.out_shapes),
        in_specs=[_hbm_spec()] * n_si, out_specs=tuple([_hbm_spec()] * n_so),
        scratch_shapes=[pltpu.SemaphoreType.DMA((side.n_sems,))] * 2,
        input_output_aliases=side.aliases,
    )(*side.inputs)


_DIMS = {"nn": ((1,), (0,)), "nt": ((1,), (1,)), "tn": ((0,), (0,))}


def _matmul(a, b, form, out_dtype, name, add=None, add_scale=1.0, tm=None, tn=None, tk=None, side=None):
    if form == "nn":
        (m, k), (_, n) = a.shape, b.shape
    elif form == "nt":
        (m, k), (n, _) = a.shape, b.shape
    else:
        (k, m), (_, n) = a.shape, b.shape
    tm = tm or _tile(m, (1024, 512, 256, 128))
    tn = tn or _tile(n, (1024, 512, 256, 128))
    tk = tk or _tile(k, (2048, 1792, 1024, 512, 256, 128))
    nk = k // tk
    if form == "tn":
        a_spec = pl.BlockSpec((tk, tm), lambda i, j, kk: (kk, i))
    else:
        a_spec = pl.BlockSpec((tm, tk), lambda i, j, kk: (i, kk))
    if form == "nt":
        b_spec = pl.BlockSpec((tn, tk), lambda i, j, kk: (j, kk))
    else:
        b_spec = pl.BlockSpec((tk, tn), lambda i, j, kk: (kk, j))
    o_spec = pl.BlockSpec((tm, tn), lambda i, j, kk: (i, j))
    dims = (_DIMS[form], ((), ()))
    has_add = add is not None

    def finish(r, add_ref, o_ref):
        if has_add:
            r = r + add_scale * add_ref[...]
        o_ref[...] = r.astype(out_dtype)

    def body_one_pass(a_ref, b_ref, *rest):
        add_ref, o_ref = rest if has_add else (None,) + rest
        finish(lax.dot_general(a_ref[...], b_ref[...], dims, preferred_element_type=F32), add_ref, o_ref)

    def body_accumulate(a_ref, b_ref, *rest):
        add_ref, o_ref, acc = rest if has_add else (None,) + rest
        kk = pl.program_id(2)
        part = lax.dot_general(a_ref[...], b_ref[...], dims, preferred_element_type=F32)

        @pl.when(kk == 0)
        def _():
            acc[...] = part

        @pl.when(jnp.logical_and(kk > 0, kk < nk - 1))
        def _():
            acc[...] += part

        @pl.when(kk == nk - 1)
        def _():
            finish(acc[...] + part, add_ref, o_ref)

    in_specs = [a_spec, b_spec] + ([o_spec] if has_add else [])
    args = (a, b) + ((add,) if has_add else ())
    return _pcall(
        body_one_pass if nk == 1 else body_accumulate, name=name, grid=(m // tm, n // tn, nk),
        in_specs=in_specs, out_specs=o_spec,
        out_shape=jax.ShapeDtypeStruct((m, n), out_dtype),
        scratch_shapes=[] if nk == 1 else [pltpu.VMEM((tm, tn), F32)],
        semantics=("parallel", "parallel", "arbitrary"), args=args, side=side)


def _tri(n, upper):
    r = lax.broadcasted_iota(jnp.int32, (n, n), 0)
    c = lax.broadcasted_iota(jnp.int32, (n, n), 1)
    return jnp.where((r <= c) if upper else (r >= c), 1.0, 0.0).astype(F32)


def _gate_fwd(fg_t, b_col):
    h, s = fg_t.shape
    nb = s // 128

    def body(fg_ref, b_ref, cum_ref):
        u = _tri(128, True)
        carry = jnp.zeros((h, 128), F32)
        for j in range(nb):
            z = fg_ref[:, j * 128:(j + 1) * 128] + b_ref[...]
            logf = -(jnp.maximum(-z, 0.0) + jnp.log(1.0 + jnp.exp(-jnp.abs(z))))
            c = jnp.dot(logf, u, precision=lax.Precision.HIGHEST, preferred_element_type=F32) + carry
            cum_ref[:, j * 128:(j + 1) * 128] = c
            carry = jnp.broadcast_to(c[:, 127:128], (h, 128))

    return pl.pallas_call(
        body, name="gate_fwd", out_shape=jax.ShapeDtypeStruct((h, s), F32),
        in_specs=[pl.BlockSpec(memory_space=pltpu.VMEM)] * 2,
        out_specs=pl.BlockSpec(memory_space=pltpu.VMEM),
    )(fg_t, b_col)


def _gate_bwd(drow_t, dcol_t, fg_t, b_col):
    h, s = fg_t.shape
    nb = s // 128

    def body(dr_ref, dc_ref, fg_ref, b_ref, dfg_ref, db_ref):
        low = _tri(128, False)
        carry = jnp.zeros((h, 128), F32)
        db = jnp.zeros((h, 128), F32)
        for j in reversed(range(nb)):
            sl = slice(j * 128, (j + 1) * 128)
            r = jnp.dot(dr_ref[:, 0, sl] - dc_ref[:, 0, sl], low, precision=lax.Precision.HIGHEST,
                        preferred_element_type=F32) + carry
            carry = jnp.broadcast_to(r[:, 0:1], (h, 128))
            z = fg_ref[:, sl] + b_ref[...]
            dfg = r / (1.0 + jnp.exp(z))
            dfg_ref[:, sl] = dfg
            db = db + dfg
        db_ref[...] = jnp.broadcast_to(jnp.sum(db, axis=1, keepdims=True), (h, 128))

    return pl.pallas_call(
        body, name="gate_bwd",
        out_shape=(jax.ShapeDtypeStruct((h, s), F32), jax.ShapeDtypeStruct((h, 128), F32)),
        in_specs=[pl.BlockSpec(memory_space=pltpu.VMEM)] * 4,
        out_specs=(pl.BlockSpec(memory_space=pltpu.VMEM),) * 2,
    )(drow_t, dcol_t, fg_t, b_col)


LOG2E = 1.4426950408889634
ATT_ROW_CHAINS = 1


def _attn_fwd(p_main, cum4, n_heads, t, side=None):
    s = p_main.shape[0]
    nq = s // t
    scale2 = HEAD_DIM ** -0.5 * LOG2E
    rep = t // 128
    tr = t // ATT_ROW_CHAINS

    def body(q_ref, k_ref, v_ref, c_ref, o_ref, lse_ref, m_s, l_s, acc_s):
        qi = pl.program_id(1)
        m_s[...] = jnp.full_like(m_s, -jnp.inf)
        l_s[...] = jnp.zeros_like(l_s)
        acc_s[...] = jnp.zeros_like(acc_s)

        def block(ki, masked):
            keys = pl.ds(pl.multiple_of(ki * t, t), t)
            k = k_ref[keys, :]
            v = v_ref[keys, :]
            c2 = c_ref[ki] * LOG2E
            for r in range(ATT_ROW_CHAINS):
                rs = slice(r * tr, (r + 1) * tr)
                sc = lax.dot_general(q_ref[rs, :], k, (_DIMS["nt"], ((), ())),
                                     preferred_element_type=F32) * scale2 - c2
                if masked:
                    row = lax.broadcasted_iota(jnp.int32, (tr, t), 0) + r * tr
                    col = lax.broadcasted_iota(jnp.int32, (tr, t), 1)
                    sc = jnp.where(col <= row, sc, -jnp.inf)
                m_prev = m_s[rs, :]
                m_new = jnp.maximum(m_prev, jnp.max(sc, axis=1, keepdims=True))
                alpha = jnp.exp2(m_prev - m_new)
                p = jnp.exp2(sc - jnp.tile(m_new, (1, rep)))
                l_s[rs, :] = alpha * l_s[rs, :] + jnp.sum(p, axis=1, keepdims=True)
                acc_s[rs, :] = alpha * acc_s[rs, :] + jnp.dot(p.astype(BF16), v, preferred_element_type=F32)
                m_s[rs, :] = m_new

        def full_block(ki, carry):
            block(ki, False)
            return carry

        lax.fori_loop(0, qi, full_block, 0)
        block(qi, True)
        o_ref[...] = (acc_s[...] / l_s[...]).astype(BF16)
        lse_ref[...] = m_s[...] + jnp.log(l_s[...]) * LOG2E

    head = lambda off: pl.BlockSpec((s, 128), lambda h, qi: (0, off + h))
    return _pcall(
        body, name="attn_fwd", grid=(n_heads, nq),
        in_specs=[pl.BlockSpec((t, 128), lambda h, qi: (qi, h)), head(n_heads), head(2 * n_heads),
                  pl.BlockSpec((None, nq, 1, t), lambda h, qi: (h, 0, 0, 0))],
        out_specs=(pl.BlockSpec((t, 128), lambda h, qi: (qi, h)),) * 2,
        out_shape=(jax.ShapeDtypeStruct((s, n_heads * 128), BF16), jax.ShapeDtypeStruct((s, n_heads * 128), F32)),
        scratch_shapes=[pltpu.VMEM((t, 128), F32)] * 3,
        semantics=("parallel", "arbitrary"), args=(p_main, p_main, p_main, cum4), side=side)


def _attn_bwd(p_main, do, lse_rep, delta_rep, cum4, n_heads, t, side=None):
    s = p_main.shape[0]
    nq = s // t
    scale = HEAD_DIM ** -0.5
    rep = t // 128
    tr = t // ATT_ROW_CHAINS

    def body(q_ref, k_ref, v_ref, do_ref, lse_ref, dl_ref, c_ref,
             dq_ref, dk_ref, dv_ref, drow_ref, dcol_ref, dq_acc, dk_acc, dv_acc):
        kb = pl.program_id(1)

        @pl.when(kb == 0)
        def _():
            dq_acc[...] = jnp.zeros_like(dq_acc)

        dk_acc[...] = jnp.zeros_like(dk_acc)
        dv_acc[...] = jnp.zeros_like(dv_acc)
        k = k_ref[...]
        v = v_ref[...]
        c2 = c_ref[...] * LOG2E
        k_ones = jnp.concatenate([k, jnp.ones((t, 128), BF16)], axis=1)
        q_ones = jnp.ones((tr, 128), BF16)

        def block(qb, masked):
            for r in range(ATT_ROW_CHAINS):
                rows = pl.ds(pl.multiple_of(qb * t + r * tr, tr), tr)
                q = q_ref[rows, :]
                dout = do_ref[rows, :]
                sc = lax.dot_general(q, k, (_DIMS["nt"], ((), ())), preferred_element_type=F32) * (scale * LOG2E) - c2
                if masked:
                    row = lax.broadcasted_iota(jnp.int32, (tr, t), 0) + r * tr
                    col = lax.broadcasted_iota(jnp.int32, (tr, t), 1)
                    sc = jnp.where(col <= row, sc, -jnp.inf)
                p = jnp.exp2(sc - jnp.tile(lse_ref[rows, :], (1, rep)))
                dp = lax.dot_general(dout, v, (_DIMS["nt"], ((), ())), preferred_element_type=F32)
                dsb = (p * (dp - jnp.tile(dl_ref[rows, :], (1, rep)))).astype(BF16)
                dv_acc[...] += lax.dot_general(p.astype(BF16), dout, (_DIMS["tn"], ((), ())),
                                               preferred_element_type=F32)
                dk_acc[...] += lax.dot_general(dsb, jnp.concatenate([q, q_ones], axis=1), (_DIMS["tn"], ((), ())),
                                               preferred_element_type=F32)
                dq_acc[rows, :] += jnp.dot(dsb, k_ones, preferred_element_type=F32)

        def full_block(qb, carry):
            block(qb, False)
            return carry

        block(kb, True)
        diag = pl.ds(pl.multiple_of(kb * t, t), t)
        dq_ref[diag, :] = (dq_acc[diag, 0:128] * scale).astype(BF16)
        drow_ref[:, diag] = dq_acc[diag, 128:256].T[0:8, :]
        lax.fori_loop(kb + 1, nq, full_block, 0)
        dk_ref[...] = (dk_acc[:, 0:128] * scale).astype(BF16)
        dcol_ref[...] = dk_acc[:, 128:256].T[0:8, :]
        dv_ref[...] = dv_acc[...].astype(BF16)

    kside = lambda off: pl.BlockSpec((t, 128), lambda h, kb: (kb, off + h))
    whole = pl.BlockSpec((s, 128), lambda h, kb: (0, h))
    hw = n_heads * 128
    return _pcall(
        body, name="attn_bwd", grid=(n_heads, nq),
        in_specs=[whole, kside(n_heads), kside(2 * n_heads), whole, whole, whole,
                  pl.BlockSpec((None, None, 1, t), lambda h, kb: (h, kb, 0, 0))],
        out_specs=(whole, kside(0), kside(0), pl.BlockSpec((None, 8, s), lambda h, kb: (h, 0, 0)),
                   pl.BlockSpec((None, 8, t), lambda h, kb: (h, 0, kb))),
        out_shape=(jax.ShapeDtypeStruct((s, hw), BF16), jax.ShapeDtypeStruct((s, hw), BF16),
                   jax.ShapeDtypeStruct((s, hw), BF16), jax.ShapeDtypeStruct((n_heads, 8, s), F32),
                   jax.ShapeDtypeStruct((n_heads, 8, s), F32)),
        scratch_shapes=[pltpu.VMEM((s, 256), F32), pltpu.VMEM((t, 256), F32), pltpu.VMEM((t, 128), F32)],
        semantics=("parallel", "arbitrary"), args=(p_main, p_main, p_main, do, lse_rep, delta_rep, cum4), side=side)


def _conv_fwd(u_ext, cw_ref):
    r1 = pltpu.roll(u_ext, 1, 0)
    r2 = pltpu.roll(u_ext, 2, 0)
    conv = cw_ref[2:3, :] * u_ext + cw_ref[1:2, :] * r1 + cw_ref[0:1, :] * r2
    return conv[HALO:], r1[HALO:], r2[HALO:]


def _pool_z(pu, pu_halo, row0, tm, pg):
    ext = jnp.concatenate([pu_halo, pu], axis=0)
    t1 = (row0 + lax.broadcasted_iota(jnp.int32, (tm, pg), 0) + 1).astype(F32)
    zs = []
    for g, w in enumerate(POOL_WINDOWS):
        sm = ext[:, g * pg:(g + 1) * pg]
        sh = 1
        while sh < w:
            sm = sm + pltpu.roll(sm, sh, 0)
            sh *= 2
        mean = sm[HALO:] / jnp.minimum(t1, float(w))
        zs.append(mean - pu[:, g * pg:(g + 1) * pg])
    return zs


def _mix_specs(tm, s):
    per = tm // HALO
    last = s // HALO - 1
    cur = lambda w, j: pl.BlockSpec((tm, w), lambda i: (i, j))
    prev = lambda j: pl.BlockSpec((HALO, 512), lambda i: (jnp.maximum(i * per - 1, 0), j))
    nxt = lambda j: pl.BlockSpec((HALO, 512), lambda i: (jnp.minimum((i + 1) * per, last), j))
    full = lambda shape: pl.BlockSpec(shape, lambda i: (0,) * len(shape))
    return cur, prev, nxt, full


def _mix_fwd(p_main, o, conv_w, pool_w, pool_scale, tm):
    s = p_main.shape[0]
    d = 2048
    pg = 128
    cur, prev, nxt, full = _mix_specs(tm, s)

    def body(ga_ref, cb_ref, cc_ref, ch_ref, gc_ref, pu_ref, gp_ref, cch_ref, chh_ref, puh_ref,
             o_ref, cw_ref, pw_ref, ps_ref, y_ref):
        i = pl.program_id(0)
        first = i == 0
        ga = ga_ref[...].astype(F32)
        y_ref[:, 0:1024] = (o_ref[...].astype(F32) * ga * _sigmoid(ga)).astype(BF16)

        u = cc_ref[...].astype(F32) * ch_ref[...].astype(F32)
        uh = jnp.where(first, 0.0, cch_ref[...].astype(F32) * chh_ref[...].astype(F32))
        conv, _, _ = _conv_fwd(jnp.concatenate([uh, u], axis=0), cw_ref)
        gc = gc_ref[...].astype(F32)
        y_ref[:, 1024:1536] = (cb_ref[...].astype(F32) * conv * gc * _sigmoid(gc)).astype(BF16)

        pu = pu_ref[...].astype(F32)
        puh = jnp.where(first, 0.0, puh_ref[...].astype(F32))
        zs = _pool_z(pu, puh, i * tm, tm, pg)
        gp = gp_ref[...].astype(F32)
        gate = gp * _sigmoid(gp) * ps_ref[...]
        for g in range(4):
            r = jnp.dot(zs[g].astype(BF16), pw_ref[g], preferred_element_type=F32)
            y_ref[:, 1536 + g * pg:1536 + (g + 1) * pg] = (r * gate[:, g * pg:(g + 1) * pg]).astype(BF16)

    return pl.pallas_call(
        body, name="mix_fwd", grid=(s // tm,),
        in_specs=[cur(1024, 3), cur(512, 8), cur(512, 9), cur(512, 10), cur(512, 11), cur(512, 12), cur(512, 13),
                  prev(9), prev(10), prev(12),
                  cur(1024, 0), full((3, 512)), full((4, pg, pg)), full((1, 512))],
        out_specs=pl.BlockSpec((tm, d), lambda i: (i, 0)),
        out_shape=jax.ShapeDtypeStruct((s, d), BF16),
        compiler_params=_params(("parallel",)),
    )(*([p_main] * 10), o, conv_w, pool_w, pool_scale)


def _mix_bwd(p_main, o, dy, conv_w, pool_w, pool_scale, tm):
    s = p_main.shape[0]
    pg = 128
    n_heads = 8
    cur, prev, nxt, full = _mix_specs(tm, s)
    nblk = s // tm
    n_ext = tm + HALO

    def silu_and_grad(x):
        sg = _sigmoid(x)
        return x * sg, sg * (1.0 + x * (1.0 - sg))

    def body(ga_ref, cb_ref, cc_ref, ch_ref, gc_ref, pu_ref, gp_ref, cch_ref, chh_ref, puh_ref,
             cbn_ref, gcn_ref, gpn_ref, o_ref, dy_ref, dycn_ref, dypn_ref, cw_ref, pw_ref, ps_ref,
             do_ref, dl_ref, dp_ref, dsm_ref, dpw_ref):
        i = pl.program_id(0)
        first = i == 0
        last = i == nblk - 1

        @pl.when(first)
        def _():
            dsm_ref[...] = jnp.zeros_like(dsm_ref)
            dpw_ref[...] = jnp.zeros_like(dpw_ref)

        ga = ga_ref[...].astype(F32)
        of = o_ref[...].astype(F32)
        dya = dy_ref[:, 0:1024]
        sa, dsa = silu_and_grad(ga)
        dout = dya * sa
        do_ref[...] = dout.astype(BF16)
        dp_ref[:, 0:1024] = (dya * of * dsa).astype(BF16)
        prod = dout * of
        for h in range(n_heads):
            dsum = jnp.sum(prod[:, h * 128:(h + 1) * 128], axis=1, keepdims=True)
            dl_ref[:, h * 128:(h + 1) * 128] = jnp.broadcast_to(dsum, (tm, 128))

        cb = cb_ref[...].astype(F32)
        cc = cc_ref[...].astype(F32)
        ch = ch_ref[...].astype(F32)
        gc = gc_ref[...].astype(F32)
        u = cc * ch
        uh = jnp.where(first, 0.0, cch_ref[...].astype(F32) * chh_ref[...].astype(F32))
        conv, u1, u2 = _conv_fwd(jnp.concatenate([uh, u], axis=0), cw_ref)
        sc_, dsc = silu_and_grad(gc)
        dyc = dy_ref[:, 1024:1536]
        dp_ref[:, 1024:1536] = (dyc * conv * sc_).astype(BF16)
        dp_ref[:, 2560:3072] = (dyc * cb * conv * dsc).astype(BF16)
        dconv = dyc * cb * sc_
        gcn = gcn_ref[...].astype(F32)
        dconv_n = jnp.where(last, 0.0, dycn_ref[...] * cbn_ref[...].astype(F32) * gcn * _sigmoid(gcn))
        dext = jnp.concatenate([dconv, dconv_n], axis=0)
        du = (cw_ref[2:3, :] * dext + cw_ref[1:2, :] * pltpu.roll(dext, n_ext - 1, 0)
              + cw_ref[0:1, :] * pltpu.roll(dext, n_ext - 2, 0))[:tm]
        dp_ref[:, 1536:2048] = (du * ch).astype(BF16)
        dp_ref[:, 2048:2560] = (du * cc).astype(BF16)
        dsm_ref[0:1, :] += jnp.sum(dconv * u2, axis=0, keepdims=True)
        dsm_ref[1:2, :] += jnp.sum(dconv * u1, axis=0, keepdims=True)
        dsm_ref[2:3, :] += jnp.sum(dconv * u, axis=0, keepdims=True)

        pu = pu_ref[...].astype(F32)
        puh = jnp.where(first, 0.0, puh_ref[...].astype(F32))
        zs = _pool_z(pu, puh, i * tm, tm, pg)
        gp = gp_ref[...].astype(F32)
        sp, dsp = silu_and_grad(gp)
        dyp = dy_ref[:, 1536:2048]
        gpn = gpn_ref[...].astype(F32)
        dr_n = jnp.where(last, 0.0, dypn_ref[...] * gpn * _sigmoid(gpn) * ps_ref[...])
        drs = dyp * sp
        dr = drs * ps_ref[...]
        t1 = (i * tm + lax.broadcasted_iota(jnp.int32, (tm, pg), 0) + 1).astype(F32)
        r_parts, dpu_parts = [], []
        for g, w in enumerate(POOL_WINDOWS):
            cols = slice(g * pg, (g + 1) * pg)
            zb = zs[g].astype(BF16)
            r_parts.append(jnp.dot(zb, pw_ref[g], preferred_element_type=F32))
            drb = dr[:, cols].astype(BF16)
            dpw_ref[g] += lax.dot_general(zb, drb, (_DIMS["tn"], ((), ())), preferred_element_type=F32)
            dz = lax.dot_general(drb, pw_ref[g], (_DIMS["nt"], ((), ())), preferred_element_type=F32)
            dz_n = lax.dot_general(dr_n[:, cols].astype(BF16), pw_ref[g], (_DIMS["nt"], ((), ())),
                                   preferred_element_type=F32)
            sm = jnp.concatenate([dz / jnp.minimum(t1, float(w)), dz_n / float(w)], axis=0)
            sh = 1
            while sh < w:
                sm = sm + pltpu.roll(sm, n_ext - sh, 0)
                sh *= 2
            dpu_parts.append(sm[:tm] - dz)
        r = jnp.concatenate(r_parts, axis=1)
        dp_ref[:, 3072:3584] = jnp.concatenate(dpu_parts, axis=1).astype(BF16)
        dp_ref[:, 3584:4096] = (dyp * r * ps_ref[...] * dsp).astype(BF16)
        dsm_ref[3:4, :] += jnp.sum(drs * r, axis=0, keepdims=True)

    dy_next = lambda j: pl.BlockSpec((HALO, 512), lambda i: (jnp.minimum((i + 1) * (tm // HALO), s // HALO - 1), j))
    return pl.pallas_call(
        body, name="mix_bwd", grid=(nblk,),
        in_specs=[cur(1024, 3), cur(512, 8), cur(512, 9), cur(512, 10), cur(512, 11), cur(512, 12), cur(512, 13),
                  prev(9), prev(10), prev(12), nxt(8), nxt(11), nxt(13),
                  cur(1024, 0), pl.BlockSpec((tm, 2048), lambda i: (i, 0)), dy_next(2), dy_next(3),
                  full((3, 512)), full((4, pg, pg)), full((1, 512))],
        out_specs=(pl.BlockSpec((tm, 1024), lambda i: (i, 0)), pl.BlockSpec((tm, 1024), lambda i: (i, 0)),
                   pl.BlockSpec((tm, 4096), lambda i: (i, 0)), full((8, 512)), full((4, pg, pg))),
        out_shape=(jax.ShapeDtypeStruct((s, 1024), BF16), jax.ShapeDtypeStruct((s, 1024), F32),
                   jax.ShapeDtypeStruct((s, 4096), BF16), jax.ShapeDtypeStruct((8, 512), F32),
                   jax.ShapeDtypeStruct((4, pg, pg), F32)),
        compiler_params=_params(("arbitrary",)),
    )(*([p_main] * 13), o, dy, dy, dy, conv_w, pool_w, pool_scale)


def _outproj_ln(y, w_out, x, ln_g, ln_b, alpha, tm, side=None):
    s, d = x.shape

    def body(y_ref, w_ref, x_ref, g_ref, b_ref, xn_ref, xb_ref, pre_ref):
        pre = alpha * x_ref[...] + jnp.dot(y_ref[...], w_ref[...], preferred_element_type=F32)
        mu = jnp.mean(pre, axis=1, keepdims=True)
        cen = pre - mu
        var = jnp.mean(cen * cen, axis=1, keepdims=True)
        xn = cen * lax.rsqrt(var + LN_EPS) * g_ref[...] + b_ref[...]
        pre_ref[...] = pre
        xn_ref[...] = xn
        xb_ref[...] = xn.astype(BF16)

    row = pl.BlockSpec((tm, d), lambda i: (i, 0))
    vec = pl.BlockSpec((1, d), lambda i: (0, 0))
    return _pcall(
        body, name="outproj_ln", grid=(s // tm,),
        in_specs=[row, pl.BlockSpec((d, d), lambda i: (0, 0)), row, vec, vec],
        out_specs=(row, row, row),
        out_shape=(jax.ShapeDtypeStruct((s, d), F32), jax.ShapeDtypeStruct((s, d), BF16),
                   jax.ShapeDtypeStruct((s, d), F32)),
        scratch_shapes=[], semantics=("parallel",), args=(y, w_out, x, ln_g, ln_b), side=side)


def _ln_bwd(dxn, pre, ln_g, tm, side=None):
    s, d = pre.shape

    def body(dx_ref, pre_ref, g_ref, dpre_ref, dpb_ref, dgb_ref):
        @pl.when(pl.program_id(0) == 0)
        def _():
            dgb_ref[...] = jnp.zeros_like(dgb_ref)

        pre_ = pre_ref[...]
        dx = dx_ref[...]
        mu = jnp.mean(pre_, axis=1, keepdims=True)
        cen = pre_ - mu
        var = jnp.mean(cen * cen, axis=1, keepdims=True)
        rstd = lax.rsqrt(var + LN_EPS)
        xhat = cen * rstd
        dxh = dx * g_ref[...]
        dpre = rstd * (dxh - jnp.mean(dxh, axis=1, keepdims=True)
                       - xhat * jnp.mean(dxh * xhat, axis=1, keepdims=True))
        dpre_ref[...] = dpre
        dpb_ref[...] = dpre.astype(BF16)
        dgb_ref[0:1, :] += jnp.sum(dx * xhat, axis=0, keepdims=True)
        dgb_ref[1:2, :] += jnp.sum(dx, axis=0, keepdims=True)

    row = pl.BlockSpec((tm, d), lambda i: (i, 0))
    return _pcall(
        body, name="ln_bwd", grid=(s // tm,),
        in_specs=[row, row, pl.BlockSpec((1, d), lambda i: (0, 0))],
        out_specs=(row, row, pl.BlockSpec((8, d), lambda i: (0, 0))),
        out_shape=(jax.ShapeDtypeStruct((s, d), F32), jax.ShapeDtypeStruct((s, d), BF16),
                   jax.ShapeDtypeStruct((8, d), F32)),
        scratch_shapes=[], semantics=("arbitrary",), args=(dxn, pre, ln_g), side=side)


def _loss_grad(y, target, tm):
    s, d = y.shape

    def body(y_ref, t_ref, dy_ref, loss_ref):
        @pl.when(pl.program_id(0) == 0)
        def _():
            loss_ref[...] = jnp.zeros_like(loss_ref)

        diff = y_ref[...] - t_ref[...]
        dy_ref[...] = diff / d
        loss_ref[...] += 0.5 * jnp.sum(jnp.sum(diff * diff, axis=1, keepdims=True) / d)

    row = pl.BlockSpec((tm, d), lambda i: (i, 0))
    return pl.pallas_call(
        body, name="loss_grad", grid=(s // tm,),
        in_specs=[row, row],
        out_specs=(row, pl.BlockSpec((8, 128), lambda i: (0, 0))),
        out_shape=(jax.ShapeDtypeStruct((s, d), F32), jax.ShapeDtypeStruct((8, 128), F32)),
        compiler_params=_params(("arbitrary",)),
    )(y, target)


def _adamw(w, g, m, v, name, lead_block=None):
    a, r, c = w.shape
    tr = r if lead_block else _tile(r, (256, 128, 64, 32, 16, 8))
    c1 = 1.0 - ADAM_B1 ** ADAM_STEP
    c2 = 1.0 - ADAM_B2 ** ADAM_STEP

    def body(w_ref, g_ref, m_ref, v_ref, d_ref, mo_ref, vo_ref):
        g_ = g_ref[...]
        m_new = ADAM_B1 * m_ref[...] + (1.0 - ADAM_B1) * g_
        v_new = ADAM_B2 * v_ref[...] + (1.0 - ADAM_B2) * (g_ * g_)
        d_ref[...] = -ADAM_LR * ((m_new / c1) / (jnp.sqrt(v_new / c2) + ADAM_EPS) + ADAM_WD * w_ref[...])
        mo_ref[...] = m_new
        vo_ref[...] = v_new

    ta = lead_block or 1
    spec = pl.BlockSpec((ta, tr, c), lambda i, j: (i, j, 0))
    shp = jax.ShapeDtypeStruct(w.shape, F32)
    return pl.pallas_call(
        body, name=name, grid=(a // ta, r // tr), in_specs=[spec] * 4, out_specs=(spec,) * 3, out_shape=(shp,) * 3,
        compiler_params=_params(("parallel", "parallel")),
    )(w, g, m, v)


def _place():
    x, y, c = lax.axis_index("x"), lax.axis_index("y"), lax.axis_index("c")
    return x, y, c, [(1 - x, y), (x, 1 - y), (1 - x, 1 - y)]


def _remote(src, dst, send, recv, k, to):
    return pltpu.make_async_remote_copy(src_ref=src, dst_ref=dst, send_sem=send.at[k], recv_sem=recv.at[k],
                                        device_id=to, device_id_type=MESH)


def _slot(ref, px, py, pc):
    return ref.at[pl.ds(2 * px + py, 1), pl.ds(pc, 1)]


def _side_gather_ici(shard):
    _, two, r, w = shard.shape

    def sends(sins, souts, send, recv):
        x, y, c, chips = _place()
        return [_remote(sins[0].at[:, pl.ds(c, 1)], _slot(souts[0], x, y, c), send, recv, k, (*chip, c))
                for k, chip in enumerate(chips)]

    def start(sins, souts, send, recv):
        for cp in sends(sins, souts, send, recv):
            cp.start()

    def wait(sins, souts, send, recv):
        x, y, c, chips = _place()
        for cp in sends(sins, souts, send, recv):
            cp.wait_send()
        for k, chip in enumerate(chips):
            _remote(_slot(souts[0], *chip, c), _slot(souts[0], *chip, c), send, recv, k, (x, y, c)).wait_recv()

    return _Side([shard], [jax.ShapeDtypeStruct((4, two, r, w), shard.dtype)], 3, start, wait)


def _side_gather_d2d(gathered):
    def sends(souts, send, recv):
        x, y, c, chips = _place()
        return [_remote(_slot(souts[0], *chip, c), _slot(souts[0], *chip, c), send, recv, k, (x, y, 1 - c))
                for k, chip in enumerate(chips)]

    def start(sins, souts, send, recv):
        for cp in sends(souts, send, recv):
            cp.start()

    def wait(sins, souts, send, recv):
        x, y, c, chips = _place()
        for cp in sends(souts, send, recv):
            cp.wait_send()
        for k, chip in enumerate(chips):
            _remote(_slot(souts[0], *chip, 1 - c), _slot(souts[0], *chip, 1 - c), send, recv, k, (x, y, c)).wait_recv()

    return _Side([gathered], [jax.ShapeDtypeStruct(gathered.shape, gathered.dtype)], 3, start, wait, aliases={0: 0})


def _side_swap(g):
    nchip, _, r, w = g.shape

    def copy(sins, souts, send, recv):
        x, y, c, _ = _place()
        return _remote(sins[0].at[:, pl.ds(1 - c, 1)], souts[0], send, recv, 0, (x, y, 1 - c))

    return _Side([g], [jax.ShapeDtypeStruct((nchip, 1, r, w), g.dtype)], 1,
                 lambda *a: copy(*a).start(), lambda *a: copy(*a).wait())


def _side_exchange(hsum):
    _, _, r, w = hsum.shape

    def copies(sins, souts, send, recv):
        x, y, c, chips = _place()
        return [_remote(sins[0].at[pl.ds(2 * px + py, 1)], souts[0].at[pl.ds(k, 1)], send, recv, k, (px, py, c))
                for k, (px, py) in enumerate(chips)]

    def start(*a):
        for cp in copies(*a):
            cp.start()

    def wait(*a):
        for cp in copies(*a):
            cp.wait()

    return _Side([hsum], [jax.ShapeDtypeStruct((3, 1, r, w), hsum.dtype)], 3, start, wait)


def _side_share(f):
    def copy(sins, souts, send, recv):
        x, y, c, _ = _place()
        return _remote(sins[0], souts[0], send, recv, 0, (x, y, 1 - c))

    return _Side([f], [jax.ShapeDtypeStruct(f.shape, f.dtype)], 1,
                 lambda *a: copy(*a).start(), lambda *a: copy(*a).wait())


def _own_slot_filled(gathered, shard):
    chip = 2 * lax.axis_index("x") + lax.axis_index("y")
    return lax.dynamic_update_slice(gathered, shard, (chip, 0, 0, 0))


def _both_parts(mine, theirs):
    mine_first = lax.axis_index("c") == 0
    return jnp.concatenate([jnp.where(mine_first, mine, theirs), jnp.where(mine_first, theirs, mine)], axis=0)


def _plane_tile(r, cdim):
    if r % 8 == 0:
        return _tile(r, (256, 128, 64, 32, 16, 8)), cdim
    return r, _tile(cdim, (256, 128))


def _add_halves(g, recv, core, name):
    nchip, nl, r, cdim = g.shape
    half = nl // 2
    tr, tc = _plane_tile(r, cdim)

    def body(c_ref, g_ref, r_ref, o_ref):
        o_ref[...] = (g_ref[...].astype(F32) + r_ref[...].astype(F32)).astype(BF16)

    blk = (1, 1, tr, tc)
    return pl.pallas_call(
        body, name=name,
        grid_spec=pltpu.PrefetchScalarGridSpec(
            num_scalar_prefetch=1, grid=(nchip, half, r // tr, cdim // tc),
            in_specs=[pl.BlockSpec(blk, lambda j, l, i, k, c_ref: (j, c_ref[0] * half + l, i, k)),
                      pl.BlockSpec(blk, lambda j, l, i, k, c_ref: (j, l, i, k))],
            out_specs=pl.BlockSpec(blk, lambda j, l, i, k, c_ref: (j, l, i, k))),
        out_shape=jax.ShapeDtypeStruct((nchip, half, r, cdim), BF16),
        compiler_params=_params(("parallel",) * 4),
    )(core, g, recv)


def _add_partials(hsum, others, chip, name):
    nchip, half, r, cdim = hsum.shape
    tr, tc = _plane_tile(r, cdim)

    def body(i_ref, h_ref, a_ref, b_ref, c_ref, o_ref):
        o_ref[...] = ((h_ref[...].astype(F32) + a_ref[...].astype(F32))
                      + (b_ref[...].astype(F32) + c_ref[...].astype(F32)))[0]

    blk = (1, 1, tr, tc)
    other = lambda n: pl.BlockSpec(blk, lambda l, i, k, i_ref: (n, l, i, k))
    return pl.pallas_call(
        body, name=name,
        grid_spec=pltpu.PrefetchScalarGridSpec(
            num_scalar_prefetch=1, grid=(half, r // tr, cdim // tc),
            in_specs=[pl.BlockSpec(blk, lambda l, i, k, i_ref: (i_ref[0], l, i, k)), other(0), other(1), other(2)],
            out_specs=pl.BlockSpec((1, tr, tc), lambda l, i, k, i_ref: (l, i, k))),
        out_shape=jax.ShapeDtypeStruct((half, r, cdim), F32),
        compiler_params=_params(("parallel",) * 3),
    )(chip, hsum, others, others, others)


def _all_gather_small(v, name):
    m_per, n = v.shape

    def body(x_ref, out_ref, send_sems, recv_sems, local_sem):
        x, y, c, chips = _place()
        me, sib = (x, y, c), (x, y, 1 - c)

        def rows(px, py, pc):
            return out_ref.at[pl.ds((4 * px + 2 * py + pc) * m_per, m_per), :]

        def copy(k, block, to, src=None):
            return pltpu.make_async_remote_copy(
                src_ref=rows(*block) if src is None else src, dst_ref=rows(*block), send_sem=send_sems.at[k],
                recv_sem=recv_sems.at[k], device_id=to, device_id_type=MESH)

        mine = pltpu.make_async_copy(x_ref, rows(*me), local_sem)
        mine.start()
        first = [copy(0, me, sib, src=x_ref)]
        first += [copy(1 + j, me, (*chip, c), src=x_ref) for j, chip in enumerate(chips)]
        for cp in first:
            cp.start()
        passed = [copy(4 + j, (*chip, c), sib) for j, chip in enumerate(chips)]
        for j, chip in enumerate(chips):
            copy(1 + j, (*chip, c), me).wait_recv()
            passed[j].start()
        copy(0, sib, me).wait_recv()
        for j, chip in enumerate(chips):
            copy(4 + j, (*chip, 1 - c), me).wait_recv()
        for cp in first + passed:
            cp.wait_send()
        mine.wait()

    return pl.pallas_call(
        body, name=name, out_shape=jax.ShapeDtypeStruct((8 * m_per, n), v.dtype),
        in_specs=[pl.BlockSpec(memory_space=pltpu.VMEM)], out_specs=pl.BlockSpec(memory_space=pltpu.VMEM),
        scratch_shapes=[pltpu.SemaphoreType.DMA((7,)), pltpu.SemaphoreType.DMA((7,)), pltpu.SemaphoreType.DMA],
        compiler_params=pltpu.CompilerParams(vmem_limit_bytes=VMEM_LIMIT),
    )(v)


def _sum_blocks(v, nblk, name):
    m_per = v.shape[0] // nblk

    def body(v_ref, o_ref):
        acc = v_ref[0:m_per, :]
        for j in range(1, nblk):
            acc = acc + v_ref[j * m_per:(j + 1) * m_per, :]
        o_ref[...] = acc

    return pl.pallas_call(
        body, name=name, out_shape=jax.ShapeDtypeStruct((m_per, 128), F32),
        in_specs=[pl.BlockSpec(memory_space=pltpu.VMEM)], out_specs=pl.BlockSpec(memory_space=pltpu.VMEM),
        compiler_params=pltpu.CompilerParams(vmem_limit_bytes=VMEM_LIMIT),
    )(v)


def _rows128(a, rows=None):
    flat = a.reshape(-1).astype(F32)
    need = -(-flat.shape[0] // 128)
    rows = rows or -(-need // 8) * 8
    return jnp.pad(flat, (0, rows * 128 - flat.shape[0])).reshape(rows, 128)


def kernel(x, w_in, b_f, conv_w, pool_w, pool_scale, w_out, ln_g, ln_b, loss_target, m_w_in, m_b_f, m_conv_w, m_pool_w, m_pool_scale, m_w_out, m_ln_g, m_ln_b, v_w_in, v_b_f, v_conv_w, v_pool_w, v_pool_scale, v_w_out, v_ln_g, v_ln_b):
    depth, d, in_shard = w_in.shape
    s = x.shape[1]
    att = d // 2
    n_heads = att // HEAD_DIM
    cw = d // 4
    pw = d - att - cw
    in_w = 4 * att + n_heads + 4 * cw + 2 * pw
    assert in_shard * 4 == in_w and (d, n_heads, cw, pw) == (2048, 8, 512, 512) and s % 512 == 0
    main_w = in_w - n_heads
    fg0 = 4 * att
    alpha = (2 * depth) ** 0.25
    t_att = 512
    tm_mix = 256

    cx, cy, cc = lax.axis_index("x"), lax.axis_index("y"), lax.axis_index("c")
    chip = 2 * cx + cy
    core_arr = jnp.reshape(cc, (1,)).astype(jnp.int32)
    chip_arr = jnp.reshape(chip, (1,)).astype(jnp.int32)
    half_d = d // 2
    out_rows = w_out.shape[1]
    r_comb = out_rows + in_shard

    to_rows = lambda a: jnp.transpose(a, (2, 0, 1))
    from_rows = lambda a: jnp.transpose(a, (1, 2, 0))
    wt_b = jnp.transpose(to_rows(w_in).astype(BF16), (1, 0, 2))
    comb = jnp.concatenate([w_out.astype(BF16), wt_b], axis=1)
    shards = jnp.transpose(comb.reshape(depth, r_comb, 2, half_d), (0, 2, 1, 3))

    def weights_of(gathered):
        full = jnp.transpose(gathered, (0, 2, 1, 3)).reshape(4, r_comb, d)
        wt = full[:, out_rows:].reshape(in_w, d)
        wt_main = jnp.concatenate([wt[:fg0], wt[fg0 + n_heads:]], axis=0)
        wt_fg = jnp.pad(wt[fg0:fg0 + n_heads], ((0, 128 - n_heads), (0, 0)))
        return wt_main, wt_fg, full[:, :out_rows].reshape(d, d)

    first = shards[0][None]
    arrived = _run_side(_side_gather_ici(first), "gather_first_ici")[0]
    arrived = _run_side(_side_gather_d2d(arrived), "gather_first_d2d")[0]
    weights = weights_of(_own_slot_filled(arrived, first))

    cw_rows = _rows128(conv_w)
    cw_all = _all_gather_small(cw_rows, "gather_conv_w").reshape(4, 2, cw_rows.shape[0] * 128)
    conv_full = cw_all[:, 0, :depth * 3 * (cw // 4)].reshape(4, depth, 3, cw // 4)
    conv_full = jnp.moveaxis(conv_full, 0, 2).reshape(depth, 3, cw)
    pool_b = pool_w.astype(BF16)

    xf = x[0]
    xb = xf.astype(BF16)
    saved = []
    for l in range(depth):
        wt_main, wt_fg, w_out_l = weights
        nxt = shards[l + 1][None] if l + 1 < depth else None
        p_main = _matmul(xb, wt_main, "nt", BF16, "proj_main")
        fg = _matmul(xb, wt_fg, "nt", F32, "proj_fg")
        fg_t = fg[:, :n_heads].T
        b_col = b_f[l].reshape(n_heads, 1)
        cum4 = _gate_fwd(fg_t, b_col).reshape(n_heads, s // t_att, 1, t_att)
        if nxt is None:
            o, lse = _attn_fwd(p_main, cum4, n_heads, t_att)
        else:
            (o, lse), (arrived,) = _attn_fwd(p_main, cum4, n_heads, t_att, side=_side_gather_ici(nxt))
        y = _mix_fwd(p_main, o, conv_full[l], pool_b[l], pool_scale[l].reshape(1, pw), tm_mix)
        ln_args = (y, w_out_l, xf, ln_g[l].reshape(1, d), ln_b[l].reshape(1, d), alpha, tm_mix)
        if nxt is None:
            x_new, xb_new, pre = _outproj_ln(*ln_args)
        else:
            (x_new, xb_new, pre), (arrived,) = _outproj_ln(*ln_args, side=_side_gather_d2d(arrived))
            weights = weights_of(_own_slot_filled(arrived, nxt))
        saved.append((xb, p_main, fg_t, b_col, cum4, o, lse, y, pre, wt_main, wt_fg, w_out_l))
        xf, xb = x_new, xb_new

    dx, loss_blk = _loss_grad(xf, loss_target[0], tm_mix)

    g_bf, g_sm, g_pw, g_ln, finished = ([None] * depth for _ in range(5))
    pending = None
    for l in reversed(range(depth)):
        xb, p_main, fg_t, b_col, cum4, o, lse, y, pre, wt_main, wt_fg, w_out_l = saved[l]
        if pending is None:
            dpre, dpre_b, g_ln[l] = _ln_bwd(dx, pre, ln_g[l].reshape(1, d), tm_mix)
        else:
            (dpre, dpre_b, g_ln[l]), (recv,) = _ln_bwd(dx, pre, ln_g[l].reshape(1, d), tm_mix,
                                                       side=_side_swap(pending))
            hsum = _add_halves(pending, recv, core_arr, "rs_add_halves")
        dy = _matmul(dpre_b, w_out_l, "nt", F32, "d_y")
        gw_out_l = _matmul(y, dpre_b, "tn", F32, "d_w_out")
        do, delta, dp_rest, g_sm[l], g_pw[l] = _mix_bwd(p_main, o, dy, conv_full[l], pool_b[l],
                                                        pool_scale[l].reshape(1, pw), tm_mix)
        if pending is None:
            dq, dk, dv, drow, dcol = _attn_bwd(p_main, do, lse, delta, cum4, n_heads, t_att)
        else:
            (dq, dk, dv, drow, dcol), (others,) = _attn_bwd(p_main, do, lse, delta, cum4, n_heads, t_att,
                                                            side=_side_exchange(hsum))
            mine = _add_partials(hsum, others, chip_arr, "rs_add_partials")
        dfg_t, g_bf[l] = _gate_bwd(drow, dcol, fg_t, b_col)
        dfg_b = jnp.pad(dfg_t.T, ((0, 0), (0, 128 - n_heads))).astype(BF16)
        dp_main = jnp.concatenate([dq, dk, dv, dp_rest], axis=1)
        if pending is None:
            gw_main_l = _matmul(dp_main, xb, "tn", F32, "d_w_main")
        else:
            gw_main_l, (theirs,) = _matmul(dp_main, xb, "tn", F32, "d_w_main", side=_side_share(mine))
            finished[l + 1] = _both_parts(mine, theirs)
        gw_fg_l = _matmul(dfg_b, xb, "tn", F32, "d_w_fg")
        gw_in_l = jnp.concatenate([gw_main_l[:fg0], gw_fg_l[:n_heads], gw_main_l[fg0:]], axis=0)
        by_chip = jnp.concatenate([gw_out_l.reshape(4, out_rows, d), gw_in_l.reshape(4, in_shard, d)], axis=1)
        pending = jnp.transpose(by_chip.astype(BF16).reshape(4, r_comb, 2, half_d), (0, 2, 1, 3))
        dx = _matmul(dp_main, wt_main, "nn", F32, "d_x_main", add=dpre, add_scale=alpha)
        dx = _matmul(dfg_b, wt_fg, "nn", F32, "d_x_fg", add=dx)

    recv = _run_side(_side_swap(pending), "rs_swap_last")[0]
    hsum = _add_halves(pending, recv, core_arr, "rs_add_halves")
    others = _run_side(_side_exchange(hsum), "rs_exchange_last")[0]
    mine = _add_partials(hsum, others, chip_arr, "rs_add_partials")
    theirs = _run_side(_side_share(mine), "rs_share_last")[0]
    finished[0] = _both_parts(mine, theirs)
    grads = jnp.transpose(jnp.stack(finished), (0, 2, 1, 3)).reshape(depth, r_comb, d)
    grad_w_out = grads[:, :out_rows]
    grad_rows = jnp.transpose(grads[:, out_rows:], (1, 0, 2))
    grad_w_in = from_rows(grad_rows)

    g_sm = jnp.stack(g_sm)
    g_ln = jnp.stack(g_ln)
    parts = [jnp.stack(g_pw), g_sm[:, 0:3, :], g_sm[:, 3, :], g_ln[:, 0, :], g_ln[:, 1, :],
             jnp.stack(g_bf)[:, :, 0], loss_blk[0, 0]]
    packed = jnp.concatenate([_rows128(p) for p in parts], axis=0)
    summed = _sum_blocks(_all_gather_small(packed, "gather_small_grads"), 8, "sum_small_grads")
    outs, off = [], 0
    for p in parts:
        nr = _rows128(p).shape[0]
        outs.append(summed[off:off + nr].reshape(-1)[:p.size].reshape(p.shape))
        off += nr
    grad_pool_w, gconv_full, grad_pool_scale, grad_ln_g, grad_ln_b, grad_b_f, loss = outs
    grad_conv_w = lax.dynamic_slice_in_dim(gconv_full, chip * (cw // 4), cw // 4, axis=2)

    d_w_in, nm_w_in, nv_w_in = (from_rows(a) for a in _adamw(
        to_rows(w_in), grad_rows, to_rows(m_w_in), to_rows(v_w_in), "adamw_w_in", lead_block=46))
    d_w_out, nm_w_out, nv_w_out = _adamw(w_out, grad_w_out, m_w_out, v_w_out, "adamw_w_out")
    small = [(b_f, grad_b_f, m_b_f, v_b_f), (conv_w, grad_conv_w, m_conv_w, v_conv_w),
             (pool_w, grad_pool_w, m_pool_w, v_pool_w), (pool_scale, grad_pool_scale, m_pool_scale, v_pool_scale),
             (ln_g, grad_ln_g, m_ln_g, v_ln_g), (ln_b, grad_ln_b, m_ln_b, v_ln_b)]
    pk = [jnp.concatenate([_rows128(t[j]) for t in small], axis=0)[None] for j in range(4)]
    sm_out = _adamw(pk[0], pk[1], pk[2], pk[3], "adamw_small")
    res = {j: [] for j in range(3)}
    off = 0
    for t in small:
        nr = _rows128(t[0]).shape[0]
        for j in range(3):
            res[j].append(sm_out[j][0, off:off + nr].reshape(-1)[:t[0].size].reshape(t[0].shape))
        off += nr
    (d_b_f, d_conv_w, d_pool_w, d_pool_scale, d_ln_g, d_ln_b) = res[0]
    (nm_b_f, nm_conv_w, nm_pool_w, nm_pool_scale, nm_ln_g, nm_ln_b) = res[1]
    (nv_b_f, nv_conv_w, nv_pool_w, nv_pool_scale, nv_ln_g, nv_ln_b) = res[2]

    return (loss, dx[None], grad_w_in, grad_b_f, grad_conv_w, grad_pool_w, grad_pool_scale, grad_w_out,
            grad_ln_g, grad_ln_b,
            d_w_in, d_b_f, d_conv_w, d_pool_w, d_pool_scale, d_w_out, d_ln_g, d_ln_b,
            nm_w_in, nm_b_f, nm_conv_w, nm_pool_w, nm_pool_scale, nm_w_out, nm_ln_g, nm_ln_b,
            nv_w_in, nv_b_f, nv_conv_w, nv_pool_w, nv_pool_scale, nv_w_out, nv_ln_g, nv_ln_b)
```

```python
import functools

import jax
import jax.numpy as jnp
from jax import lax
from jax.experimental import pallas as pl
from jax.experimental.pallas import tpu as pltpu

F32 = jnp.float32
BF16 = jnp.bfloat16
MESH = pl.DeviceIdType.MESH

HEAD_DIM = 128
POOL_WINDOWS = (2, 4, 8, 16)
HALO = 16
LN_EPS = 1e-5
ADAM_LR = 0.001
ADAM_B1 = 0.9
ADAM_B2 = 0.999
ADAM_EPS = 1e-08
ADAM_WD = 0.01
ADAM_STEP = 10
VMEM_LIMIT = 56 * 1024 * 1024


def _params(sem, vmem=VMEM_LIMIT):
    return pltpu.CompilerParams(dimension_semantics=sem, vmem_limit_bytes=vmem)


def _tile(n, prefs):
    for t in prefs:
        if n % t == 0:
            return t
    return n


def _sigmoid(x):
    return 1.0 / (1.0 + jnp.exp(-x))


def _hbm_spec():
    return pl.BlockSpec(memory_space=pltpu.HBM)


class _Side:
    def __init__(self, inputs, out_shapes, n_sems, start, wait, aliases=None):
        self.inputs, self.out_shapes, self.n_sems = list(inputs), list(out_shapes), n_sems
        self.start, self.wait, self.aliases = start, wait, dict(aliases or {})


def _pcall(body, *, name, grid, in_specs, out_specs, out_shape, scratch_shapes, semantics, args, side=None):
    if side is None:
        return pl.pallas_call(
            body, name=name, grid=grid, in_specs=in_specs, out_specs=out_specs, out_shape=out_shape,
            scratch_shapes=scratch_shapes, compiler_params=_params(semantics))(*args)
    single = not isinstance(out_shape, (tuple, list))
    out_specs_l = [out_specs] if single else list(out_specs)
    out_shape_l = [out_shape] if single else list(out_shape)
    n_i, n_si, n_o, n_so, n_s = len(in_specs), len(side.inputs), len(out_shape_l), len(side.out_shapes), len(scratch_shapes)

    def wrapped(*refs):
        ins, sins = refs[:n_i], refs[n_i:n_i + n_si]
        o0 = n_i + n_si
        outs, souts = refs[o0:o0 + n_o], refs[o0 + n_o:o0 + n_o + n_so]
        s0 = o0 + n_o + n_so
        scr, (send, recv) = refs[s0:s0 + n_s], refs[s0 + n_s:]
        first = pl.program_id(0) == 0
        last = pl.program_id(0) == grid[0] - 1
        for ax in range(1, len(grid)):
            first = jnp.logical_and(first, pl.program_id(ax) == 0)
            last = jnp.logical_and(last, pl.program_id(ax) == grid[ax] - 1)

        @pl.when(first)
        def _():
            side.start(sins, souts, send, recv)

        body(*ins, *outs, *scr)

        @pl.when(last)
        def _():
            side.wait(sins, souts, send, recv)

    res = pl.pallas_call(
        wrapped, name=name, grid=grid,
        in_specs=list(in_specs) + [_hbm_spec()] * n_si,
        out_specs=tuple(out_specs_l + [_hbm_spec()] * n_so),
        out_shape=tuple(out_shape_l + side.out_shapes),
        scratch_shapes=list(scratch_shapes) + [pltpu.SemaphoreType.DMA((side.n_sems,))] * 2,
        input_output_aliases={n_i + i: n_o + o for i, o in side.aliases.items()},
        compiler_params=_params(("arbitrary",) * len(grid)),
    )(*args, *side.inputs)
    main = res[:n_o]
    return (main[0] if single else tuple(main)), tuple(res[n_o:])


def _run_side(side, name):
    n_si, n_so = len(side.inputs), len(side.out_shapes)

    def body(*refs):
        sins, souts = refs[:n_si], refs[n_si:n_si + n_so]
        send, recv = refs[n_si + n_so:]
        side.start(sins, souts, send, recv)
        side.wait(sins, souts, send, recv)

    return pl.pallas_call(
        body, name=name, out_shape=tuple(side.out_shapes),
        in_specs=[_hbm_spec()] * n_si, out_specs=tuple([_hbm_spec()] * n_so),
        scratch_shapes=[pltpu.SemaphoreType.DMA((side.n_sems,))] * 2,
        input_output_aliases=side.aliases,
    )(*side.inputs)


_DIMS = {"nn": ((1,), (0,)), "nt": ((1,), (1,)), "tn": ((0,), (0,))}


def _matmul(a, b, form, out_dtype, name, add=None, add_scale=1.0, tm=None, tn=None, tk=None, side=None):
    if form == "nn":
        (m, k), (_, n) = a.shape, b.shape
    elif form == "nt":
        (m, k), (n, _) = a.shape, b.shape
    else:
        (k, m), (_, n) = a.shape, b.shape
    tm = tm or _tile(m, (1024, 512, 256, 128))
    tn = tn or _tile(n, (1024, 512, 256, 128))
    tk = tk or _tile(k, (2048, 1792, 1024, 512, 256, 128))
    nk = k // tk
    if form == "tn":
        a_spec = pl.BlockSpec((tk, tm), lambda i, j, kk: (kk, i))
    else:
        a_spec = pl.BlockSpec((tm, tk), lambda i, j, kk: (i, kk))
    if form == "nt":
        b_spec = pl.BlockSpec((tn, tk), lambda i, j, kk: (j, kk))
    else:
        b_spec = pl.BlockSpec((tk, tn), lambda i, j, kk: (kk, j))
    o_spec = pl.BlockSpec((tm, tn), lambda i, j, kk: (i, j))
    dims = (_DIMS[form], ((), ()))
    has_add = add is not None

    def finish(r, add_ref, o_ref):
        if has_add:
            r = r + add_scale * add_ref[...]
        o_ref[...] = r.astype(out_dtype)

    def body_one_pass(a_ref, b_ref, *rest):
        add_ref, o_ref = rest if has_add else (None,) + rest
        finish(lax.dot_general(a_ref[...], b_ref[...], dims, preferred_element_type=F32), add_ref, o_ref)

    def body_accumulate(a_ref, b_ref, *rest):
        add_ref, o_ref, acc = rest if has_add else (None,) + rest
        kk = pl.program_id(2)
        part = lax.dot_general(a_ref[...], b_ref[...], dims, preferred_element_type=F32)

        @pl.when(kk == 0)
        def _():
            acc[...] = part

        @pl.when(jnp.logical_and(kk > 0, kk < nk - 1))
        def _():
            acc[...] += part

        @pl.when(kk == nk - 1)
        def _():
            finish(acc[...] + part, add_ref, o_ref)

    in_specs = [a_spec, b_spec] + ([o_spec] if has_add else [])
    args = (a, b) + ((add,) if has_add else ())
    return _pcall(
        body_one_pass if nk == 1 else body_accumulate, name=name, grid=(m // tm, n // tn, nk),
        in_specs=in_specs, out_specs=o_spec,
        out_shape=jax.ShapeDtypeStruct((m, n), out_dtype),
        scratch_shapes=[] if nk == 1 else [pltpu.VMEM((tm, tn), F32)],
        semantics=("parallel", "parallel", "arbitrary"), args=args, side=side)


def _tri(n, upper):
    r = lax.broadcasted_iota(jnp.int32, (n, n), 0)
    c = lax.broadcasted_iota(jnp.int32, (n, n), 1)
    return jnp.where((r <= c) if upper else (r >= c), 1.0, 0.0).astype(F32)


def _gate_fwd(fg_t, b_col):
    h, s = fg_t.shape
    nb = s // 128

    def body(fg_ref, b_ref, cum_ref):
        u = _tri(128, True)
        carry = jnp.zeros((h, 128), F32)
        for j in range(nb):
            z = fg_ref[:, j * 128:(j + 1) * 128] + b_ref[...]
            logf = -(jnp.maximum(-z, 0.0) + jnp.log(1.0 + jnp.exp(-jnp.abs(z))))
            c = jnp.dot(logf, u, precision=lax.Precision.HIGHEST, preferred_element_type=F32) + carry
            cum_ref[:, j * 128:(j + 1) * 128] = c
            carry = jnp.broadcast_to(c[:, 127:128], (h, 128))

    return pl.pallas_call(
        body, name="gate_fwd", out_shape=jax.ShapeDtypeStruct((h, s), F32),
        in_specs=[pl.BlockSpec(memory_space=pltpu.VMEM)] * 2,
        out_specs=pl.BlockSpec(memory_space=pltpu.VMEM),
    )(fg_t, b_col)


def _gate_bwd(drow_t, dcol_t, fg_t, b_col):
    h, s = fg_t.shape
    nb = s // 128

    def body(dr_ref, dc_ref, fg_ref, b_ref, dfg_ref, db_ref):
        low = _tri(128, False)
        carry = jnp.zeros((h, 128), F32)
        db = jnp.zeros((h, 128), F32)
        for j in reversed(range(nb)):
            sl = slice(j * 128, (j + 1) * 128)
            r = jnp.dot(dr_ref[:, 0, sl] - dc_ref[:, 0, sl], low, precision=lax.Precision.HIGHEST,
                        preferred_element_type=F32) + carry
            carry = jnp.broadcast_to(r[:, 0:1], (h, 128))
            z = fg_ref[:, sl] + b_ref[...]
            dfg = r / (1.0 + jnp.exp(z))
            dfg_ref[:, sl] = dfg
            db = db + dfg
        db_ref[...] = jnp.broadcast_to(jnp.sum(db, axis=1, keepdims=True), (h, 128))

    return pl.pallas_call(
        body, name="gate_bwd",
        out_shape=(jax.ShapeDtypeStruct((h, s), F32), jax.ShapeDtypeStruct((h, 128), F32)),
        in_specs=[pl.BlockSpec(memory_space=pltpu.VMEM)] * 4,
        out_specs=(pl.BlockSpec(memory_space=pltpu.VMEM),) * 2,
    )(drow_t, dcol_t, fg_t, b_col)


LOG2E = 1.4426950408889634
ATT_ROW_CHAINS = 1


def _attn_fwd(p_main, cum4, n_heads, t, side=None):
    s = p_main.shape[0]
    nq = s // t
    scale2 = HEAD_DIM ** -0.5 * LOG2E
    rep = t // 128
    tr = t // ATT_ROW_CHAINS

    def body(q_ref, k_ref, v_ref, c_ref, o_ref, lse_ref, m_s, l_s, acc_s):
        qi = pl.program_id(1)
        m_s[...] = jnp.full_like(m_s, -jnp.inf)
        l_s[...] = jnp.zeros_like(l_s)
        acc_s[...] = jnp.zeros_like(acc_s)

        def block(ki, masked):
            keys = pl.ds(pl.multiple_of(ki * t, t), t)
            k = k_ref[keys, :]
            v = v_ref[keys, :]
            c2 = c_ref[ki] * LOG2E
            for r in range(ATT_ROW_CHAINS):
                rs = slice(r * tr, (r + 1) * tr)
                sc = lax.dot_general(q_ref[rs, :], k, (_DIMS["nt"], ((), ())),
                                     preferred_element_type=F32) * scale2 - c2
                if masked:
                    row = lax.broadcasted_iota(jnp.int32, (tr, t), 0) + r * tr
                    col = lax.broadcasted_iota(jnp.int32, (tr, t), 1)
                    sc = jnp.where(col <= row, sc, -jnp.inf)
                m_prev = m_s[rs, :]
                m_new = jnp.maximum(m_prev, jnp.max(sc, axis=1, keepdims=True))
                alpha = jnp.exp2(m_prev - m_new)
                p = jnp.exp2(sc - jnp.tile(m_new, (1, rep)))
                l_s[rs, :] = alpha * l_s[rs, :] + jnp.sum(p, axis=1, keepdims=True)
                acc_s[rs, :] = alpha * acc_s[rs, :] + jnp.dot(p.astype(BF16), v, preferred_element_type=F32)
                m_s[rs, :] = m_new

        def full_block(ki, carry):
            block(ki, False)
            return carry

        lax.fori_loop(0, qi, full_block, 0)
        block(qi, True)
        o_ref[...] = (acc_s[...] / l_s[...]).astype(BF16)
        lse_ref[...] = m_s[...] + jnp.log(l_s[...]) * LOG2E

    head = lambda off: pl.BlockSpec((s, 128), lambda h, qi: (0, off + h))
    return _pcall(
        body, name="attn_fwd", grid=(n_heads, nq),
        in_specs=[pl.BlockSpec((t, 128), lambda h, qi: (qi, h)), head(n_heads), head(2 * n_heads),
                  pl.BlockSpec((None, nq, 1, t), lambda h, qi: (h, 0, 0, 0))],
        out_specs=(pl.BlockSpec((t, 128), lambda h, qi: (qi, h)),) * 2,
        out_shape=(jax.ShapeDtypeStruct((s, n_heads * 128), BF16), jax.ShapeDtypeStruct((s, n_heads * 128), F32)),
        scratch_shapes=[pltpu.VMEM((t, 128), F32)] * 3,
        semantics=("parallel", "arbitrary"), args=(p_main, p_main, p_main, cum4), side=side)


def _attn_bwd(p_main, do, lse_rep, delta_rep, cum4, n_heads, t, side=None):
    s = p_main.shape[0]
    nq = s // t
    scale = HEAD_DIM ** -0.5
    rep = t // 128
    tr = t // ATT_ROW_CHAINS

    def body(q_ref, k_ref, v_ref, do_ref, lse_ref, dl_ref, c_ref,
             dq_ref, dk_ref, dv_ref, drow_ref, dcol_ref, dq_acc, dk_acc, dv_acc):
        kb = pl.program_id(1)

        @pl.when(kb == 0)
        def _():
            dq_acc[...] = jnp.zeros_like(dq_acc)

        dk_acc[...] = jnp.zeros_like(dk_acc)
        dv_acc[...] = jnp.zeros_like(dv_acc)
        k = k_ref[...]
        v = v_ref[...]
        c2 = c_ref[...] * LOG2E
        k_ones = jnp.concatenate([k, jnp.ones((t, 128), BF16)], axis=1)
        q_ones = jnp.ones((tr, 128), BF16)

        def block(qb, masked):
            for r in range(ATT_ROW_CHAINS):
                rows = pl.ds(pl.multiple_of(qb * t + r * tr, tr), tr)
                q = q_ref[rows, :]
                dout = do_ref[rows, :]
                sc = lax.dot_general(q, k, (_DIMS["nt"], ((), ())), preferred_element_type=F32) * (scale * LOG2E) - c2
                if masked:
                    row = lax.broadcasted_iota(jnp.int32, (tr, t), 0) + r * tr
                    col = lax.broadcasted_iota(jnp.int32, (tr, t), 1)
                    sc = jnp.where(col <= row, sc, -jnp.inf)
                p = jnp.exp2(sc - jnp.tile(lse_ref[rows, :], (1, rep)))
                dp = lax.dot_general(dout, v, (_DIMS["nt"], ((), ())), preferred_element_type=F32)
                dsb = (p * (dp - jnp.tile(dl_ref[rows, :], (1, rep)))).astype(BF16)
                dv_acc[...] += lax.dot_general(p.astype(BF16), dout, (_DIMS["tn"], ((), ())),
                                               preferred_element_type=F32)
                dk_acc[...] += lax.dot_general(dsb, jnp.concatenate([q, q_ones], axis=1), (_DIMS["tn"], ((), ())),
                                               preferred_element_type=F32)
                dq_acc[rows, :] += jnp.dot(dsb, k_ones, preferred_element_type=F32)

        def full_block(qb, carry):
            block(qb, False)
            return carry

        block(kb, True)
        diag = pl.ds(pl.multiple_of(kb * t, t), t)
        dq_ref[diag, :] = (dq_acc[diag, 0:128] * scale).astype(BF16)
        drow_ref[:, diag] = dq_acc[diag, 128:256].T[0:8, :]
        lax.fori_loop(kb + 1, nq, full_block, 0)
        dk_ref[...] = (dk_acc[:, 0:128] * scale).astype(BF16)
        dcol_ref[...] = dk_acc[:, 128:256].T[0:8, :]
        dv_ref[...] = dv_acc[...].astype(BF16)

    kside = lambda off: pl.BlockSpec((t, 128), lambda h, kb: (kb, off + h))
    whole = pl.BlockSpec((s, 128), lambda h, kb: (0, h))
    hw = n_heads * 128
    return _pcall(
        body, name="attn_bwd", grid=(n_heads, nq),
        in_specs=[whole, kside(n_heads), kside(2 * n_heads), whole, whole, whole,
                  pl.BlockSpec((None, None, 1, t), lambda h, kb: (h, kb, 0, 0))],
        out_specs=(whole, kside(0), kside(0), pl.BlockSpec((None, 8, s), lambda h, kb: (h, 0, 0)),
                   pl.BlockSpec((None, 8, t), lambda h, kb: (h, 0, kb))),
        out_shape=(jax.ShapeDtypeStruct((s, hw), BF16), jax.ShapeDtypeStruct((s, hw), BF16),
                   jax.ShapeDtypeStruct((s, hw), BF16), jax.ShapeDtypeStruct((n_heads, 8, s), F32),
                   jax.ShapeDtypeStruct((n_heads, 8, s), F32)),
        scratch_shapes=[pltpu.VMEM((s, 256), F32), pltpu.VMEM((t, 256), F32), pltpu.VMEM((t, 128), F32)],
        semantics=("parallel", "arbitrary"), args=(p_main, p_main, p_main, do, lse_rep, delta_rep, cum4), side=side)


def _conv_fwd(u_ext, cw_ref):
    r1 = pltpu.roll(u_ext, 1, 0)
    r2 = pltpu.roll(u_ext, 2, 0)
    conv = cw_ref[2:3, :] * u_ext + cw_ref[1:2, :] * r1 + cw_ref[0:1, :] * r2
    return conv[HALO:], r1[HALO:], r2[HALO:]


def _pool_z(pu, pu_halo, row0, tm, pg):
    ext = jnp.concatenate([pu_halo, pu], axis=0)
    t1 = (row0 + lax.broadcasted_iota(jnp.int32, (tm, pg), 0) + 1).astype(F32)
    zs = []
    for g, w in enumerate(POOL_WINDOWS):
        sm = ext[:, g * pg:(g + 1) * pg]
        sh = 1
        while sh < w:
            sm = sm + pltpu.roll(sm, sh, 0)
            sh *= 2
        mean = sm[HALO:] / jnp.minimum(t1, float(w))
        zs.append(mean - pu[:, g * pg:(g + 1) * pg])
    return zs


def _mix_specs(tm, s):
    per = tm // HALO
    last = s // HALO - 1
    cur = lambda w, j: pl.BlockSpec((tm, w), lambda i: (i, j))
    prev = lambda j: pl.BlockSpec((HALO, 512), lambda i: (jnp.maximum(i * per - 1, 0), j))
    nxt = lambda j: pl.BlockSpec((HALO, 512), lambda i: (jnp.minimum((i + 1) * per, last), j))
    full = lambda shape: pl.BlockSpec(shape, lambda i: (0,) * len(shape))
    return cur, prev, nxt, full


def _mix_fwd(p_main, o, conv_w, pool_w, pool_scale, tm):
    s = p_main.shape[0]
    d = 2048
    pg = 128
    cur, prev, nxt, full = _mix_specs(tm, s)

    def body(ga_ref, cb_ref, cc_ref, ch_ref, gc_ref, pu_ref, gp_ref, cch_ref, chh_ref, puh_ref,
             o_ref, cw_ref, pw_ref, ps_ref, y_ref):
        i = pl.program_id(0)
        first = i == 0
        ga = ga_ref[...].astype(F32)
        y_ref[:, 0:1024] = (o_ref[...].astype(F32) * ga * _sigmoid(ga)).astype(BF16)

        u = cc_ref[...].astype(F32) * ch_ref[...].astype(F32)
        uh = jnp.where(first, 0.0, cch_ref[...].astype(F32) * chh_ref[...].astype(F32))
        conv, _, _ = _conv_fwd(jnp.concatenate([uh, u], axis=0), cw_ref)
        gc = gc_ref[...].astype(F32)
        y_ref[:, 1024:1536] = (cb_ref[...].astype(F32) * conv * gc * _sigmoid(gc)).astype(BF16)

        pu = pu_ref[...].astype(F32)
        puh = jnp.where(first, 0.0, puh_ref[...].astype(F32))
        zs = _pool_z(pu, puh, i * tm, tm, pg)
        gp = gp_ref[...].astype(F32)
        gate = gp * _sigmoid(gp) * ps_ref[...]
        for g in range(4):
            r = jnp.dot(zs[g].astype(BF16), pw_ref[g], preferred_element_type=F32)
            y_ref[:, 1536 + g * pg:1536 + (g + 1) * pg] = (r * gate[:, g * pg:(g + 1) * pg]).astype(BF16)

    return pl.pallas_call(
        body, name="mix_fwd", grid=(s // tm,),
        in_specs=[cur(1024, 3), cur(512, 8), cur(512, 9), cur(512, 10), cur(512, 11), cur(512, 12), cur(512, 13),
                  prev(9), prev(10), prev(12),
                  cur(1024, 0), full((3, 512)), full((4, pg, pg)), full((1, 512))],
        out_specs=pl.BlockSpec((tm, d), lambda i: (i, 0)),
        out_shape=jax.ShapeDtypeStruct((s, d), BF16),
        compiler_params=_params(("parallel",)),
    )(*([p_main] * 10), o, conv_w, pool_w, pool_scale)


def _mix_bwd(p_main, o, dy, conv_w, pool_w, pool_scale, tm):
    s = p_main.shape[0]
    pg = 128
    n_heads = 8
    cur, prev, nxt, full = _mix_specs(tm, s)
    nblk = s // tm
    n_ext = tm + HALO

    def silu_and_grad(x):
        sg = _sigmoid(x)
        return x * sg, sg * (1.0 + x * (1.0 - sg))

    def body(ga_ref, cb_ref, cc_ref, ch_ref, gc_ref, pu_ref, gp_ref, cch_ref, chh_ref, puh_ref,
             cbn_ref, gcn_ref, gpn_ref, o_ref, dy_ref, dycn_ref, dypn_ref, cw_ref, pw_ref, ps_ref,
             do_ref, dl_ref, dp_ref, dsm_ref, dpw_ref):
        i = pl.program_id(0)
        first = i == 0
        last = i == nblk - 1

        @pl.when(first)
        def _():
            dsm_ref[...] = jnp.zeros_like(dsm_ref)
            dpw_ref[...] = jnp.zeros_like(dpw_ref)

        ga = ga_ref[...].astype(F32)
        of = o_ref[...].astype(F32)
        dya = dy_ref[:, 0:1024]
        sa, dsa = silu_and_grad(ga)
        dout = dya * sa
        do_ref[...] = dout.astype(BF16)
        dp_ref[:, 0:1024] = (dya * of * dsa).astype(BF16)
        prod = dout * of
        for h in range(n_heads):
            dsum = jnp.sum(prod[:, h * 128:(h + 1) * 128], axis=1, keepdims=True)
            dl_ref[:, h * 128:(h + 1) * 128] = jnp.broadcast_to(dsum, (tm, 128))

        cb = cb_ref[...].astype(F32)
        cc = cc_ref[...].astype(F32)
        ch = ch_ref[...].astype(F32)
        gc = gc_ref[...].astype(F32)
        u = cc * ch
        uh = jnp.where(first, 0.0, cch_ref[...].astype(F32) * chh_ref[...].astype(F32))
        conv, u1, u2 = _conv_fwd(jnp.concatenate([uh, u], axis=0), cw_ref)
        sc_, dsc = silu_and_grad(gc)
        dyc = dy_ref[:, 1024:1536]
        dp_ref[:, 1024:1536] = (dyc * conv * sc_).astype(BF16)
        dp_ref[:, 2560:3072] = (dyc * cb * conv * dsc).astype(BF16)
        dconv = dyc * cb * sc_
        gcn = gcn_ref[...].astype(F32)
        dconv_n = jnp.where(last, 0.0, dycn_ref[...] * cbn_ref[...].astype(F32) * gcn * _sigmoid(gcn))
        dext = jnp.concatenate([dconv, dconv_n], axis=0)
        du = (cw_ref[2:3, :] * dext + cw_ref[1:2, :] * pltpu.roll(dext, n_ext - 1, 0)
              + cw_ref[0:1, :] * pltpu.roll(dext, n_ext - 2, 0))[:tm]
        dp_ref[:, 1536:2048] = (du * ch).astype(BF16)
        dp_ref[:, 2048:2560] = (du * cc).astype(BF16)
        dsm_ref[0:1, :] += jnp.sum(dconv * u2, axis=0, keepdims=True)
        dsm_ref[1:2, :] += jnp.sum(dconv * u1, axis=0, keepdims=True)
        dsm_ref[2:3, :] += jnp.sum(dconv * u, axis=0, keepdims=True)

        pu = pu_ref[...].astype(F32)
        puh = jnp.where(first, 0.0, puh_ref[...].astype(F32))
        zs = _pool_z(pu, puh, i * tm, tm, pg)
        gp = gp_ref[...].astype(F32)
        sp, dsp = silu_and_grad(gp)
        dyp = dy_ref[:, 1536:2048]
        gpn = gpn_ref[...].astype(F32)
        dr_n = jnp.where(last, 0.0, dypn_ref[...] * gpn * _sigmoid(gpn) * ps_ref[...])
        drs = dyp * sp
        dr = drs * ps_ref[...]
        t1 = (i * tm + lax.broadcasted_iota(jnp.int32, (tm, pg), 0) + 1).astype(F32)
        r_parts, dpu_parts = [], []
        for g, w in enumerate(POOL_WINDOWS):
            cols = slice(g * pg, (g + 1) * pg)
            zb = zs[g].astype(BF16)
            r_parts.append(jnp.dot(zb, pw_ref[g], preferred_element_type=F32))
            drb = dr[:, cols].astype(BF16)
            dpw_ref[g] += lax.dot_general(zb, drb, (_DIMS["tn"], ((), ())), preferred_element_type=F32)
            dz = lax.dot_general(drb, pw_ref[g], (_DIMS["nt"], ((), ())), preferred_element_type=F32)
            dz_n = lax.dot_general(dr_n[:, cols].astype(BF16), pw_ref[g], (_DIMS["nt"], ((), ())),
                                   preferred_element_type=F32)
            sm = jnp.concatenate([dz / jnp.minimum(t1, float(w)), dz_n / float(w)], axis=0)
            sh = 1
            while sh < w:
                sm = sm + pltpu.roll(sm, n_ext - sh, 0)
                sh *= 2
            dpu_parts.append(sm[:tm] - dz)
        r = jnp.concatenate(r_parts, axis=1)
        dp_ref[:, 3072:3584] = jnp.concatenate(dpu_parts, axis=1).astype(BF16)
        dp_ref[:, 3584:4096] = (dyp * r * ps_ref[...] * dsp).astype(BF16)
        dsm_ref[3:4, :] += jnp.sum(drs * r, axis=0, keepdims=True)

    dy_next = lambda j: pl.BlockSpec((HALO, 512), lambda i: (jnp.minimum((i + 1) * (tm // HALO), s // HALO - 1), j))
    return pl.pallas_call(
        body, name="mix_bwd", grid=(nblk,),
        in_specs=[cur(1024, 3), cur(512, 8), cur(512, 9), cur(512, 10), cur(512, 11), cur(512, 12), cur(512, 13),
                  prev(9), prev(10), prev(12), nxt(8), nxt(11), nxt(13),
                  cur(1024, 0), pl.BlockSpec((tm, 2048), lambda i: (i, 0)), dy_next(2), dy_next(3),
                  full((3, 512)), full((4, pg, pg)), full((1, 512))],
        out_specs=(pl.BlockSpec((tm, 1024), lambda i: (i, 0)), pl.BlockSpec((tm, 1024), lambda i: (i, 0)),
                   pl.BlockSpec((tm, 4096), lambda i: (i, 0)), full((8, 512)), full((4, pg, pg))),
        out_shape=(jax.ShapeDtypeStruct((s, 1024), BF16), jax.ShapeDtypeStruct((s, 1024), F32),
                   jax.ShapeDtypeStruct((s, 4096), BF16), jax.ShapeDtypeStruct((8, 512), F32),
                   jax.ShapeDtypeStruct((4, pg, pg), F32)),
        compiler_params=_params(("arbitrary",)),
    )(*([p_main] * 13), o, dy, dy, dy, conv_w, pool_w, pool_scale)


def _outproj_ln(y, w_out, x, ln_g, ln_b, alpha, tm, side=None):
    s, d = x.shape

    def body(y_ref, w_ref, x_ref, g_ref, b_ref, xn_ref, xb_ref, pre_ref):
        pre = alpha * x_ref[...] + jnp.dot(y_ref[...], w_ref[...], preferred_element_type=F32)
        mu = jnp.mean(pre, axis=1, keepdims=True)
        cen = pre - mu
        var = jnp.mean(cen * cen, axis=1, keepdims=True)
        xn = cen * lax.rsqrt(var + LN_EPS) * g_ref[...] + b_ref[...]
        pre_ref[...] = pre
        xn_ref[...] = xn
        xb_ref[...] = xn.astype(BF16)

    row = pl.BlockSpec((tm, d), lambda i: (i, 0))
    vec = pl.BlockSpec((1, d), lambda i: (0, 0))
    return _pcall(
        body, name="outproj_ln", grid=(s // tm,),
        in_specs=[row, pl.BlockSpec((d, d), lambda i: (0, 0)), row, vec, vec],
        out_specs=(row, row, row),
        out_shape=(jax.ShapeDtypeStruct((s, d), F32), jax.ShapeDtypeStruct((s, d), BF16),
                   jax.ShapeDtypeStruct((s, d), F32)),
        scratch_shapes=[], semantics=("parallel",), args=(y, w_out, x, ln_g, ln_b), side=side)


def _ln_bwd(dxn, pre, ln_g, tm, side=None):
    s, d = pre.shape

    def body(dx_ref, pre_ref, g_ref, dpre_ref, dpb_ref, dgb_ref):
        @pl.when(pl.program_id(0) == 0)
        def _():
            dgb_ref[...] = jnp.zeros_like(dgb_ref)

        pre_ = pre_ref[...]
        dx = dx_ref[...]
        mu = jnp.mean(pre_, axis=1, keepdims=True)
        cen = pre_ - mu
        var = jnp.mean(cen * cen, axis=1, keepdims=True)
        rstd = lax.rsqrt(var + LN_EPS)
        xhat = cen * rstd
        dxh = dx * g_ref[...]
        dpre = rstd * (dxh - jnp.mean(dxh, axis=1, keepdims=True)
                       - xhat * jnp.mean(dxh * xhat, axis=1, keepdims=True))
        dpre_ref[...] = dpre
        dpb_ref[...] = dpre.astype(BF16)
        dgb_ref[0:1, :] += jnp.sum(dx * xhat, axis=0, keepdims=True)
        dgb_ref[1:2, :] += jnp.sum(dx, axis=0, keepdims=True)

    row = pl.BlockSpec((tm, d), lambda i: (i, 0))
    return _pcall(
        body, name="ln_bwd", grid=(s // tm,),
        in_specs=[row, row, pl.BlockSpec((1, d), lambda i: (0, 0))],
        out_specs=(row, row, pl.BlockSpec((8, d), lambda i: (0, 0))),
        out_shape=(jax.ShapeDtypeStruct((s, d), F32), jax.ShapeDtypeStruct((s, d), BF16),
                   jax.ShapeDtypeStruct((8, d), F32)),
        scratch_shapes=[], semantics=("arbitrary",), args=(dxn, pre, ln_g), side=side)


def _loss_grad(y, target, tm):
    s, d = y.shape

    def body(y_ref, t_ref, dy_ref, loss_ref):
        @pl.when(pl.program_id(0) == 0)
        def _():
            loss_ref[...] = jnp.zeros_like(loss_ref)

        diff = y_ref[...] - t_ref[...]
        dy_ref[...] = diff / d
        loss_ref[...] += 0.5 * jnp.sum(jnp.sum(diff * diff, axis=1, keepdims=True) / d)

    row = pl.BlockSpec((tm, d), lambda i: (i, 0))
    return pl.pallas_call(
        body, name="loss_grad", grid=(s // tm,),
        in_specs=[row, row],
        out_specs=(row, pl.BlockSpec((8, 128), lambda i: (0, 0))),
        out_shape=(jax.ShapeDtypeStruct((s, d), F32), jax.ShapeDtypeStruct((8, 128), F32)),
        compiler_params=_params(("arbitrary",)),
    )(y, target)


def _adamw(w, g, m, v, name, lead_block=None):
    a, r, c = w.shape
    tr = r if lead_block else _tile(r, (256, 128, 64, 32, 16, 8))
    c1 = 1.0 - ADAM_B1 ** ADAM_STEP
    c2 = 1.0 - ADAM_B2 ** ADAM_STEP

    def body(w_ref, g_ref, m_ref, v_ref, d_ref, mo_ref, vo_ref):
        g_ = g_ref[...]
        m_new = ADAM_B1 * m_ref[...] + (1.0 - ADAM_B1) * g_
        v_new = ADAM_B2 * v_ref[...] + (1.0 - ADAM_B2) * (g_ * g_)
        d_ref[...] = -ADAM_LR * ((m_new / c1) / (jnp.sqrt(v_new / c2) + ADAM_EPS) + ADAM_WD * w_ref[...])
        mo_ref[...] = m_new
        vo_ref[...] = v_new

    ta = lead_block or 1
    spec = pl.BlockSpec((ta, tr, c), lambda i, j: (i, j, 0))
    shp = jax.ShapeDtypeStruct(w.shape, F32)
    return pl.pallas_call(
        body, name=name, grid=(a // ta, r // tr), in_specs=[spec] * 4, out_specs=(spec,) * 3, out_shape=(shp,) * 3,
        compiler_params=_params(("parallel", "parallel")),
    )(w, g, m, v)


def _place():
    x, y, c = lax.axis_index("x"), lax.axis_index("y"), lax.axis_index("c")
    return x, y, c, [(1 - x, y), (x, 1 - y), (1 - x, 1 - y)]


def _remote(src, dst, send, recv, k, to):
    return pltpu.make_async_remote_copy(src_ref=src, dst_ref=dst, send_sem=send.at[k], recv_sem=recv.at[k],
                                        device_id=to, device_id_type=MESH)


def _slot(ref, px, py, pc):
    return ref.at[pl.ds(2 * px + py, 1), pl.ds(pc, 1)]


def _side_gather_ici(shard):
    _, two, r, w = shard.shape

    def sends(sins, souts, send, recv):
        x, y, c, chips = _place()
        return [_remote(sins[0].at[:, pl.ds(c, 1)], _slot(souts[0], x, y, c), send, recv, k, (*chip, c))
                for k, chip in enumerate(chips)]

    def start(sins, souts, send, recv):
        for cp in sends(sins, souts, send, recv):
            cp.start()

    def wait(sins, souts, send, recv):
        x, y, c, chips = _place()
        for cp in sends(sins, souts, send, recv):
            cp.wait_send()
        for k, chip in enumerate(chips):
            _remote(_slot(souts[0], *chip, c), _slot(souts[0], *chip, c), send, recv, k, (x, y, c)).wait_recv()

    return _Side([shard], [jax.ShapeDtypeStruct((4, two, r, w), shard.dtype)], 3, start, wait)


def _side_gather_d2d(gathered):
    def sends(souts, send, recv):
        x, y, c, chips = _place()
        return [_remote(_slot(souts[0], *chip, c), _slot(souts[0], *chip, c), send, recv, k, (x, y, 1 - c))
                for k, chip in enumerate(chips)]

    def start(sins, souts, send, recv):
        for cp in sends(souts, send, recv):
            cp.start()

    def wait(sins, souts, send, recv):
        x, y, c, chips = _place()
        for cp in sends(souts, send, recv):
            cp.wait_send()
        for k, chip in enumerate(chips):
            _remote(_slot(souts[0], *chip, 1 - c), _slot(souts[0], *chip, 1 - c), send, recv, k, (x, y, c)).wait_recv()

    return _Side([gathered], [jax.ShapeDtypeStruct(gathered.shape, gathered.dtype)], 3, start, wait, aliases={0: 0})


def _side_swap(g):
    nchip, _, r, w = g.shape

    def copy(sins, souts, send, recv):
        x, y, c, _ = _place()
        return _remote(sins[0].at[:, pl.ds(1 - c, 1)], souts[0], send, recv, 0, (x, y, 1 - c))

    return _Side([g], [jax.ShapeDtypeStruct((nchip, 1, r, w), g.dtype)], 1,
                 lambda *a: copy(*a).start(), lambda *a: copy(*a).wait())


def _side_exchange(hsum):
    _, _, r, w = hsum.shape

    def copies(sins, souts, send, recv):
        x, y, c, chips = _place()
        return [_remote(sins[0].at[pl.ds(2 * px + py, 1)], souts[0].at[pl.ds(k, 1)], send, recv, k, (px, py, c))
                for k, (px, py) in enumerate(chips)]

    def start(*a):
        for cp in copies(*a):
            cp.start()

    def wait(*a):
        for cp in copies(*a):
            cp.wait()

    return _Side([hsum], [jax.ShapeDtypeStruct((3, 1, r, w), hsum.dtype)], 3, start, wait)


def _side_share(f):
    def copy(sins, souts, send, recv):
        x, y, c, _ = _place()
        return _remote(sins[0], souts[0], send, recv, 0, (x, y, 1 - c))

    return _Side([f], [jax.ShapeDtypeStruct(f.shape, f.dtype)], 1,
                 lambda *a: copy(*a).start(), lambda *a: copy(*a).wait())


def _own_slot_filled(gathered, shard):
    chip = 2 * lax.axis_index("x") + lax.axis_index("y")
    return lax.dynamic_update_slice(gathered, shard, (chip, 0, 0, 0))


def _both_parts(mine, theirs):
    mine_first = lax.axis_index("c") == 0
    return jnp.concatenate([jnp.where(mine_first, mine, theirs), jnp.where(mine_first, theirs, mine)], axis=0)


def _plane_tile(r, cdim):
    tr = _tile(r, (256, 128))
    if tr != r:
        return tr, cdim
    return r, _tile(cdim, (256, 128))


def _add_halves(g, recv, core, name):
    nchip, nl, r, cdim = g.shape
    half = nl // 2
    tr, tc = _plane_tile(r, cdim)

    def body(c_ref, g_ref, r_ref, o_ref):
        o_ref[...] = (g_ref[...].astype(F32) + r_ref[...].astype(F32)).astype(BF16)

    blk = (1, 1, tr, tc)
    return pl.pallas_call(
        body, name=name,
        grid_spec=pltpu.PrefetchScalarGridSpec(
            num_scalar_prefetch=1, grid=(nchip, half, r // tr, cdim // tc),
            in_specs=[pl.BlockSpec(blk, lambda j, l, i, k, c_ref: (j, c_ref[0] * half + l, i, k)),
                      pl.BlockSpec(blk, lambda j, l, i, k, c_ref: (j, l, i, k))],
            out_specs=pl.BlockSpec(blk, lambda j, l, i, k, c_ref: (j, l, i, k))),
        out_shape=jax.ShapeDtypeStruct((nchip, half, r, cdim), BF16),
        compiler_params=_params(("parallel",) * 4),
    )(core, g, recv)


def _add_partials(hsum, others, chip, name):
    nchip, half, r, cdim = hsum.shape
    tr, tc = _plane_tile(r, cdim)

    def body(i_ref, h_ref, a_ref, b_ref, c_ref, o_ref):
        o_ref[...] = ((h_ref[...].astype(F32) + a_ref[...].astype(F32))
                      + (b_ref[...].astype(F32) + c_ref[...].astype(F32)))[0]

    blk = (1, 1, tr, tc)
    other = lambda n: pl.BlockSpec(blk, lambda l, i, k, i_ref: (n, l, i, k))
    return pl.pallas_call(
        body, name=name,
        grid_spec=pltpu.PrefetchScalarGridSpec(
            num_scalar_prefetch=1, grid=(half, r // tr, cdim // tc),
            in_specs=[pl.BlockSpec(blk, lambda l, i, k, i_ref: (i_ref[0], l, i, k)), other(0), other(1), other(2)],
            out_specs=pl.BlockSpec((1, tr, tc), lambda l, i, k, i_ref: (l, i, k))),
        out_shape=jax.ShapeDtypeStruct((half, r, cdim), F32),
        compiler_params=_params(("parallel",) * 3),
    )(chip, hsum, others, others, others)


def _all_gather_small(v, name):
    m_per, n = v.shape

    def body(x_ref, out_ref, send_sems, recv_sems, local_sem):
        x, y, c, chips = _place()
        me, sib = (x, y, c), (x, y, 1 - c)

        def rows(px, py, pc):
            return out_ref.at[pl.ds((4 * px + 2 * py + pc) * m_per, m_per), :]

        def copy(k, block, to, src=None):
            return pltpu.make_async_remote_copy(
                src_ref=rows(*block) if src is None else src, dst_ref=rows(*block), send_sem=send_sems.at[k],
                recv_sem=recv_sems.at[k], device_id=to, device_id_type=MESH)

        mine = pltpu.make_async_copy(x_ref, rows(*me), local_sem)
        mine.start()
        first = [copy(0, me, sib, src=x_ref)]
        first += [copy(1 + j, me, (*chip, c), src=x_ref) for j, chip in enumerate(chips)]
        for cp in first:
            cp.start()
        passed = [copy(4 + j, (*chip, c), sib) for j, chip in enumerate(chips)]
        for j, chip in enumerate(chips):
            copy(1 + j, (*chip, c), me).wait_recv()
            passed[j].start()
        copy(0, sib, me).wait_recv()
        for j, chip in enumerate(chips):
            copy(4 + j, (*chip, 1 - c), me).wait_recv()
        for cp in first + passed:
            cp.wait_send()
        mine.wait()

    return pl.pallas_call(
        body, name=name, out_shape=jax.ShapeDtypeStruct((8 * m_per, n), v.dtype),
        in_specs=[pl.BlockSpec(memory_space=pltpu.VMEM)], out_specs=pl.BlockSpec(memory_space=pltpu.VMEM),
        scratch_shapes=[pltpu.SemaphoreType.DMA((7,)), pltpu.SemaphoreType.DMA((7,)), pltpu.SemaphoreType.DMA],
        compiler_params=pltpu.CompilerParams(vmem_limit_bytes=VMEM_LIMIT),
    )(v)


def _sum_blocks(v, nblk, name):
    m_per = v.shape[0] // nblk

    def body(v_ref, o_ref):
        acc = v_ref[0:m_per, :]
        for j in range(1, nblk):
            acc = acc + v_ref[j * m_per:(j + 1) * m_per, :]
        o_ref[...] = acc

    return pl.pallas_call(
        body, name=name, out_shape=jax.ShapeDtypeStruct((m_per, 128), F32),
        in_specs=[pl.BlockSpec(memory_space=pltpu.VMEM)], out_specs=pl.BlockSpec(memory_space=pltpu.VMEM),
        compiler_params=pltpu.CompilerParams(vmem_limit_bytes=VMEM_LIMIT),
    )(v)


def _rows128(a, rows=None):
    flat = a.reshape(-1).astype(F32)
    need = -(-flat.shape[0] // 128)
    rows = rows or -(-need // 8) * 8
    return jnp.pad(flat, (0, rows * 128 - flat.shape[0])).reshape(rows, 128)


def kernel(x, w_in, b_f, conv_w, pool_w, pool_scale, w_out, ln_g, ln_b, loss_target, m_w_in, m_b_f, m_conv_w, m_pool_w, m_pool_scale, m_w_out, m_ln_g, m_ln_b, v_w_in, v_b_f, v_conv_w, v_pool_w, v_pool_scale, v_w_out, v_ln_g, v_ln_b):
    depth, d, in_shard = w_in.shape
    s = x.shape[1]
    att = d // 2
    n_heads = att // HEAD_DIM
    cw = d // 4
    pw = d - att - cw
    in_w = 4 * att + n_heads + 4 * cw + 2 * pw
    assert in_shard * 4 == in_w and (d, n_heads, cw, pw) == (2048, 8, 512, 512) and s % 512 == 0
    main_w = in_w - n_heads
    fg0 = 4 * att
    alpha = (2 * depth) ** 0.25
    t_att = 512
    tm_mix = 256

    cx, cy, cc = lax.axis_index("x"), lax.axis_index("y"), lax.axis_index("c")
    chip = 2 * cx + cy
    core_arr = jnp.reshape(cc, (1,)).astype(jnp.int32)
    chip_arr = jnp.reshape(chip, (1,)).astype(jnp.int32)
    out_rows = w_out.shape[1]
    r_comb = out_rows + in_shard
    pad_rows = -r_comb % 32
    r_half = (r_comb + pad_rows) // 2

    to_rows = lambda a: jnp.transpose(a, (2, 0, 1))
    from_rows = lambda a: jnp.transpose(a, (1, 2, 0))
    wt_b = jnp.transpose(to_rows(w_in).astype(BF16), (1, 0, 2))
    in_pad = -in_shard % 16
    to_tile = lambda a: jnp.pad(a, ((0, 0), (0, in_pad), (0, 0)))
    comb = jnp.concatenate([w_out.astype(BF16), to_tile(wt_b), jnp.zeros((depth, pad_rows - in_pad, d), BF16)], axis=1)
    shards = comb.reshape(depth, 2, r_half, d)

    def weights_of(gathered):
        full = gathered.reshape(4, 2 * r_half, d)
        wt = full[:, out_rows:r_comb].reshape(in_w, d)
        wt_main = jnp.concatenate([wt[:fg0], wt[fg0 + n_heads:]], axis=0)
        wt_fg = jnp.pad(wt[fg0:fg0 + n_heads], ((0, 128 - n_heads), (0, 0)))
        return wt_main, wt_fg, full[:, :out_rows].reshape(d, d)

    first = shards[0][None]
    arrived = _run_side(_side_gather_ici(first), "gather_first_ici")[0]
    arrived = _run_side(_side_gather_d2d(arrived), "gather_first_d2d")[0]
    weights = weights_of(_own_slot_filled(arrived, first))

    cw_rows = _rows128(conv_w)
    cw_all = _all_gather_small(cw_rows, "gather_conv_w").reshape(4, 2, cw_rows.shape[0] * 128)
    conv_full = cw_all[:, 0, :depth * 3 * (cw // 4)].reshape(4, depth, 3, cw // 4)
    conv_full = jnp.moveaxis(conv_full, 0, 2).reshape(depth, 3, cw)
    pool_b = pool_w.astype(BF16)

    xf = x[0]
    xb = xf.astype(BF16)
    saved = []
    for l in range(depth):
        wt_main, wt_fg, w_out_l = weights
        nxt = shards[l + 1][None] if l + 1 < depth else None
        p_main = _matmul(xb, wt_main, "nt", BF16, "proj_main")
        fg = _matmul(xb, wt_fg, "nt", F32, "proj_fg")
        fg_t = fg[:, :n_heads].T
        b_col = b_f[l].reshape(n_heads, 1)
        cum4 = _gate_fwd(fg_t, b_col).reshape(n_heads, s // t_att, 1, t_att)
        if nxt is None:
            o, lse = _attn_fwd(p_main, cum4, n_heads, t_att)
        else:
            (o, lse), (arrived,) = _attn_fwd(p_main, cum4, n_heads, t_att, side=_side_gather_ici(nxt))
        y = _mix_fwd(p_main, o, conv_full[l], pool_b[l], pool_scale[l].reshape(1, pw), tm_mix)
        ln_args = (y, w_out_l, xf, ln_g[l].reshape(1, d), ln_b[l].reshape(1, d), alpha, tm_mix)
        if nxt is None:
            x_new, xb_new, pre = _outproj_ln(*ln_args)
        else:
            (x_new, xb_new, pre), (arrived,) = _outproj_ln(*ln_args, side=_side_gather_d2d(arrived))
            weights = weights_of(_own_slot_filled(arrived, nxt))
        saved.append((xb, p_main, fg_t, b_col, cum4, o, lse, y, pre, wt_main, wt_fg, w_out_l))
        xf, xb = x_new, xb_new

    dx, loss_blk = _loss_grad(xf, loss_target[0], tm_mix)

    g_bf, g_sm, g_pw, g_ln, finished = ([None] * depth for _ in range(5))
    pending = None
    for l in reversed(range(depth)):
        xb, p_main, fg_t, b_col, cum4, o, lse, y, pre, wt_main, wt_fg, w_out_l = saved[l]
        if pending is None:
            dpre, dpre_b, g_ln[l] = _ln_bwd(dx, pre, ln_g[l].reshape(1, d), tm_mix)
        else:
            (dpre, dpre_b, g_ln[l]), (recv,) = _ln_bwd(dx, pre, ln_g[l].reshape(1, d), tm_mix,
                                                       side=_side_swap(pending))
            hsum = _add_halves(pending, recv, core_arr, "rs_add_halves")
        dy = _matmul(dpre_b, w_out_l, "nt", F32, "d_y")
        gw_out_l = _matmul(y, dpre_b, "tn", BF16, "d_w_out")
        do, delta, dp_rest, g_sm[l], g_pw[l] = _mix_bwd(p_main, o, dy, conv_full[l], pool_b[l],
                                                        pool_scale[l].reshape(1, pw), tm_mix)
        if pending is None:
            dq, dk, dv, drow, dcol = _attn_bwd(p_main, do, lse, delta, cum4, n_heads, t_att)
        else:
            (dq, dk, dv, drow, dcol), (others,) = _attn_bwd(p_main, do, lse, delta, cum4, n_heads, t_att,
                                                            side=_side_exchange(hsum))
            mine = _add_partials(hsum, others, chip_arr, "rs_add_partials")
        dfg_t, g_bf[l] = _gate_bwd(drow, dcol, fg_t, b_col)
        dfg_b = jnp.pad(dfg_t.T, ((0, 0), (0, 128 - n_heads))).astype(BF16)
        dp_main = jnp.concatenate([dq, dk, dv, dp_rest], axis=1)
        if pending is None:
            gw_main_l = _matmul(dp_main, xb, "tn", BF16, "d_w_main")
        else:
            gw_main_l, (theirs,) = _matmul(dp_main, xb, "tn", BF16, "d_w_main", side=_side_share(mine))
            finished[l + 1] = _both_parts(mine, theirs)
        gw_fg_l = _matmul(dfg_b, xb, "tn", BF16, "d_w_fg")
        gw_in_l = jnp.concatenate([gw_main_l[:fg0], gw_fg_l[:n_heads], gw_main_l[fg0:]], axis=0)
        by_chip = jnp.concatenate([gw_out_l.reshape(4, out_rows, d), to_tile(gw_in_l.reshape(4, in_shard, d)),
                                   jnp.zeros((4, pad_rows - in_pad, d), BF16)], axis=1)
        pending = by_chip.reshape(4, 2, r_half, d)
        dx = _matmul(dp_main, wt_main, "nn", F32, "d_x_main", add=dpre, add_scale=alpha)
        dx = _matmul(dfg_b, wt_fg, "nn", F32, "d_x_fg", add=dx)

    recv = _run_side(_side_swap(pending), "rs_swap_last")[0]
    hsum = _add_halves(pending, recv, core_arr, "rs_add_halves")
    others = _run_side(_side_exchange(hsum), "rs_exchange_last")[0]
    mine = _add_partials(hsum, others, chip_arr, "rs_add_partials")
    theirs = _run_side(_side_share(mine), "rs_share_last")[0]
    finished[0] = _both_parts(mine, theirs)
    grads = jnp.stack(finished).reshape(depth, 2 * r_half, d)
    grad_w_out = grads[:, :out_rows]
    grad_rows = jnp.transpose(grads[:, out_rows:r_comb], (1, 0, 2))
    grad_w_in = from_rows(grad_rows)

    g_sm = jnp.stack(g_sm)
    g_ln = jnp.stack(g_ln)
    parts = [jnp.stack(g_pw), g_sm[:, 0:3, :], g_sm[:, 3, :], g_ln[:, 0, :], g_ln[:, 1, :],
             jnp.stack(g_bf)[:, :, 0], loss_blk[0, 0]]
    packed = jnp.concatenate([_rows128(p) for p in parts], axis=0)
    summed = _sum_blocks(_all_gather_small(packed, "gather_small_grads"), 8, "sum_small_grads")
    outs, off = [], 0
    for p in parts:
        nr = _rows128(p).shape[0]
        outs.append(summed[off:off + nr].reshape(-1)[:p.size].reshape(p.shape))
        off += nr
    grad_pool_w, gconv_full, grad_pool_scale, grad_ln_g, grad_ln_b, grad_b_f, loss = outs
    grad_conv_w = lax.dynamic_slice_in_dim(gconv_full, chip * (cw // 4), cw // 4, axis=2)

    d_w_in, nm_w_in, nv_w_in = (from_rows(a) for a in _adamw(
        to_rows(w_in), grad_rows, to_rows(m_w_in), to_rows(v_w_in), "adamw_w_in", lead_block=46))
    d_w_out, nm_w_out, nv_w_out = _adamw(w_out, grad_w_out, m_w_out, v_w_out, "adamw_w_out")
    small = [(b_f, grad_b_f, m_b_f, v_b_f), (conv_w, grad_conv_w, m_conv_w, v_conv_w),
             (pool_w, grad_pool_w, m_pool_w, v_pool_w), (pool_scale, grad_pool_scale, m_pool_scale, v_pool_scale),
             (ln_g, grad_ln_g, m_ln_g, v_ln_g), (ln_b, grad_ln_b, m_ln_b, v_ln_b)]
    pk = [jnp.concatenate([_rows128(t[j]) for t in small], axis=0)[None] for j in range(4)]
    sm_out = _adamw(pk[0], pk[1], pk[2], pk[3], "adamw_small")
    res = {j: [] for j in range(3)}
    off = 0
    for t in small:
        nr = _rows128(t[0]).shape[0]
        for j in range(3):
            res[j].append(sm_out[j][0, off:off + nr].reshape(-1)[:t[0].size].reshape(t[0].shape))
        off += nr
    (d_b_f, d_conv_w, d_pool_w, d_pool_scale, d_ln_g, d_ln_b) = res[0]
    (nm_b_f, nm_conv_w, nm_pool_w, nm_pool_scale, nm_ln_g, nm_ln_b) = res[1]
    (nv_b_f, nv_conv_w, nv_pool_w, nv_pool_scale, nv_ln_g, nv_ln_b) = res[2]

    return (loss, dx[None], grad_w_in, grad_b_f, grad_conv_w, grad_pool_w, grad_pool_scale, grad_w_out,
            grad_ln_g, grad_ln_b,
            d_w_in, d_b_f, d_conv_w, d_pool_w, d_pool_scale, d_w_out, d_ln_g, d_ln_b,
            nm_w_in, nm_b_f, nm_conv_w, nm_pool_w, nm_pool_scale, nm_w_out, nm_ln_g, nm_ln_b,
            nv_w_in, nv_b_f, nv_conv_w, nv_pool_w, nv_pool_scale, nv_w_out, nv_ln_g, nv_ln_b)
```

```python
import functools

import jax
import jax.numpy as jnp
from jax import lax
from jax.experimental import pallas as pl
from jax.experimental.pallas import tpu as pltpu

F32 = jnp.float32
BF16 = jnp.bfloat16
MESH = pl.DeviceIdType.MESH

HEAD_DIM = 128
POOL_WINDOWS = (2, 4, 8, 16)
HALO = 16
LN_EPS = 1e-5
ADAM_LR = 0.001
ADAM_B1 = 0.9
ADAM_B2 = 0.999
ADAM_EPS = 1e-08
ADAM_WD = 0.01
ADAM_STEP = 10
VMEM_LIMIT = 56 * 1024 * 1024


def _params(sem, vmem=VMEM_LIMIT):
    return pltpu.CompilerParams(dimension_semantics=sem, vmem_limit_bytes=vmem)


def _tile(n, prefs):
    for t in prefs:
        if n % t == 0:
            return t
    return n


def _sigmoid(x):
    return 1.0 / (1.0 + jnp.exp(-x))


def _hbm_spec():
    return pl.BlockSpec(memory_space=pltpu.HBM)


class _Side:
    def __init__(self, inputs, out_shapes, n_sems, start, wait, aliases=None):
        self.inputs, self.out_shapes, self.n_sems = list(inputs), list(out_shapes), n_sems
        self.start, self.wait, self.aliases = start, wait, dict(aliases or {})


def _pcall(body, *, name, grid, in_specs, out_specs, out_shape, scratch_shapes, semantics, args, side=None):
    if side is None:
        return pl.pallas_call(
            body, name=name, grid=grid, in_specs=in_specs, out_specs=out_specs, out_shape=out_shape,
            scratch_shapes=scratch_shapes, compiler_params=_params(semantics))(*args)
    single = not isinstance(out_shape, (tuple, list))
    out_specs_l = [out_specs] if single else list(out_specs)
    out_shape_l = [out_shape] if single else list(out_shape)
    n_i, n_si, n_o, n_so, n_s = len(in_specs), len(side.inputs), len(out_shape_l), len(side.out_shapes), len(scratch_shapes)

    def wrapped(*refs):
        ins, sins = refs[:n_i], refs[n_i:n_i + n_si]
        o0 = n_i + n_si
        outs, souts = refs[o0:o0 + n_o], refs[o0 + n_o:o0 + n_o + n_so]
        s0 = o0 + n_o + n_so
        scr, (send, recv) = refs[s0:s0 + n_s], refs[s0 + n_s:]
        first = pl.program_id(0) == 0
        last = pl.program_id(0) == grid[0] - 1
        for ax in range(1, len(grid)):
            first = jnp.logical_and(first, pl.program_id(ax) == 0)
            last = jnp.logical_and(last, pl.program_id(ax) == grid[ax] - 1)

        @pl.when(first)
        def _():
            side.start(sins, souts, send, recv)

        body(*ins, *outs, *scr)

        @pl.when(last)
        def _():
            side.wait(sins, souts, send, recv)

    res = pl.pallas_call(
        wrapped, name=name, grid=grid,
        in_specs=list(in_specs) + [_hbm_spec()] * n_si,
        out_specs=tuple(out_specs_l + [_hbm_spec()] * n_so),
        out_shape=tuple(out_shape_l + side.out_shapes),
        scratch_shapes=list(scratch_shapes) + [pltpu.SemaphoreType.DMA((side.n_sems,))] * 2,
        input_output_aliases={n_i + i: n_o + o for i, o in side.aliases.items()},
        compiler_params=_params(("arbitrary",) * len(grid)),
    )(*args, *side.inputs)
    main = res[:n_o]
    return (main[0] if single else tuple(main)), tuple(res[n_o:])


def _run_side(side, name):
    n_si, n_so = len(side.inputs), len(side.out_shapes)

    def body(*refs):
        sins, souts = refs[:n_si], refs[n_si:n_si + n_so]
        send, recv = refs[n_si + n_so:]
        side.start(sins, souts, send, recv)
        side.wait(sins, souts, send, recv)

    return pl.pallas_call(
        body, name=name, out_shape=tuple(side.out_shapes),
        in_specs=[_hbm_spec()] * n_si, out_specs=tuple([_hbm_spec()] * n_so),
        scratch_shapes=[pltpu.SemaphoreType.DMA((side.n_sems,))] * 2,
        input_output_aliases=side.aliases,
    )(*side.inputs)


_DIMS = {"nn": ((1,), (0,)), "nt": ((1,), (1,)), "tn": ((0,), (0,))}


def _matmul(a, b, form, out_dtype, name, add=None, add_scale=1.0, tm=None, tn=None, tk=None, side=None):
    if form == "nn":
        (m, k), (_, n) = a.shape, b.shape
    elif form == "nt":
        (m, k), (n, _) = a.shape, b.shape
    else:
        (k, m), (_, n) = a.shape, b.shape
    tm = tm or _tile(m, (1024, 512, 256, 128))
    tn = tn or _tile(n, (1024, 512, 256, 128))
    tk = tk or _tile(k, (2048, 1792, 1024, 512, 256, 128))
    nk = k // tk
    if form == "tn":
        a_spec = pl.BlockSpec((tk, tm), lambda i, j, kk: (kk, i))
    else:
        a_spec = pl.BlockSpec((tm, tk), lambda i, j, kk: (i, kk))
    if form == "nt":
        b_spec = pl.BlockSpec((tn, tk), lambda i, j, kk: (j, kk))
    else:
        b_spec = pl.BlockSpec((tk, tn), lambda i, j, kk: (kk, j))
    o_spec = pl.BlockSpec((tm, tn), lambda i, j, kk: (i, j))
    dims = (_DIMS[form], ((), ()))
    has_add = add is not None

    def finish(r, add_ref, o_ref):
        if has_add:
            r = r + add_scale * add_ref[...]
        o_ref[...] = r.astype(out_dtype)

    def body_one_pass(a_ref, b_ref, *rest):
        add_ref, o_ref = rest if has_add else (None,) + rest
        finish(lax.dot_general(a_ref[...], b_ref[...], dims, preferred_element_type=F32), add_ref, o_ref)

    def body_accumulate(a_ref, b_ref, *rest):
        add_ref, o_ref, acc = rest if has_add else (None,) + rest
        kk = pl.program_id(2)
        part = lax.dot_general(a_ref[...], b_ref[...], dims, preferred_element_type=F32)

        @pl.when(kk == 0)
        def _():
            acc[...] = part

        @pl.when(jnp.logical_and(kk > 0, kk < nk - 1))
        def _():
            acc[...] += part

        @pl.when(kk == nk - 1)
        def _():
            finish(acc[...] + part, add_ref, o_ref)

    in_specs = [a_spec, b_spec] + ([o_spec] if has_add else [])
    args = (a, b) + ((add,) if has_add else ())
    return _pcall(
        body_one_pass if nk == 1 else body_accumulate, name=name, grid=(m // tm, n // tn, nk),
        in_specs=in_specs, out_specs=o_spec,
        out_shape=jax.ShapeDtypeStruct((m, n), out_dtype),
        scratch_shapes=[] if nk == 1 else [pltpu.VMEM((tm, tn), F32)],
        semantics=("parallel", "parallel", "arbitrary"), args=args, side=side)


def _tri(n, upper):
    r = lax.broadcasted_iota(jnp.int32, (n, n), 0)
    c = lax.broadcasted_iota(jnp.int32, (n, n), 1)
    return jnp.where((r <= c) if upper else (r >= c), 1.0, 0.0).astype(F32)


def _gate_fwd(fg_t, b_col):
    h, s = fg_t.shape
    nb = s // 128

    def body(fg_ref, b_ref, cum_ref):
        u = _tri(128, True)
        carry = jnp.zeros((h, 128), F32)
        for j in range(nb):
            z = fg_ref[:, j * 128:(j + 1) * 128] + b_ref[...]
            logf = -(jnp.maximum(-z, 0.0) + jnp.log(1.0 + jnp.exp(-jnp.abs(z))))
            c = jnp.dot(logf, u, precision=lax.Precision.HIGHEST, preferred_element_type=F32) + carry
            cum_ref[:, j * 128:(j + 1) * 128] = c
            carry = jnp.broadcast_to(c[:, 127:128], (h, 128))

    return pl.pallas_call(
        body, name="gate_fwd", out_shape=jax.ShapeDtypeStruct((h, s), F32),
        in_specs=[pl.BlockSpec(memory_space=pltpu.VMEM)] * 2,
        out_specs=pl.BlockSpec(memory_space=pltpu.VMEM),
    )(fg_t, b_col)


def _gate_bwd(drow_t, dcol_t, fg_t, b_col):
    h, s = fg_t.shape
    nb = s // 128

    def body(dr_ref, dc_ref, fg_ref, b_ref, dfg_ref, db_ref):
        low = _tri(128, False)
        carry = jnp.zeros((h, 128), F32)
        db = jnp.zeros((h, 128), F32)
        for j in reversed(range(nb)):
            sl = slice(j * 128, (j + 1) * 128)
            r = jnp.dot(dr_ref[:, 0, sl] - dc_ref[:, 0, sl], low, precision=lax.Precision.HIGHEST,
                        preferred_element_type=F32) + carry
            carry = jnp.broadcast_to(r[:, 0:1], (h, 128))
            z = fg_ref[:, sl] + b_ref[...]
            dfg = r / (1.0 + jnp.exp(z))
            dfg_ref[:, sl] = dfg
            db = db + dfg
        db_ref[...] = jnp.broadcast_to(jnp.sum(db, axis=1, keepdims=True), (h, 128))

    return pl.pallas_call(
        body, name="gate_bwd",
        out_shape=(jax.ShapeDtypeStruct((h, s), F32), jax.ShapeDtypeStruct((h, 128), F32)),
        in_specs=[pl.BlockSpec(memory_space=pltpu.VMEM)] * 4,
        out_specs=(pl.BlockSpec(memory_space=pltpu.VMEM),) * 2,
    )(drow_t, dcol_t, fg_t, b_col)


LOG2E = 1.4426950408889634
ATT_ROW_CHAINS = 1


def _attn_fwd(p_main, cum4, n_heads, t, side=None):
    s = p_main.shape[0]
    nq = s // t
    scale2 = HEAD_DIM ** -0.5 * LOG2E
    rep = t // 128
    tr = t // ATT_ROW_CHAINS

    def body(q_ref, k_ref, v_ref, c_ref, o_ref, lse_ref, m_s, l_s, acc_s):
        qi = pl.program_id(1)
        m_s[...] = jnp.full_like(m_s, -jnp.inf)
        l_s[...] = jnp.zeros_like(l_s)
        acc_s[...] = jnp.zeros_like(acc_s)

        def block(ki, masked):
            keys = pl.ds(pl.multiple_of(ki * t, t), t)
            k = k_ref[keys, :]
            v = v_ref[keys, :]
            c2 = c_ref[ki] * LOG2E
            for r in range(ATT_ROW_CHAINS):
                rs = slice(r * tr, (r + 1) * tr)
                sc = lax.dot_general(q_ref[rs, :], k, (_DIMS["nt"], ((), ())),
                                     preferred_element_type=F32) * scale2 - c2
                if masked:
                    row = lax.broadcasted_iota(jnp.int32, (tr, t), 0) + r * tr
                    col = lax.broadcasted_iota(jnp.int32, (tr, t), 1)
                    sc = jnp.where(col <= row, sc, -jnp.inf)
                m_prev = m_s[rs, :]
                m_new = jnp.maximum(m_prev, jnp.max(sc, axis=1, keepdims=True))
                alpha = jnp.exp2(m_prev - m_new)
                p = jnp.exp2(sc - jnp.tile(m_new, (1, rep)))
                l_s[rs, :] = alpha * l_s[rs, :] + jnp.sum(p, axis=1, keepdims=True)
                acc_s[rs, :] = alpha * acc_s[rs, :] + jnp.dot(p.astype(BF16), v, preferred_element_type=F32)
                m_s[rs, :] = m_new

        def full_block(ki, carry):
            block(ki, False)
            return carry

        lax.fori_loop(0, qi, full_block, 0)
        block(qi, True)
        o_ref[...] = (acc_s[...] / l_s[...]).astype(BF16)
        lse_ref[...] = m_s[...] + jnp.log(l_s[...]) * LOG2E

    head = lambda off: pl.BlockSpec((s, 128), lambda h, qi: (0, off + h))
    return _pcall(
        body, name="attn_fwd", grid=(n_heads, nq),
        in_specs=[pl.BlockSpec((t, 128), lambda h, qi: (qi, h)), head(n_heads), head(2 * n_heads),
                  pl.BlockSpec((None, nq, 1, t), lambda h, qi: (h, 0, 0, 0))],
        out_specs=(pl.BlockSpec((t, 128), lambda h, qi: (qi, h)),) * 2,
        out_shape=(jax.ShapeDtypeStruct((s, n_heads * 128), BF16), jax.ShapeDtypeStruct((s, n_heads * 128), F32)),
        scratch_shapes=[pltpu.VMEM((t, 128), F32)] * 3,
        semantics=("parallel", "arbitrary"), args=(p_main, p_main, p_main, cum4), side=side)


def _attn_bwd(p_main, do, lse_rep, delta_rep, cum4, n_heads, t, side=None):
    s = p_main.shape[0]
    nq = s // t
    scale = HEAD_DIM ** -0.5
    rep = t // 128
    tr = t // ATT_ROW_CHAINS

    def body(q_ref, k_ref, v_ref, do_ref, lse_ref, dl_ref, c_ref,
             dq_ref, dk_ref, dv_ref, drow_ref, dcol_ref, dq_acc, dk_acc, dv_acc):
        kb = pl.program_id(1)

        @pl.when(kb == 0)
        def _():
            dq_acc[...] = jnp.zeros_like(dq_acc)

        dk_acc[...] = jnp.zeros_like(dk_acc)
        dv_acc[...] = jnp.zeros_like(dv_acc)
        k = k_ref[...]
        v = v_ref[...]
        c2 = c_ref[...] * LOG2E
        k_ones = jnp.concatenate([k, jnp.ones((t, 128), BF16)], axis=1)
        q_ones = jnp.ones((tr, 128), BF16)

        def block(qb, masked):
            for r in range(ATT_ROW_CHAINS):
                rows = pl.ds(pl.multiple_of(qb * t + r * tr, tr), tr)
                q = q_ref[rows, :]
                dout = do_ref[rows, :]
                sc = lax.dot_general(q, k, (_DIMS["nt"], ((), ())), preferred_element_type=F32) * (scale * LOG2E) - c2
                if masked:
                    row = lax.broadcasted_iota(jnp.int32, (tr, t), 0) + r * tr
                    col = lax.broadcasted_iota(jnp.int32, (tr, t), 1)
                    sc = jnp.where(col <= row, sc, -jnp.inf)
                p = jnp.exp2(sc - jnp.tile(lse_ref[rows, :], (1, rep)))
                dp = lax.dot_general(dout, v, (_DIMS["nt"], ((), ())), preferred_element_type=F32)
                dsb = (p * (dp - jnp.tile(dl_ref[rows, :], (1, rep)))).astype(BF16)
                dv_acc[...] += lax.dot_general(p.astype(BF16), dout, (_DIMS["tn"], ((), ())),
                                               preferred_element_type=F32)
                dk_acc[...] += lax.dot_general(dsb, jnp.concatenate([q, q_ones], axis=1), (_DIMS["tn"], ((), ())),
                                               preferred_element_type=F32)
                dq_acc[rows, :] += jnp.dot(dsb, k_ones, preferred_element_type=F32)

        def full_block(qb, carry):
            block(qb, False)
            return carry

        block(kb, True)
        diag = pl.ds(pl.multiple_of(kb * t, t), t)
        dq_ref[diag, :] = (dq_acc[diag, 0:128] * scale).astype(BF16)
        drow_ref[:, diag] = dq_acc[diag, 128:256].T[0:8, :]
        lax.fori_loop(kb + 1, nq, full_block, 0)
        dk_ref[...] = (dk_acc[:, 0:128] * scale).astype(BF16)
        dcol_ref[...] = dk_acc[:, 128:256].T[0:8, :]
        dv_ref[...] = dv_acc[...].astype(BF16)

    kside = lambda off: pl.BlockSpec((t, 128), lambda h, kb: (kb, off + h))
    whole = pl.BlockSpec((s, 128), lambda h, kb: (0, h))
    hw = n_heads * 128
    return _pcall(
        body, name="attn_bwd", grid=(n_heads, nq),
        in_specs=[whole, kside(n_heads), kside(2 * n_heads), whole, whole, whole,
                  pl.BlockSpec((None, None, 1, t), lambda h, kb: (h, kb, 0, 0))],
        out_specs=(whole, kside(0), kside(0), pl.BlockSpec((None, 8, s), lambda h, kb: (h, 0, 0)),
                   pl.BlockSpec((None, 8, t), lambda h, kb: (h, 0, kb))),
        out_shape=(jax.ShapeDtypeStruct((s, hw), BF16), jax.ShapeDtypeStruct((s, hw), BF16),
                   jax.ShapeDtypeStruct((s, hw), BF16), jax.ShapeDtypeStruct((n_heads, 8, s), F32),
                   jax.ShapeDtypeStruct((n_heads, 8, s), F32)),
        scratch_shapes=[pltpu.VMEM((s, 256), F32), pltpu.VMEM((t, 256), F32), pltpu.VMEM((t, 128), F32)],
        semantics=("parallel", "arbitrary"), args=(p_main, p_main, p_main, do, lse_rep, delta_rep, cum4), side=side)


def _conv_fwd(u_ext, cw_ref):
    r1 = pltpu.roll(u_ext, 1, 0)
    r2 = pltpu.roll(u_ext, 2, 0)
    conv = cw_ref[2:3, :] * u_ext + cw_ref[1:2, :] * r1 + cw_ref[0:1, :] * r2
    return conv[HALO:], r1[HALO:], r2[HALO:]


def _pool_z(pu, pu_halo, row0, tm, pg):
    ext = jnp.concatenate([pu_halo, pu], axis=0)
    t1 = (row0 + lax.broadcasted_iota(jnp.int32, (tm, pg), 0) + 1).astype(F32)
    zs = []
    for g, w in enumerate(POOL_WINDOWS):
        sm = ext[:, g * pg:(g + 1) * pg]
        sh = 1
        while sh < w:
            sm = sm + pltpu.roll(sm, sh, 0)
            sh *= 2
        mean = sm[HALO:] / jnp.minimum(t1, float(w))
        zs.append(mean - pu[:, g * pg:(g + 1) * pg])
    return zs


def _mix_specs(tm, s):
    per = tm // HALO
    last = s // HALO - 1
    cur = lambda w, j: pl.BlockSpec((tm, w), lambda i: (i, j))
    prev = lambda j: pl.BlockSpec((HALO, 512), lambda i: (jnp.maximum(i * per - 1, 0), j))
    nxt = lambda j: pl.BlockSpec((HALO, 512), lambda i: (jnp.minimum((i + 1) * per, last), j))
    full = lambda shape: pl.BlockSpec(shape, lambda i: (0,) * len(shape))
    return cur, prev, nxt, full


def _mix_fwd(p_main, o, conv_w, pool_w, pool_scale, tm):
    s = p_main.shape[0]
    d = 2048
    pg = 128
    cur, prev, nxt, full = _mix_specs(tm, s)

    def body(ga_ref, cb_ref, cc_ref, ch_ref, gc_ref, pu_ref, gp_ref, cch_ref, chh_ref, puh_ref,
             o_ref, cw_ref, pw_ref, ps_ref, y_ref):
        i = pl.program_id(0)
        first = i == 0
        ga = ga_ref[...].astype(F32)
        y_ref[:, 0:1024] = (o_ref[...].astype(F32) * ga * _sigmoid(ga)).astype(BF16)

        u = cc_ref[...].astype(F32) * ch_ref[...].astype(F32)
        uh = jnp.where(first, 0.0, cch_ref[...].astype(F32) * chh_ref[...].astype(F32))
        conv, _, _ = _conv_fwd(jnp.concatenate([uh, u], axis=0), cw_ref)
        gc = gc_ref[...].astype(F32)
        y_ref[:, 1024:1536] = (cb_ref[...].astype(F32) * conv * gc * _sigmoid(gc)).astype(BF16)

        pu = pu_ref[...].astype(F32)
        puh = jnp.where(first, 0.0, puh_ref[...].astype(F32))
        zs = _pool_z(pu, puh, i * tm, tm, pg)
        gp = gp_ref[...].astype(F32)
        gate = gp * _sigmoid(gp) * ps_ref[...]
        for g in range(4):
            r = jnp.dot(zs[g].astype(BF16), pw_ref[g], preferred_element_type=F32)
            y_ref[:, 1536 + g * pg:1536 + (g + 1) * pg] = (r * gate[:, g * pg:(g + 1) * pg]).astype(BF16)

    return pl.pallas_call(
        body, name="mix_fwd", grid=(s // tm,),
        in_specs=[cur(1024, 3), cur(512, 8), cur(512, 9), cur(512, 10), cur(512, 11), cur(512, 12), cur(512, 13),
                  prev(9), prev(10), prev(12),
                  cur(1024, 0), full((3, 512)), full((4, pg, pg)), full((1, 512))],
        out_specs=pl.BlockSpec((tm, d), lambda i: (i, 0)),
        out_shape=jax.ShapeDtypeStruct((s, d), BF16),
        compiler_params=_params(("parallel",)),
    )(*([p_main] * 10), o, conv_w, pool_w, pool_scale)


def _mix_bwd(p_main, o, dy, conv_w, pool_w, pool_scale, tm):
    s = p_main.shape[0]
    pg = 128
    n_heads = 8
    cur, prev, nxt, full = _mix_specs(tm, s)
    nblk = s // tm
    n_ext = tm + HALO

    def silu_and_grad(x):
        sg = _sigmoid(x)
        return x * sg, sg * (1.0 + x * (1.0 - sg))

    def body(ga_ref, cb_ref, cc_ref, ch_ref, gc_ref, pu_ref, gp_ref, cch_ref, chh_ref, puh_ref,
             cbn_ref, gcn_ref, gpn_ref, o_ref, dy_ref, dycn_ref, dypn_ref, cw_ref, pw_ref, ps_ref,
             do_ref, dl_ref, dp_ref, dsm_ref, dpw_ref):
        i = pl.program_id(0)
        first = i == 0
        last = i == nblk - 1

        @pl.when(first)
        def _():
            dsm_ref[...] = jnp.zeros_like(dsm_ref)
            dpw_ref[...] = jnp.zeros_like(dpw_ref)

        ga = ga_ref[...].astype(F32)
        of = o_ref[...].astype(F32)
        dya = dy_ref[:, 0:1024]
        sa, dsa = silu_and_grad(ga)
        dout = dya * sa
        do_ref[...] = dout.astype(BF16)
        dp_ref[:, 0:1024] = (dya * of * dsa).astype(BF16)
        prod = dout * of
        for h in range(n_heads):
            dsum = jnp.sum(prod[:, h * 128:(h + 1) * 128], axis=1, keepdims=True)
            dl_ref[:, h * 128:(h + 1) * 128] = jnp.broadcast_to(dsum, (tm, 128))

        cb = cb_ref[...].astype(F32)
        cc = cc_ref[...].astype(F32)
        ch = ch_ref[...].astype(F32)
        gc = gc_ref[...].astype(F32)
        u = cc * ch
        uh = jnp.where(first, 0.0, cch_ref[...].astype(F32) * chh_ref[...].astype(F32))
        conv, u1, u2 = _conv_fwd(jnp.concatenate([uh, u], axis=0), cw_ref)
        sc_, dsc = silu_and_grad(gc)
        dyc = dy_ref[:, 1024:1536]
        dp_ref[:, 1024:1536] = (dyc * conv * sc_).astype(BF16)
        dp_ref[:, 2560:3072] = (dyc * cb * conv * dsc).astype(BF16)
        dconv = dyc * cb * sc_
        gcn = gcn_ref[...].astype(F32)
        dconv_n = jnp.where(last, 0.0, dycn_ref[...] * cbn_ref[...].astype(F32) * gcn * _sigmoid(gcn))
        dext = jnp.concatenate([dconv, dconv_n], axis=0)
        du = (cw_ref[2:3, :] * dext + cw_ref[1:2, :] * pltpu.roll(dext, n_ext - 1, 0)
              + cw_ref[0:1, :] * pltpu.roll(dext, n_ext - 2, 0))[:tm]
        dp_ref[:, 1536:2048] = (du * ch).astype(BF16)
        dp_ref[:, 2048:2560] = (du * cc).astype(BF16)
        dsm_ref[0:1, :] += jnp.sum(dconv * u2, axis=0, keepdims=True)
        dsm_ref[1:2, :] += jnp.sum(dconv * u1, axis=0, keepdims=True)
        dsm_ref[2:3, :] += jnp.sum(dconv * u, axis=0, keepdims=True)

        pu = pu_ref[...].astype(F32)
        puh = jnp.where(first, 0.0, puh_ref[...].astype(F32))
        zs = _pool_z(pu, puh, i * tm, tm, pg)
        gp = gp_ref[...].astype(F32)
        sp, dsp = silu_and_grad(gp)
        dyp = dy_ref[:, 1536:2048]
        gpn = gpn_ref[...].astype(F32)
        dr_n = jnp.where(last, 0.0, dypn_ref[...] * gpn * _sigmoid(gpn) * ps_ref[...])
        drs = dyp * sp
        dr = drs * ps_ref[...]
        t1 = (i * tm + lax.broadcasted_iota(jnp.int32, (tm, pg), 0) + 1).astype(F32)
        r_parts, dpu_parts = [], []
        for g, w in enumerate(POOL_WINDOWS):
            cols = slice(g * pg, (g + 1) * pg)
            zb = zs[g].astype(BF16)
            r_parts.append(jnp.dot(zb, pw_ref[g], preferred_element_type=F32))
            drb = dr[:, cols].astype(BF16)
            dpw_ref[g] += lax.dot_general(zb, drb, (_DIMS["tn"], ((), ())), preferred_element_type=F32)
            dz = lax.dot_general(drb, pw_ref[g], (_DIMS["nt"], ((), ())), preferred_element_type=F32)
            dz_n = lax.dot_general(dr_n[:, cols].astype(BF16), pw_ref[g], (_DIMS["nt"], ((), ())),
                                   preferred_element_type=F32)
            sm = jnp.concatenate([dz / jnp.minimum(t1, float(w)), dz_n / float(w)], axis=0)
            sh = 1
            while sh < w:
                sm = sm + pltpu.roll(sm, n_ext - sh, 0)
                sh *= 2
            dpu_parts.append(sm[:tm] - dz)
        r = jnp.concatenate(r_parts, axis=1)
        dp_ref[:, 3072:3584] = jnp.concatenate(dpu_parts, axis=1).astype(BF16)
        dp_ref[:, 3584:4096] = (dyp * r * ps_ref[...] * dsp).astype(BF16)
        dsm_ref[3:4, :] += jnp.sum(drs * r, axis=0, keepdims=True)

    dy_next = lambda j: pl.BlockSpec((HALO, 512), lambda i: (jnp.minimum((i + 1) * (tm // HALO), s // HALO - 1), j))
    return pl.pallas_call(
        body, name="mix_bwd", grid=(nblk,),
        in_specs=[cur(1024, 3), cur(512, 8), cur(512, 9), cur(512, 10), cur(512, 11), cur(512, 12), cur(512, 13),
                  prev(9), prev(10), prev(12), nxt(8), nxt(11), nxt(13),
                  cur(1024, 0), pl.BlockSpec((tm, 2048), lambda i: (i, 0)), dy_next(2), dy_next(3),
                  full((3, 512)), full((4, pg, pg)), full((1, 512))],
        out_specs=(pl.BlockSpec((tm, 1024), lambda i: (i, 0)), pl.BlockSpec((tm, 1024), lambda i: (i, 0)),
                   pl.BlockSpec((tm, 4096), lambda i: (i, 0)), full((8, 512)), full((4, pg, pg))),
        out_shape=(jax.ShapeDtypeStruct((s, 1024), BF16), jax.ShapeDtypeStruct((s, 1024), F32),
                   jax.ShapeDtypeStruct((s, 4096), BF16), jax.ShapeDtypeStruct((8, 512), F32),
                   jax.ShapeDtypeStruct((4, pg, pg), F32)),
        compiler_params=_params(("arbitrary",)),
    )(*([p_main] * 13), o, dy, dy, dy, conv_w, pool_w, pool_scale)


def _outproj_ln(y, w_out, x, ln_g, ln_b, alpha, tm, side=None):
    s, d = x.shape

    def body(y_ref, w_ref, x_ref, g_ref, b_ref, xn_ref, xb_ref, pre_ref):
        pre = alpha * x_ref[...] + jnp.dot(y_ref[...], w_ref[...], preferred_element_type=F32)
        mu = jnp.mean(pre, axis=1, keepdims=True)
        cen = pre - mu
        var = jnp.mean(cen * cen, axis=1, keepdims=True)
        xn = cen * lax.rsqrt(var + LN_EPS) * g_ref[...] + b_ref[...]
        pre_ref[...] = pre
        xn_ref[...] = xn
        xb_ref[...] = xn.astype(BF16)

    row = pl.BlockSpec((tm, d), lambda i: (i, 0))
    vec = pl.BlockSpec((1, d), lambda i: (0, 0))
    return _pcall(
        body, name="outproj_ln", grid=(s // tm,),
        in_specs=[row, pl.BlockSpec((d, d), lambda i: (0, 0)), row, vec, vec],
        out_specs=(row, row, row),
        out_shape=(jax.ShapeDtypeStruct((s, d), F32), jax.ShapeDtypeStruct((s, d), BF16),
                   jax.ShapeDtypeStruct((s, d), F32)),
        scratch_shapes=[], semantics=("parallel",), args=(y, w_out, x, ln_g, ln_b), side=side)


def _ln_bwd(dxn, pre, ln_g, tm, side=None):
    s, d = pre.shape

    def body(dx_ref, pre_ref, g_ref, dpre_ref, dpb_ref, dgb_ref):
        @pl.when(pl.program_id(0) == 0)
        def _():
            dgb_ref[...] = jnp.zeros_like(dgb_ref)

        pre_ = pre_ref[...]
        dx = dx_ref[...]
        mu = jnp.mean(pre_, axis=1, keepdims=True)
        cen = pre_ - mu
        var = jnp.mean(cen * cen, axis=1, keepdims=True)
        rstd = lax.rsqrt(var + LN_EPS)
        xhat = cen * rstd
        dxh = dx * g_ref[...]
        dpre = rstd * (dxh - jnp.mean(dxh, axis=1, keepdims=True)
                       - xhat * jnp.mean(dxh * xhat, axis=1, keepdims=True))
        dpre_ref[...] = dpre
        dpb_ref[...] = dpre.astype(BF16)
        dgb_ref[0:1, :] += jnp.sum(dx * xhat, axis=0, keepdims=True)
        dgb_ref[1:2, :] += jnp.sum(dx, axis=0, keepdims=True)

    row = pl.BlockSpec((tm, d), lambda i: (i, 0))
    return _pcall(
        body, name="ln_bwd", grid=(s // tm,),
        in_specs=[row, row, pl.BlockSpec((1, d), lambda i: (0, 0))],
        out_specs=(row, row, pl.BlockSpec((8, d), lambda i: (0, 0))),
        out_shape=(jax.ShapeDtypeStruct((s, d), F32), jax.ShapeDtypeStruct((s, d), BF16),
                   jax.ShapeDtypeStruct((8, d), F32)),
        scratch_shapes=[], semantics=("arbitrary",), args=(dxn, pre, ln_g), side=side)


def _loss_grad(y, target, tm):
    s, d = y.shape

    def body(y_ref, t_ref, dy_ref, loss_ref):
        @pl.when(pl.program_id(0) == 0)
        def _():
            loss_ref[...] = jnp.zeros_like(loss_ref)

        diff = y_ref[...] - t_ref[...]
        dy_ref[...] = diff / d
        loss_ref[...] += 0.5 * jnp.sum(jnp.sum(diff * diff, axis=1, keepdims=True) / d)

    row = pl.BlockSpec((tm, d), lambda i: (i, 0))
    return pl.pallas_call(
        body, name="loss_grad", grid=(s // tm,),
        in_specs=[row, row],
        out_specs=(row, pl.BlockSpec((8, 128), lambda i: (0, 0))),
        out_shape=(jax.ShapeDtypeStruct((s, d), F32), jax.ShapeDtypeStruct((8, 128), F32)),
        compiler_params=_params(("arbitrary",)),
    )(y, target)


def _adamw(w, g, m, v, name, lead_block=None, emit_grad=False):
    a, r, c = w.shape
    tr = r if lead_block else _tile(r, (256, 128, 64, 32, 16, 8))
    c1 = 1.0 - ADAM_B1 ** ADAM_STEP
    c2 = 1.0 - ADAM_B2 ** ADAM_STEP

    def body(w_ref, g_ref, m_ref, v_ref, d_ref, mo_ref, vo_ref, *go_ref):
        g_ = g_ref[...]
        if emit_grad:
            go_ref[0][...] = g_
        m_new = ADAM_B1 * m_ref[...] + (1.0 - ADAM_B1) * g_
        v_new = ADAM_B2 * v_ref[...] + (1.0 - ADAM_B2) * (g_ * g_)
        d_ref[...] = -ADAM_LR * ((m_new / c1) / (jnp.sqrt(v_new / c2) + ADAM_EPS) + ADAM_WD * w_ref[...])
        mo_ref[...] = m_new
        vo_ref[...] = v_new

    ta = lead_block or 1
    spec = pl.BlockSpec((ta, tr, c), lambda i, j: (i, j, 0))
    shp = jax.ShapeDtypeStruct(w.shape, F32)
    n_out = 4 if emit_grad else 3
    return pl.pallas_call(
        body, name=name, grid=(a // ta, r // tr), in_specs=[spec] * 4, out_specs=(spec,) * n_out,
        out_shape=(shp,) * n_out, compiler_params=_params(("parallel", "parallel")),
    )(w, g, m, v)


def _place():
    x, y, c = lax.axis_index("x"), lax.axis_index("y"), lax.axis_index("c")
    return x, y, c, [(1 - x, y), (x, 1 - y), (1 - x, 1 - y)]


def _remote(src, dst, send, recv, k, to):
    return pltpu.make_async_remote_copy(src_ref=src, dst_ref=dst, send_sem=send.at[k], recv_sem=recv.at[k],
                                        device_id=to, device_id_type=MESH)


def _slot(ref, px, py, pc):
    return ref.at[pl.ds(2 * px + py, 1), pl.ds(pc, 1)]


def _side_gather_ici(shard):
    _, two, r, w = shard.shape

    def sends(sins, souts, send, recv):
        x, y, c, chips = _place()
        return [_remote(sins[0].at[:, pl.ds(c, 1)], _slot(souts[0], x, y, c), send, recv, k, (*chip, c))
                for k, chip in enumerate(chips)]

    def start(sins, souts, send, recv):
        for cp in sends(sins, souts, send, recv):
            cp.start()

    def wait(sins, souts, send, recv):
        x, y, c, chips = _place()
        for cp in sends(sins, souts, send, recv):
            cp.wait_send()
        for k, chip in enumerate(chips):
            _remote(_slot(souts[0], *chip, c), _slot(souts[0], *chip, c), send, recv, k, (x, y, c)).wait_recv()

    return _Side([shard], [jax.ShapeDtypeStruct((4, two, r, w), shard.dtype)], 3, start, wait)


def _side_gather_d2d(gathered):
    def sends(souts, send, recv):
        x, y, c, chips = _place()
        return [_remote(_slot(souts[0], *chip, c), _slot(souts[0], *chip, c), send, recv, k, (x, y, 1 - c))
                for k, chip in enumerate(chips)]

    def start(sins, souts, send, recv):
        for cp in sends(souts, send, recv):
            cp.start()

    def wait(sins, souts, send, recv):
        x, y, c, chips = _place()
        for cp in sends(souts, send, recv):
            cp.wait_send()
        for k, chip in enumerate(chips):
            _remote(_slot(souts[0], *chip, 1 - c), _slot(souts[0], *chip, 1 - c), send, recv, k, (x, y, c)).wait_recv()

    return _Side([gathered], [jax.ShapeDtypeStruct(gathered.shape, gathered.dtype)], 3, start, wait, aliases={0: 0})


def _side_swap(g):
    nchip, _, r, w = g.shape

    def copy(sins, souts, send, recv):
        x, y, c, _ = _place()
        return _remote(sins[0].at[:, pl.ds(1 - c, 1)], souts[0], send, recv, 0, (x, y, 1 - c))

    return _Side([g], [jax.ShapeDtypeStruct((nchip, 1, r, w), g.dtype)], 1,
                 lambda *a: copy(*a).start(), lambda *a: copy(*a).wait())


def _side_exchange(hsum):
    _, _, r, w = hsum.shape

    def copies(sins, souts, send, recv):
        x, y, c, chips = _place()
        return [_remote(sins[0].at[pl.ds(2 * px + py, 1)], souts[0].at[pl.ds(k, 1)], send, recv, k, (px, py, c))
                for k, (px, py) in enumerate(chips)]

    def start(*a):
        for cp in copies(*a):
            cp.start()

    def wait(*a):
        for cp in copies(*a):
            cp.wait()

    return _Side([hsum], [jax.ShapeDtypeStruct((3, 1, r, w), hsum.dtype)], 3, start, wait)


def _side_share(f):
    def copy(sins, souts, send, recv):
        x, y, c, _ = _place()
        return _remote(sins[0], souts[0], send, recv, 0, (x, y, 1 - c))

    return _Side([f], [jax.ShapeDtypeStruct(f.shape, f.dtype)], 1,
                 lambda *a: copy(*a).start(), lambda *a: copy(*a).wait())


def _own_slot_filled(gathered, shard):
    chip = 2 * lax.axis_index("x") + lax.axis_index("y")
    return lax.dynamic_update_slice(gathered, shard, (chip, 0, 0, 0))


def _both_parts(mine, theirs):
    mine_first = lax.axis_index("c") == 0
    return jnp.concatenate([jnp.where(mine_first, mine, theirs), jnp.where(mine_first, theirs, mine)], axis=0)


def _plane_tile(r, cdim):
    tr = _tile(r, (256, 128))
    if tr != r:
        return tr, cdim
    return r, _tile(cdim, (256, 128))


def _add_halves(g, recv, core, name):
    nchip, nl, r, cdim = g.shape
    half = nl // 2
    tr, tc = _plane_tile(r, cdim)

    def body(c_ref, g_ref, r_ref, o_ref):
        o_ref[...] = (g_ref[...].astype(F32) + r_ref[...].astype(F32)).astype(BF16)

    blk = (1, 1, tr, tc)
    return pl.pallas_call(
        body, name=name,
        grid_spec=pltpu.PrefetchScalarGridSpec(
            num_scalar_prefetch=1, grid=(nchip, half, r // tr, cdim // tc),
            in_specs=[pl.BlockSpec(blk, lambda j, l, i, k, c_ref: (j, c_ref[0] * half + l, i, k)),
                      pl.BlockSpec(blk, lambda j, l, i, k, c_ref: (j, l, i, k))],
            out_specs=pl.BlockSpec(blk, lambda j, l, i, k, c_ref: (j, l, i, k))),
        out_shape=jax.ShapeDtypeStruct((nchip, half, r, cdim), BF16),
        compiler_params=_params(("parallel",) * 4),
    )(core, g, recv)


def _add_partials(hsum, others, chip, name):
    nchip, half, r, cdim = hsum.shape
    tr, tc = _plane_tile(r, cdim)

    def body(i_ref, h_ref, a_ref, b_ref, c_ref, o_ref):
        o_ref[...] = ((h_ref[...].astype(F32) + a_ref[...].astype(F32))
                      + (b_ref[...].astype(F32) + c_ref[...].astype(F32)))[0]

    blk = (1, 1, tr, tc)
    other = lambda n: pl.BlockSpec(blk, lambda l, i, k, i_ref: (n, l, i, k))
    return pl.pallas_call(
        body, name=name,
        grid_spec=pltpu.PrefetchScalarGridSpec(
            num_scalar_prefetch=1, grid=(half, r // tr, cdim // tc),
            in_specs=[pl.BlockSpec(blk, lambda l, i, k, i_ref: (i_ref[0], l, i, k)), other(0), other(1), other(2)],
            out_specs=pl.BlockSpec((1, tr, tc), lambda l, i, k, i_ref: (l, i, k))),
        out_shape=jax.ShapeDtypeStruct((half, r, cdim), F32),
        compiler_params=_params(("parallel",) * 3),
    )(chip, hsum, others, others, others)


def _all_gather_small(v, name):
    m_per, n = v.shape

    def body(x_ref, out_ref, send_sems, recv_sems, local_sem):
        x, y, c, chips = _place()
        me, sib = (x, y, c), (x, y, 1 - c)

        def rows(px, py, pc):
            return out_ref.at[pl.ds((4 * px + 2 * py + pc) * m_per, m_per), :]

        def copy(k, block, to, src=None):
            return pltpu.make_async_remote_copy(
                src_ref=rows(*block) if src is None else src, dst_ref=rows(*block), send_sem=send_sems.at[k],
                recv_sem=recv_sems.at[k], device_id=to, device_id_type=MESH)

        mine = pltpu.make_async_copy(x_ref, rows(*me), local_sem)
        mine.start()
        first = [copy(0, me, sib, src=x_ref)]
        first += [copy(1 + j, me, (*chip, c), src=x_ref) for j, chip in enumerate(chips)]
        for cp in first:
            cp.start()
        passed = [copy(4 + j, (*chip, c), sib) for j, chip in enumerate(chips)]
        for j, chip in enumerate(chips):
            copy(1 + j, (*chip, c), me).wait_recv()
            passed[j].start()
        copy(0, sib, me).wait_recv()
        for j, chip in enumerate(chips):
            copy(4 + j, (*chip, 1 - c), me).wait_recv()
        for cp in first + passed:
            cp.wait_send()
        mine.wait()

    return pl.pallas_call(
        body, name=name, out_shape=jax.ShapeDtypeStruct((8 * m_per, n), v.dtype),
        in_specs=[pl.BlockSpec(memory_space=pltpu.VMEM)], out_specs=pl.BlockSpec(memory_space=pltpu.VMEM),
        scratch_shapes=[pltpu.SemaphoreType.DMA((7,)), pltpu.SemaphoreType.DMA((7,)), pltpu.SemaphoreType.DMA],
        compiler_params=pltpu.CompilerParams(vmem_limit_bytes=VMEM_LIMIT),
    )(v)


def _sum_blocks(v, nblk, name):
    m_per = v.shape[0] // nblk

    def body(v_ref, o_ref):
        acc = v_ref[0:m_per, :]
        for j in range(1, nblk):
            acc = acc + v_ref[j * m_per:(j + 1) * m_per, :]
        o_ref[...] = acc

    return pl.pallas_call(
        body, name=name, out_shape=jax.ShapeDtypeStruct((m_per, 128), F32),
        in_specs=[pl.BlockSpec(memory_space=pltpu.VMEM)], out_specs=pl.BlockSpec(memory_space=pltpu.VMEM),
        compiler_params=pltpu.CompilerParams(vmem_limit_bytes=VMEM_LIMIT),
    )(v)


def _rows128(a, rows=None):
    flat = a.reshape(-1).astype(F32)
    need = -(-flat.shape[0] // 128)
    rows = rows or -(-need // 8) * 8
    return jnp.pad(flat, (0, rows * 128 - flat.shape[0])).reshape(rows, 128)


def kernel(x, w_in, b_f, conv_w, pool_w, pool_scale, w_out, ln_g, ln_b, loss_target, m_w_in, m_b_f, m_conv_w, m_pool_w, m_pool_scale, m_w_out, m_ln_g, m_ln_b, v_w_in, v_b_f, v_conv_w, v_pool_w, v_pool_scale, v_w_out, v_ln_g, v_ln_b):
    depth, d, in_shard = w_in.shape
    s = x.shape[1]
    att = d // 2
    n_heads = att // HEAD_DIM
    cw = d // 4
    pw = d - att - cw
    in_w = 4 * att + n_heads + 4 * cw + 2 * pw
    assert in_shard * 4 == in_w and (d, n_heads, cw, pw) == (2048, 8, 512, 512) and s % 512 == 0
    main_w = in_w - n_heads
    fg0 = 4 * att
    alpha = (2 * depth) ** 0.25
    t_att = 512
    tm_mix = 256

    cx, cy, cc = lax.axis_index("x"), lax.axis_index("y"), lax.axis_index("c")
    chip = 2 * cx + cy
    core_arr = jnp.reshape(cc, (1,)).astype(jnp.int32)
    chip_arr = jnp.reshape(chip, (1,)).astype(jnp.int32)
    out_rows = w_out.shape[1]

    starts = [j * in_shard for j in range(4)]
    main_start = [g if g < fg0 else g - n_heads for g in starts]
    base = [m // 16 * 16 for m in main_start]
    lead = [m - b for m, b in zip(main_start, base)]
    fg_chip = max(j for j in range(4) if starts[j] <= fg0)
    hole = fg0 - starts[fg_chip]
    blk = -(-(max(lead) + in_shard) // 16) * 16
    assert hole + n_heads <= in_shard and all(b + blk <= main_w for b in base) and base[3] + blk == main_w
    r_comb = out_rows + blk
    pad_rows = -r_comb % 32
    r_half = (r_comb + pad_rows) // 2
    lead_dyn = jnp.take(jnp.array(lead, jnp.int32), chip)
    has_fg = chip == fg_chip

    to_rows = lambda a: jnp.transpose(a, (2, 0, 1))
    from_rows = lambda a: jnp.transpose(a, (1, 2, 0))
    wt_b = jnp.transpose(to_rows(w_in).astype(BF16), (1, 0, 2))
    padded = jnp.pad(wt_b, ((0, 0), (16, 32), (0, 0)))
    plain = lax.dynamic_slice_in_dim(padded, 16 - lead_dyn, blk, axis=1)
    skipping = lax.dynamic_slice_in_dim(padded, 16 - lead_dyn + n_heads, blk, axis=1)
    pos = lax.broadcasted_iota(jnp.int32, (1, blk, 1), 1)
    block = jnp.where(jnp.logical_and(has_fg, pos >= lead_dyn + hole), skipping, plain)
    comb = jnp.concatenate([w_out.astype(BF16), block, jnp.zeros((depth, pad_rows, d), BF16)], axis=1)
    shards = comb.reshape(depth, 2, r_half, d)

    def weights_of(gathered):
        full = gathered.reshape(4, 2 * r_half, d)
        wt_main = sum(jnp.pad(full[j, out_rows:r_comb], ((base[j], main_w - base[j] - blk), (0, 0)))
                      for j in range(4))
        return wt_main, full[:, :out_rows].reshape(d, d)

    first = shards[0][None]
    arrived = _run_side(_side_gather_ici(first), "gather_first_ici")[0]
    arrived = _run_side(_side_gather_d2d(arrived), "gather_first_d2d")[0]
    weights = weights_of(_own_slot_filled(arrived, first))

    small_w = jnp.concatenate([_rows128(conv_w), _rows128(wt_b[:, hole:hole + n_heads])], axis=0)
    n_cw = _rows128(conv_w).shape[0] * 128
    sw_all = _all_gather_small(small_w, "gather_small_weights").reshape(4, 2, small_w.shape[0] * 128)[:, 0]
    conv_full = sw_all[:, :depth * 3 * (cw // 4)].reshape(4, depth, 3, cw // 4)
    conv_full = jnp.moveaxis(conv_full, 0, 2).reshape(depth, 3, cw)
    wt_fg_all = sw_all[fg_chip, n_cw:n_cw + depth * n_heads * d].reshape(depth, n_heads, d).astype(BF16)
    wt_fg_all = jnp.pad(wt_fg_all, ((0, 0), (0, 128 - n_heads), (0, 0)))
    pool_b = pool_w.astype(BF16)

    xf = x[0]
    xb = xf.astype(BF16)
    saved = []
    for l in range(depth):
        wt_main, w_out_l = weights
        wt_fg = wt_fg_all[l]
        nxt = shards[l + 1][None] if l + 1 < depth else None
        p_main = _matmul(xb, wt_main, "nt", BF16, "proj_main")
        fg = _matmul(xb, wt_fg, "nt", F32, "proj_fg")
        fg_t = fg[:, :n_heads].T
        b_col = b_f[l].reshape(n_heads, 1)
        cum4 = _gate_fwd(fg_t, b_col).reshape(n_heads, s // t_att, 1, t_att)
        if nxt is None:
            o, lse = _attn_fwd(p_main, cum4, n_heads, t_att)
        else:
            (o, lse), (arrived,) = _attn_fwd(p_main, cum4, n_heads, t_att, side=_side_gather_ici(nxt))
        y = _mix_fwd(p_main, o, conv_full[l], pool_b[l], pool_scale[l].reshape(1, pw), tm_mix)
        ln_args = (y, w_out_l, xf, ln_g[l].reshape(1, d), ln_b[l].reshape(1, d), alpha, tm_mix)
        if nxt is None:
            x_new, xb_new, pre = _outproj_ln(*ln_args)
        else:
            (x_new, xb_new, pre), (arrived,) = _outproj_ln(*ln_args, side=_side_gather_d2d(arrived))
            weights = weights_of(_own_slot_filled(arrived, nxt))
        saved.append((xb, p_main, fg_t, b_col, cum4, o, lse, y, pre, wt_main, wt_fg, w_out_l))
        xf, xb = x_new, xb_new

    dx, loss_blk = _loss_grad(xf, loss_target[0], tm_mix)

    g_bf, g_sm, g_pw, g_ln, g_fgw, finished = ([None] * depth for _ in range(6))
    pending = None
    for l in reversed(range(depth)):
        xb, p_main, fg_t, b_col, cum4, o, lse, y, pre, wt_main, wt_fg, w_out_l = saved[l]
        if pending is None:
            dpre, dpre_b, g_ln[l] = _ln_bwd(dx, pre, ln_g[l].reshape(1, d), tm_mix)
        else:
            (dpre, dpre_b, g_ln[l]), (recv,) = _ln_bwd(dx, pre, ln_g[l].reshape(1, d), tm_mix,
                                                       side=_side_swap(pending))
            hsum = _add_halves(pending, recv, core_arr, "rs_add_halves")
        dy = _matmul(dpre_b, w_out_l, "nt", F32, "d_y")
        gw_out_l = _matmul(y, dpre_b, "tn", BF16, "d_w_out")
        do, delta, dp_rest, g_sm[l], g_pw[l] = _mix_bwd(p_main, o, dy, conv_full[l], pool_b[l],
                                                        pool_scale[l].reshape(1, pw), tm_mix)
        if pending is None:
            dq, dk, dv, drow, dcol = _attn_bwd(p_main, do, lse, delta, cum4, n_heads, t_att)
        else:
            (dq, dk, dv, drow, dcol), (others,) = _attn_bwd(p_main, do, lse, delta, cum4, n_heads, t_att,
                                                            side=_side_exchange(hsum))
            mine = _add_partials(hsum, others, chip_arr, "rs_add_partials")
        dfg_t, g_bf[l] = _gate_bwd(drow, dcol, fg_t, b_col)
        dfg_b = jnp.pad(dfg_t.T, ((0, 0), (0, 128 - n_heads))).astype(BF16)
        dp_main = jnp.concatenate([dq, dk, dv, dp_rest], axis=1)
        if pending is None:
            gw_main_l = _matmul(dp_main, xb, "tn", BF16, "d_w_main")
        else:
            gw_main_l, (theirs,) = _matmul(dp_main, xb, "tn", BF16, "d_w_main", side=_side_share(mine))
            finished[l + 1] = _both_parts(mine, theirs)
        g_fgw[l] = _matmul(dfg_b, xb, "tn", F32, "d_w_fg")[:n_heads]
        by_chip = jnp.concatenate([gw_out_l.reshape(4, out_rows, d), jnp.stack([gw_main_l[b:b + blk] for b in base]),
                                   jnp.zeros((4, pad_rows, d), BF16)], axis=1)
        pending = by_chip.reshape(4, 2, r_half, d)
        dx = _matmul(dp_main, wt_main, "nn", F32, "d_x_main", add=dpre, add_scale=alpha)
        dx = _matmul(dfg_b, wt_fg, "nn", F32, "d_x_fg", add=dx)

    recv = _run_side(_side_swap(pending), "rs_swap_last")[0]
    hsum = _add_halves(pending, recv, core_arr, "rs_add_halves")
    others = _run_side(_side_exchange(hsum), "rs_exchange_last")[0]
    mine = _add_partials(hsum, others, chip_arr, "rs_add_partials")
    theirs = _run_side(_side_share(mine), "rs_share_last")[0]
    finished[0] = _both_parts(mine, theirs)
    grads = jnp.stack(finished).reshape(depth, 2 * r_half, d)
    grad_w_out = grads[:, :out_rows]

    g_sm = jnp.stack(g_sm)
    g_ln = jnp.stack(g_ln)
    parts = [jnp.stack(g_pw), g_sm[:, 0:3, :], g_sm[:, 3, :], g_ln[:, 0, :], g_ln[:, 1, :],
             jnp.stack(g_bf)[:, :, 0], loss_blk[0, 0], jnp.stack(g_fgw)]
    packed = jnp.concatenate([_rows128(p) for p in parts], axis=0)
    summed = _sum_blocks(_all_gather_small(packed, "gather_small_grads"), 8, "sum_small_grads")
    outs, off = [], 0
    for p in parts:
        nr = _rows128(p).shape[0]
        outs.append(summed[off:off + nr].reshape(-1)[:p.size].reshape(p.shape))
        off += nr
    grad_pool_w, gconv_full, grad_pool_scale, grad_ln_g, grad_ln_b, grad_b_f, loss, grad_fg = outs
    grad_conv_w = lax.dynamic_slice_in_dim(gconv_full, chip * (cw // 4), cw // 4, axis=2)

    gblk = jnp.pad(grads[:, out_rows:r_comb], ((0, 0), (16, 16), (0, 0)))
    plain = lax.dynamic_slice_in_dim(gblk, 16 + lead_dyn, in_shard, axis=1)
    after_fg = lax.dynamic_slice_in_dim(gblk, 16 + lead_dyn - n_heads, in_shard, axis=1)
    fg_placed = jnp.pad(grad_fg, ((0, 0), (hole, in_shard - hole - n_heads), (0, 0)))
    r_idx = lax.broadcasted_iota(jnp.int32, (1, in_shard, 1), 1)
    g_shard = jnp.where(jnp.logical_and(has_fg, r_idx >= hole + n_heads), after_fg,
                        jnp.where(jnp.logical_and(has_fg, r_idx >= hole), fg_placed, plain))
    grad_rows = jnp.transpose(g_shard, (1, 0, 2))

    d_w_in, nm_w_in, nv_w_in, grad_w_in = (from_rows(a) for a in _adamw(
        to_rows(w_in), grad_rows, to_rows(m_w_in), to_rows(v_w_in), "adamw_w_in", lead_block=46, emit_grad=True))
    d_w_out, nm_w_out, nv_w_out = _adamw(w_out, grad_w_out, m_w_out, v_w_out, "adamw_w_out")
    small = [(b_f, grad_b_f, m_b_f, v_b_f), (conv_w, grad_conv_w, m_conv_w, v_conv_w),
             (pool_w, grad_pool_w, m_pool_w, v_pool_w), (pool_scale, grad_pool_scale, m_pool_scale, v_pool_scale),
             (ln_g, grad_ln_g, m_ln_g, v_ln_g), (ln_b, grad_ln_b, m_ln_b, v_ln_b)]
    pk = [jnp.concatenate([_rows128(t[j]) for t in small], axis=0)[None] for j in range(4)]
    sm_out = _adamw(pk[0], pk[1], pk[2], pk[3], "adamw_small", lead_block=1)
    res = {j: [] for j in range(3)}
    off = 0
    for t in small:
        nr = _rows128(t[0]).shape[0]
        for j in range(3):
            res[j].append(sm_out[j][0, off:off + nr].reshape(-1)[:t[0].size].reshape(t[0].shape))
        off += nr
    (d_b_f, d_conv_w, d_pool_w, d_pool_scale, d_ln_g, d_ln_b) = res[0]
    (nm_b_f, nm_conv_w, nm_pool_w, nm_pool_scale, nm_ln_g, nm_ln_b) = res[1]
    (nv_b_f, nv_conv_w, nv_pool_w, nv_pool_scale, nv_ln_g, nv_ln_b) = res[2]

    return (loss, dx[None], grad_w_in, grad_b_f, grad_conv_w, grad_pool_w, grad_pool_scale, grad_w_out,
            grad_ln_g, grad_ln_b,
            d_w_in, d_b_f, d_conv_w, d_pool_w, d_pool_scale, d_w_out, d_ln_g, d_ln_b,
            nm_w_in, nm_b_f, nm_conv_w, nm_pool_w, nm_pool_scale, nm_w_out, nm_ln_g, nm_ln_b,
            nv_w_in, nv_b_f, nv_conv_w, nv_pool_w, nv_pool_scale, nv_w_out, nv_ln_g, nv_ln_b)
```

```python
import functools

import jax
import jax.numpy as jnp
from jax import lax
from jax.experimental import pallas as pl
from jax.experimental.pallas import tpu as pltpu

F32 = jnp.float32
BF16 = jnp.bfloat16
MESH = pl.DeviceIdType.MESH

HEAD_DIM = 128
POOL_WINDOWS = (2, 4, 8, 16)
HALO = 16
LN_EPS = 1e-5
ADAM_LR = 0.001
ADAM_B1 = 0.9
ADAM_B2 = 0.999
ADAM_EPS = 1e-08
ADAM_WD = 0.01
ADAM_STEP = 10
VMEM_LIMIT = 56 * 1024 * 1024


def _params(sem, vmem=VMEM_LIMIT):
    return pltpu.CompilerParams(dimension_semantics=sem, vmem_limit_bytes=vmem)


def _tile(n, prefs):
    for t in prefs:
        if n % t == 0:
            return t
    return n


def _sigmoid(x):
    return 1.0 / (1.0 + jnp.exp(-x))


def _hbm_spec():
    return pl.BlockSpec(memory_space=pltpu.HBM)


class _Side:
    def __init__(self, inputs, out_shapes, n_sems, start, wait, aliases=None):
        self.inputs, self.out_shapes, self.n_sems = list(inputs), list(out_shapes), n_sems
        self.start, self.wait, self.aliases = start, wait, dict(aliases or {})


def _pcall(body, *, name, grid, in_specs, out_specs, out_shape, scratch_shapes, semantics, args, side=None):
    if side is None:
        return pl.pallas_call(
            body, name=name, grid=grid, in_specs=in_specs, out_specs=out_specs, out_shape=out_shape,
            scratch_shapes=scratch_shapes, compiler_params=_params(semantics))(*args)
    single = not isinstance(out_shape, (tuple, list))
    out_specs_l = [out_specs] if single else list(out_specs)
    out_shape_l = [out_shape] if single else list(out_shape)
    n_i, n_si, n_o, n_so, n_s = len(in_specs), len(side.inputs), len(out_shape_l), len(side.out_shapes), len(scratch_shapes)

    def wrapped(*refs):
        ins, sins = refs[:n_i], refs[n_i:n_i + n_si]
        o0 = n_i + n_si
        outs, souts = refs[o0:o0 + n_o], refs[o0 + n_o:o0 + n_o + n_so]
        s0 = o0 + n_o + n_so
        scr, (send, recv) = refs[s0:s0 + n_s], refs[s0 + n_s:]
        first = pl.program_id(0) == 0
        last = pl.program_id(0) == grid[0] - 1
        for ax in range(1, len(grid)):
            first = jnp.logical_and(first, pl.program_id(ax) == 0)
            last = jnp.logical_and(last, pl.program_id(ax) == grid[ax] - 1)

        @pl.when(first)
        def _():
            side.start(sins, souts, send, recv)

        body(*ins, *outs, *scr)

        @pl.when(last)
        def _():
            side.wait(sins, souts, send, recv)

    res = pl.pallas_call(
        wrapped, name=name, grid=grid,
        in_specs=list(in_specs) + [_hbm_spec()] * n_si,
        out_specs=tuple(out_specs_l + [_hbm_spec()] * n_so),
        out_shape=tuple(out_shape_l + side.out_shapes),
        scratch_shapes=list(scratch_shapes) + [pltpu.SemaphoreType.DMA((side.n_sems,))] * 2,
        input_output_aliases={n_i + i: n_o + o for i, o in side.aliases.items()},
        compiler_params=_params(("arbitrary",) * len(grid)),
    )(*args, *side.inputs)
    main = res[:n_o]
    return (main[0] if single else tuple(main)), tuple(res[n_o:])


def _run_side(side, name):
    n_si, n_so = len(side.inputs), len(side.out_shapes)

    def body(*refs):
        sins, souts = refs[:n_si], refs[n_si:n_si + n_so]
        send, recv = refs[n_si + n_so:]
        side.start(sins, souts, send, recv)
        side.wait(sins, souts, send, recv)

    return pl.pallas_call(
        body, name=name, out_shape=tuple(side.out_shapes),
        in_specs=[_hbm_spec()] * n_si, out_specs=tuple([_hbm_spec()] * n_so),
        scratch_shapes=[pltpu.SemaphoreType.DMA((side.n_sems,))] * 2,
        input_output_aliases=side.aliases,
    )(*side.inputs)


_DIMS = {"nn": ((1,), (0,)), "nt": ((1,), (1,)), "tn": ((0,), (0,))}


def _matmul(a, b, form, out_dtype, name, add=None, add_scale=1.0, tm=None, tn=None, tk=None, side=None):
    if form == "nn":
        (m, k), (_, n) = a.shape, b.shape
    elif form == "nt":
        (m, k), (n, _) = a.shape, b.shape
    else:
        (k, m), (_, n) = a.shape, b.shape
    tm = tm or _tile(m, (1024, 512, 256, 128))
    tn = tn or _tile(n, (1024, 512, 256, 128))
    tk = tk or _tile(k, (2048, 1792, 1024, 512, 256, 128))
    nk = k // tk
    if form == "tn":
        a_spec = pl.BlockSpec((tk, tm), lambda i, j, kk: (kk, i))
    else:
        a_spec = pl.BlockSpec((tm, tk), lambda i, j, kk: (i, kk))
    if form == "nt":
        b_spec = pl.BlockSpec((tn, tk), lambda i, j, kk: (j, kk))
    else:
        b_spec = pl.BlockSpec((tk, tn), lambda i, j, kk: (kk, j))
    o_spec = pl.BlockSpec((tm, tn), lambda i, j, kk: (i, j))
    dims = (_DIMS[form], ((), ()))
    has_add = add is not None

    def finish(r, add_ref, o_ref):
        if has_add:
            r = r + add_scale * add_ref[...]
        o_ref[...] = r.astype(out_dtype)

    def body_one_pass(a_ref, b_ref, *rest):
        add_ref, o_ref = rest if has_add else (None,) + rest
        finish(lax.dot_general(a_ref[...], b_ref[...], dims, preferred_element_type=F32), add_ref, o_ref)

    def body_accumulate(a_ref, b_ref, *rest):
        add_ref, o_ref, acc = rest if has_add else (None,) + rest
        kk = pl.program_id(2)
        part = lax.dot_general(a_ref[...], b_ref[...], dims, preferred_element_type=F32)

        @pl.when(kk == 0)
        def _():
            acc[...] = part

        @pl.when(jnp.logical_and(kk > 0, kk < nk - 1))
        def _():
            acc[...] += part

        @pl.when(kk == nk - 1)
        def _():
            finish(acc[...] + part, add_ref, o_ref)

    in_specs = [a_spec, b_spec] + ([o_spec] if has_add else [])
    args = (a, b) + ((add,) if has_add else ())
    return _pcall(
        body_one_pass if nk == 1 else body_accumulate, name=name, grid=(m // tm, n // tn, nk),
        in_specs=in_specs, out_specs=o_spec,
        out_shape=jax.ShapeDtypeStruct((m, n), out_dtype),
        scratch_shapes=[] if nk == 1 else [pltpu.VMEM((tm, tn), F32)],
        semantics=("parallel", "parallel", "arbitrary"), args=args, side=side)


def _tri(n, upper):
    r = lax.broadcasted_iota(jnp.int32, (n, n), 0)
    c = lax.broadcasted_iota(jnp.int32, (n, n), 1)
    return jnp.where((r <= c) if upper else (r >= c), 1.0, 0.0).astype(F32)


def _gate_fwd(fg_t, b_col):
    h, s = fg_t.shape
    nb = s // 128

    def body(fg_ref, b_ref, cum_ref):
        u = _tri(128, True)
        carry = jnp.zeros((h, 128), F32)
        for j in range(nb):
            z = fg_ref[:, j * 128:(j + 1) * 128] + b_ref[...]
            logf = -(jnp.maximum(-z, 0.0) + jnp.log(1.0 + jnp.exp(-jnp.abs(z))))
            c = jnp.dot(logf, u, precision=lax.Precision.HIGHEST, preferred_element_type=F32) + carry
            cum_ref[:, j * 128:(j + 1) * 128] = c
            carry = jnp.broadcast_to(c[:, 127:128], (h, 128))

    return pl.pallas_call(
        body, name="gate_fwd", out_shape=jax.ShapeDtypeStruct((h, s), F32),
        in_specs=[pl.BlockSpec(memory_space=pltpu.VMEM)] * 2,
        out_specs=pl.BlockSpec(memory_space=pltpu.VMEM),
    )(fg_t, b_col)


def _gate_bwd(drow_t, dcol_t, fg_t, b_col):
    h, s = fg_t.shape
    nb = s // 128

    def body(dr_ref, dc_ref, fg_ref, b_ref, dfg_ref, db_ref):
        low = _tri(128, False)
        carry = jnp.zeros((h, 128), F32)
        db = jnp.zeros((h, 128), F32)
        for j in reversed(range(nb)):
            sl = slice(j * 128, (j + 1) * 128)
            r = jnp.dot(dr_ref[:, 0, sl] - dc_ref[:, 0, sl], low, precision=lax.Precision.HIGHEST,
                        preferred_element_type=F32) + carry
            carry = jnp.broadcast_to(r[:, 0:1], (h, 128))
            z = fg_ref[:, sl] + b_ref[...]
            dfg = r / (1.0 + jnp.exp(z))
            dfg_ref[:, sl] = dfg
            db = db + dfg
        db_ref[...] = jnp.broadcast_to(jnp.sum(db, axis=1, keepdims=True), (h, 128))

    return pl.pallas_call(
        body, name="gate_bwd",
        out_shape=(jax.ShapeDtypeStruct((h, s), F32), jax.ShapeDtypeStruct((h, 128), F32)),
        in_specs=[pl.BlockSpec(memory_space=pltpu.VMEM)] * 4,
        out_specs=(pl.BlockSpec(memory_space=pltpu.VMEM),) * 2,
    )(drow_t, dcol_t, fg_t, b_col)


LOG2E = 1.4426950408889634


def _attn_fwd(p_main, cum4, n_heads, t, side=None):
    s = p_main.shape[0]
    nq = s // t
    scale2 = HEAD_DIM ** -0.5 * LOG2E
    rep = t // 128

    def body(q_ref, k_ref, v_ref, c_ref, o_ref, lse_ref, m_s, l_s, acc_s, s_a, s_b):
        qi = pl.program_id(1)
        m_s[...] = jnp.full_like(m_s, -jnp.inf)
        l_s[...] = jnp.zeros_like(l_s)
        acc_s[...] = jnp.zeros_like(acc_s)

        def keys(ki):
            return pl.ds(pl.multiple_of(ki * t, t), t)

        def logits(ki, buf):
            buf[...] = lax.dot_general(q_ref[...], k_ref[keys(ki), :], (_DIMS["nt"], ((), ())),
                                       preferred_element_type=F32)

        def consume(ki, buf, masked):
            sc = buf[...] * scale2 - c_ref[ki] * LOG2E
            if masked:
                row = lax.broadcasted_iota(jnp.int32, (t, t), 0)
                col = lax.broadcasted_iota(jnp.int32, (t, t), 1)
                sc = jnp.where(col <= row, sc, -jnp.inf)
            m_prev = m_s[...]
            m_new = jnp.maximum(m_prev, jnp.max(sc, axis=1, keepdims=True))
            alpha = jnp.exp2(m_prev - m_new)
            p = jnp.exp2(sc - jnp.tile(m_new, (1, rep)))
            l_s[...] = alpha * l_s[...] + jnp.sum(p, axis=1, keepdims=True)
            acc_s[...] = alpha * acc_s[...] + jnp.dot(p.astype(BF16), v_ref[keys(ki), :], preferred_element_type=F32)
            m_s[...] = m_new

        def pair(j, carry):
            logits(2 * j + 1, s_b)
            consume(2 * j, s_a, False)
            logits(2 * j + 2, s_a)
            consume(2 * j + 1, s_b, False)
            return carry

        logits(0, s_a)
        lax.fori_loop(0, qi // 2, pair, 0)

        @pl.when(qi % 2 == 0)
        def _():
            consume(qi, s_a, True)

        @pl.when(qi % 2 == 1)
        def _():
            logits(qi, s_b)
            consume(qi - 1, s_a, False)
            consume(qi, s_b, True)

        o_ref[...] = (acc_s[...] / l_s[...]).astype(BF16)
        lse_ref[...] = m_s[...] + jnp.log(l_s[...]) * LOG2E

    head = lambda off: pl.BlockSpec((s, 128), lambda h, qi: (0, off + h))
    return _pcall(
        body, name="attn_fwd", grid=(n_heads, nq),
        in_specs=[pl.BlockSpec((t, 128), lambda h, qi: (qi, h)), head(n_heads), head(2 * n_heads),
                  pl.BlockSpec((None, nq, 1, t), lambda h, qi: (h, 0, 0, 0))],
        out_specs=(pl.BlockSpec((t, 128), lambda h, qi: (qi, h)),) * 2,
        out_shape=(jax.ShapeDtypeStruct((s, n_heads * 128), BF16), jax.ShapeDtypeStruct((s, n_heads * 128), F32)),
        scratch_shapes=[pltpu.VMEM((t, 128), F32)] * 3 + [pltpu.VMEM((t, t), F32)] * 2,
        semantics=("parallel", "arbitrary"), args=(p_main, p_main, p_main, cum4), side=side)


def _attn_bwd(p_main, do, lse_rep, delta_rep, cum4, n_heads, t, side=None):
    s = p_main.shape[0]
    nq = s // t
    scale = HEAD_DIM ** -0.5
    rep = t // 128

    def body(q_ref, k_ref, v_ref, do_ref, lse_ref, dl_ref, c_ref,
             dq_ref, dk_ref, dv_ref, drow_ref, dcol_ref, dq_acc, dk_acc, dv_acc, s_a, dp_a, s_b, dp_b):
        kb = pl.program_id(1)

        @pl.when(kb == 0)
        def _():
            dq_acc[...] = jnp.zeros_like(dq_acc)

        dk_acc[...] = jnp.zeros_like(dk_acc)
        dv_acc[...] = jnp.zeros_like(dv_acc)
        k = k_ref[...]
        v = v_ref[...]
        c2 = c_ref[...] * LOG2E
        k_ones = jnp.concatenate([k, jnp.ones((t, 128), BF16)], axis=1)
        q_ones = jnp.ones((t, 128), BF16)

        def rows_of(qb):
            return pl.ds(pl.multiple_of(qb * t, t), t)

        def inputs(qb, s_buf, dp_buf):
            s_buf[...] = lax.dot_general(q_ref[rows_of(qb), :], k, (_DIMS["nt"], ((), ())), preferred_element_type=F32)
            dp_buf[...] = lax.dot_general(do_ref[rows_of(qb), :], v, (_DIMS["nt"], ((), ())), preferred_element_type=F32)

        def consume(qb, s_buf, dp_buf, masked):
            rows = rows_of(qb)
            sc = s_buf[...] * (scale * LOG2E) - c2
            if masked:
                row = lax.broadcasted_iota(jnp.int32, (t, t), 0)
                col = lax.broadcasted_iota(jnp.int32, (t, t), 1)
                sc = jnp.where(col <= row, sc, -jnp.inf)
            p = jnp.exp2(sc - jnp.tile(lse_ref[rows, :], (1, rep)))
            dsb = (p * (dp_buf[...] - jnp.tile(dl_ref[rows, :], (1, rep)))).astype(BF16)
            dv_acc[...] += lax.dot_general(p.astype(BF16), do_ref[rows, :], (_DIMS["tn"], ((), ())),
                                           preferred_element_type=F32)
            dk_acc[...] += lax.dot_general(dsb, jnp.concatenate([q_ref[rows, :], q_ones], axis=1),
                                           (_DIMS["tn"], ((), ())), preferred_element_type=F32)
            dq_acc[rows, :] += jnp.dot(dsb, k_ones, preferred_element_type=F32)

        n_rest = nq - 1 - kb
        inputs(kb, s_a, dp_a)

        @pl.when(n_rest > 0)
        def _():
            inputs(kb + 1, s_b, dp_b)

        consume(kb, s_a, dp_a, True)
        diag = rows_of(kb)
        dq_ref[diag, :] = (dq_acc[diag, 0:128] * scale).astype(BF16)
        drow_ref[:, diag] = dq_acc[diag, 128:256].T[0:8, :]

        def pair(j, carry):
            qb = kb + 1 + 2 * j
            inputs(qb + 1, s_a, dp_a)
            consume(qb, s_b, dp_b, False)
            inputs(qb + 2, s_b, dp_b)
            consume(qb + 1, s_a, dp_a, False)
            return carry

        n_pairs = jnp.maximum(n_rest - 1, 0) // 2
        lax.fori_loop(0, n_pairs, pair, 0)
        left = kb + 1 + 2 * n_pairs

        @pl.when(n_rest - 2 * n_pairs == 1)
        def _():
            consume(left, s_b, dp_b, False)

        @pl.when(n_rest - 2 * n_pairs == 2)
        def _():
            inputs(left + 1, s_a, dp_a)
            consume(left, s_b, dp_b, False)
            consume(left + 1, s_a, dp_a, False)

        dk_ref[...] = (dk_acc[:, 0:128] * scale).astype(BF16)
        dcol_ref[...] = dk_acc[:, 128:256].T[0:8, :]
        dv_ref[...] = dv_acc[...].astype(BF16)

    kside = lambda off: pl.BlockSpec((t, 128), lambda h, kb: (kb, off + h))
    whole = pl.BlockSpec((s, 128), lambda h, kb: (0, h))
    hw = n_heads * 128
    return _pcall(
        body, name="attn_bwd", grid=(n_heads, nq),
        in_specs=[whole, kside(n_heads), kside(2 * n_heads), whole, whole, whole,
                  pl.BlockSpec((None, None, 1, t), lambda h, kb: (h, kb, 0, 0))],
        out_specs=(whole, kside(0), kside(0), pl.BlockSpec((None, 8, s), lambda h, kb: (h, 0, 0)),
                   pl.BlockSpec((None, 8, t), lambda h, kb: (h, 0, kb))),
        out_shape=(jax.ShapeDtypeStruct((s, hw), BF16), jax.ShapeDtypeStruct((s, hw), BF16),
                   jax.ShapeDtypeStruct((s, hw), BF16), jax.ShapeDtypeStruct((n_heads, 8, s), F32),
                   jax.ShapeDtypeStruct((n_heads, 8, s), F32)),
        scratch_shapes=[pltpu.VMEM((s, 256), F32), pltpu.VMEM((t, 256), F32), pltpu.VMEM((t, 128), F32)]
        + [pltpu.VMEM((t, t), F32)] * 4,
        semantics=("parallel", "arbitrary"), args=(p_main, p_main, p_main, do, lse_rep, delta_rep, cum4), side=side)


def _conv_fwd(u_ext, cw_ref):
    r1 = pltpu.roll(u_ext, 1, 0)
    r2 = pltpu.roll(u_ext, 2, 0)
    conv = cw_ref[2:3, :] * u_ext + cw_ref[1:2, :] * r1 + cw_ref[0:1, :] * r2
    return conv[HALO:], r1[HALO:], r2[HALO:]


def _pool_z(pu, pu_halo, row0, tm, pg):
    ext = jnp.concatenate([pu_halo, pu], axis=0)
    t1 = (row0 + lax.broadcasted_iota(jnp.int32, (tm, pg), 0) + 1).astype(F32)
    zs = []
    for g, w in enumerate(POOL_WINDOWS):
        sm = ext[:, g * pg:(g + 1) * pg]
        sh = 1
        while sh < w:
            sm = sm + pltpu.roll(sm, sh, 0)
            sh *= 2
        mean = sm[HALO:] / jnp.minimum(t1, float(w))
        zs.append(mean - pu[:, g * pg:(g + 1) * pg])
    return zs


def _mix_specs(tm, s):
    per = tm // HALO
    last = s // HALO - 1
    cur = lambda w, j: pl.BlockSpec((tm, w), lambda i: (i, j))
    prev = lambda j: pl.BlockSpec((HALO, 512), lambda i: (jnp.maximum(i * per - 1, 0), j))
    nxt = lambda j: pl.BlockSpec((HALO, 512), lambda i: (jnp.minimum((i + 1) * per, last), j))
    full = lambda shape: pl.BlockSpec(shape, lambda i: (0,) * len(shape))
    return cur, prev, nxt, full


def _mix_fwd(p_main, o, conv_w, pool_w, pool_scale, tm):
    s = p_main.shape[0]
    d = 2048
    pg = 128
    cur, prev, nxt, full = _mix_specs(tm, s)

    def body(ga_ref, cb_ref, cc_ref, ch_ref, gc_ref, pu_ref, gp_ref, cch_ref, chh_ref, puh_ref,
             o_ref, cw_ref, pw_ref, ps_ref, y_ref):
        i = pl.program_id(0)
        first = i == 0
        ga = ga_ref[...].astype(F32)
        y_ref[:, 0:1024] = (o_ref[...].astype(F32) * ga * _sigmoid(ga)).astype(BF16)

        u = cc_ref[...].astype(F32) * ch_ref[...].astype(F32)
        uh = jnp.where(first, 0.0, cch_ref[...].astype(F32) * chh_ref[...].astype(F32))
        conv, _, _ = _conv_fwd(jnp.concatenate([uh, u], axis=0), cw_ref)
        gc = gc_ref[...].astype(F32)
        y_ref[:, 1024:1536] = (cb_ref[...].astype(F32) * conv * gc * _sigmoid(gc)).astype(BF16)

        pu = pu_ref[...].astype(F32)
        puh = jnp.where(first, 0.0, puh_ref[...].astype(F32))
        zs = _pool_z(pu, puh, i * tm, tm, pg)
        gp = gp_ref[...].astype(F32)
        gate = gp * _sigmoid(gp) * ps_ref[...]
        for g in range(4):
            r = jnp.dot(zs[g].astype(BF16), pw_ref[g], preferred_element_type=F32)
            y_ref[:, 1536 + g * pg:1536 + (g + 1) * pg] = (r * gate[:, g * pg:(g + 1) * pg]).astype(BF16)

    return pl.pallas_call(
        body, name="mix_fwd", grid=(s // tm,),
        in_specs=[cur(1024, 3), cur(512, 8), cur(512, 9), cur(512, 10), cur(512, 11), cur(512, 12), cur(512, 13),
                  prev(9), prev(10), prev(12),
                  cur(1024, 0), full((3, 512)), full((4, pg, pg)), full((1, 512))],
        out_specs=pl.BlockSpec((tm, d), lambda i: (i, 0)),
        out_shape=jax.ShapeDtypeStruct((s, d), BF16),
        compiler_params=_params(("parallel",)),
    )(*([p_main] * 10), o, conv_w, pool_w, pool_scale)


def _mix_bwd(p_main, o, dy, conv_w, pool_w, pool_scale, tm):
    s = p_main.shape[0]
    pg = 128
    n_heads = 8
    cur, prev, nxt, full = _mix_specs(tm, s)
    nblk = s // tm
    n_ext = tm + HALO

    def silu_and_grad(x):
        sg = _sigmoid(x)
        return x * sg, sg * (1.0 + x * (1.0 - sg))

    def body(ga_ref, cb_ref, cc_ref, ch_ref, gc_ref, pu_ref, gp_ref, cch_ref, chh_ref, puh_ref,
             cbn_ref, gcn_ref, gpn_ref, o_ref, dy_ref, dycn_ref, dypn_ref, cw_ref, pw_ref, ps_ref,
             do_ref, dl_ref, dp_ref, dsm_ref, dpw_ref):
        i = pl.program_id(0)
        first = i == 0
        last = i == nblk - 1

        @pl.when(first)
        def _():
            dsm_ref[...] = jnp.zeros_like(dsm_ref)
            dpw_ref[...] = jnp.zeros_like(dpw_ref)

        ga = ga_ref[...].astype(F32)
        of = o_ref[...].astype(F32)
        dya = dy_ref[:, 0:1024]
        sa, dsa = silu_and_grad(ga)
        dout = dya * sa
        do_ref[...] = dout.astype(BF16)
        dp_ref[:, 0:1024] = (dya * of * dsa).astype(BF16)
        prod = dout * of
        for h in range(n_heads):
            dsum = jnp.sum(prod[:, h * 128:(h + 1) * 128], axis=1, keepdims=True)
            dl_ref[:, h * 128:(h + 1) * 128] = jnp.broadcast_to(dsum, (tm, 128))

        cb = cb_ref[...].astype(F32)
        cc = cc_ref[...].astype(F32)
        ch = ch_ref[...].astype(F32)
        gc = gc_ref[...].astype(F32)
        u = cc * ch
        uh = jnp.where(first, 0.0, cch_ref[...].astype(F32) * chh_ref[...].astype(F32))
        conv, u1, u2 = _conv_fwd(jnp.concatenate([uh, u], axis=0), cw_ref)
        sc_, dsc = silu_and_grad(gc)
        dyc = dy_ref[:, 1024:1536]
        dp_ref[:, 1024:1536] = (dyc * conv * sc_).astype(BF16)
        dp_ref[:, 2560:3072] = (dyc * cb * conv * dsc).astype(BF16)
        dconv = dyc * cb * sc_
        gcn = gcn_ref[...].astype(F32)
        dconv_n = jnp.where(last, 0.0, dycn_ref[...] * cbn_ref[...].astype(F32) * gcn * _sigmoid(gcn))
        dext = jnp.concatenate([dconv, dconv_n], axis=0)
        du = (cw_ref[2:3, :] * dext + cw_ref[1:2, :] * pltpu.roll(dext, n_ext - 1, 0)
              + cw_ref[0:1, :] * pltpu.roll(dext, n_ext - 2, 0))[:tm]
        dp_ref[:, 1536:2048] = (du * ch).astype(BF16)
        dp_ref[:, 2048:2560] = (du * cc).astype(BF16)
        dsm_ref[0:1, :] += jnp.sum(dconv * u2, axis=0, keepdims=True)
        dsm_ref[1:2, :] += jnp.sum(dconv * u1, axis=0, keepdims=True)
        dsm_ref[2:3, :] += jnp.sum(dconv * u, axis=0, keepdims=True)

        pu = pu_ref[...].astype(F32)
        puh = jnp.where(first, 0.0, puh_ref[...].astype(F32))
        zs = _pool_z(pu, puh, i * tm, tm, pg)
        gp = gp_ref[...].astype(F32)
        sp, dsp = silu_and_grad(gp)
        dyp = dy_ref[:, 1536:2048]
        gpn = gpn_ref[...].astype(F32)
        dr_n = jnp.where(last, 0.0, dypn_ref[...] * gpn * _sigmoid(gpn) * ps_ref[...])
        drs = dyp * sp
        dr = drs * ps_ref[...]
        t1 = (i * tm + lax.broadcasted_iota(jnp.int32, (tm, pg), 0) + 1).astype(F32)
        r_parts, dpu_parts = [], []
        for g, w in enumerate(POOL_WINDOWS):
            cols = slice(g * pg, (g + 1) * pg)
            zb = zs[g].astype(BF16)
            r_parts.append(jnp.dot(zb, pw_ref[g], preferred_element_type=F32))
            drb = dr[:, cols].astype(BF16)
            dpw_ref[g] += lax.dot_general(zb, drb, (_DIMS["tn"], ((), ())), preferred_element_type=F32)
            dz = lax.dot_general(drb, pw_ref[g], (_DIMS["nt"], ((), ())), preferred_element_type=F32)
            dz_n = lax.dot_general(dr_n[:, cols].astype(BF16), pw_ref[g], (_DIMS["nt"], ((), ())),
                                   preferred_element_type=F32)
            sm = jnp.concatenate([dz / jnp.minimum(t1, float(w)), dz_n / float(w)], axis=0)
            sh = 1
            while sh < w:
                sm = sm + pltpu.roll(sm, n_ext - sh, 0)
                sh *= 2
            dpu_parts.append(sm[:tm] - dz)
        r = jnp.concatenate(r_parts, axis=1)
        dp_ref[:, 3072:3584] = jnp.concatenate(dpu_parts, axis=1).astype(BF16)
        dp_ref[:, 3584:4096] = (dyp * r * ps_ref[...] * dsp).astype(BF16)
        dsm_ref[3:4, :] += jnp.sum(drs * r, axis=0, keepdims=True)

    dy_next = lambda j: pl.BlockSpec((HALO, 512), lambda i: (jnp.minimum((i + 1) * (tm // HALO), s // HALO - 1), j))
    return pl.pallas_call(
        body, name="mix_bwd", grid=(nblk,),
        in_specs=[cur(1024, 3), cur(512, 8), cur(512, 9), cur(512, 10), cur(512, 11), cur(512, 12), cur(512, 13),
                  prev(9), prev(10), prev(12), nxt(8), nxt(11), nxt(13),
                  cur(1024, 0), pl.BlockSpec((tm, 2048), lambda i: (i, 0)), dy_next(2), dy_next(3),
                  full((3, 512)), full((4, pg, pg)), full((1, 512))],
        out_specs=(pl.BlockSpec((tm, 1024), lambda i: (i, 0)), pl.BlockSpec((tm, 1024), lambda i: (i, 0)),
                   pl.BlockSpec((tm, 4096), lambda i: (i, 0)), full((8, 512)), full((4, pg, pg))),
        out_shape=(jax.ShapeDtypeStruct((s, 1024), BF16), jax.ShapeDtypeStruct((s, 1024), F32),
                   jax.ShapeDtypeStruct((s, 4096), BF16), jax.ShapeDtypeStruct((8, 512), F32),
                   jax.ShapeDtypeStruct((4, pg, pg), F32)),
        compiler_params=_params(("arbitrary",)),
    )(*([p_main] * 13), o, dy, dy, dy, conv_w, pool_w, pool_scale)


def _outproj_ln(y, w_out, x, ln_g, ln_b, alpha, tm, side=None):
    s, d = x.shape

    def body(y_ref, w_ref, x_ref, g_ref, b_ref, xn_ref, xb_ref, pre_ref):
        pre = alpha * x_ref[...] + jnp.dot(y_ref[...], w_ref[...], preferred_element_type=F32)
        mu = jnp.mean(pre, axis=1, keepdims=True)
        cen = pre - mu
        var = jnp.mean(cen * cen, axis=1, keepdims=True)
        xn = cen * lax.rsqrt(var + LN_EPS) * g_ref[...] + b_ref[...]
        pre_ref[...] = pre
        xn_ref[...] = xn
        xb_ref[...] = xn.astype(BF16)

    row = pl.BlockSpec((tm, d), lambda i: (i, 0))
    vec = pl.BlockSpec((1, d), lambda i: (0, 0))
    return _pcall(
        body, name="outproj_ln", grid=(s // tm,),
        in_specs=[row, pl.BlockSpec((d, d), lambda i: (0, 0)), row, vec, vec],
        out_specs=(row, row, row),
        out_shape=(jax.ShapeDtypeStruct((s, d), F32), jax.ShapeDtypeStruct((s, d), BF16),
                   jax.ShapeDtypeStruct((s, d), F32)),
        scratch_shapes=[], semantics=("parallel",), args=(y, w_out, x, ln_g, ln_b), side=side)


def _ln_bwd(dxn, pre, ln_g, tm, side=None):
    s, d = pre.shape

    def body(dx_ref, pre_ref, g_ref, dpre_ref, dpb_ref, dgb_ref):
        @pl.when(pl.program_id(0) == 0)
        def _():
            dgb_ref[...] = jnp.zeros_like(dgb_ref)

        pre_ = pre_ref[...]
        dx = dx_ref[...]
        mu = jnp.mean(pre_, axis=1, keepdims=True)
        cen = pre_ - mu
        var = jnp.mean(cen * cen, axis=1, keepdims=True)
        rstd = lax.rsqrt(var + LN_EPS)
        xhat = cen * rstd
        dxh = dx * g_ref[...]
        dpre = rstd * (dxh - jnp.mean(dxh, axis=1, keepdims=True)
                       - xhat * jnp.mean(dxh * xhat, axis=1, keepdims=True))
        dpre_ref[...] = dpre
        dpb_ref[...] = dpre.astype(BF16)
        dgb_ref[0:1, :] += jnp.sum(dx * xhat, axis=0, keepdims=True)
        dgb_ref[1:2, :] += jnp.sum(dx, axis=0, keepdims=True)

    row = pl.BlockSpec((tm, d), lambda i: (i, 0))
    return _pcall(
        body, name="ln_bwd", grid=(s // tm,),
        in_specs=[row, row, pl.BlockSpec((1, d), lambda i: (0, 0))],
        out_specs=(row, row, pl.BlockSpec((8, d), lambda i: (0, 0))),
        out_shape=(jax.ShapeDtypeStruct((s, d), F32), jax.ShapeDtypeStruct((s, d), BF16),
                   jax.ShapeDtypeStruct((8, d), F32)),
        scratch_shapes=[], semantics=("arbitrary",), args=(dxn, pre, ln_g), side=side)


def _loss_grad(y, target, tm):
    s, d = y.shape

    def body(y_ref, t_ref, dy_ref, loss_ref):
        @pl.when(pl.program_id(0) == 0)
        def _():
            loss_ref[...] = jnp.zeros_like(loss_ref)

        diff = y_ref[...] - t_ref[...]
        dy_ref[...] = diff / d
        loss_ref[...] += 0.5 * jnp.sum(jnp.sum(diff * diff, axis=1, keepdims=True) / d)

    row = pl.BlockSpec((tm, d), lambda i: (i, 0))
    return pl.pallas_call(
        body, name="loss_grad", grid=(s // tm,),
        in_specs=[row, row],
        out_specs=(row, pl.BlockSpec((8, 128), lambda i: (0, 0))),
        out_shape=(jax.ShapeDtypeStruct((s, d), F32), jax.ShapeDtypeStruct((8, 128), F32)),
        compiler_params=_params(("arbitrary",)),
    )(y, target)


def _adamw(w, g, m, v, name, lead_block=None, emit_grad=False):
    a, r, c = w.shape
    tr = r if lead_block else _tile(r, (256, 128, 64, 32, 16, 8))
    c1 = 1.0 - ADAM_B1 ** ADAM_STEP
    c2 = 1.0 - ADAM_B2 ** ADAM_STEP

    def body(w_ref, g_ref, m_ref, v_ref, d_ref, mo_ref, vo_ref, *go_ref):
        g_ = g_ref[...]
        if emit_grad:
            go_ref[0][...] = g_
        m_new = ADAM_B1 * m_ref[...] + (1.0 - ADAM_B1) * g_
        v_new = ADAM_B2 * v_ref[...] + (1.0 - ADAM_B2) * (g_ * g_)
        d_ref[...] = -ADAM_LR * ((m_new / c1) / (jnp.sqrt(v_new / c2) + ADAM_EPS) + ADAM_WD * w_ref[...])
        mo_ref[...] = m_new
        vo_ref[...] = v_new

    ta = lead_block or 1
    spec = pl.BlockSpec((ta, tr, c), lambda i, j: (i, j, 0))
    shp = jax.ShapeDtypeStruct(w.shape, F32)
    n_out = 4 if emit_grad else 3
    return pl.pallas_call(
        body, name=name, grid=(a // ta, r // tr), in_specs=[spec] * 4, out_specs=(spec,) * n_out,
        out_shape=(shp,) * n_out, compiler_params=_params(("parallel", "parallel")),
    )(w, g, m, v)


def _place():
    x, y, c = lax.axis_index("x"), lax.axis_index("y"), lax.axis_index("c")
    return x, y, c, [(1 - x, y), (x, 1 - y), (1 - x, 1 - y)]


def _remote(src, dst, send, recv, k, to):
    return pltpu.make_async_remote_copy(src_ref=src, dst_ref=dst, send_sem=send.at[k], recv_sem=recv.at[k],
                                        device_id=to, device_id_type=MESH)


def _slot(ref, px, py, pc):
    return ref.at[pl.ds(2 * px + py, 1), pl.ds(pc, 1)]


def _side_gather_ici(shard):
    _, two, r, w = shard.shape

    def sends(sins, souts, send, recv):
        x, y, c, chips = _place()
        return [_remote(sins[0].at[:, pl.ds(c, 1)], _slot(souts[0], x, y, c), send, recv, k, (*chip, c))
                for k, chip in enumerate(chips)]

    def start(sins, souts, send, recv):
        for cp in sends(sins, souts, send, recv):
            cp.start()

    def wait(sins, souts, send, recv):
        x, y, c, chips = _place()
        for cp in sends(sins, souts, send, recv):
            cp.wait_send()
        for k, chip in enumerate(chips):
            _remote(_slot(souts[0], *chip, c), _slot(souts[0], *chip, c), send, recv, k, (x, y, c)).wait_recv()

    return _Side([shard], [jax.ShapeDtypeStruct((4, two, r, w), shard.dtype)], 3, start, wait)


def _side_gather_d2d(gathered):
    def sends(souts, send, recv):
        x, y, c, chips = _place()
        return [_remote(_slot(souts[0], *chip, c), _slot(souts[0], *chip, c), send, recv, k, (x, y, 1 - c))
                for k, chip in enumerate(chips)]

    def start(sins, souts, send, recv):
        for cp in sends(souts, send, recv):
            cp.start()

    def wait(sins, souts, send, recv):
        x, y, c, chips = _place()
        for cp in sends(souts, send, recv):
            cp.wait_send()
        for k, chip in enumerate(chips):
            _remote(_slot(souts[0], *chip, 1 - c), _slot(souts[0], *chip, 1 - c), send, recv, k, (x, y, c)).wait_recv()

    return _Side([gathered], [jax.ShapeDtypeStruct(gathered.shape, gathered.dtype)], 3, start, wait, aliases={0: 0})


def _side_swap(g):
    nchip, _, r, w = g.shape

    def copy(sins, souts, send, recv):
        x, y, c, _ = _place()
        return _remote(sins[0].at[:, pl.ds(1 - c, 1)], souts[0], send, recv, 0, (x, y, 1 - c))

    return _Side([g], [jax.ShapeDtypeStruct((nchip, 1, r, w), g.dtype)], 1,
                 lambda *a: copy(*a).start(), lambda *a: copy(*a).wait())


def _side_exchange(hsum):
    _, _, r, w = hsum.shape

    def copies(sins, souts, send, recv):
        x, y, c, chips = _place()
        return [_remote(sins[0].at[pl.ds(2 * px + py, 1)], souts[0].at[pl.ds(k, 1)], send, recv, k, (px, py, c))
                for k, (px, py) in enumerate(chips)]

    def start(*a):
        for cp in copies(*a):
            cp.start()

    def wait(*a):
        for cp in copies(*a):
            cp.wait()

    return _Side([hsum], [jax.ShapeDtypeStruct((3, 1, r, w), hsum.dtype)], 3, start, wait)


def _side_share(f):
    def copy(sins, souts, send, recv):
        x, y, c, _ = _place()
        return _remote(sins[0], souts[0], send, recv, 0, (x, y, 1 - c))

    return _Side([f], [jax.ShapeDtypeStruct(f.shape, f.dtype)], 1,
                 lambda *a: copy(*a).start(), lambda *a: copy(*a).wait())


def _own_slot_filled(gathered, shard):
    chip = 2 * lax.axis_index("x") + lax.axis_index("y")
    return lax.dynamic_update_slice(gathered, shard, (chip, 0, 0, 0))


def _both_parts(mine, theirs):
    mine_first = lax.axis_index("c") == 0
    return jnp.concatenate([jnp.where(mine_first, mine, theirs), jnp.where(mine_first, theirs, mine)], axis=0)


def _plane_tile(r, cdim):
    tr = _tile(r, (256, 128))
    if tr != r:
        return tr, cdim
    return r, _tile(cdim, (256, 128))


def _add_halves(g, recv, core, name):
    nchip, nl, r, cdim = g.shape
    half = nl // 2
    tr, tc = _plane_tile(r, cdim)

    def body(c_ref, g_ref, r_ref, o_ref):
        o_ref[...] = (g_ref[...].astype(F32) + r_ref[...].astype(F32)).astype(BF16)

    blk = (1, 1, tr, tc)
    return pl.pallas_call(
        body, name=name,
        grid_spec=pltpu.PrefetchScalarGridSpec(
            num_scalar_prefetch=1, grid=(nchip, half, r // tr, cdim // tc),
            in_specs=[pl.BlockSpec(blk, lambda j, l, i, k, c_ref: (j, c_ref[0] * half + l, i, k)),
                      pl.BlockSpec(blk, lambda j, l, i, k, c_ref: (j, l, i, k))],
            out_specs=pl.BlockSpec(blk, lambda j, l, i, k, c_ref: (j, l, i, k))),
        out_shape=jax.ShapeDtypeStruct((nchip, half, r, cdim), BF16),
        compiler_params=_params(("parallel",) * 4),
    )(core, g, recv)


def _add_partials(hsum, others, chip, name):
    nchip, half, r, cdim = hsum.shape
    tr, tc = _plane_tile(r, cdim)

    def body(i_ref, h_ref, a_ref, b_ref, c_ref, o_ref):
        o_ref[...] = ((h_ref[...].astype(F32) + a_ref[...].astype(F32))
                      + (b_ref[...].astype(F32) + c_ref[...].astype(F32)))[0]

    blk = (1, 1, tr, tc)
    other = lambda n: pl.BlockSpec(blk, lambda l, i, k, i_ref: (n, l, i, k))
    return pl.pallas_call(
        body, name=name,
        grid_spec=pltpu.PrefetchScalarGridSpec(
            num_scalar_prefetch=1, grid=(half, r // tr, cdim // tc),
            in_specs=[pl.BlockSpec(blk, lambda l, i, k, i_ref: (i_ref[0], l, i, k)), other(0), other(1), other(2)],
            out_specs=pl.BlockSpec((1, tr, tc), lambda l, i, k, i_ref: (l, i, k))),
        out_shape=jax.ShapeDtypeStruct((half, r, cdim), F32),
        compiler_params=_params(("parallel",) * 3),
    )(chip, hsum, others, others, others)


def _all_gather_small(v, name):
    m_per, n = v.shape

    def body(x_ref, out_ref, send_sems, recv_sems, local_sem):
        x, y, c, chips = _place()
        me, sib = (x, y, c), (x, y, 1 - c)

        def rows(px, py, pc):
            return out_ref.at[pl.ds((4 * px + 2 * py + pc) * m_per, m_per), :]

        def copy(k, block, to, src=None):
            return pltpu.make_async_remote_copy(
                src_ref=rows(*block) if src is None else src, dst_ref=rows(*block), send_sem=send_sems.at[k],
                recv_sem=recv_sems.at[k], device_id=to, device_id_type=MESH)

        mine = pltpu.make_async_copy(x_ref, rows(*me), local_sem)
        mine.start()
        first = [copy(0, me, sib, src=x_ref)]
        first += [copy(1 + j, me, (*chip, c), src=x_ref) for j, chip in enumerate(chips)]
        for cp in first:
            cp.start()
        passed = [copy(4 + j, (*chip, c), sib) for j, chip in enumerate(chips)]
        for j, chip in enumerate(chips):
            copy(1 + j, (*chip, c), me).wait_recv()
            passed[j].start()
        copy(0, sib, me).wait_recv()
        for j, chip in enumerate(chips):
            copy(4 + j, (*chip, 1 - c), me).wait_recv()
        for cp in first + passed:
            cp.wait_send()
        mine.wait()

    return pl.pallas_call(
        body, name=name, out_shape=jax.ShapeDtypeStruct((8 * m_per, n), v.dtype),
        in_specs=[pl.BlockSpec(memory_space=pltpu.VMEM)], out_specs=pl.BlockSpec(memory_space=pltpu.VMEM),
        scratch_shapes=[pltpu.SemaphoreType.DMA((7,)), pltpu.SemaphoreType.DMA((7,)), pltpu.SemaphoreType.DMA],
        compiler_params=pltpu.CompilerParams(vmem_limit_bytes=VMEM_LIMIT),
    )(v)


def _sum_blocks(v, nblk, name):
    m_per = v.shape[0] // nblk

    def body(v_ref, o_ref):
        acc = v_ref[0:m_per, :]
        for j in range(1, nblk):
            acc = acc + v_ref[j * m_per:(j + 1) * m_per, :]
        o_ref[...] = acc

    return pl.pallas_call(
        body, name=name, out_shape=jax.ShapeDtypeStruct((m_per, 128), F32),
        in_specs=[pl.BlockSpec(memory_space=pltpu.VMEM)], out_specs=pl.BlockSpec(memory_space=pltpu.VMEM),
        compiler_params=pltpu.CompilerParams(vmem_limit_bytes=VMEM_LIMIT),
    )(v)


def _rows128(a, rows=None):
    flat = a.reshape(-1).astype(F32)
    need = -(-flat.shape[0] // 128)
    rows = rows or -(-need // 8) * 8
    return jnp.pad(flat, (0, rows * 128 - flat.shape[0])).reshape(rows, 128)


def kernel(x, w_in, b_f, conv_w, pool_w, pool_scale, w_out, ln_g, ln_b, loss_target, m_w_in, m_b_f, m_conv_w, m_pool_w, m_pool_scale, m_w_out, m_ln_g, m_ln_b, v_w_in, v_b_f, v_conv_w, v_pool_w, v_pool_scale, v_w_out, v_ln_g, v_ln_b):
    depth, d, in_shard = w_in.shape
    s = x.shape[1]
    att = d // 2
    n_heads = att // HEAD_DIM
    cw = d // 4
    pw = d - att - cw
    in_w = 4 * att + n_heads + 4 * cw + 2 * pw
    assert in_shard * 4 == in_w and (d, n_heads, cw, pw) == (2048, 8, 512, 512) and s % 512 == 0
    main_w = in_w - n_heads
    fg0 = 4 * att
    alpha = (2 * depth) ** 0.25
    t_att = 512
    tm_mix = 256

    cx, cy, cc = lax.axis_index("x"), lax.axis_index("y"), lax.axis_index("c")
    chip = 2 * cx + cy
    core_arr = jnp.reshape(cc, (1,)).astype(jnp.int32)
    chip_arr = jnp.reshape(chip, (1,)).astype(jnp.int32)
    out_rows = w_out.shape[1]

    starts = [j * in_shard for j in range(4)]
    main_start = [g if g < fg0 else g - n_heads for g in starts]
    base = [m // 16 * 16 for m in main_start]
    lead = [m - b for m, b in zip(main_start, base)]
    fg_chip = max(j for j in range(4) if starts[j] <= fg0)
    hole = fg0 - starts[fg_chip]
    blk = -(-(max(lead) + in_shard) // 16) * 16
    assert hole + n_heads <= in_shard and all(b + blk <= main_w for b in base) and base[3] + blk == main_w
    r_comb = out_rows + blk
    pad_rows = -r_comb % 32
    r_half = (r_comb + pad_rows) // 2
    lead_dyn = jnp.take(jnp.array(lead, jnp.int32), chip)
    has_fg = chip == fg_chip

    to_rows = lambda a: jnp.transpose(a, (2, 0, 1))
    from_rows = lambda a: jnp.transpose(a, (1, 2, 0))
    wt_b = jnp.transpose(to_rows(w_in).astype(BF16), (1, 0, 2))
    padded = jnp.pad(wt_b, ((0, 0), (16, 32), (0, 0)))
    plain = lax.dynamic_slice_in_dim(padded, 16 - lead_dyn, blk, axis=1)
    skipping = lax.dynamic_slice_in_dim(padded, 16 - lead_dyn + n_heads, blk, axis=1)
    pos = lax.broadcasted_iota(jnp.int32, (1, blk, 1), 1)
    block = jnp.where(jnp.logical_and(has_fg, pos >= lead_dyn + hole), skipping, plain)
    comb = jnp.concatenate([w_out.astype(BF16), block, jnp.zeros((depth, pad_rows, d), BF16)], axis=1)
    shards = comb.reshape(depth, 2, r_half, d)

    def weights_of(gathered):
        full = gathered.reshape(4, 2 * r_half, d)
        wt_main = sum(jnp.pad(full[j, out_rows:r_comb], ((base[j], main_w - base[j] - blk), (0, 0)))
                      for j in range(4))
        return wt_main, full[:, :out_rows].reshape(d, d)

    first = shards[0][None]
    arrived = _run_side(_side_gather_ici(first), "gather_first_ici")[0]
    arrived = _run_side(_side_gather_d2d(arrived), "gather_first_d2d")[0]
    weights = weights_of(_own_slot_filled(arrived, first))

    small_w = jnp.concatenate([_rows128(conv_w), _rows128(wt_b[:, hole:hole + n_heads])], axis=0)
    n_cw = _rows128(conv_w).shape[0] * 128
    sw_all = _all_gather_small(small_w, "gather_small_weights").reshape(4, 2, small_w.shape[0] * 128)[:, 0]
    conv_full = sw_all[:, :depth * 3 * (cw // 4)].reshape(4, depth, 3, cw // 4)
    conv_full = jnp.moveaxis(conv_full, 0, 2).reshape(depth, 3, cw)
    wt_fg_all = sw_all[fg_chip, n_cw:n_cw + depth * n_heads * d].reshape(depth, n_heads, d).astype(BF16)
    wt_fg_all = jnp.pad(wt_fg_all, ((0, 0), (0, 128 - n_heads), (0, 0)))
    pool_b = pool_w.astype(BF16)

    xf = x[0]
    xb = xf.astype(BF16)
    saved = []
    for l in range(depth):
        wt_main, w_out_l = weights
        wt_fg = wt_fg_all[l]
        nxt = shards[l + 1][None] if l + 1 < depth else None
        p_main = _matmul(xb, wt_main, "nt", BF16, "proj_main")
        fg = _matmul(xb, wt_fg, "nt", F32, "proj_fg")
        fg_t = fg[:, :n_heads].T
        b_col = b_f[l].reshape(n_heads, 1)
        cum4 = _gate_fwd(fg_t, b_col).reshape(n_heads, s // t_att, 1, t_att)
        if nxt is None:
            o, lse = _attn_fwd(p_main, cum4, n_heads, t_att)
        else:
            (o, lse), (arrived,) = _attn_fwd(p_main, cum4, n_heads, t_att, side=_side_gather_ici(nxt))
        y = _mix_fwd(p_main, o, conv_full[l], pool_b[l], pool_scale[l].reshape(1, pw), tm_mix)
        ln_args = (y, w_out_l, xf, ln_g[l].reshape(1, d), ln_b[l].reshape(1, d), alpha, tm_mix)
        if nxt is None:
            x_new, xb_new, pre = _outproj_ln(*ln_args)
        else:
            (x_new, xb_new, pre), (arrived,) = _outproj_ln(*ln_args, side=_side_gather_d2d(arrived))
            weights = weights_of(_own_slot_filled(arrived, nxt))
        saved.append((xb, p_main, fg_t, b_col, cum4, o, lse, y, pre, wt_main, wt_fg, w_out_l))
        xf, xb = x_new, xb_new

    dx, loss_blk = _loss_grad(xf, loss_target[0], tm_mix)

    g_bf, g_sm, g_pw, g_ln, g_fgw, finished = ([None] * depth for _ in range(6))
    pending = None
    for l in reversed(range(depth)):
        xb, p_main, fg_t, b_col, cum4, o, lse, y, pre, wt_main, wt_fg, w_out_l = saved[l]
        if pending is None:
            dpre, dpre_b, g_ln[l] = _ln_bwd(dx, pre, ln_g[l].reshape(1, d), tm_mix)
        else:
            (dpre, dpre_b, g_ln[l]), (recv,) = _ln_bwd(dx, pre, ln_g[l].reshape(1, d), tm_mix,
                                                       side=_side_swap(pending))
            hsum = _add_halves(pending, recv, core_arr, "rs_add_halves")
        dy = _matmul(dpre_b, w_out_l, "nt", F32, "d_y")
        gw_out_l = _matmul(y, dpre_b, "tn", BF16, "d_w_out")
        do, delta, dp_rest, g_sm[l], g_pw[l] = _mix_bwd(p_main, o, dy, conv_full[l], pool_b[l],
                                                        pool_scale[l].reshape(1, pw), tm_mix)
        if pending is None:
            dq, dk, dv, drow, dcol = _attn_bwd(p_main, do, lse, delta, cum4, n_heads, t_att)
        else:
            (dq, dk, dv, drow, dcol), (others,) = _attn_bwd(p_main, do, lse, delta, cum4, n_heads, t_att,
                                                            side=_side_exchange(hsum))
            mine = _add_partials(hsum, others, chip_arr, "rs_add_partials")
        dfg_t, g_bf[l] = _gate_bwd(drow, dcol, fg_t, b_col)
        dfg_b = jnp.pad(dfg_t.T, ((0, 0), (0, 128 - n_heads))).astype(BF16)
        dp_main = jnp.concatenate([dq, dk, dv, dp_rest], axis=1)
        if pending is None:
            gw_main_l = _matmul(dp_main, xb, "tn", BF16, "d_w_main")
        else:
            gw_main_l, (theirs,) = _matmul(dp_main, xb, "tn", BF16, "d_w_main", side=_side_share(mine))
            finished[l + 1] = _both_parts(mine, theirs)
        g_fgw[l] = _matmul(dfg_b, xb, "tn", F32, "d_w_fg")[:n_heads]
        by_chip = jnp.concatenate([gw_out_l.reshape(4, out_rows, d), jnp.stack([gw_main_l[b:b + blk] for b in base]),
                                   jnp.zeros((4, pad_rows, d), BF16)], axis=1)
        pending = by_chip.reshape(4, 2, r_half, d)
        dx = _matmul(dp_main, wt_main, "nn", F32, "d_x_main", add=dpre, add_scale=alpha)
        dx = _matmul(dfg_b, wt_fg, "nn", F32, "d_x_fg", add=dx)

    recv = _run_side(_side_swap(pending), "rs_swap_last")[0]
    hsum = _add_halves(pending, recv, core_arr, "rs_add_halves")
    others = _run_side(_side_exchange(hsum), "rs_exchange_last")[0]
    mine = _add_partials(hsum, others, chip_arr, "rs_add_partials")
    theirs = _run_side(_side_share(mine), "rs_share_last")[0]
    finished[0] = _both_parts(mine, theirs)
    grads = jnp.stack(finished).reshape(depth, 2 * r_half, d)
    grad_w_out = grads[:, :out_rows]

    g_sm = jnp.stack(g_sm)
    g_ln = jnp.stack(g_ln)
    parts = [jnp.stack(g_pw), g_sm[:, 0:3, :], g_sm[:, 3, :], g_ln[:, 0, :], g_ln[:, 1, :],
             jnp.stack(g_bf)[:, :, 0], loss_blk[0, 0], jnp.stack(g_fgw)]
    packed = jnp.concatenate([_rows128(p) for p in parts], axis=0)
    summed = _sum_blocks(_all_gather_small(packed, "gather_small_grads"), 8, "sum_small_grads")
    outs, off = [], 0
    for p in parts:
        nr = _rows128(p).shape[0]
        outs.append(summed[off:off + nr].reshape(-1)[:p.size].reshape(p.shape))
        off += nr
    grad_pool_w, gconv_full, grad_pool_scale, grad_ln_g, grad_ln_b, grad_b_f, loss, grad_fg = outs
    grad_conv_w = lax.dynamic_slice_in_dim(gconv_full, chip * (cw // 4), cw // 4, axis=2)

    gblk = jnp.pad(grads[:, out_rows:r_comb], ((0, 0), (16, 16), (0, 0)))
    plain = lax.dynamic_slice_in_dim(gblk, 16 + lead_dyn, in_shard, axis=1)
    after_fg = lax.dynamic_slice_in_dim(gblk, 16 + lead_dyn - n_heads, in_shard, axis=1)
    fg_placed = jnp.pad(grad_fg, ((0, 0), (hole, in_shard - hole - n_heads), (0, 0)))
    r_idx = lax.broadcasted_iota(jnp.int32, (1, in_shard, 1), 1)
    g_shard = jnp.where(jnp.logical_and(has_fg, r_idx >= hole + n_heads), after_fg,
                        jnp.where(jnp.logical_and(has_fg, r_idx >= hole), fg_placed, plain))
    grad_rows = jnp.transpose(g_shard, (1, 0, 2))

    d_w_in, nm_w_in, nv_w_in, grad_w_in = (from_rows(a) for a in _adamw(
        to_rows(w_in), grad_rows, to_rows(m_w_in), to_rows(v_w_in), "adamw_w_in", lead_block=46, emit_grad=True))
    d_w_out, nm_w_out, nv_w_out = _adamw(w_out, grad_w_out, m_w_out, v_w_out, "adamw_w_out")
    small = [(b_f, grad_b_f, m_b_f, v_b_f), (conv_w, grad_conv_w, m_conv_w, v_conv_w),
             (pool_w, grad_pool_w, m_pool_w, v_pool_w), (pool_scale, grad_pool_scale, m_pool_scale, v_pool_scale),
             (ln_g, grad_ln_g, m_ln_g, v_ln_g), (ln_b, grad_ln_b, m_ln_b, v_ln_b)]
    pk = [jnp.concatenate([_rows128(t[j]) for t in small], axis=0)[None] for j in range(4)]
    sm_out = _adamw(pk[0], pk[1], pk[2], pk[3], "adamw_small", lead_block=1)
    res = {j: [] for j in range(3)}
    off = 0
    for t in small:
        nr = _rows128(t[0]).shape[0]
        for j in range(3):
            res[j].append(sm_out[j][0, off:off + nr].reshape(-1)[:t[0].size].reshape(t[0].shape))
        off += nr
    (d_b_f, d_conv_w, d_pool_w, d_pool_scale, d_ln_g, d_ln_b) = res[0]
    (nm_b_f, nm_conv_w, nm_pool_w, nm_pool_scale, nm_ln_g, nm_ln_b) = res[1]
    (nv_b_f, nv_conv_w, nv_pool_w, nv_pool_scale, nv_ln_g, nv_ln_b) = res[2]

    return (loss, dx[None], grad_w_in, grad_b_f, grad_conv_w, grad_pool_w, grad_pool_scale, grad_w_out,
            grad_ln_g, grad_ln_b,
            d_w_in, d_b_f, d_conv_w, d_pool_w, d_pool_scale, d_w_out, d_ln_g, d_ln_b,
            nm_w_in, nm_b_f, nm_conv_w, nm_pool_w, nm_pool_scale, nm_w_out, nm_ln_g, nm_ln_b,
            nv_w_in, nv_b_f, nv_conv_w, nv_pool_w, nv_pool_scale, nv_w_out, nv_ln_g, nv_ln_b)
```

```python
import functools

import jax
import jax.numpy as jnp
from jax import lax
from jax.experimental import pallas as pl
from jax.experimental.pallas import tpu as pltpu

F32 = jnp.float32
BF16 = jnp.bfloat16
MESH = pl.DeviceIdType.MESH

HEAD_DIM = 128
POOL_WINDOWS = (2, 4, 8, 16)
HALO = 16
LN_EPS = 1e-5
ADAM_LR = 0.001
ADAM_B1 = 0.9
ADAM_B2 = 0.999
ADAM_EPS = 1e-08
ADAM_WD = 0.01
ADAM_STEP = 10
VMEM_LIMIT = 56 * 1024 * 1024


def _params(sem, vmem=VMEM_LIMIT):
    return pltpu.CompilerParams(dimension_semantics=sem, vmem_limit_bytes=vmem)


def _tile(n, prefs):
    for t in prefs:
        if n % t == 0:
            return t
    return n


def _sigmoid(x):
    return 1.0 / (1.0 + jnp.exp(-x))


def _hbm_spec():
    return pl.BlockSpec(memory_space=pltpu.HBM)


class _Side:
    def __init__(self, inputs, out_shapes, n_sems, start, wait, aliases=None):
        self.inputs, self.out_shapes, self.n_sems = list(inputs), list(out_shapes), n_sems
        self.start, self.wait, self.aliases = start, wait, dict(aliases or {})


def _pcall(body, *, name, grid, in_specs, out_specs, out_shape, scratch_shapes, semantics, args, side=None):
    if side is None:
        return pl.pallas_call(
            body, name=name, grid=grid, in_specs=in_specs, out_specs=out_specs, out_shape=out_shape,
            scratch_shapes=scratch_shapes, compiler_params=_params(semantics))(*args)
    single = not isinstance(out_shape, (tuple, list))
    out_specs_l = [out_specs] if single else list(out_specs)
    out_shape_l = [out_shape] if single else list(out_shape)
    n_i, n_si, n_o, n_so, n_s = len(in_specs), len(side.inputs), len(out_shape_l), len(side.out_shapes), len(scratch_shapes)

    def wrapped(*refs):
        ins, sins = refs[:n_i], refs[n_i:n_i + n_si]
        o0 = n_i + n_si
        outs, souts = refs[o0:o0 + n_o], refs[o0 + n_o:o0 + n_o + n_so]
        s0 = o0 + n_o + n_so
        scr, (send, recv) = refs[s0:s0 + n_s], refs[s0 + n_s:]
        first = pl.program_id(0) == 0
        last = pl.program_id(0) == grid[0] - 1
        for ax in range(1, len(grid)):
            first = jnp.logical_and(first, pl.program_id(ax) == 0)
            last = jnp.logical_and(last, pl.program_id(ax) == grid[ax] - 1)

        @pl.when(first)
        def _():
            side.start(sins, souts, send, recv)

        body(*ins, *outs, *scr)

        @pl.when(last)
        def _():
            side.wait(sins, souts, send, recv)

    res = pl.pallas_call(
        wrapped, name=name, grid=grid,
        in_specs=list(in_specs) + [_hbm_spec()] * n_si,
        out_specs=tuple(out_specs_l + [_hbm_spec()] * n_so),
        out_shape=tuple(out_shape_l + side.out_shapes),
        scratch_shapes=list(scratch_shapes) + [pltpu.SemaphoreType.DMA((side.n_sems,))] * 2,
        input_output_aliases={n_i + i: n_o + o for i, o in side.aliases.items()},
        compiler_params=_params(("arbitrary",) * len(grid)),
    )(*args, *side.inputs)
    main = res[:n_o]
    return (main[0] if single else tuple(main)), tuple(res[n_o:])


def _run_side(side, name):
    n_si, n_so = len(side.inputs), len(side.out_shapes)

    def body(*refs):
        sins, souts = refs[:n_si], refs[n_si:n_si + n_so]
        send, recv = refs[n_si + n_so:]
        side.start(sins, souts, send, recv)
        side.wait(sins, souts, send, recv)

    return pl.pallas_call(
        body, name=name, out_shape=tuple(side.out_shapes),
        in_specs=[_hbm_spec()] * n_si, out_specs=tuple([_hbm_spec()] * n_so),
        scratch_shapes=[pltpu.SemaphoreType.DMA((side.n_sems,))] * 2,
        input_output_aliases=side.aliases,
    )(*side.inputs)


_DIMS = {"nn": ((1,), (0,)), "nt": ((1,), (1,)), "tn": ((0,), (0,))}


def _matmul(a, b, form, out_dtype, name, add=None, add_scale=1.0, tm=None, tn=None, tk=None, side=None,
            m_used=None, n_used=None):
    if form == "nn":
        (m, k), (_, n) = a.shape, b.shape
    elif form == "nt":
        (m, k), (n, _) = a.shape, b.shape
    else:
        (k, m), (_, n) = a.shape, b.shape
    m, n = m_used or m, n_used or n
    tm = tm or _tile(m, (1024, 512, 256, 128))
    tn = tn or _tile(n, (1024, 512, 256, 128))
    tk = tk or _tile(k, (2432, 2048, 1792, 1024, 512, 256, 128))
    nk = k // tk
    if form == "tn":
        a_spec = pl.BlockSpec((tk, tm), lambda i, j, kk: (kk, i))
    else:
        a_spec = pl.BlockSpec((tm, tk), lambda i, j, kk: (i, kk))
    if form == "nt":
        b_spec = pl.BlockSpec((tn, tk), lambda i, j, kk: (j, kk))
    else:
        b_spec = pl.BlockSpec((tk, tn), lambda i, j, kk: (kk, j))
    o_spec = pl.BlockSpec((tm, tn), lambda i, j, kk: (i, j))
    dims = (_DIMS[form], ((), ()))
    has_add = add is not None

    def finish(r, add_ref, o_ref):
        if has_add:
            r = r + add_scale * add_ref[...]
        o_ref[...] = r.astype(out_dtype)

    def body_one_pass(a_ref, b_ref, *rest):
        add_ref, o_ref = rest if has_add else (None,) + rest
        finish(lax.dot_general(a_ref[...], b_ref[...], dims, preferred_element_type=F32), add_ref, o_ref)

    def body_accumulate(a_ref, b_ref, *rest):
        add_ref, o_ref, acc = rest if has_add else (None,) + rest
        kk = pl.program_id(2)
        part = lax.dot_general(a_ref[...], b_ref[...], dims, preferred_element_type=F32)

        @pl.when(kk == 0)
        def _():
            acc[...] = part

        @pl.when(jnp.logical_and(kk > 0, kk < nk - 1))
        def _():
            acc[...] += part

        @pl.when(kk == nk - 1)
        def _():
            finish(acc[...] + part, add_ref, o_ref)

    in_specs = [a_spec, b_spec] + ([o_spec] if has_add else [])
    args = (a, b) + ((add,) if has_add else ())
    return _pcall(
        body_one_pass if nk == 1 else body_accumulate, name=name, grid=(m // tm, n // tn, nk),
        in_specs=in_specs, out_specs=o_spec,
        out_shape=jax.ShapeDtypeStruct((m, n), out_dtype),
        scratch_shapes=[] if nk == 1 else [pltpu.VMEM((tm, tn), F32)],
        semantics=("parallel", "parallel", "arbitrary"), args=args, side=side)


def _tri(n, upper):
    r = lax.broadcasted_iota(jnp.int32, (n, n), 0)
    c = lax.broadcasted_iota(jnp.int32, (n, n), 1)
    return jnp.where((r <= c) if upper else (r >= c), 1.0, 0.0).astype(F32)


def _gate_fwd(fg_t, b_col):
    h, s = fg_t.shape
    nb = s // 128

    def body(fg_ref, b_ref, cum_ref):
        u = _tri(128, True)
        carry = jnp.zeros((h, 128), F32)
        for j in range(nb):
            z = fg_ref[:, j * 128:(j + 1) * 128] + b_ref[...]
            logf = -(jnp.maximum(-z, 0.0) + jnp.log(1.0 + jnp.exp(-jnp.abs(z))))
            c = jnp.dot(logf, u, precision=lax.Precision.HIGHEST, preferred_element_type=F32) + carry
            cum_ref[:, j * 128:(j + 1) * 128] = c
            carry = jnp.broadcast_to(c[:, 127:128], (h, 128))

    return pl.pallas_call(
        body, name="gate_fwd", out_shape=jax.ShapeDtypeStruct((h, s), F32),
        in_specs=[pl.BlockSpec(memory_space=pltpu.VMEM)] * 2,
        out_specs=pl.BlockSpec(memory_space=pltpu.VMEM),
    )(fg_t, b_col)


def _gate_bwd(drow_t, dcol_t, fg_t, b_col):
    h, s = fg_t.shape
    nb = s // 128

    def body(dr_ref, dc_ref, fg_ref, b_ref, dfg_ref, db_ref):
        low = _tri(128, False)
        carry = jnp.zeros((h, 128), F32)
        db = jnp.zeros((h, 128), F32)
        for j in reversed(range(nb)):
            sl = slice(j * 128, (j + 1) * 128)
            r = jnp.dot(dr_ref[:, 0, sl] - dc_ref[:, 0, sl], low, precision=lax.Precision.HIGHEST,
                        preferred_element_type=F32) + carry
            carry = jnp.broadcast_to(r[:, 0:1], (h, 128))
            z = fg_ref[:, sl] + b_ref[...]
            dfg = r / (1.0 + jnp.exp(z))
            dfg_ref[:, sl] = dfg
            db = db + dfg
        db_ref[...] = jnp.broadcast_to(jnp.sum(db, axis=1, keepdims=True), (h, 128))

    return pl.pallas_call(
        body, name="gate_bwd",
        out_shape=(jax.ShapeDtypeStruct((h, s), F32), jax.ShapeDtypeStruct((h, 128), F32)),
        in_specs=[pl.BlockSpec(memory_space=pltpu.VMEM)] * 4,
        out_specs=(pl.BlockSpec(memory_space=pltpu.VMEM),) * 2,
    )(drow_t, dcol_t, fg_t, b_col)


LOG2E = 1.4426950408889634


def _attn_fwd(p_main, cum4, n_heads, t, side=None):
    s = p_main.shape[0]
    nq = s // t
    scale2 = HEAD_DIM ** -0.5 * LOG2E
    rep = t // 128

    def body(q_ref, k_ref, v_ref, c_ref, o_ref, lse_ref, m_s, l_s, acc_s, s_a, s_b):
        qi = pl.program_id(1)
        m_s[...] = jnp.full_like(m_s, -jnp.inf)
        l_s[...] = jnp.zeros_like(l_s)
        acc_s[...] = jnp.zeros_like(acc_s)

        def keys(ki):
            return pl.ds(pl.multiple_of(ki * t, t), t)

        def logits(ki, buf):
            buf[...] = lax.dot_general(q_ref[...], k_ref[keys(ki), :], (_DIMS["nt"], ((), ())),
                                       preferred_element_type=F32)

        def consume(ki, buf, masked):
            sc = buf[...] * scale2 - c_ref[ki] * LOG2E
            if masked:
                row = lax.broadcasted_iota(jnp.int32, (t, t), 0)
                col = lax.broadcasted_iota(jnp.int32, (t, t), 1)
                sc = jnp.where(col <= row, sc, -jnp.inf)
            m_prev = m_s[...]
            m_new = jnp.maximum(m_prev, jnp.max(sc, axis=1, keepdims=True))
            alpha = jnp.exp2(m_prev - m_new)
            p = jnp.exp2(sc - jnp.tile(m_new, (1, rep)))
            l_s[...] = alpha * l_s[...] + jnp.sum(p, axis=1, keepdims=True)
            acc_s[...] = alpha * acc_s[...] + jnp.dot(p.astype(BF16), v_ref[keys(ki), :], preferred_element_type=F32)
            m_s[...] = m_new

        def pair(j, carry):
            logits(2 * j + 1, s_b)
            consume(2 * j, s_a, False)
            logits(2 * j + 2, s_a)
            consume(2 * j + 1, s_b, False)
            return carry

        logits(0, s_a)
        lax.fori_loop(0, qi // 2, pair, 0)

        @pl.when(qi % 2 == 0)
        def _():
            consume(qi, s_a, True)

        @pl.when(qi % 2 == 1)
        def _():
            logits(qi, s_b)
            consume(qi - 1, s_a, False)
            consume(qi, s_b, True)

        o_ref[...] = (acc_s[...] / l_s[...]).astype(BF16)
        lse_ref[...] = m_s[...] + jnp.log(l_s[...]) * LOG2E

    head = lambda off: pl.BlockSpec((s, 128), lambda h, qi: (0, off + h))
    return _pcall(
        body, name="attn_fwd", grid=(n_heads, nq),
        in_specs=[pl.BlockSpec((t, 128), lambda h, qi: (qi, h)), head(n_heads), head(2 * n_heads),
                  pl.BlockSpec((None, nq, 1, t), lambda h, qi: (h, 0, 0, 0))],
        out_specs=(pl.BlockSpec((t, 128), lambda h, qi: (qi, h)),) * 2,
        out_shape=(jax.ShapeDtypeStruct((s, n_heads * 128), BF16), jax.ShapeDtypeStruct((s, n_heads * 128), F32)),
        scratch_shapes=[pltpu.VMEM((t, 128), F32)] * 3 + [pltpu.VMEM((t, t), F32)] * 2,
        semantics=("parallel", "arbitrary"), args=(p_main, p_main, p_main, cum4), side=side)


def _attn_bwd(p_main, do, lse_rep, delta_rep, cum4, n_heads, t, side=None):
    s = p_main.shape[0]
    nq = s // t
    scale = HEAD_DIM ** -0.5
    rep = t // 128

    def body(q_ref, k_ref, v_ref, do_ref, lse_ref, dl_ref, c_ref,
             dq_ref, dk_ref, dv_ref, drow_ref, dcol_ref, dq_acc, dk_acc, dv_acc, s_a, dp_a, s_b, dp_b):
        kb = pl.program_id(1)

        @pl.when(kb == 0)
        def _():
            dq_acc[...] = jnp.zeros_like(dq_acc)

        dk_acc[...] = jnp.zeros_like(dk_acc)
        dv_acc[...] = jnp.zeros_like(dv_acc)
        k = k_ref[...]
        v = v_ref[...]
        c2 = c_ref[...] * LOG2E
        k_ones = jnp.concatenate([k, jnp.ones((t, 128), BF16)], axis=1)
        q_ones = jnp.ones((t, 128), BF16)

        def rows_of(qb):
            return pl.ds(pl.multiple_of(qb * t, t), t)

        def inputs(qb, s_buf, dp_buf):
            s_buf[...] = lax.dot_general(q_ref[rows_of(qb), :], k, (_DIMS["nt"], ((), ())), preferred_element_type=F32)
            dp_buf[...] = lax.dot_general(do_ref[rows_of(qb), :], v, (_DIMS["nt"], ((), ())), preferred_element_type=F32)

        def consume(qb, s_buf, dp_buf, masked):
            rows = rows_of(qb)
            sc = s_buf[...] * (scale * LOG2E) - c2
            if masked:
                row = lax.broadcasted_iota(jnp.int32, (t, t), 0)
                col = lax.broadcasted_iota(jnp.int32, (t, t), 1)
                sc = jnp.where(col <= row, sc, -jnp.inf)
            p = jnp.exp2(sc - jnp.tile(lse_ref[rows, :], (1, rep)))
            dsb = (p * (dp_buf[...] - jnp.tile(dl_ref[rows, :], (1, rep)))).astype(BF16)
            dv_acc[...] += lax.dot_general(p.astype(BF16), do_ref[rows, :], (_DIMS["tn"], ((), ())),
                                           preferred_element_type=F32)
            dk_acc[...] += lax.dot_general(dsb, jnp.concatenate([q_ref[rows, :], q_ones], axis=1),
                                           (_DIMS["tn"], ((), ())), preferred_element_type=F32)
            dq_acc[rows, :] += jnp.dot(dsb, k_ones, preferred_element_type=F32)

        n_rest = nq - 1 - kb
        inputs(kb, s_a, dp_a)

        @pl.when(n_rest > 0)
        def _():
            inputs(kb + 1, s_b, dp_b)

        consume(kb, s_a, dp_a, True)
        diag = rows_of(kb)
        dq_ref[diag, :] = (dq_acc[diag, 0:128] * scale).astype(BF16)
        drow_ref[:, diag] = dq_acc[diag, 128:256].T[0:8, :]

        def pair(j, carry):
            qb = kb + 1 + 2 * j
            inputs(qb + 1, s_a, dp_a)
            consume(qb, s_b, dp_b, False)
            inputs(qb + 2, s_b, dp_b)
            consume(qb + 1, s_a, dp_a, False)
            return carry

        n_pairs = jnp.maximum(n_rest - 1, 0) // 2
        lax.fori_loop(0, n_pairs, pair, 0)
        left = kb + 1 + 2 * n_pairs

        @pl.when(n_rest - 2 * n_pairs == 1)
        def _():
            consume(left, s_b, dp_b, False)

        @pl.when(n_rest - 2 * n_pairs == 2)
        def _():
            inputs(left + 1, s_a, dp_a)
            consume(left, s_b, dp_b, False)
            consume(left + 1, s_a, dp_a, False)

        dk_ref[...] = (dk_acc[:, 0:128] * scale).astype(BF16)
        dcol_ref[...] = dk_acc[:, 128:256].T[0:8, :]
        dv_ref[...] = dv_acc[...].astype(BF16)

    kside = lambda off: pl.BlockSpec((t, 128), lambda h, kb: (kb, off + h))
    whole = pl.BlockSpec((s, 128), lambda h, kb: (0, h))
    hw = n_heads * 128
    return _pcall(
        body, name="attn_bwd", grid=(n_heads, nq),
        in_specs=[whole, kside(n_heads), kside(2 * n_heads), whole, whole, whole,
                  pl.BlockSpec((None, None, 1, t), lambda h, kb: (h, kb, 0, 0))],
        out_specs=(whole, kside(0), kside(0), pl.BlockSpec((None, 8, s), lambda h, kb: (h, 0, 0)),
                   pl.BlockSpec((None, 8, t), lambda h, kb: (h, 0, kb))),
        out_shape=(jax.ShapeDtypeStruct((s, hw), BF16), jax.ShapeDtypeStruct((s, hw), BF16),
                   jax.ShapeDtypeStruct((s, hw), BF16), jax.ShapeDtypeStruct((n_heads, 8, s), F32),
                   jax.ShapeDtypeStruct((n_heads, 8, s), F32)),
        scratch_shapes=[pltpu.VMEM((s, 256), F32), pltpu.VMEM((t, 256), F32), pltpu.VMEM((t, 128), F32)]
        + [pltpu.VMEM((t, t), F32)] * 4,
        semantics=("parallel", "arbitrary"), args=(p_main, p_main, p_main, do, lse_rep, delta_rep, cum4), side=side)


def _conv_fwd(u_ext, cw_ref):
    r1 = pltpu.roll(u_ext, 1, 0)
    r2 = pltpu.roll(u_ext, 2, 0)
    conv = cw_ref[2:3, :] * u_ext + cw_ref[1:2, :] * r1 + cw_ref[0:1, :] * r2
    return conv[HALO:], r1[HALO:], r2[HALO:]


def _pool_z(pu, pu_halo, row0, tm, pg):
    ext = jnp.concatenate([pu_halo, pu], axis=0)
    t1 = (row0 + lax.broadcasted_iota(jnp.int32, (tm, pg), 0) + 1).astype(F32)
    zs = []
    for g, w in enumerate(POOL_WINDOWS):
        sm = ext[:, g * pg:(g + 1) * pg]
        sh = 1
        while sh < w:
            sm = sm + pltpu.roll(sm, sh, 0)
            sh *= 2
        mean = sm[HALO:] / jnp.minimum(t1, float(w))
        zs.append(mean - pu[:, g * pg:(g + 1) * pg])
    return zs


def _mix_specs(tm, s):
    per = tm // HALO
    last = s // HALO - 1
    cur = lambda w, j: pl.BlockSpec((tm, w), lambda i: (i, j))
    prev = lambda j: pl.BlockSpec((HALO, 512), lambda i: (jnp.maximum(i * per - 1, 0), j))
    nxt = lambda j: pl.BlockSpec((HALO, 512), lambda i: (jnp.minimum((i + 1) * per, last), j))
    full = lambda shape: pl.BlockSpec(shape, lambda i: (0,) * len(shape))
    return cur, prev, nxt, full


def _mix_fwd(p_main, o, conv_w, pool_w, pool_scale, tm):
    s = p_main.shape[0]
    d = 2048
    pg = 128
    cur, prev, nxt, full = _mix_specs(tm, s)

    def body(ga_ref, cb_ref, cc_ref, ch_ref, gc_ref, pu_ref, gp_ref, cch_ref, chh_ref, puh_ref,
             o_ref, cw_ref, pw_ref, ps_ref, y_ref):
        i = pl.program_id(0)
        first = i == 0
        ga = ga_ref[...].astype(F32)
        y_ref[:, 0:1024] = (o_ref[...].astype(F32) * ga * _sigmoid(ga)).astype(BF16)

        u = cc_ref[...].astype(F32) * ch_ref[...].astype(F32)
        uh = jnp.where(first, 0.0, cch_ref[...].astype(F32) * chh_ref[...].astype(F32))
        conv, _, _ = _conv_fwd(jnp.concatenate([uh, u], axis=0), cw_ref)
        gc = gc_ref[...].astype(F32)
        y_ref[:, 1024:1536] = (cb_ref[...].astype(F32) * conv * gc * _sigmoid(gc)).astype(BF16)

        pu = pu_ref[...].astype(F32)
        puh = jnp.where(first, 0.0, puh_ref[...].astype(F32))
        zs = _pool_z(pu, puh, i * tm, tm, pg)
        gp = gp_ref[...].astype(F32)
        gate = gp * _sigmoid(gp) * ps_ref[...]
        for g in range(4):
            r = jnp.dot(zs[g].astype(BF16), pw_ref[g], preferred_element_type=F32)
            y_ref[:, 1536 + g * pg:1536 + (g + 1) * pg] = (r * gate[:, g * pg:(g + 1) * pg]).astype(BF16)

    return pl.pallas_call(
        body, name="mix_fwd", grid=(s // tm,),
        in_specs=[cur(1024, 3), cur(512, 8), cur(512, 9), cur(512, 10), cur(512, 11), cur(512, 12), cur(512, 13),
                  prev(9), prev(10), prev(12),
                  cur(1024, 0), full((3, 512)), full((4, pg, pg)), full((1, 512))],
        out_specs=pl.BlockSpec((tm, d), lambda i: (i, 0)),
        out_shape=jax.ShapeDtypeStruct((s, d), BF16),
        compiler_params=_params(("parallel",)),
    )(*([p_main] * 10), o, conv_w, pool_w, pool_scale)


def _mix_bwd(p_main, o, dy, conv_w, pool_w, pool_scale, tm):
    s = p_main.shape[0]
    pg = 128
    n_heads = 8
    cur, prev, nxt, full = _mix_specs(tm, s)
    nblk = s // tm
    n_ext = tm + HALO

    def silu_and_grad(x):
        sg = _sigmoid(x)
        return x * sg, sg * (1.0 + x * (1.0 - sg))

    def body(ga_ref, cb_ref, cc_ref, ch_ref, gc_ref, pu_ref, gp_ref, cch_ref, chh_ref, puh_ref,
             cbn_ref, gcn_ref, gpn_ref, o_ref, dy_ref, dycn_ref, dypn_ref, cw_ref, pw_ref, ps_ref,
             do_ref, dl_ref, dp_ref, dsm_ref, dpw_ref):
        i = pl.program_id(0)
        first = i == 0
        last = i == nblk - 1

        @pl.when(first)
        def _():
            dsm_ref[...] = jnp.zeros_like(dsm_ref)
            dpw_ref[...] = jnp.zeros_like(dpw_ref)

        ga = ga_ref[...].astype(F32)
        of = o_ref[...].astype(F32)
        dya = dy_ref[:, 0:1024]
        sa, dsa = silu_and_grad(ga)
        dout = dya * sa
        do_ref[...] = dout.astype(BF16)
        dp_ref[:, 0:1024] = (dya * of * dsa).astype(BF16)
        prod = dout * of
        for h in range(n_heads):
            dsum = jnp.sum(prod[:, h * 128:(h + 1) * 128], axis=1, keepdims=True)
            dl_ref[:, h * 128:(h + 1) * 128] = jnp.broadcast_to(dsum, (tm, 128))

        cb = cb_ref[...].astype(F32)
        cc = cc_ref[...].astype(F32)
        ch = ch_ref[...].astype(F32)
        gc = gc_ref[...].astype(F32)
        u = cc * ch
        uh = jnp.where(first, 0.0, cch_ref[...].astype(F32) * chh_ref[...].astype(F32))
        conv, u1, u2 = _conv_fwd(jnp.concatenate([uh, u], axis=0), cw_ref)
        sc_, dsc = silu_and_grad(gc)
        dyc = dy_ref[:, 1024:1536]
        dp_ref[:, 1024:1536] = (dyc * conv * sc_).astype(BF16)
        dp_ref[:, 2560:3072] = (dyc * cb * conv * dsc).astype(BF16)
        dconv = dyc * cb * sc_
        gcn = gcn_ref[...].astype(F32)
        dconv_n = jnp.where(last, 0.0, dycn_ref[...] * cbn_ref[...].astype(F32) * gcn * _sigmoid(gcn))
        dext = jnp.concatenate([dconv, dconv_n], axis=0)
        du = (cw_ref[2:3, :] * dext + cw_ref[1:2, :] * pltpu.roll(dext, n_ext - 1, 0)
              + cw_ref[0:1, :] * pltpu.roll(dext, n_ext - 2, 0))[:tm]
        dp_ref[:, 1536:2048] = (du * ch).astype(BF16)
        dp_ref[:, 2048:2560] = (du * cc).astype(BF16)
        dsm_ref[0:1, :] += jnp.sum(dconv * u2, axis=0, keepdims=True)
        dsm_ref[1:2, :] += jnp.sum(dconv * u1, axis=0, keepdims=True)
        dsm_ref[2:3, :] += jnp.sum(dconv * u, axis=0, keepdims=True)

        pu = pu_ref[...].astype(F32)
        puh = jnp.where(first, 0.0, puh_ref[...].astype(F32))
        zs = _pool_z(pu, puh, i * tm, tm, pg)
        gp = gp_ref[...].astype(F32)
        sp, dsp = silu_and_grad(gp)
        dyp = dy_ref[:, 1536:2048]
        gpn = gpn_ref[...].astype(F32)
        dr_n = jnp.where(last, 0.0, dypn_ref[...] * gpn * _sigmoid(gpn) * ps_ref[...])
        drs = dyp * sp
        dr = drs * ps_ref[...]
        t1 = (i * tm + lax.broadcasted_iota(jnp.int32, (tm, pg), 0) + 1).astype(F32)
        r_parts, dpu_parts = [], []
        for g, w in enumerate(POOL_WINDOWS):
            cols = slice(g * pg, (g + 1) * pg)
            zb = zs[g].astype(BF16)
            r_parts.append(jnp.dot(zb, pw_ref[g], preferred_element_type=F32))
            drb = dr[:, cols].astype(BF16)
            dpw_ref[g] += lax.dot_general(zb, drb, (_DIMS["tn"], ((), ())), preferred_element_type=F32)
            dz = lax.dot_general(drb, pw_ref[g], (_DIMS["nt"], ((), ())), preferred_element_type=F32)
            dz_n = lax.dot_general(dr_n[:, cols].astype(BF16), pw_ref[g], (_DIMS["nt"], ((), ())),
                                   preferred_element_type=F32)
            sm = jnp.concatenate([dz / jnp.minimum(t1, float(w)), dz_n / float(w)], axis=0)
            sh = 1
            while sh < w:
                sm = sm + pltpu.roll(sm, n_ext - sh, 0)
                sh *= 2
            dpu_parts.append(sm[:tm] - dz)
        r = jnp.concatenate(r_parts, axis=1)
        dp_ref[:, 3072:3584] = jnp.concatenate(dpu_parts, axis=1).astype(BF16)
        dp_ref[:, 3584:4096] = (dyp * r * ps_ref[...] * dsp).astype(BF16)
        dsm_ref[3:4, :] += jnp.sum(drs * r, axis=0, keepdims=True)

    dy_next = lambda j: pl.BlockSpec((HALO, 512), lambda i: (jnp.minimum((i + 1) * (tm // HALO), s // HALO - 1), j))
    return pl.pallas_call(
        body, name="mix_bwd", grid=(nblk,),
        in_specs=[cur(1024, 3), cur(512, 8), cur(512, 9), cur(512, 10), cur(512, 11), cur(512, 12), cur(512, 13),
                  prev(9), prev(10), prev(12), nxt(8), nxt(11), nxt(13),
                  cur(1024, 0), pl.BlockSpec((tm, 2048), lambda i: (i, 0)), dy_next(2), dy_next(3),
                  full((3, 512)), full((4, pg, pg)), full((1, 512))],
        out_specs=(pl.BlockSpec((tm, 1024), lambda i: (i, 0)), pl.BlockSpec((tm, 1024), lambda i: (i, 0)),
                   pl.BlockSpec((tm, 4096), lambda i: (i, 0)), full((8, 512)), full((4, pg, pg))),
        out_shape=(jax.ShapeDtypeStruct((s, 1024), BF16), jax.ShapeDtypeStruct((s, 1024), F32),
                   jax.ShapeDtypeStruct((s, 4096), BF16), jax.ShapeDtypeStruct((8, 512), F32),
                   jax.ShapeDtypeStruct((4, pg, pg), F32)),
        compiler_params=_params(("arbitrary",)),
    )(*([p_main] * 13), o, dy, dy, dy, conv_w, pool_w, pool_scale)


def _outproj_ln(y, w_out, x, ln_g, ln_b, alpha, tm, side=None):
    s, d = x.shape

    def body(y_ref, w_ref, x_ref, g_ref, b_ref, xn_ref, xb_ref, pre_ref):
        pre = alpha * x_ref[...] + jnp.dot(y_ref[...], w_ref[...], preferred_element_type=F32)
        mu = jnp.mean(pre, axis=1, keepdims=True)
        cen = pre - mu
        var = jnp.mean(cen * cen, axis=1, keepdims=True)
        xn = cen * lax.rsqrt(var + LN_EPS) * g_ref[...] + b_ref[...]
        pre_ref[...] = pre
        xn_ref[...] = xn
        xb_ref[...] = xn.astype(BF16)

    row = pl.BlockSpec((tm, d), lambda i: (i, 0))
    vec = pl.BlockSpec((1, d), lambda i: (0, 0))
    return _pcall(
        body, name="outproj_ln", grid=(s // tm,),
        in_specs=[row, pl.BlockSpec((d, d), lambda i: (0, 0)), row, vec, vec],
        out_specs=(row, row, row),
        out_shape=(jax.ShapeDtypeStruct((s, d), F32), jax.ShapeDtypeStruct((s, d), BF16),
                   jax.ShapeDtypeStruct((s, d), F32)),
        scratch_shapes=[], semantics=("parallel",), args=(y, w_out, x, ln_g, ln_b), side=side)


def _ln_bwd(dxn, pre, ln_g, tm, side=None):
    s, d = pre.shape

    def body(dx_ref, pre_ref, g_ref, dpre_ref, dpb_ref, dgb_ref):
        @pl.when(pl.program_id(0) == 0)
        def _():
            dgb_ref[...] = jnp.zeros_like(dgb_ref)

        pre_ = pre_ref[...]
        dx = dx_ref[...]
        mu = jnp.mean(pre_, axis=1, keepdims=True)
        cen = pre_ - mu
        var = jnp.mean(cen * cen, axis=1, keepdims=True)
        rstd = lax.rsqrt(var + LN_EPS)
        xhat = cen * rstd
        dxh = dx * g_ref[...]
        dpre = rstd * (dxh - jnp.mean(dxh, axis=1, keepdims=True)
                       - xhat * jnp.mean(dxh * xhat, axis=1, keepdims=True))
        dpre_ref[...] = dpre
        dpb_ref[...] = dpre.astype(BF16)
        dgb_ref[0:1, :] += jnp.sum(dx * xhat, axis=0, keepdims=True)
        dgb_ref[1:2, :] += jnp.sum(dx, axis=0, keepdims=True)

    row = pl.BlockSpec((tm, d), lambda i: (i, 0))
    return _pcall(
        body, name="ln_bwd", grid=(s // tm,),
        in_specs=[row, row, pl.BlockSpec((1, d), lambda i: (0, 0))],
        out_specs=(row, row, pl.BlockSpec((8, d), lambda i: (0, 0))),
        out_shape=(jax.ShapeDtypeStruct((s, d), F32), jax.ShapeDtypeStruct((s, d), BF16),
                   jax.ShapeDtypeStruct((8, d), F32)),
        scratch_shapes=[], semantics=("arbitrary",), args=(dxn, pre, ln_g), side=side)


def _loss_grad(y, target, tm):
    s, d = y.shape

    def body(y_ref, t_ref, dy_ref, loss_ref):
        @pl.when(pl.program_id(0) == 0)
        def _():
            loss_ref[...] = jnp.zeros_like(loss_ref)

        diff = y_ref[...] - t_ref[...]
        dy_ref[...] = diff / d
        loss_ref[...] += 0.5 * jnp.sum(jnp.sum(diff * diff, axis=1, keepdims=True) / d)

    row = pl.BlockSpec((tm, d), lambda i: (i, 0))
    return pl.pallas_call(
        body, name="loss_grad", grid=(s // tm,),
        in_specs=[row, row],
        out_specs=(row, pl.BlockSpec((8, 128), lambda i: (0, 0))),
        out_shape=(jax.ShapeDtypeStruct((s, d), F32), jax.ShapeDtypeStruct((8, 128), F32)),
        compiler_params=_params(("arbitrary",)),
    )(y, target)


def _adamw(w, g, m, v, name, lead_block=None, emit_grad=False):
    a, r, c = w.shape
    tr = r if lead_block else _tile(r, (256, 128, 64, 32, 16, 8))
    c1 = 1.0 - ADAM_B1 ** ADAM_STEP
    c2 = 1.0 - ADAM_B2 ** ADAM_STEP

    def body(w_ref, g_ref, m_ref, v_ref, d_ref, mo_ref, vo_ref, *go_ref):
        g_ = g_ref[...]
        if emit_grad:
            go_ref[0][...] = g_
        m_new = ADAM_B1 * m_ref[...] + (1.0 - ADAM_B1) * g_
        v_new = ADAM_B2 * v_ref[...] + (1.0 - ADAM_B2) * (g_ * g_)
        d_ref[...] = -ADAM_LR * ((m_new / c1) / (jnp.sqrt(v_new / c2) + ADAM_EPS) + ADAM_WD * w_ref[...])
        mo_ref[...] = m_new
        vo_ref[...] = v_new

    ta = lead_block or 1
    spec = pl.BlockSpec((ta, tr, c), lambda i, j: (i, j, 0))
    shp = jax.ShapeDtypeStruct(w.shape, F32)
    n_out = 4 if emit_grad else 3
    return pl.pallas_call(
        body, name=name, grid=(a // ta, r // tr), in_specs=[spec] * 4, out_specs=(spec,) * n_out,
        out_shape=(shp,) * n_out, compiler_params=_params(("parallel", "parallel")),
    )(w, g, m, v)


def _place():
    x, y, c = lax.axis_index("x"), lax.axis_index("y"), lax.axis_index("c")
    return x, y, c, [(1 - x, y), (x, 1 - y), (1 - x, 1 - y)]


def _remote(src, dst, send, recv, k, to):
    return pltpu.make_async_remote_copy(src_ref=src, dst_ref=dst, send_sem=send.at[k], recv_sem=recv.at[k],
                                        device_id=to, device_id_type=MESH)


def _slot(ref, px, py, pc):
    return ref.at[pl.ds(2 * px + py, 1), pl.ds(pc, 1)]


def _side_gather_ici(shard, rows=None, into=None):
    _, two, r, w = shard.shape
    lo, hi = rows or (0, r)

    def _slot(ref, px, py, pc):
        return ref.at[pl.ds(2 * px + py, 1), pl.ds(pc, 1), pl.ds(lo, hi - lo)]

    def sends(sins, souts, send, recv):
        x, y, c, chips = _place()
        return [_remote(sins[0].at[:, pl.ds(c, 1), pl.ds(lo, hi - lo)], _slot(souts[0], x, y, c), send, recv, k,
                        (*chip, c)) for k, chip in enumerate(chips)]

    def start(sins, souts, send, recv):
        for cp in sends(sins, souts, send, recv):
            cp.start()

    def wait(sins, souts, send, recv):
        x, y, c, chips = _place()
        for cp in sends(sins, souts, send, recv):
            cp.wait_send()
        for k, chip in enumerate(chips):
            _remote(_slot(souts[0], *chip, c), _slot(souts[0], *chip, c), send, recv, k, (x, y, c)).wait_recv()

    more = [] if into is None else [into]
    return _Side([shard] + more, [jax.ShapeDtypeStruct((4, two, r, w), shard.dtype)], 3, start, wait,
                 aliases={} if into is None else {1: 0})


def _side_gather_d2d(gathered):
    def sends(souts, send, recv):
        x, y, c, chips = _place()
        return [_remote(_slot(souts[0], *chip, c), _slot(souts[0], *chip, c), send, recv, k, (x, y, 1 - c))
                for k, chip in enumerate(chips)]

    def start(sins, souts, send, recv):
        for cp in sends(souts, send, recv):
            cp.start()

    def wait(sins, souts, send, recv):
        x, y, c, chips = _place()
        for cp in sends(souts, send, recv):
            cp.wait_send()
        for k, chip in enumerate(chips):
            _remote(_slot(souts[0], *chip, 1 - c), _slot(souts[0], *chip, 1 - c), send, recv, k, (x, y, c)).wait_recv()

    return _Side([gathered], [jax.ShapeDtypeStruct(gathered.shape, gathered.dtype)], 3, start, wait, aliases={0: 0})


def _side_swap(g):
    nchip, _, r, w = g.shape

    def copy(sins, souts, send, recv):
        x, y, c, _ = _place()
        return _remote(sins[0].at[:, pl.ds(1 - c, 1)], souts[0], send, recv, 0, (x, y, 1 - c))

    return _Side([g], [jax.ShapeDtypeStruct((nchip, 1, r, w), g.dtype)], 1,
                 lambda *a: copy(*a).start(), lambda *a: copy(*a).wait())


def _side_exchange(hsum):
    _, _, r, w = hsum.shape

    def copies(sins, souts, send, recv):
        x, y, c, chips = _place()
        return [_remote(sins[0].at[pl.ds(2 * px + py, 1)], souts[0].at[pl.ds(k, 1)], send, recv, k, (px, py, c))
                for k, (px, py) in enumerate(chips)]

    def start(*a):
        for cp in copies(*a):
            cp.start()

    def wait(*a):
        for cp in copies(*a):
            cp.wait()

    return _Side([hsum], [jax.ShapeDtypeStruct((3, 1, r, w), hsum.dtype)], 3, start, wait)


def _side_share(f):
    def copy(sins, souts, send, recv):
        x, y, c, _ = _place()
        return _remote(sins[0], souts[0], send, recv, 0, (x, y, 1 - c))

    return _Side([f], [jax.ShapeDtypeStruct(f.shape, f.dtype)], 1,
                 lambda *a: copy(*a).start(), lambda *a: copy(*a).wait())


def _own_slot_filled(gathered, shard):
    chip = 2 * lax.axis_index("x") + lax.axis_index("y")
    return lax.dynamic_update_slice(gathered, shard, (chip, 0, 0, 0))


def _both_parts(mine, theirs):
    mine_first = lax.axis_index("c") == 0
    return jnp.concatenate([jnp.where(mine_first, mine, theirs), jnp.where(mine_first, theirs, mine)], axis=0)


def _plane_tile(r, cdim):
    tr = _tile(r, (256, 128))
    if tr != r:
        return tr, cdim
    return r, _tile(cdim, (256, 128))


def _add_halves(g, recv, core, name):
    nchip, nl, r, cdim = g.shape
    half = nl // 2
    tr, tc = _plane_tile(r, cdim)

    def body(c_ref, g_ref, r_ref, o_ref):
        o_ref[...] = (g_ref[...].astype(F32) + r_ref[...].astype(F32)).astype(BF16)

    blk = (1, 1, tr, tc)
    return pl.pallas_call(
        body, name=name,
        grid_spec=pltpu.PrefetchScalarGridSpec(
            num_scalar_prefetch=1, grid=(nchip, half, r // tr, cdim // tc),
            in_specs=[pl.BlockSpec(blk, lambda j, l, i, k, c_ref: (j, c_ref[0] * half + l, i, k)),
                      pl.BlockSpec(blk, lambda j, l, i, k, c_ref: (j, l, i, k))],
            out_specs=pl.BlockSpec(blk, lambda j, l, i, k, c_ref: (j, l, i, k))),
        out_shape=jax.ShapeDtypeStruct((nchip, half, r, cdim), BF16),
        compiler_params=_params(("parallel",) * 4),
    )(core, g, recv)


def _add_partials(hsum, others, chip, name):
    nchip, half, r, cdim = hsum.shape
    tr, tc = _plane_tile(r, cdim)

    def body(i_ref, h_ref, a_ref, b_ref, c_ref, o_ref):
        o_ref[...] = ((h_ref[...].astype(F32) + a_ref[...].astype(F32))
                      + (b_ref[...].astype(F32) + c_ref[...].astype(F32)))[0]

    blk = (1, 1, tr, tc)
    other = lambda n: pl.BlockSpec(blk, lambda l, i, k, i_ref: (n, l, i, k))
    return pl.pallas_call(
        body, name=name,
        grid_spec=pltpu.PrefetchScalarGridSpec(
            num_scalar_prefetch=1, grid=(half, r // tr, cdim // tc),
            in_specs=[pl.BlockSpec(blk, lambda l, i, k, i_ref: (i_ref[0], l, i, k)), other(0), other(1), other(2)],
            out_specs=pl.BlockSpec((1, tr, tc), lambda l, i, k, i_ref: (l, i, k))),
        out_shape=jax.ShapeDtypeStruct((half, r, cdim), F32),
        compiler_params=_params(("parallel",) * 3),
    )(chip, hsum, others, others, others)


def _all_gather_small(v, name):
    m_per, n = v.shape

    def body(x_ref, out_ref, send_sems, recv_sems, local_sem):
        x, y, c, chips = _place()
        me, sib = (x, y, c), (x, y, 1 - c)

        def rows(px, py, pc):
            return out_ref.at[pl.ds((4 * px + 2 * py + pc) * m_per, m_per), :]

        def copy(k, block, to, src=None):
            return pltpu.make_async_remote_copy(
                src_ref=rows(*block) if src is None else src, dst_ref=rows(*block), send_sem=send_sems.at[k],
                recv_sem=recv_sems.at[k], device_id=to, device_id_type=MESH)

        mine = pltpu.make_async_copy(x_ref, rows(*me), local_sem)
        mine.start()
        first = [copy(0, me, sib, src=x_ref)]
        first += [copy(1 + j, me, (*chip, c), src=x_ref) for j, chip in enumerate(chips)]
        for cp in first:
            cp.start()
        passed = [copy(4 + j, (*chip, c), sib) for j, chip in enumerate(chips)]
        for j, chip in enumerate(chips):
            copy(1 + j, (*chip, c), me).wait_recv()
            passed[j].start()
        copy(0, sib, me).wait_recv()
        for j, chip in enumerate(chips):
            copy(4 + j, (*chip, 1 - c), me).wait_recv()
        for cp in first + passed:
            cp.wait_send()
        mine.wait()

    return pl.pallas_call(
        body, name=name, out_shape=jax.ShapeDtypeStruct((8 * m_per, n), v.dtype),
        in_specs=[pl.BlockSpec(memory_space=pltpu.VMEM)], out_specs=pl.BlockSpec(memory_space=pltpu.VMEM),
        scratch_shapes=[pltpu.SemaphoreType.DMA((7,)), pltpu.SemaphoreType.DMA((7,)), pltpu.SemaphoreType.DMA],
        compiler_params=pltpu.CompilerParams(vmem_limit_bytes=VMEM_LIMIT),
    )(v)


def _sum_blocks(v, nblk, name):
    m_per = v.shape[0] // nblk

    def body(v_ref, o_ref):
        acc = v_ref[0:m_per, :]
        for j in range(1, nblk):
            acc = acc + v_ref[j * m_per:(j + 1) * m_per, :]
        o_ref[...] = acc

    return pl.pallas_call(
        body, name=name, out_shape=jax.ShapeDtypeStruct((m_per, 128), F32),
        in_specs=[pl.BlockSpec(memory_space=pltpu.VMEM)], out_specs=pl.BlockSpec(memory_space=pltpu.VMEM),
        compiler_params=pltpu.CompilerParams(vmem_limit_bytes=VMEM_LIMIT),
    )(v)


def _rows128(a, rows=None):
    flat = a.reshape(-1).astype(F32)
    need = -(-flat.shape[0] // 128)
    rows = rows or -(-need // 8) * 8
    return jnp.pad(flat, (0, rows * 128 - flat.shape[0])).reshape(rows, 128)


def kernel(x, w_in, b_f, conv_w, pool_w, pool_scale, w_out, ln_g, ln_b, loss_target, m_w_in, m_b_f, m_conv_w, m_pool_w, m_pool_scale, m_w_out, m_ln_g, m_ln_b, v_w_in, v_b_f, v_conv_w, v_pool_w, v_pool_scale, v_w_out, v_ln_g, v_ln_b):
    depth, d, in_shard = w_in.shape
    s = x.shape[1]
    att = d // 2
    n_heads = att // HEAD_DIM
    cw = d // 4
    pw = d - att - cw
    in_w = 4 * att + n_heads + 4 * cw + 2 * pw
    assert in_shard * 4 == in_w and (d, n_heads, cw, pw) == (2048, 8, 512, 512) and s % 512 == 0
    main_w = in_w - n_heads
    fg0 = 4 * att
    alpha = (2 * depth) ** 0.25
    t_att = 512
    tm_mix = 256

    cx, cy, cc = lax.axis_index("x"), lax.axis_index("y"), lax.axis_index("c")
    chip = 2 * cx + cy
    core_arr = jnp.reshape(cc, (1,)).astype(jnp.int32)
    chip_arr = jnp.reshape(chip, (1,)).astype(jnp.int32)
    out_rows = w_out.shape[1]

    starts = [j * in_shard for j in range(4)]
    main_start = [g if g < fg0 else g - n_heads for g in starts]
    base = [m // 16 * 16 for m in main_start]
    lead = [m - b for m, b in zip(main_start, base)]
    fg_chip = max(j for j in range(4) if starts[j] <= fg0)
    hole = fg0 - starts[fg_chip]
    blk = -(-(max(lead) + in_shard) // 16) * 16
    assert hole + n_heads <= in_shard and all(b + blk <= main_w for b in base) and base[3] + blk == main_w
    r_comb = out_rows + blk
    pad_rows = -r_comb % 32
    r_half = (r_comb + pad_rows) // 2
    lead_dyn = jnp.take(jnp.array(lead, jnp.int32), chip)
    has_fg = chip == fg_chip

    to_rows = lambda a: jnp.transpose(a, (2, 0, 1))
    from_rows = lambda a: jnp.transpose(a, (1, 2, 0))
    wt_b = jnp.transpose(to_rows(w_in).astype(BF16), (1, 0, 2))
    padded = jnp.pad(wt_b, ((0, 0), (16, 32), (0, 0)))
    plain = lax.dynamic_slice_in_dim(padded, 16 - lead_dyn, blk, axis=1)
    skipping = lax.dynamic_slice_in_dim(padded, 16 - lead_dyn + n_heads, blk, axis=1)
    pos = lax.broadcasted_iota(jnp.int32, (1, blk, 1), 1)
    block = jnp.where(jnp.logical_and(has_fg, pos >= lead_dyn + hole), skipping, plain)
    comb = jnp.concatenate([w_out.astype(BF16), block, jnp.zeros((depth, pad_rows, d), BF16)], axis=1)
    shards = comb.reshape(depth, 2, r_half, d)

    def weights_of(gathered, wt_fg):
        full = gathered.reshape(4, 2 * r_half, d)
        wt_all = sum(jnp.pad(full[j, out_rows:r_comb], ((base[j], main_w + 128 - base[j] - blk), (0, 0)))
                     for j in range(4)) + jnp.pad(wt_fg, ((main_w, 0), (0, 0)))
        return wt_all, full[:, :out_rows].reshape(d, d)

    small_w = jnp.concatenate([_rows128(conv_w), _rows128(wt_b[:, hole:hole + n_heads])], axis=0)
    n_cw = _rows128(conv_w).shape[0] * 128
    sw_all = _all_gather_small(small_w, "gather_small_weights").reshape(4, 2, small_w.shape[0] * 128)[:, 0]
    conv_full = sw_all[:, :depth * 3 * (cw // 4)].reshape(4, depth, 3, cw // 4)
    conv_full = jnp.moveaxis(conv_full, 0, 2).reshape(depth, 3, cw)
    wt_fg_all = sw_all[fg_chip, n_cw:n_cw + depth * n_heads * d].reshape(depth, n_heads, d).astype(BF16)
    wt_fg_all = jnp.pad(wt_fg_all, ((0, 0), (0, 128 - n_heads), (0, 0)))

    first = shards[0][None]
    arrived = _run_side(_side_gather_ici(first), "gather_first_ici")[0]
    arrived = _run_side(_side_gather_d2d(arrived), "gather_first_d2d")[0]
    weights = weights_of(_own_slot_filled(arrived, first), wt_fg_all[0])
    pool_b = pool_w.astype(BF16)

    split = r_half * 2 // 5 // 16 * 16
    xf = x[0]
    xb = xf.astype(BF16)
    saved = []
    for l in range(depth):
        wt_all, w_out_l = weights
        wt_fg = wt_fg_all[l]
        nxt = shards[l + 1][None] if l + 1 < depth else None
        if nxt is None:
            p_main = _matmul(xb, wt_all, "nt", BF16, "proj_main", n_used=main_w)
        else:
            p_main, (arrived,) = _matmul(xb, wt_all, "nt", BF16, "proj_main", n_used=main_w,
                                         side=_side_gather_ici(nxt, rows=(0, split)))
        fg = _matmul(xb, wt_fg, "nt", F32, "proj_fg")
        fg_t = fg[:, :n_heads].T
        b_col = b_f[l].reshape(n_heads, 1)
        cum4 = _gate_fwd(fg_t, b_col).reshape(n_heads, s // t_att, 1, t_att)
        if nxt is None:
            o, lse = _attn_fwd(p_main, cum4, n_heads, t_att)
        else:
            (o, lse), (arrived,) = _attn_fwd(p_main, cum4, n_heads, t_att,
                                             side=_side_gather_ici(nxt, rows=(split, r_half), into=arrived))
        y = _mix_fwd(p_main, o, conv_full[l], pool_b[l], pool_scale[l].reshape(1, pw), tm_mix)
        ln_args = (y, w_out_l, xf, ln_g[l].reshape(1, d), ln_b[l].reshape(1, d), alpha, tm_mix)
        if nxt is None:
            x_new, xb_new, pre = _outproj_ln(*ln_args)
        else:
            (x_new, xb_new, pre), (arrived,) = _outproj_ln(*ln_args, side=_side_gather_d2d(arrived))
            weights = weights_of(_own_slot_filled(arrived, nxt), wt_fg_all[l + 1])
        saved.append((xb, p_main, fg_t, b_col, cum4, o, lse, y, pre, wt_all, w_out_l))
        xf, xb = x_new, xb_new

    dx, loss_blk = _loss_grad(xf, loss_target[0], tm_mix)

    g_bf, g_sm, g_pw, g_ln, g_fgw, finished = ([None] * depth for _ in range(6))
    pending = None
    for l in reversed(range(depth)):
        xb, p_main, fg_t, b_col, cum4, o, lse, y, pre, wt_all, w_out_l = saved[l]
        if pending is None:
            dpre, dpre_b, g_ln[l] = _ln_bwd(dx, pre, ln_g[l].reshape(1, d), tm_mix)
        else:
            (dpre, dpre_b, g_ln[l]), (recv,) = _ln_bwd(dx, pre, ln_g[l].reshape(1, d), tm_mix,
                                                       side=_side_swap(pending))
            hsum = _add_halves(pending, recv, core_arr, "rs_add_halves")
        dy = _matmul(dpre_b, w_out_l, "nt", F32, "d_y")
        gw_out_l = _matmul(y, dpre_b, "tn", BF16, "d_w_out")
        do, delta, dp_rest, g_sm[l], g_pw[l] = _mix_bwd(p_main, o, dy, conv_full[l], pool_b[l],
                                                        pool_scale[l].reshape(1, pw), tm_mix)
        if pending is None:
            dq, dk, dv, drow, dcol = _attn_bwd(p_main, do, lse, delta, cum4, n_heads, t_att)
        else:
            (dq, dk, dv, drow, dcol), (others,) = _attn_bwd(p_main, do, lse, delta, cum4, n_heads, t_att,
                                                            side=_side_exchange(hsum))
            mine = _add_partials(hsum, others, chip_arr, "rs_add_partials")
        dfg_t, g_bf[l] = _gate_bwd(drow, dcol, fg_t, b_col)
        dfg_b = jnp.pad(dfg_t.T, ((0, 0), (0, 128 - n_heads))).astype(BF16)
        dp_all = jnp.concatenate([dq, dk, dv, dp_rest, dfg_b], axis=1)
        if pending is None:
            gw_main_l = _matmul(dp_all, xb, "tn", BF16, "d_w_main", m_used=main_w)
        else:
            gw_main_l, (theirs,) = _matmul(dp_all, xb, "tn", BF16, "d_w_main", m_used=main_w, side=_side_share(mine))
            finished[l + 1] = _both_parts(mine, theirs)
        g_fgw[l] = _matmul(dfg_b, xb, "tn", F32, "d_w_fg")[:n_heads]
        by_chip = jnp.concatenate([gw_out_l.reshape(4, out_rows, d), jnp.stack([gw_main_l[b:b + blk] for b in base]),
                                   jnp.zeros((4, pad_rows, d), BF16)], axis=1)
        pending = by_chip.reshape(4, 2, r_half, d)
        if l > 0:
            dx = _matmul(dp_all, wt_all, "nn", F32, "d_x", add=dpre, add_scale=alpha)
        else:
            recv = _run_side(_side_swap(pending), "rs_swap_last")[0]
            hsum = _add_halves(pending, recv, core_arr, "rs_add_halves")
            dx, (others,) = _matmul(dp_all, wt_all, "nn", F32, "d_x", add=dpre, add_scale=alpha,
                                    side=_side_exchange(hsum))

    mine = _add_partials(hsum, others, chip_arr, "rs_add_partials")
    theirs = _run_side(_side_share(mine), "rs_share_last")[0]
    finished[0] = _both_parts(mine, theirs)
    grads = jnp.stack(finished).reshape(depth, 2 * r_half, d)
    grad_w_out = grads[:, :out_rows]

    g_sm = jnp.stack(g_sm)
    g_ln = jnp.stack(g_ln)
    parts = [jnp.stack(g_pw), g_sm[:, 0:3, :], g_sm[:, 3, :], g_ln[:, 0, :], g_ln[:, 1, :],
             jnp.stack(g_bf)[:, :, 0], loss_blk[0, 0], jnp.stack(g_fgw)]
    packed = jnp.concatenate([_rows128(p) for p in parts], axis=0)
    summed = _sum_blocks(_all_gather_small(packed, "gather_small_grads"), 8, "sum_small_grads")
    outs, off = [], 0
    for p in parts:
        nr = _rows128(p).shape[0]
        outs.append(summed[off:off + nr].reshape(-1)[:p.size].reshape(p.shape))
        off += nr
    grad_pool_w, gconv_full, grad_pool_scale, grad_ln_g, grad_ln_b, grad_b_f, loss, grad_fg = outs
    grad_conv_w = lax.dynamic_slice_in_dim(gconv_full, chip * (cw // 4), cw // 4, axis=2)

    gblk = jnp.pad(grads[:, out_rows:r_comb], ((0, 0), (16, 16), (0, 0)))
    plain = lax.dynamic_slice_in_dim(gblk, 16 + lead_dyn, in_shard, axis=1)
    after_fg = lax.dynamic_slice_in_dim(gblk, 16 + lead_dyn - n_heads, in_shard, axis=1)
    fg_placed = jnp.pad(grad_fg, ((0, 0), (hole, in_shard - hole - n_heads), (0, 0)))
    r_idx = lax.broadcasted_iota(jnp.int32, (1, in_shard, 1), 1)
    g_shard = jnp.where(jnp.logical_and(has_fg, r_idx >= hole + n_heads), after_fg,
                        jnp.where(jnp.logical_and(has_fg, r_idx >= hole), fg_placed, plain))
    grad_rows = jnp.transpose(g_shard, (1, 0, 2))

    d_w_in, nm_w_in, nv_w_in, grad_w_in = (from_rows(a) for a in _adamw(
        to_rows(w_in), grad_rows, to_rows(m_w_in), to_rows(v_w_in), "adamw_w_in", lead_block=46, emit_grad=True))
    d_w_out, nm_w_out, nv_w_out = _adamw(w_out, grad_w_out, m_w_out, v_w_out, "adamw_w_out")
    small = [(b_f, grad_b_f, m_b_f, v_b_f), (conv_w, grad_conv_w, m_conv_w, v_conv_w),
             (pool_w, grad_pool_w, m_pool_w, v_pool_w), (pool_scale, grad_pool_scale, m_pool_scale, v_pool_scale),
             (ln_g, grad_ln_g, m_ln_g, v_ln_g), (ln_b, grad_ln_b, m_ln_b, v_ln_b)]
    pk = [jnp.concatenate([_rows128(t[j]) for t in small], axis=0)[None] for j in range(4)]
    sm_out = _adamw(pk[0], pk[1], pk[2], pk[3], "adamw_small", lead_block=1)
    res = {j: [] for j in range(3)}
    off = 0
    for t in small:
        nr = _rows128(t[0]).shape[0]
        for j in range(3):
            res[j].append(sm_out[j][0, off:off + nr].reshape(-1)[:t[0].size].reshape(t[0].shape))
        off += nr
    (d_b_f, d_conv_w, d_pool_w, d_pool_scale, d_ln_g, d_ln_b) = res[0]
    (nm_b_f, nm_conv_w, nm_pool_w, nm_pool_scale, nm_ln_g, nm_ln_b) = res[1]
    (nv_b_f, nv_conv_w, nv_pool_w, nv_pool_scale, nv_ln_g, nv_ln_b) = res[2]

    return (loss, dx[None], grad_w_in, grad_b_f, grad_conv_w, grad_pool_w, grad_pool_scale, grad_w_out,
            grad_ln_g, grad_ln_b,
            d_w_in, d_b_f, d_conv_w, d_pool_w, d_pool_scale, d_w_out, d_ln_g, d_ln_b,
            nm_w_in, nm_b_f, nm_conv_w, nm_pool_w, nm_pool_scale, nm_w_out, nm_ln_g, nm_ln_b,
            nv_w_in, nv_b_f, nv_conv_w, nv_pool_w, nv_pool_scale, nv_w_out, nv_ln_g, nv_ln_b)
```

```python
import functools

import jax
import jax.numpy as jnp
from jax import lax
from jax.experimental import pallas as pl
from jax.experimental.pallas import tpu as pltpu

F32 = jnp.float32
BF16 = jnp.bfloat16
MESH = pl.DeviceIdType.MESH

HEAD_DIM = 128
POOL_WINDOWS = (2, 4, 8, 16)
HALO = 16
LN_EPS = 1e-5
ADAM_LR = 0.001
ADAM_B1 = 0.9
ADAM_B2 = 0.999
ADAM_EPS = 1e-08
ADAM_WD = 0.01
ADAM_STEP = 10
VMEM_LIMIT = 56 * 1024 * 1024


def _params(sem, vmem=VMEM_LIMIT):
    return pltpu.CompilerParams(dimension_semantics=sem, vmem_limit_bytes=vmem)


def _tile(n, prefs):
    for t in prefs:
        if n % t == 0:
            return t
    return n


def _sigmoid(x):
    return 1.0 / (1.0 + jnp.exp(-x))


def _hbm_spec():
    return pl.BlockSpec(memory_space=pltpu.HBM)


class _Side:
    def __init__(self, inputs, out_shapes, n_sems, start, wait, aliases=None):
        self.inputs, self.out_shapes, self.n_sems = list(inputs), list(out_shapes), n_sems
        self.start, self.wait, self.aliases = start, wait, dict(aliases or {})


def _pcall(body, *, name, grid, in_specs, out_specs, out_shape, scratch_shapes, semantics, args, side=None):
    if side is None:
        return pl.pallas_call(
            body, name=name, grid=grid, in_specs=in_specs, out_specs=out_specs, out_shape=out_shape,
            scratch_shapes=scratch_shapes, compiler_params=_params(semantics))(*args)
    single = not isinstance(out_shape, (tuple, list))
    out_specs_l = [out_specs] if single else list(out_specs)
    out_shape_l = [out_shape] if single else list(out_shape)
    n_i, n_si, n_o, n_so, n_s = len(in_specs), len(side.inputs), len(out_shape_l), len(side.out_shapes), len(scratch_shapes)

    def wrapped(*refs):
        ins, sins = refs[:n_i], refs[n_i:n_i + n_si]
        o0 = n_i + n_si
        outs, souts = refs[o0:o0 + n_o], refs[o0 + n_o:o0 + n_o + n_so]
        s0 = o0 + n_o + n_so
        scr, (send, recv) = refs[s0:s0 + n_s], refs[s0 + n_s:]
        first = pl.program_id(0) == 0
        last = pl.program_id(0) == grid[0] - 1
        for ax in range(1, len(grid)):
            first = jnp.logical_and(first, pl.program_id(ax) == 0)
            last = jnp.logical_and(last, pl.program_id(ax) == grid[ax] - 1)

        @pl.when(first)
        def _():
            side.start(sins, souts, send, recv)

        body(*ins, *outs, *scr)

        @pl.when(last)
        def _():
            side.wait(sins, souts, send, recv)

    res = pl.pallas_call(
        wrapped, name=name, grid=grid,
        in_specs=list(in_specs) + [_hbm_spec()] * n_si,
        out_specs=tuple(out_specs_l + [_hbm_spec()] * n_so),
        out_shape=tuple(out_shape_l + side.out_shapes),
        scratch_shapes=list(scratch_shapes) + [pltpu.SemaphoreType.DMA((side.n_sems,))] * 2,
        input_output_aliases={n_i + i: n_o + o for i, o in side.aliases.items()},
        compiler_params=_params(("arbitrary",) * len(grid)),
    )(*args, *side.inputs)
    main = res[:n_o]
    return (main[0] if single else tuple(main)), tuple(res[n_o:])


def _run_side(side, name):
    n_si, n_so = len(side.inputs), len(side.out_shapes)

    def body(*refs):
        sins, souts = refs[:n_si], refs[n_si:n_si + n_so]
        send, recv = refs[n_si + n_so:]
        side.start(sins, souts, send, recv)
        side.wait(sins, souts, send, recv)

    return pl.pallas_call(
        body, name=name, out_shape=tuple(side.out_shapes),
        in_specs=[_hbm_spec()] * n_si, out_specs=tuple([_hbm_spec()] * n_so),
        scratch_shapes=[pltpu.SemaphoreType.DMA((side.n_sems,))] * 2,
        input_output_aliases=side.aliases,
    )(*side.inputs)


_DIMS = {"nn": ((1,), (0,)), "nt": ((1,), (1,)), "tn": ((0,), (0,))}


def _matmul(a, b, form, out_dtype, name, add=None, add_scale=1.0, tm=None, tn=None, tk=None, side=None,
            m_used=None, n_used=None):
    if form == "nn":
        (m, k), (_, n) = a.shape, b.shape
    elif form == "nt":
        (m, k), (n, _) = a.shape, b.shape
    else:
        (k, m), (_, n) = a.shape, b.shape
    m, n = m_used or m, n_used or n
    tm = tm or _tile(m, (1024, 512, 256, 128))
    tn = tn or _tile(n, (1024, 512, 256, 128))
    tk = tk or _tile(k, (2432, 2048, 1792, 1024, 512, 256, 128))
    nk = k // tk
    if form == "tn":
        a_spec = pl.BlockSpec((tk, tm), lambda i, j, kk: (kk, i))
    else:
        a_spec = pl.BlockSpec((tm, tk), lambda i, j, kk: (i, kk))
    if form == "nt":
        b_spec = pl.BlockSpec((tn, tk), lambda i, j, kk: (j, kk))
    else:
        b_spec = pl.BlockSpec((tk, tn), lambda i, j, kk: (kk, j))
    o_spec = pl.BlockSpec((tm, tn), lambda i, j, kk: (i, j))
    dims = (_DIMS[form], ((), ()))
    has_add = add is not None

    def finish(r, add_ref, o_ref):
        if has_add:
            r = r + add_scale * add_ref[...]
        o_ref[...] = r.astype(out_dtype)

    def body_one_pass(a_ref, b_ref, *rest):
        add_ref, o_ref = rest if has_add else (None,) + rest
        finish(lax.dot_general(a_ref[...], b_ref[...], dims, preferred_element_type=F32), add_ref, o_ref)

    def body_accumulate(a_ref, b_ref, *rest):
        add_ref, o_ref, acc = rest if has_add else (None,) + rest
        kk = pl.program_id(2)
        part = lax.dot_general(a_ref[...], b_ref[...], dims, preferred_element_type=F32)

        @pl.when(kk == 0)
        def _():
            acc[...] = part

        @pl.when(jnp.logical_and(kk > 0, kk < nk - 1))
        def _():
            acc[...] += part

        @pl.when(kk == nk - 1)
        def _():
            finish(acc[...] + part, add_ref, o_ref)

    in_specs = [a_spec, b_spec] + ([o_spec] if has_add else [])
    args = (a, b) + ((add,) if has_add else ())
    return _pcall(
        body_one_pass if nk == 1 else body_accumulate, name=name, grid=(m // tm, n // tn, nk),
        in_specs=in_specs, out_specs=o_spec,
        out_shape=jax.ShapeDtypeStruct((m, n), out_dtype),
        scratch_shapes=[] if nk == 1 else [pltpu.VMEM((tm, tn), F32)],
        semantics=("parallel", "parallel", "arbitrary"), args=args, side=side)


def _tri(n, upper):
    r = lax.broadcasted_iota(jnp.int32, (n, n), 0)
    c = lax.broadcasted_iota(jnp.int32, (n, n), 1)
    return jnp.where((r <= c) if upper else (r >= c), 1.0, 0.0).astype(F32)


def _gate_fwd(fg_t, b_col):
    h, s = fg_t.shape
    nb = s // 128

    def body(fg_ref, b_ref, cum_ref):
        u = _tri(128, True)
        carry = jnp.zeros((h, 128), F32)
        for j in range(nb):
            z = fg_ref[:, j * 128:(j + 1) * 128] + b_ref[...]
            logf = -(jnp.maximum(-z, 0.0) + jnp.log(1.0 + jnp.exp(-jnp.abs(z))))
            c = jnp.dot(logf, u, precision=lax.Precision.HIGHEST, preferred_element_type=F32) + carry
            cum_ref[:, j * 128:(j + 1) * 128] = c
            carry = jnp.broadcast_to(c[:, 127:128], (h, 128))

    return pl.pallas_call(
        body, name="gate_fwd", out_shape=jax.ShapeDtypeStruct((h, s), F32),
        in_specs=[pl.BlockSpec(memory_space=pltpu.VMEM)] * 2,
        out_specs=pl.BlockSpec(memory_space=pltpu.VMEM),
    )(fg_t, b_col)


def _gate_bwd(drow_t, dcol_t, fg_t, b_col):
    h, s = fg_t.shape
    nb = s // 128

    def body(dr_ref, dc_ref, fg_ref, b_ref, dfg_ref, db_ref):
        low = _tri(128, False)
        carry = jnp.zeros((h, 128), F32)
        db = jnp.zeros((h, 128), F32)
        for j in reversed(range(nb)):
            sl = slice(j * 128, (j + 1) * 128)
            r = jnp.dot(dr_ref[:, 0, sl] - dc_ref[:, 0, sl], low, precision=lax.Precision.HIGHEST,
                        preferred_element_type=F32) + carry
            carry = jnp.broadcast_to(r[:, 0:1], (h, 128))
            z = fg_ref[:, sl] + b_ref[...]
            dfg = r / (1.0 + jnp.exp(z))
            dfg_ref[:, sl] = dfg
            db = db + dfg
        db_ref[...] = jnp.broadcast_to(jnp.sum(db, axis=1, keepdims=True), (h, 128))

    return pl.pallas_call(
        body, name="gate_bwd",
        out_shape=(jax.ShapeDtypeStruct((h, s), F32), jax.ShapeDtypeStruct((h, 128), F32)),
        in_specs=[pl.BlockSpec(memory_space=pltpu.VMEM)] * 4,
        out_specs=(pl.BlockSpec(memory_space=pltpu.VMEM),) * 2,
    )(drow_t, dcol_t, fg_t, b_col)


LOG2E = 1.4426950408889634


def _attn_fwd(p_main, cum4, n_heads, t, side=None):
    s = p_main.shape[0]
    nq = s // t
    scale2 = HEAD_DIM ** -0.5 * LOG2E
    rep = t // 128

    def body(q_ref, k_ref, v_ref, c_ref, o_ref, lse_ref, m_s, l_s, acc_s, s_a, s_b):
        qi = pl.program_id(1)
        m_s[...] = jnp.full_like(m_s, -jnp.inf)
        l_s[...] = jnp.zeros_like(l_s)
        acc_s[...] = jnp.zeros_like(acc_s)

        def keys(ki):
            return pl.ds(pl.multiple_of(ki * t, t), t)

        def logits(ki, buf):
            buf[...] = lax.dot_general(q_ref[...], k_ref[keys(ki), :], (_DIMS["nt"], ((), ())),
                                       preferred_element_type=F32)

        def consume(ki, buf, masked):
            sc = buf[...] * scale2 - c_ref[ki] * LOG2E
            if masked:
                row = lax.broadcasted_iota(jnp.int32, (t, t), 0)
                col = lax.broadcasted_iota(jnp.int32, (t, t), 1)
                sc = jnp.where(col <= row, sc, -jnp.inf)
            m_prev = m_s[...]
            m_new = jnp.maximum(m_prev, jnp.max(sc, axis=1, keepdims=True))
            alpha = jnp.exp2(m_prev - m_new)
            p = jnp.exp2(sc - jnp.tile(m_new, (1, rep)))
            l_s[...] = alpha * l_s[...] + jnp.sum(p, axis=1, keepdims=True)
            acc_s[...] = alpha * acc_s[...] + jnp.dot(p.astype(BF16), v_ref[keys(ki), :], preferred_element_type=F32)
            m_s[...] = m_new

        def pair(j, carry):
            logits(2 * j + 1, s_b)
            consume(2 * j, s_a, False)
            logits(2 * j + 2, s_a)
            consume(2 * j + 1, s_b, False)
            return carry

        logits(0, s_a)
        lax.fori_loop(0, qi // 2, pair, 0)

        @pl.when(qi % 2 == 0)
        def _():
            consume(qi, s_a, True)

        @pl.when(qi % 2 == 1)
        def _():
            logits(qi, s_b)
            consume(qi - 1, s_a, False)
            consume(qi, s_b, True)

        o_ref[...] = (acc_s[...] / l_s[...]).astype(BF16)
        lse_ref[...] = m_s[...] + jnp.log(l_s[...]) * LOG2E

    head = lambda off: pl.BlockSpec((s, 128), lambda h, qi: (0, off + h))
    return _pcall(
        body, name="attn_fwd", grid=(n_heads, nq),
        in_specs=[pl.BlockSpec((t, 128), lambda h, qi: (qi, h)), head(n_heads), head(2 * n_heads),
                  pl.BlockSpec((None, nq, 1, t), lambda h, qi: (h, 0, 0, 0))],
        out_specs=(pl.BlockSpec((t, 128), lambda h, qi: (qi, h)),) * 2,
        out_shape=(jax.ShapeDtypeStruct((s, n_heads * 128), BF16), jax.ShapeDtypeStruct((s, n_heads * 128), F32)),
        scratch_shapes=[pltpu.VMEM((t, 128), F32)] * 3 + [pltpu.VMEM((t, t), F32)] * 2,
        semantics=("parallel", "arbitrary"), args=(p_main, p_main, p_main, cum4), side=side)


def _attn_bwd(p_main, do, lse_rep, delta_rep, cum4, n_heads, t, side=None):
    s = p_main.shape[0]
    nq = s // t
    scale = HEAD_DIM ** -0.5
    rep = t // 128

    def body(q_ref, k_ref, v_ref, do_ref, lse_ref, dl_ref, c_ref,
             dq_ref, dk_ref, dv_ref, drow_ref, dcol_ref, dq_acc, dk_acc, dv_acc, s_a, dp_a, s_b, dp_b):
        kb = pl.program_id(1)

        @pl.when(kb == 0)
        def _():
            dq_acc[...] = jnp.zeros_like(dq_acc)

        dk_acc[...] = jnp.zeros_like(dk_acc)
        dv_acc[...] = jnp.zeros_like(dv_acc)
        k = k_ref[...]
        v = v_ref[...]
        c2 = c_ref[...] * LOG2E
        k_ones = jnp.concatenate([k, jnp.ones((t, 128), BF16)], axis=1)
        q_ones = jnp.ones((t, 128), BF16)

        def rows_of(qb):
            return pl.ds(pl.multiple_of(qb * t, t), t)

        def inputs(qb, s_buf, dp_buf):
            s_buf[...] = lax.dot_general(q_ref[rows_of(qb), :], k, (_DIMS["nt"], ((), ())), preferred_element_type=F32)
            dp_buf[...] = lax.dot_general(do_ref[rows_of(qb), :], v, (_DIMS["nt"], ((), ())), preferred_element_type=F32)

        def consume(qb, s_buf, dp_buf, masked):
            rows = rows_of(qb)
            sc = s_buf[...] * (scale * LOG2E) - c2
            if masked:
                row = lax.broadcasted_iota(jnp.int32, (t, t), 0)
                col = lax.broadcasted_iota(jnp.int32, (t, t), 1)
                sc = jnp.where(col <= row, sc, -jnp.inf)
            p = jnp.exp2(sc - jnp.tile(lse_ref[rows, :], (1, rep)))
            dsb = (p * (dp_buf[...] - jnp.tile(dl_ref[rows, :], (1, rep)))).astype(BF16)
            dv_acc[...] += lax.dot_general(p.astype(BF16), do_ref[rows, :], (_DIMS["tn"], ((), ())),
                                           preferred_element_type=F32)
            dk_acc[...] += lax.dot_general(dsb, jnp.concatenate([q_ref[rows, :], q_ones], axis=1),
                                           (_DIMS["tn"], ((), ())), preferred_element_type=F32)
            dq_acc[rows, :] += jnp.dot(dsb, k_ones, preferred_element_type=F32)

        n_rest = nq - 1 - kb
        inputs(kb, s_a, dp_a)

        @pl.when(n_rest > 0)
        def _():
            inputs(kb + 1, s_b, dp_b)

        consume(kb, s_a, dp_a, True)
        diag = rows_of(kb)
        dq_ref[diag, :] = (dq_acc[diag, 0:128] * scale).astype(BF16)
        drow_ref[:, diag] = dq_acc[diag, 128:256].T[0:8, :]

        def pair(j, carry):
            qb = kb + 1 + 2 * j
            inputs(qb + 1, s_a, dp_a)
            consume(qb, s_b, dp_b, False)
            inputs(qb + 2, s_b, dp_b)
            consume(qb + 1, s_a, dp_a, False)
            return carry

        n_pairs = jnp.maximum(n_rest - 1, 0) // 2
        lax.fori_loop(0, n_pairs, pair, 0)
        left = kb + 1 + 2 * n_pairs

        @pl.when(n_rest - 2 * n_pairs == 1)
        def _():
            consume(left, s_b, dp_b, False)

        @pl.when(n_rest - 2 * n_pairs == 2)
        def _():
            inputs(left + 1, s_a, dp_a)
            consume(left, s_b, dp_b, False)
            consume(left + 1, s_a, dp_a, False)

        dk_ref[...] = (dk_acc[:, 0:128] * scale).astype(BF16)
        dcol_ref[...] = dk_acc[:, 128:256].T[0:8, :]
        dv_ref[...] = dv_acc[...].astype(BF16)

    kside = lambda off: pl.BlockSpec((t, 128), lambda h, kb: (kb, off + h))
    whole = pl.BlockSpec((s, 128), lambda h, kb: (0, h))
    hw = n_heads * 128
    return _pcall(
        body, name="attn_bwd", grid=(n_heads, nq),
        in_specs=[whole, kside(n_heads), kside(2 * n_heads), whole, whole, whole,
                  pl.BlockSpec((None, None, 1, t), lambda h, kb: (h, kb, 0, 0))],
        out_specs=(whole, kside(0), kside(0), pl.BlockSpec((None, 8, s), lambda h, kb: (h, 0, 0)),
                   pl.BlockSpec((None, 8, t), lambda h, kb: (h, 0, kb))),
        out_shape=(jax.ShapeDtypeStruct((s, hw), BF16), jax.ShapeDtypeStruct((s, hw), BF16),
                   jax.ShapeDtypeStruct((s, hw), BF16), jax.ShapeDtypeStruct((n_heads, 8, s), F32),
                   jax.ShapeDtypeStruct((n_heads, 8, s), F32)),
        scratch_shapes=[pltpu.VMEM((s, 256), F32), pltpu.VMEM((t, 256), F32), pltpu.VMEM((t, 128), F32)]
        + [pltpu.VMEM((t, t), F32)] * 4,
        semantics=("parallel", "arbitrary"), args=(p_main, p_main, p_main, do, lse_rep, delta_rep, cum4), side=side)


def _conv_fwd(u_ext, cw_ref):
    r1 = pltpu.roll(u_ext, 1, 0)
    r2 = pltpu.roll(u_ext, 2, 0)
    conv = cw_ref[2:3, :] * u_ext + cw_ref[1:2, :] * r1 + cw_ref[0:1, :] * r2
    return conv[HALO:], r1[HALO:], r2[HALO:]


def _pool_z(pu, pu_halo, row0, tm, pg):
    ext = jnp.concatenate([pu_halo, pu], axis=0)
    t1 = (row0 + lax.broadcasted_iota(jnp.int32, (tm, pg), 0) + 1).astype(F32)
    zs = []
    for g, w in enumerate(POOL_WINDOWS):
        sm = ext[:, g * pg:(g + 1) * pg]
        sh = 1
        while sh < w:
            sm = sm + pltpu.roll(sm, sh, 0)
            sh *= 2
        mean = sm[HALO:] / jnp.minimum(t1, float(w))
        zs.append(mean - pu[:, g * pg:(g + 1) * pg])
    return zs


def _mix_specs(tm, s):
    per = tm // HALO
    last = s // HALO - 1
    cur = lambda w, j: pl.BlockSpec((tm, w), lambda i: (i, j))
    prev = lambda j: pl.BlockSpec((HALO, 512), lambda i: (jnp.maximum(i * per - 1, 0), j))
    nxt = lambda j: pl.BlockSpec((HALO, 512), lambda i: (jnp.minimum((i + 1) * per, last), j))
    full = lambda shape: pl.BlockSpec(shape, lambda i: (0,) * len(shape))
    return cur, prev, nxt, full


def _mix_fwd(p_main, o, conv_w, pool_w, pool_scale, tm):
    s = p_main.shape[0]
    d = 2048
    pg = 128
    cur, prev, nxt, full = _mix_specs(tm, s)

    def body(ga_ref, cb_ref, cc_ref, ch_ref, gc_ref, pu_ref, gp_ref, cch_ref, chh_ref, puh_ref,
             o_ref, cw_ref, pw_ref, ps_ref, y_ref):
        i = pl.program_id(0)
        first = i == 0
        ga = ga_ref[...].astype(F32)
        y_ref[:, 0:1024] = (o_ref[...].astype(F32) * ga * _sigmoid(ga)).astype(BF16)

        u = cc_ref[...].astype(F32) * ch_ref[...].astype(F32)
        uh = jnp.where(first, 0.0, cch_ref[...].astype(F32) * chh_ref[...].astype(F32))
        conv, _, _ = _conv_fwd(jnp.concatenate([uh, u], axis=0), cw_ref)
        gc = gc_ref[...].astype(F32)
        y_ref[:, 1024:1536] = (cb_ref[...].astype(F32) * conv * gc * _sigmoid(gc)).astype(BF16)

        pu = pu_ref[...].astype(F32)
        puh = jnp.where(first, 0.0, puh_ref[...].astype(F32))
        zs = _pool_z(pu, puh, i * tm, tm, pg)
        gp = gp_ref[...].astype(F32)
        gate = gp * _sigmoid(gp) * ps_ref[...]
        for g in range(4):
            r = jnp.dot(zs[g].astype(BF16), pw_ref[g], preferred_element_type=F32)
            y_ref[:, 1536 + g * pg:1536 + (g + 1) * pg] = (r * gate[:, g * pg:(g + 1) * pg]).astype(BF16)

    return pl.pallas_call(
        body, name="mix_fwd", grid=(s // tm,),
        in_specs=[cur(1024, 3), cur(512, 8), cur(512, 9), cur(512, 10), cur(512, 11), cur(512, 12), cur(512, 13),
                  prev(9), prev(10), prev(12),
                  cur(1024, 0), full((3, 512)), full((4, pg, pg)), full((1, 512))],
        out_specs=pl.BlockSpec((tm, d), lambda i: (i, 0)),
        out_shape=jax.ShapeDtypeStruct((s, d), BF16),
        compiler_params=_params(("parallel",)),
    )(*([p_main] * 10), o, conv_w, pool_w, pool_scale)


def _mix_bwd(p_main, o, dy, conv_w, pool_w, pool_scale, tm):
    s = p_main.shape[0]
    pg = 128
    n_heads = 8
    cur, prev, nxt, full = _mix_specs(tm, s)
    nblk = s // tm
    n_ext = tm + HALO

    def silu_and_grad(x):
        sg = _sigmoid(x)
        return x * sg, sg * (1.0 + x * (1.0 - sg))

    def body(ga_ref, cb_ref, cc_ref, ch_ref, gc_ref, pu_ref, gp_ref, cch_ref, chh_ref, puh_ref,
             cbn_ref, gcn_ref, gpn_ref, o_ref, dy_ref, dycn_ref, dypn_ref, cw_ref, pw_ref, ps_ref,
             do_ref, dl_ref, dp_ref, dsm_ref, dpw_ref):
        i = pl.program_id(0)
        first = i == 0
        last = i == nblk - 1

        @pl.when(first)
        def _():
            dsm_ref[...] = jnp.zeros_like(dsm_ref)
            dpw_ref[...] = jnp.zeros_like(dpw_ref)

        ga = ga_ref[...].astype(F32)
        of = o_ref[...].astype(F32)
        dya = dy_ref[:, 0:1024]
        sa, dsa = silu_and_grad(ga)
        dout = dya * sa
        do_ref[...] = dout.astype(BF16)
        dp_ref[:, 0:1024] = (dya * of * dsa).astype(BF16)
        prod = dout * of
        for h in range(n_heads):
            dsum = jnp.sum(prod[:, h * 128:(h + 1) * 128], axis=1, keepdims=True)
            dl_ref[:, h * 128:(h + 1) * 128] = jnp.broadcast_to(dsum, (tm, 128))

        cb = cb_ref[...].astype(F32)
        cc = cc_ref[...].astype(F32)
        ch = ch_ref[...].astype(F32)
        gc = gc_ref[...].astype(F32)
        u = cc * ch
        uh = jnp.where(first, 0.0, cch_ref[...].astype(F32) * chh_ref[...].astype(F32))
        conv, u1, u2 = _conv_fwd(jnp.concatenate([uh, u], axis=0), cw_ref)
        sc_, dsc = silu_and_grad(gc)
        dyc = dy_ref[:, 1024:1536]
        dp_ref[:, 1024:1536] = (dyc * conv * sc_).astype(BF16)
        dp_ref[:, 2560:3072] = (dyc * cb * conv * dsc).astype(BF16)
        dconv = dyc * cb * sc_
        gcn = gcn_ref[...].astype(F32)
        dconv_n = jnp.where(last, 0.0, dycn_ref[...] * cbn_ref[...].astype(F32) * gcn * _sigmoid(gcn))
        dext = jnp.concatenate([dconv, dconv_n], axis=0)
        du = (cw_ref[2:3, :] * dext + cw_ref[1:2, :] * pltpu.roll(dext, n_ext - 1, 0)
              + cw_ref[0:1, :] * pltpu.roll(dext, n_ext - 2, 0))[:tm]
        dp_ref[:, 1536:2048] = (du * ch).astype(BF16)
        dp_ref[:, 2048:2560] = (du * cc).astype(BF16)
        dsm_ref[0:1, :] += jnp.sum(dconv * u2, axis=0, keepdims=True)
        dsm_ref[1:2, :] += jnp.sum(dconv * u1, axis=0, keepdims=True)
        dsm_ref[2:3, :] += jnp.sum(dconv * u, axis=0, keepdims=True)

        pu = pu_ref[...].astype(F32)
        puh = jnp.where(first, 0.0, puh_ref[...].astype(F32))
        zs = _pool_z(pu, puh, i * tm, tm, pg)
        gp = gp_ref[...].astype(F32)
        sp, dsp = silu_and_grad(gp)
        dyp = dy_ref[:, 1536:2048]
        gpn = gpn_ref[...].astype(F32)
        dr_n = jnp.where(last, 0.0, dypn_ref[...] * gpn * _sigmoid(gpn) * ps_ref[...])
        drs = dyp * sp
        dr = drs * ps_ref[...]
        t1 = (i * tm + lax.broadcasted_iota(jnp.int32, (tm, pg), 0) + 1).astype(F32)
        r_parts, dpu_parts = [], []
        for g, w in enumerate(POOL_WINDOWS):
            cols = slice(g * pg, (g + 1) * pg)
            zb = zs[g].astype(BF16)
            r_parts.append(jnp.dot(zb, pw_ref[g], preferred_element_type=F32))
            drb = dr[:, cols].astype(BF16)
            dpw_ref[g] += lax.dot_general(zb, drb, (_DIMS["tn"], ((), ())), preferred_element_type=F32)
            dz = lax.dot_general(drb, pw_ref[g], (_DIMS["nt"], ((), ())), preferred_element_type=F32)
            dz_n = lax.dot_general(dr_n[:, cols].astype(BF16), pw_ref[g], (_DIMS["nt"], ((), ())),
                                   preferred_element_type=F32)
            sm = jnp.concatenate([dz / jnp.minimum(t1, float(w)), dz_n / float(w)], axis=0)
            sh = 1
            while sh < w:
                sm = sm + pltpu.roll(sm, n_ext - sh, 0)
                sh *= 2
            dpu_parts.append(sm[:tm] - dz)
        r = jnp.concatenate(r_parts, axis=1)
        dp_ref[:, 3072:3584] = jnp.concatenate(dpu_parts, axis=1).astype(BF16)
        dp_ref[:, 3584:4096] = (dyp * r * ps_ref[...] * dsp).astype(BF16)
        dsm_ref[3:4, :] += jnp.sum(drs * r, axis=0, keepdims=True)

    dy_next = lambda j: pl.BlockSpec((HALO, 512), lambda i: (jnp.minimum((i + 1) * (tm // HALO), s // HALO - 1), j))
    return pl.pallas_call(
        body, name="mix_bwd", grid=(nblk,),
        in_specs=[cur(1024, 3), cur(512, 8), cur(512, 9), cur(512, 10), cur(512, 11), cur(512, 12), cur(512, 13),
                  prev(9), prev(10), prev(12), nxt(8), nxt(11), nxt(13),
                  cur(1024, 0), pl.BlockSpec((tm, 2048), lambda i: (i, 0)), dy_next(2), dy_next(3),
                  full((3, 512)), full((4, pg, pg)), full((1, 512))],
        out_specs=(pl.BlockSpec((tm, 1024), lambda i: (i, 0)), pl.BlockSpec((tm, 1024), lambda i: (i, 0)),
                   pl.BlockSpec((tm, 4096), lambda i: (i, 0)), full((8, 512)), full((4, pg, pg))),
        out_shape=(jax.ShapeDtypeStruct((s, 1024), BF16), jax.ShapeDtypeStruct((s, 1024), F32),
                   jax.ShapeDtypeStruct((s, 4096), BF16), jax.ShapeDtypeStruct((8, 512), F32),
                   jax.ShapeDtypeStruct((4, pg, pg), F32)),
        compiler_params=_params(("arbitrary",)),
    )(*([p_main] * 13), o, dy, dy, dy, conv_w, pool_w, pool_scale)


def _outproj_ln(y, w_out, x, ln_g, ln_b, alpha, tm, side=None):
    s, d = x.shape

    def body(y_ref, w_ref, x_ref, g_ref, b_ref, xn_ref, xb_ref, pre_ref):
        pre = alpha * x_ref[...] + jnp.dot(y_ref[...], w_ref[...], preferred_element_type=F32)
        mu = jnp.mean(pre, axis=1, keepdims=True)
        cen = pre - mu
        var = jnp.mean(cen * cen, axis=1, keepdims=True)
        xn = cen * lax.rsqrt(var + LN_EPS) * g_ref[...] + b_ref[...]
        pre_ref[...] = pre
        xn_ref[...] = xn
        xb_ref[...] = xn.astype(BF16)

    row = pl.BlockSpec((tm, d), lambda i: (i, 0))
    vec = pl.BlockSpec((1, d), lambda i: (0, 0))
    return _pcall(
        body, name="outproj_ln", grid=(s // tm,),
        in_specs=[row, pl.BlockSpec((d, d), lambda i: (0, 0)), row, vec, vec],
        out_specs=(row, row, row),
        out_shape=(jax.ShapeDtypeStruct((s, d), F32), jax.ShapeDtypeStruct((s, d), BF16),
                   jax.ShapeDtypeStruct((s, d), F32)),
        scratch_shapes=[], semantics=("parallel",), args=(y, w_out, x, ln_g, ln_b), side=side)


def _ln_bwd(dxn, pre, ln_g, tm, side=None):
    s, d = pre.shape

    def body(dx_ref, pre_ref, g_ref, dpre_ref, dpb_ref, dgb_ref):
        @pl.when(pl.program_id(0) == 0)
        def _():
            dgb_ref[...] = jnp.zeros_like(dgb_ref)

        pre_ = pre_ref[...]
        dx = dx_ref[...]
        mu = jnp.mean(pre_, axis=1, keepdims=True)
        cen = pre_ - mu
        var = jnp.mean(cen * cen, axis=1, keepdims=True)
        rstd = lax.rsqrt(var + LN_EPS)
        xhat = cen * rstd
        dxh = dx * g_ref[...]
        dpre = rstd * (dxh - jnp.mean(dxh, axis=1, keepdims=True)
                       - xhat * jnp.mean(dxh * xhat, axis=1, keepdims=True))
        dpre_ref[...] = dpre
        dpb_ref[...] = dpre.astype(BF16)
        dgb_ref[0:1, :] += jnp.sum(dx * xhat, axis=0, keepdims=True)
        dgb_ref[1:2, :] += jnp.sum(dx, axis=0, keepdims=True)

    row = pl.BlockSpec((tm, d), lambda i: (i, 0))
    return _pcall(
        body, name="ln_bwd", grid=(s // tm,),
        in_specs=[row, row, pl.BlockSpec((1, d), lambda i: (0, 0))],
        out_specs=(row, row, pl.BlockSpec((8, d), lambda i: (0, 0))),
        out_shape=(jax.ShapeDtypeStruct((s, d), F32), jax.ShapeDtypeStruct((s, d), BF16),
                   jax.ShapeDtypeStruct((8, d), F32)),
        scratch_shapes=[], semantics=("arbitrary",), args=(dxn, pre, ln_g), side=side)


def _loss_grad(y, target, tm):
    s, d = y.shape

    def body(y_ref, t_ref, dy_ref, loss_ref):
        @pl.when(pl.program_id(0) == 0)
        def _():
            loss_ref[...] = jnp.zeros_like(loss_ref)

        diff = y_ref[...] - t_ref[...]
        dy_ref[...] = diff / d
        loss_ref[...] += 0.5 * jnp.sum(jnp.sum(diff * diff, axis=1, keepdims=True) / d)

    row = pl.BlockSpec((tm, d), lambda i: (i, 0))
    return pl.pallas_call(
        body, name="loss_grad", grid=(s // tm,),
        in_specs=[row, row],
        out_specs=(row, pl.BlockSpec((8, 128), lambda i: (0, 0))),
        out_shape=(jax.ShapeDtypeStruct((s, d), F32), jax.ShapeDtypeStruct((8, 128), F32)),
        compiler_params=_params(("arbitrary",)),
    )(y, target)


def _adamw(w, g, m, v, name, lead_block=None, emit_grad=False):
    a, r, c = w.shape
    tr = r if lead_block else _tile(r, (256, 128, 64, 32, 16, 8))
    c1 = 1.0 - ADAM_B1 ** ADAM_STEP
    c2 = 1.0 - ADAM_B2 ** ADAM_STEP

    def body(w_ref, g_ref, m_ref, v_ref, d_ref, mo_ref, vo_ref, *go_ref):
        g_ = g_ref[...]
        if emit_grad:
            go_ref[0][...] = g_
        m_new = ADAM_B1 * m_ref[...] + (1.0 - ADAM_B1) * g_
        v_new = ADAM_B2 * v_ref[...] + (1.0 - ADAM_B2) * (g_ * g_)
        d_ref[...] = -ADAM_LR * ((m_new / c1) / (jnp.sqrt(v_new / c2) + ADAM_EPS) + ADAM_WD * w_ref[...])
        mo_ref[...] = m_new
        vo_ref[...] = v_new

    ta = lead_block or 1
    spec = pl.BlockSpec((ta, tr, c), lambda i, j: (i, j, 0))
    shp = jax.ShapeDtypeStruct(w.shape, F32)
    n_out = 4 if emit_grad else 3
    return pl.pallas_call(
        body, name=name, grid=(a // ta, r // tr), in_specs=[spec] * 4, out_specs=(spec,) * n_out,
        out_shape=(shp,) * n_out, compiler_params=_params(("parallel", "parallel")),
    )(w, g, m, v)


def _place():
    x, y, c = lax.axis_index("x"), lax.axis_index("y"), lax.axis_index("c")
    return x, y, c, [(1 - x, y), (x, 1 - y), (1 - x, 1 - y)]


def _remote(src, dst, send, recv, k, to):
    return pltpu.make_async_remote_copy(src_ref=src, dst_ref=dst, send_sem=send.at[k], recv_sem=recv.at[k],
                                        device_id=to, device_id_type=MESH)


def _slot(ref, px, py, pc):
    return ref.at[pl.ds(2 * px + py, 1), pl.ds(pc, 1)]


def _side_gather_ici(shard, rows=None, into=None):
    _, two, r, w = shard.shape
    lo, hi = rows or (0, r)

    def _slot(ref, px, py, pc):
        return ref.at[pl.ds(2 * px + py, 1), pl.ds(pc, 1), pl.ds(lo, hi - lo)]

    def sends(sins, souts, send, recv):
        x, y, c, chips = _place()
        return [_remote(sins[0].at[:, pl.ds(c, 1), pl.ds(lo, hi - lo)], _slot(souts[0], x, y, c), send, recv, k,
                        (*chip, c)) for k, chip in enumerate(chips)]

    def start(sins, souts, send, recv):
        for cp in sends(sins, souts, send, recv):
            cp.start()

    def wait(sins, souts, send, recv):
        x, y, c, chips = _place()
        for cp in sends(sins, souts, send, recv):
            cp.wait_send()
        for k, chip in enumerate(chips):
            _remote(_slot(souts[0], *chip, c), _slot(souts[0], *chip, c), send, recv, k, (x, y, c)).wait_recv()

    more = [] if into is None else [into]
    return _Side([shard] + more, [jax.ShapeDtypeStruct((4, two, r, w), shard.dtype)], 3, start, wait,
                 aliases={} if into is None else {1: 0})


def _side_gather_d2d(gathered):
    def sends(souts, send, recv):
        x, y, c, chips = _place()
        return [_remote(_slot(souts[0], *chip, c), _slot(souts[0], *chip, c), send, recv, k, (x, y, 1 - c))
                for k, chip in enumerate(chips)]

    def start(sins, souts, send, recv):
        for cp in sends(souts, send, recv):
            cp.start()

    def wait(sins, souts, send, recv):
        x, y, c, chips = _place()
        for cp in sends(souts, send, recv):
            cp.wait_send()
        for k, chip in enumerate(chips):
            _remote(_slot(souts[0], *chip, 1 - c), _slot(souts[0], *chip, 1 - c), send, recv, k, (x, y, c)).wait_recv()

    return _Side([gathered], [jax.ShapeDtypeStruct(gathered.shape, gathered.dtype)], 3, start, wait, aliases={0: 0})


def _side_swap(g):
    nchip, _, r, w = g.shape

    def copy(sins, souts, send, recv):
        x, y, c, _ = _place()
        return _remote(sins[0].at[:, pl.ds(1 - c, 1)], souts[0], send, recv, 0, (x, y, 1 - c))

    return _Side([g], [jax.ShapeDtypeStruct((nchip, 1, r, w), g.dtype)], 1,
                 lambda *a: copy(*a).start(), lambda *a: copy(*a).wait())


def _side_exchange(hsum):
    _, _, r, w = hsum.shape

    def copies(sins, souts, send, recv):
        x, y, c, chips = _place()
        return [_remote(sins[0].at[pl.ds(2 * px + py, 1)], souts[0].at[pl.ds(k, 1)], send, recv, k, (px, py, c))
                for k, (px, py) in enumerate(chips)]

    def start(*a):
        for cp in copies(*a):
            cp.start()

    def wait(*a):
        for cp in copies(*a):
            cp.wait()

    return _Side([hsum], [jax.ShapeDtypeStruct((3, 1, r, w), hsum.dtype)], 3, start, wait)


def _side_share(f):
    def copy(sins, souts, send, recv):
        x, y, c, _ = _place()
        return _remote(sins[0], souts[0], send, recv, 0, (x, y, 1 - c))

    return _Side([f], [jax.ShapeDtypeStruct(f.shape, f.dtype)], 1,
                 lambda *a: copy(*a).start(), lambda *a: copy(*a).wait())


def _own_slot_filled(gathered, shard):
    chip = 2 * lax.axis_index("x") + lax.axis_index("y")
    return lax.dynamic_update_slice(gathered, shard, (chip, 0, 0, 0))


def _both_parts(mine, theirs):
    mine_first = lax.axis_index("c") == 0
    return jnp.concatenate([jnp.where(mine_first, mine, theirs), jnp.where(mine_first, theirs, mine)], axis=0)


def _plane_tile(r, cdim):
    tr = _tile(r, (256, 128))
    if tr != r:
        return tr, cdim
    return r, _tile(cdim, (256, 128))


def _add_halves(g, recv, core, name):
    nchip, nl, r, cdim = g.shape
    half = nl // 2
    tr, tc = _plane_tile(r, cdim)

    def body(c_ref, g_ref, r_ref, o_ref):
        o_ref[...] = (g_ref[...].astype(F32) + r_ref[...].astype(F32)).astype(BF16)

    blk = (1, 1, tr, tc)
    return pl.pallas_call(
        body, name=name,
        grid_spec=pltpu.PrefetchScalarGridSpec(
            num_scalar_prefetch=1, grid=(nchip, half, r // tr, cdim // tc),
            in_specs=[pl.BlockSpec(blk, lambda j, l, i, k, c_ref: (j, c_ref[0] * half + l, i, k)),
                      pl.BlockSpec(blk, lambda j, l, i, k, c_ref: (j, l, i, k))],
            out_specs=pl.BlockSpec(blk, lambda j, l, i, k, c_ref: (j, l, i, k))),
        out_shape=jax.ShapeDtypeStruct((nchip, half, r, cdim), BF16),
        compiler_params=_params(("parallel",) * 4),
    )(core, g, recv)


def _add_partials(hsum, others, chip, name):
    nchip, half, r, cdim = hsum.shape
    tr, tc = _plane_tile(r, cdim)

    def body(i_ref, h_ref, a_ref, b_ref, c_ref, o_ref):
        o_ref[...] = ((h_ref[...].astype(F32) + a_ref[...].astype(F32))
                      + (b_ref[...].astype(F32) + c_ref[...].astype(F32)))[0]

    blk = (1, 1, tr, tc)
    other = lambda n: pl.BlockSpec(blk, lambda l, i, k, i_ref: (n, l, i, k))
    return pl.pallas_call(
        body, name=name,
        grid_spec=pltpu.PrefetchScalarGridSpec(
            num_scalar_prefetch=1, grid=(half, r // tr, cdim // tc),
            in_specs=[pl.BlockSpec(blk, lambda l, i, k, i_ref: (i_ref[0], l, i, k)), other(0), other(1), other(2)],
            out_specs=pl.BlockSpec((1, tr, tc), lambda l, i, k, i_ref: (l, i, k))),
        out_shape=jax.ShapeDtypeStruct((half, r, cdim), F32),
        compiler_params=_params(("parallel",) * 3),
    )(chip, hsum, others, others, others)


def _all_gather_small(v, name):
    m_per, n = v.shape

    def body(x_ref, out_ref, send_sems, recv_sems, local_sem):
        x, y, c, chips = _place()
        me, sib = (x, y, c), (x, y, 1 - c)

        def rows(px, py, pc):
            return out_ref.at[pl.ds((4 * px + 2 * py + pc) * m_per, m_per), :]

        def copy(k, block, to, src=None):
            return pltpu.make_async_remote_copy(
                src_ref=rows(*block) if src is None else src, dst_ref=rows(*block), send_sem=send_sems.at[k],
                recv_sem=recv_sems.at[k], device_id=to, device_id_type=MESH)

        mine = pltpu.make_async_copy(x_ref, rows(*me), local_sem)
        mine.start()
        first = [copy(0, me, sib, src=x_ref)]
        first += [copy(1 + j, me, (*chip, c), src=x_ref) for j, chip in enumerate(chips)]
        for cp in first:
            cp.start()
        passed = [copy(4 + j, (*chip, c), sib) for j, chip in enumerate(chips)]
        for j, chip in enumerate(chips):
            copy(1 + j, (*chip, c), me).wait_recv()
            passed[j].start()
        copy(0, sib, me).wait_recv()
        for j, chip in enumerate(chips):
            copy(4 + j, (*chip, 1 - c), me).wait_recv()
        for cp in first + passed:
            cp.wait_send()
        mine.wait()

    return pl.pallas_call(
        body, name=name, out_shape=jax.ShapeDtypeStruct((8 * m_per, n), v.dtype),
        in_specs=[pl.BlockSpec(memory_space=pltpu.VMEM)], out_specs=pl.BlockSpec(memory_space=pltpu.VMEM),
        scratch_shapes=[pltpu.SemaphoreType.DMA((7,)), pltpu.SemaphoreType.DMA((7,)), pltpu.SemaphoreType.DMA],
        compiler_params=pltpu.CompilerParams(vmem_limit_bytes=VMEM_LIMIT),
    )(v)


def _sum_blocks(v, nblk, name):
    m_per = v.shape[0] // nblk

    def body(v_ref, o_ref):
        acc = v_ref[0:m_per, :]
        for j in range(1, nblk):
            acc = acc + v_ref[j * m_per:(j + 1) * m_per, :]
        o_ref[...] = acc

    return pl.pallas_call(
        body, name=name, out_shape=jax.ShapeDtypeStruct((m_per, 128), F32),
        in_specs=[pl.BlockSpec(memory_space=pltpu.VMEM)], out_specs=pl.BlockSpec(memory_space=pltpu.VMEM),
        compiler_params=pltpu.CompilerParams(vmem_limit_bytes=VMEM_LIMIT),
    )(v)


def _rows128(a, rows=None):
    flat = a.reshape(-1).astype(F32)
    need = -(-flat.shape[0] // 128)
    rows = rows or -(-need // 8) * 8
    return jnp.pad(flat, (0, rows * 128 - flat.shape[0])).reshape(rows, 128)


def kernel(x, w_in, b_f, conv_w, pool_w, pool_scale, w_out, ln_g, ln_b, loss_target, m_w_in, m_b_f, m_conv_w, m_pool_w, m_pool_scale, m_w_out, m_ln_g, m_ln_b, v_w_in, v_b_f, v_conv_w, v_pool_w, v_pool_scale, v_w_out, v_ln_g, v_ln_b):
    depth, d, in_shard = w_in.shape
    s = x.shape[1]
    att = d // 2
    n_heads = att // HEAD_DIM
    cw = d // 4
    pw = d - att - cw
    in_w = 4 * att + n_heads + 4 * cw + 2 * pw
    assert in_shard * 4 == in_w and (d, n_heads, cw, pw) == (2048, 8, 512, 512) and s % 512 == 0
    main_w = in_w - n_heads
    fg0 = 4 * att
    alpha = (2 * depth) ** 0.25
    t_att = 512
    tm_mix = 256

    cx, cy, cc = lax.axis_index("x"), lax.axis_index("y"), lax.axis_index("c")
    chip = 2 * cx + cy
    core_arr = jnp.reshape(cc, (1,)).astype(jnp.int32)
    chip_arr = jnp.reshape(chip, (1,)).astype(jnp.int32)
    out_rows = w_out.shape[1]

    starts = [j * in_shard for j in range(4)]
    main_start = [g if g < fg0 else g - n_heads for g in starts]
    base = [m // 16 * 16 for m in main_start]
    lead = [m - b for m, b in zip(main_start, base)]
    fg_chip = max(j for j in range(4) if starts[j] <= fg0)
    hole = fg0 - starts[fg_chip]
    blk = -(-(max(lead) + in_shard) // 16) * 16
    assert hole + n_heads <= in_shard and all(b + blk <= main_w for b in base) and base[3] + blk == main_w
    r_comb = out_rows + blk
    pad_rows = -r_comb % 32
    r_half = (r_comb + pad_rows) // 2
    lead_dyn = jnp.take(jnp.array(lead, jnp.int32), chip)
    has_fg = chip == fg_chip

    to_rows = lambda a: jnp.transpose(a, (2, 0, 1))
    from_rows = lambda a: jnp.transpose(a, (1, 2, 0))
    wt_b = jnp.transpose(to_rows(w_in).astype(BF16), (1, 0, 2))
    padded = jnp.pad(wt_b, ((0, 0), (16, 32), (0, 0)))
    plain = lax.dynamic_slice_in_dim(padded, 16 - lead_dyn, blk, axis=1)
    skipping = lax.dynamic_slice_in_dim(padded, 16 - lead_dyn + n_heads, blk, axis=1)
    pos = lax.broadcasted_iota(jnp.int32, (1, blk, 1), 1)
    block = jnp.where(jnp.logical_and(has_fg, pos >= lead_dyn + hole), skipping, plain)
    comb = jnp.concatenate([w_out.astype(BF16), block, jnp.zeros((depth, pad_rows, d), BF16)], axis=1)
    shards = comb.reshape(depth, 2, r_half, d)

    def weights_of(gathered, wt_fg):
        full = gathered.reshape(4, 2 * r_half, d)
        wt_all = sum(jnp.pad(full[j, out_rows:r_comb], ((base[j], main_w + 128 - base[j] - blk), (0, 0)))
                     for j in range(4)) + jnp.pad(wt_fg, ((main_w, 0), (0, 0)))
        return wt_all, full[:, :out_rows].reshape(d, d)

    small_w = jnp.concatenate([_rows128(conv_w), _rows128(wt_b[:, hole:hole + n_heads])], axis=0)
    n_cw = _rows128(conv_w).shape[0] * 128
    sw_all = _all_gather_small(small_w, "gather_small_weights").reshape(4, 2, small_w.shape[0] * 128)[:, 0]
    conv_full = sw_all[:, :depth * 3 * (cw // 4)].reshape(4, depth, 3, cw // 4)
    conv_full = jnp.moveaxis(conv_full, 0, 2).reshape(depth, 3, cw)
    wt_fg_all = sw_all[fg_chip, n_cw:n_cw + depth * n_heads * d].reshape(depth, n_heads, d).astype(BF16)
    wt_fg_all = jnp.pad(wt_fg_all, ((0, 0), (0, 128 - n_heads), (0, 0)))

    first = shards[0][None]
    arrived = _run_side(_side_gather_ici(first), "gather_first_ici")[0]
    arrived = _run_side(_side_gather_d2d(arrived), "gather_first_d2d")[0]
    weights = weights_of(_own_slot_filled(arrived, first), wt_fg_all[0])
    pool_b = pool_w.astype(BF16)

    split = r_half * 2 // 5 // 16 * 16
    xf = x[0]
    xb = xf.astype(BF16)
    saved = []
    for l in range(depth):
        wt_all, w_out_l = weights
        wt_fg = wt_fg_all[l]
        nxt = shards[l + 1][None] if l + 1 < depth else None
        if nxt is None:
            p_main = _matmul(xb, wt_all, "nt", BF16, "proj_main", n_used=main_w)
        else:
            p_main, (arrived,) = _matmul(xb, wt_all, "nt", BF16, "proj_main", n_used=main_w,
                                         side=_side_gather_ici(nxt, rows=(0, split)))
        fg = _matmul(xb, wt_fg, "nt", F32, "proj_fg")
        fg_t = fg[:, :n_heads].T
        b_col = b_f[l].reshape(n_heads, 1)
        cum4 = _gate_fwd(fg_t, b_col).reshape(n_heads, s // t_att, 1, t_att)
        if nxt is None:
            o, lse = _attn_fwd(p_main, cum4, n_heads, t_att)
        else:
            (o, lse), (arrived,) = _attn_fwd(p_main, cum4, n_heads, t_att,
                                             side=_side_gather_ici(nxt, rows=(split, r_half), into=arrived))
        y = _mix_fwd(p_main, o, conv_full[l], pool_b[l], pool_scale[l].reshape(1, pw), tm_mix)
        ln_args = (y, w_out_l, xf, ln_g[l].reshape(1, d), ln_b[l].reshape(1, d), alpha, tm_mix)
        if nxt is None:
            x_new, xb_new, pre = _outproj_ln(*ln_args)
        else:
            (x_new, xb_new, pre), (arrived,) = _outproj_ln(*ln_args, side=_side_gather_d2d(arrived))
            weights = weights_of(_own_slot_filled(arrived, nxt), wt_fg_all[l + 1])
        saved.append((xb, p_main, fg_t, b_col, cum4, o, lse, y, pre, wt_all, w_out_l))
        xf, xb = x_new, xb_new

    dx, loss_blk = _loss_grad(xf, loss_target[0], tm_mix)

    g_bf, g_sm, g_pw, g_ln, g_fgw, finished = ([None] * depth for _ in range(6))
    pending = None
    for l in reversed(range(depth)):
        xb, p_main, fg_t, b_col, cum4, o, lse, y, pre, wt_all, w_out_l = saved[l]
        if pending is None:
            dpre, dpre_b, g_ln[l] = _ln_bwd(dx, pre, ln_g[l].reshape(1, d), tm_mix)
        else:
            (dpre, dpre_b, g_ln[l]), (recv,) = _ln_bwd(dx, pre, ln_g[l].reshape(1, d), tm_mix,
                                                       side=_side_swap(pending))
            hsum = _add_halves(pending, recv, core_arr, "rs_add_halves")
        dy = _matmul(dpre_b, w_out_l, "nt", F32, "d_y")
        gw_out_l = _matmul(y, dpre_b, "tn", BF16, "d_w_out")
        do, delta, dp_rest, g_sm[l], g_pw[l] = _mix_bwd(p_main, o, dy, conv_full[l], pool_b[l],
                                                        pool_scale[l].reshape(1, pw), tm_mix)
        if pending is None:
            dq, dk, dv, drow, dcol = _attn_bwd(p_main, do, lse, delta, cum4, n_heads, t_att)
        else:
            (dq, dk, dv, drow, dcol), (others,) = _attn_bwd(p_main, do, lse, delta, cum4, n_heads, t_att,
                                                            side=_side_exchange(hsum))
            mine = _add_partials(hsum, others, chip_arr, "rs_add_partials")
        dfg_t, g_bf[l] = _gate_bwd(drow, dcol, fg_t, b_col)
        dfg_b = jnp.pad(dfg_t.T, ((0, 0), (0, 128 - n_heads))).astype(BF16)
        dp_all = jnp.concatenate([dq, dk, dv, dp_rest, dfg_b], axis=1)
        if pending is None:
            gw_main_l = _matmul(dp_all, xb, "tn", BF16, "d_w_main", m_used=main_w)
        else:
            gw_main_l, (theirs,) = _matmul(dp_all, xb, "tn", BF16, "d_w_main", m_used=main_w, side=_side_share(mine))
            finished[l + 1] = _both_parts(mine, theirs)
        g_fgw[l] = _matmul(dfg_b, xb, "tn", F32, "d_w_fg")[:n_heads]
        by_chip = jnp.concatenate([gw_out_l.reshape(4, out_rows, d), jnp.stack([gw_main_l[b:b + blk] for b in base]),
                                   jnp.zeros((4, pad_rows, d), BF16)], axis=1)
        pending = by_chip.reshape(4, 2, r_half, d)
        if l > 0:
            dx = _matmul(dp_all, wt_all, "nn", F32, "d_x", add=dpre, add_scale=alpha)
        else:
            recv = _run_side(_side_swap(pending), "rs_swap_last")[0]
            hsum = _add_halves(pending, recv, core_arr, "rs_add_halves")
            dx, (others,) = _matmul(dp_all, wt_all, "nn", F32, "d_x", add=dpre, add_scale=alpha,
                                    side=_side_exchange(hsum))

    mine = _add_partials(hsum, others, chip_arr, "rs_add_partials")
    theirs = _run_side(_side_share(mine), "rs_share_last")[0]
    finished[0] = _both_parts(mine, theirs)
    grads = [f.reshape(2 * r_half, d) for f in finished]
    grad_w_out = jnp.stack([g[:out_rows] for g in grads])

    g_sm = jnp.stack(g_sm)
    g_ln = jnp.stack(g_ln)
    parts = [jnp.stack(g_pw), g_sm[:, 0:3, :], g_sm[:, 3, :], g_ln[:, 0, :], g_ln[:, 1, :],
             jnp.stack(g_bf)[:, :, 0], loss_blk[0, 0], jnp.stack(g_fgw)]
    packed = jnp.concatenate([_rows128(p) for p in parts], axis=0)
    summed = _sum_blocks(_all_gather_small(packed, "gather_small_grads"), 8, "sum_small_grads")
    outs, off = [], 0
    for p in parts:
        nr = _rows128(p).shape[0]
        outs.append(summed[off:off + nr].reshape(-1)[:p.size].reshape(p.shape))
        off += nr
    grad_pool_w, gconv_full, grad_pool_scale, grad_ln_g, grad_ln_b, grad_b_f, loss, grad_fg = outs
    grad_conv_w = lax.dynamic_slice_in_dim(gconv_full, chip * (cw // 4), cw // 4, axis=2)

    r_idx = lax.broadcasted_iota(jnp.int32, (in_shard, 1), 0)

    def shard_rows(g, fg_rows):
        gblk = jnp.pad(g[out_rows:r_comb], ((16, 16), (0, 0)))
        plain = lax.dynamic_slice_in_dim(gblk, 16 + lead_dyn, in_shard, axis=0)
        after_fg = lax.dynamic_slice_in_dim(gblk, 16 + lead_dyn - n_heads, in_shard, axis=0)
        fg_placed = jnp.pad(fg_rows, ((hole, in_shard - hole - n_heads), (0, 0)))
        return jnp.where(jnp.logical_and(has_fg, r_idx >= hole + n_heads), after_fg,
                         jnp.where(jnp.logical_and(has_fg, r_idx >= hole), fg_placed, plain))

    grad_rows = jnp.stack([shard_rows(g, grad_fg[l]) for l, g in enumerate(grads)], axis=1)

    d_w_in, nm_w_in, nv_w_in, grad_w_in = (from_rows(a) for a in _adamw(
        to_rows(w_in), grad_rows, to_rows(m_w_in), to_rows(v_w_in), "adamw_w_in", lead_block=46, emit_grad=True))
    d_w_out, nm_w_out, nv_w_out = _adamw(w_out, grad_w_out, m_w_out, v_w_out, "adamw_w_out")
    small = [(b_f, grad_b_f, m_b_f, v_b_f), (conv_w, grad_conv_w, m_conv_w, v_conv_w),
             (pool_w, grad_pool_w, m_pool_w, v_pool_w), (pool_scale, grad_pool_scale, m_pool_scale, v_pool_scale),
             (ln_g, grad_ln_g, m_ln_g, v_ln_g), (ln_b, grad_ln_b, m_ln_b, v_ln_b)]
    pk = [jnp.concatenate([_rows128(t[j]) for t in small], axis=0)[None] for j in range(4)]
    sm_out = _adamw(pk[0], pk[1], pk[2], pk[3], "adamw_small", lead_block=1)
    res = {j: [] for j in range(3)}
    off = 0
    for t in small:
        nr = _rows128(t[0]).shape[0]
        for j in range(3):
            res[j].append(sm_out[j][0, off:off + nr].reshape(-1)[:t[0].size].reshape(t[0].shape))
        off += nr
    (d_b_f, d_conv_w, d_pool_w, d_pool_scale, d_ln_g, d_ln_b) = res[0]
    (nm_b_f, nm_conv_w, nm_pool_w, nm_pool_scale, nm_ln_g, nm_ln_b) = res[1]
    (nv_b_f, nv_conv_w, nv_pool_w, nv_pool_scale, nv_ln_g, nv_ln_b) = res[2]

    return (loss, dx[None], grad_w_in, grad_b_f, grad_conv_w, grad_pool_w, grad_pool_scale, grad_w_out,
            grad_ln_g, grad_ln_b,
            d_w_in, d_b_f, d_conv_w, d_pool_w, d_pool_scale, d_w_out, d_ln_g, d_ln_b,
            nm_w_in, nm_b_f, nm_conv_w, nm_pool_w, nm_pool_scale, nm_w_out, nm_ln_g, nm_ln_b,
            nv_w_in, nv_b_f, nv_conv_w, nv_pool_w, nv_pool_scale, nv_w_out, nv_ln_g, nv_ln_b)
```

```python
import jax
import jax.numpy as jnp
from jax import lax
from jax.experimental import pallas as pl
from jax.experimental.pallas import tpu as pltpu

F32 = jnp.float32
BF16 = jnp.bfloat16
MESH = pl.DeviceIdType.MESH

HEAD_DIM = 128
POOL_WINDOWS = (2, 4, 8, 16)
HALO = 16
LN_EPS = 1e-5
ADAM_LR = 0.001
ADAM_B1 = 0.9
ADAM_B2 = 0.999
ADAM_EPS = 1e-08
ADAM_WD = 0.01
ADAM_STEP = 10
VMEM_LIMIT = 56 * 1024 * 1024


def _params(sem, vmem=VMEM_LIMIT):
    return pltpu.CompilerParams(dimension_semantics=sem, vmem_limit_bytes=vmem)


def _tile(n, prefs):
    for t in prefs:
        if n % t == 0:
            return t
    return n


def _sigmoid(x):
    return 1.0 / (1.0 + jnp.exp(-x))


def _hbm_spec():
    return pl.BlockSpec(memory_space=pltpu.HBM)


class _Side:
    def __init__(self, inputs, out_shapes, n_sems, start, wait, aliases=None):
        self.inputs, self.out_shapes, self.n_sems = list(inputs), list(out_shapes), n_sems
        self.start, self.wait, self.aliases = start, wait, dict(aliases or {})


def _pcall(body, *, name, grid, in_specs, out_specs, out_shape, scratch_shapes, semantics, args, side=None):
    if side is None:
        return pl.pallas_call(
            body, name=name, grid=grid, in_specs=in_specs, out_specs=out_specs, out_shape=out_shape,
            scratch_shapes=scratch_shapes, compiler_params=_params(semantics))(*args)
    single = not isinstance(out_shape, (tuple, list))
    out_specs_l = [out_specs] if single else list(out_specs)
    out_shape_l = [out_shape] if single else list(out_shape)
    n_i, n_si, n_o, n_so, n_s = len(in_specs), len(side.inputs), len(out_shape_l), len(side.out_shapes), len(scratch_shapes)

    def wrapped(*refs):
        ins, sins = refs[:n_i], refs[n_i:n_i + n_si]
        o0 = n_i + n_si
        outs, souts = refs[o0:o0 + n_o], refs[o0 + n_o:o0 + n_o + n_so]
        s0 = o0 + n_o + n_so
        scr, (send, recv) = refs[s0:s0 + n_s], refs[s0 + n_s:]
        first = pl.program_id(0) == 0
        last = pl.program_id(0) == grid[0] - 1
        for ax in range(1, len(grid)):
            first = jnp.logical_and(first, pl.program_id(ax) == 0)
            last = jnp.logical_and(last, pl.program_id(ax) == grid[ax] - 1)

        @pl.when(first)
        def _():
            side.start(sins, souts, send, recv)

        body(*ins, *outs, *scr)

        @pl.when(last)
        def _():
            side.wait(sins, souts, send, recv)

    res = pl.pallas_call(
        wrapped, name=name, grid=grid,
        in_specs=list(in_specs) + [_hbm_spec()] * n_si,
        out_specs=tuple(out_specs_l + [_hbm_spec()] * n_so),
        out_shape=tuple(out_shape_l + side.out_shapes),
        scratch_shapes=list(scratch_shapes) + [pltpu.SemaphoreType.DMA((side.n_sems,))] * 2,
        input_output_aliases={n_i + i: n_o + o for i, o in side.aliases.items()},
        compiler_params=_params(("arbitrary",) * len(grid)),
    )(*args, *side.inputs)
    main = res[:n_o]
    return (main[0] if single else tuple(main)), tuple(res[n_o:])


def _run_side(side, name):
    n_si, n_so = len(side.inputs), len(side.out_shapes)

    def body(*refs):
        sins, souts = refs[:n_si], refs[n_si:n_si + n_so]
        send, recv = refs[n_si + n_so:]
        side.start(sins, souts, send, recv)
        side.wait(sins, souts, send, recv)

    return pl.pallas_call(
        body, name=name, out_shape=tuple(side.out_shapes),
        in_specs=[_hbm_spec()] * n_si, out_specs=tuple([_hbm_spec()] * n_so),
        scratch_shapes=[pltpu.SemaphoreType.DMA((side.n_sems,))] * 2,
        input_output_aliases=side.aliases,
    )(*side.inputs)


_DIMS = {"nn": ((1,), (0,)), "nt": ((1,), (1,)), "tn": ((0,), (0,))}


def _matmul(a, b, form, out_dtype, name, add=None, add_scale=1.0, tm=None, tn=None, tk=None, side=None,
            m_used=None, n_used=None):
    if form == "nn":
        (m, k), (_, n) = a.shape, b.shape
    elif form == "nt":
        (m, k), (n, _) = a.shape, b.shape
    else:
        (k, m), (_, n) = a.shape, b.shape
    m, n = m_used or m, n_used or n
    tm = tm or _tile(m, (1024, 512, 256, 128))
    tn = tn or _tile(n, (1024, 512, 256, 128))
    tk = tk or _tile(k, (2432, 2048, 1792, 1024, 512, 256, 128))
    nk = k // tk
    if form == "tn":
        a_spec = pl.BlockSpec((tk, tm), lambda i, j, kk: (kk, i))
    else:
        a_spec = pl.BlockSpec((tm, tk), lambda i, j, kk: (i, kk))
    if form == "nt":
        b_spec = pl.BlockSpec((tn, tk), lambda i, j, kk: (j, kk))
    else:
        b_spec = pl.BlockSpec((tk, tn), lambda i, j, kk: (kk, j))
    o_spec = pl.BlockSpec((tm, tn), lambda i, j, kk: (i, j))
    dims = (_DIMS[form], ((), ()))
    has_add = add is not None

    def finish(r, add_ref, o_ref):
        if has_add:
            r = r + add_scale * add_ref[...]
        o_ref[...] = r.astype(out_dtype)

    def body_one_pass(a_ref, b_ref, *rest):
        add_ref, o_ref = rest if has_add else (None,) + rest
        finish(lax.dot_general(a_ref[...], b_ref[...], dims, preferred_element_type=F32), add_ref, o_ref)

    def body_accumulate(a_ref, b_ref, *rest):
        add_ref, o_ref, acc = rest if has_add else (None,) + rest
        kk = pl.program_id(2)
        part = lax.dot_general(a_ref[...], b_ref[...], dims, preferred_element_type=F32)

        @pl.when(kk == 0)
        def _():
            acc[...] = part

        @pl.when(jnp.logical_and(kk > 0, kk < nk - 1))
        def _():
            acc[...] += part

        @pl.when(kk == nk - 1)
        def _():
            finish(acc[...] + part, add_ref, o_ref)

    in_specs = [a_spec, b_spec] + ([o_spec] if has_add else [])
    args = (a, b) + ((add,) if has_add else ())
    return _pcall(
        body_one_pass if nk == 1 else body_accumulate, name=name, grid=(m // tm, n // tn, nk),
        in_specs=in_specs, out_specs=o_spec,
        out_shape=jax.ShapeDtypeStruct((m, n), out_dtype),
        scratch_shapes=[] if nk == 1 else [pltpu.VMEM((tm, tn), F32)],
        semantics=("parallel", "parallel", "arbitrary"), args=args, side=side)


def _tri(n, upper):
    r = lax.broadcasted_iota(jnp.int32, (n, n), 0)
    c = lax.broadcasted_iota(jnp.int32, (n, n), 1)
    return jnp.where((r <= c) if upper else (r >= c), 1.0, 0.0).astype(F32)


def _gate_fwd(fg_t, b_col):
    h, s = fg_t.shape
    nb = s // 128

    def body(fg_ref, b_ref, cum_ref):
        u = _tri(128, True)
        carry = jnp.zeros((h, 128), F32)
        for j in range(nb):
            z = fg_ref[:, j * 128:(j + 1) * 128] + b_ref[...]
            logf = -(jnp.maximum(-z, 0.0) + jnp.log(1.0 + jnp.exp(-jnp.abs(z))))
            c = jnp.dot(logf, u, precision=lax.Precision.HIGHEST, preferred_element_type=F32) + carry
            cum_ref[:, j * 128:(j + 1) * 128] = c
            carry = jnp.broadcast_to(c[:, 127:128], (h, 128))

    return pl.pallas_call(
        body, name="gate_fwd", out_shape=jax.ShapeDtypeStruct((h, s), F32),
        in_specs=[pl.BlockSpec(memory_space=pltpu.VMEM)] * 2,
        out_specs=pl.BlockSpec(memory_space=pltpu.VMEM),
    )(fg_t, b_col)


def _gate_bwd(drow_t, dcol_t, fg_t, b_col):
    h, s = fg_t.shape
    nb = s // 128

    def body(dr_ref, dc_ref, fg_ref, b_ref, dfg_ref, db_ref):
        low = _tri(128, False)
        carry = jnp.zeros((h, 128), F32)
        db = jnp.zeros((h, 128), F32)
        for j in reversed(range(nb)):
            sl = slice(j * 128, (j + 1) * 128)
            r = jnp.dot(dr_ref[:, 0, sl] - dc_ref[:, 0, sl], low, precision=lax.Precision.HIGHEST,
                        preferred_element_type=F32) + carry
            carry = jnp.broadcast_to(r[:, 0:1], (h, 128))
            z = fg_ref[:, sl] + b_ref[...]
            dfg = r / (1.0 + jnp.exp(z))
            dfg_ref[:, sl] = dfg
            db = db + dfg
        db_ref[...] = jnp.broadcast_to(jnp.sum(db, axis=1, keepdims=True), (h, 128))

    return pl.pallas_call(
        body, name="gate_bwd",
        out_shape=(jax.ShapeDtypeStruct((h, s), F32), jax.ShapeDtypeStruct((h, 128), F32)),
        in_specs=[pl.BlockSpec(memory_space=pltpu.VMEM)] * 4,
        out_specs=(pl.BlockSpec(memory_space=pltpu.VMEM),) * 2,
    )(drow_t, dcol_t, fg_t, b_col)


LOG2E = 1.4426950408889634


def _attn_fwd(p_main, cum4, n_heads, t, side=None):
    s = p_main.shape[0]
    nq = s // t
    scale2 = HEAD_DIM ** -0.5 * LOG2E
    rep = t // 128

    def body(q_ref, k_ref, v_ref, c_ref, o_ref, lse_ref, m_s, l_s, acc_s, s_a, s_b):
        qi = pl.program_id(1)
        m_s[...] = jnp.full_like(m_s, -jnp.inf)
        l_s[...] = jnp.zeros_like(l_s)
        acc_s[...] = jnp.zeros_like(acc_s)

        def keys(ki):
            return pl.ds(pl.multiple_of(ki * t, t), t)

        def logits(ki, buf):
            buf[...] = lax.dot_general(q_ref[...], k_ref[keys(ki), :], (_DIMS["nt"], ((), ())),
                                       preferred_element_type=F32)

        def consume(ki, buf, masked):
            sc = buf[...] * scale2 - c_ref[ki] * LOG2E
            if masked:
                row = lax.broadcasted_iota(jnp.int32, (t, t), 0)
                col = lax.broadcasted_iota(jnp.int32, (t, t), 1)
                sc = jnp.where(col <= row, sc, -jnp.inf)
            m_prev = m_s[...]
            m_new = jnp.maximum(m_prev, jnp.max(sc, axis=1, keepdims=True))
            alpha = jnp.exp2(m_prev - m_new)
            p = jnp.exp2(sc - jnp.tile(m_new, (1, rep)))
            l_s[...] = alpha * l_s[...] + jnp.sum(p, axis=1, keepdims=True)
            acc_s[...] = alpha * acc_s[...] + jnp.dot(p.astype(BF16), v_ref[keys(ki), :], preferred_element_type=F32)
            m_s[...] = m_new

        def pair(j, carry):
            logits(2 * j + 1, s_b)
            consume(2 * j, s_a, False)
            logits(2 * j + 2, s_a)
            consume(2 * j + 1, s_b, False)
            return carry

        logits(0, s_a)
        lax.fori_loop(0, qi // 2, pair, 0)

        @pl.when(qi % 2 == 0)
        def _():
            consume(qi, s_a, True)

        @pl.when(qi % 2 == 1)
        def _():
            logits(qi, s_b)
            consume(qi - 1, s_a, False)
            consume(qi, s_b, True)

        o_ref[...] = (acc_s[...] / l_s[...]).astype(BF16)
        lse_ref[...] = m_s[...] + jnp.log(l_s[...]) * LOG2E

    head = lambda off: pl.BlockSpec((s, 128), lambda h, qi: (0, off + h))
    return _pcall(
        body, name="attn_fwd", grid=(n_heads, nq),
        in_specs=[pl.BlockSpec((t, 128), lambda h, qi: (qi, h)), head(n_heads), head(2 * n_heads),
                  pl.BlockSpec((None, nq, 1, t), lambda h, qi: (h, 0, 0, 0))],
        out_specs=(pl.BlockSpec((t, 128), lambda h, qi: (qi, h)),) * 2,
        out_shape=(jax.ShapeDtypeStruct((s, n_heads * 128), BF16), jax.ShapeDtypeStruct((s, n_heads * 128), F32)),
        scratch_shapes=[pltpu.VMEM((t, 128), F32)] * 3 + [pltpu.VMEM((t, t), F32)] * 2,
        semantics=("parallel", "arbitrary"), args=(p_main, p_main, p_main, cum4), side=side)


def _attn_bwd(p_main, do, lse_rep, delta_rep, cum4, n_heads, t, side=None):
    s = p_main.shape[0]
    nq = s // t
    scale = HEAD_DIM ** -0.5
    rep = t // 128

    def body(q_ref, k_ref, v_ref, do_ref, lse_ref, dl_ref, c_ref,
             dq_ref, dk_ref, dv_ref, drow_ref, dcol_ref, dq_acc, dk_acc, dv_acc, s_a, dp_a, s_b, dp_b):
        kb = pl.program_id(1)

        @pl.when(kb == 0)
        def _():
            dq_acc[...] = jnp.zeros_like(dq_acc)

        dk_acc[...] = jnp.zeros_like(dk_acc)
        dv_acc[...] = jnp.zeros_like(dv_acc)
        k = k_ref[...]
        v = v_ref[...]
        c2 = c_ref[...] * LOG2E
        k_ones = jnp.concatenate([k, jnp.ones((t, 128), BF16)], axis=1)
        q_ones = jnp.ones((t, 128), BF16)

        def rows_of(qb):
            return pl.ds(pl.multiple_of(qb * t, t), t)

        def inputs(qb, s_buf, dp_buf):
            s_buf[...] = lax.dot_general(q_ref[rows_of(qb), :], k, (_DIMS["nt"], ((), ())), preferred_element_type=F32)
            dp_buf[...] = lax.dot_general(do_ref[rows_of(qb), :], v, (_DIMS["nt"], ((), ())), preferred_element_type=F32)

        def consume(qb, s_buf, dp_buf, masked):
            rows = rows_of(qb)
            sc = s_buf[...] * (scale * LOG2E) - c2
            if masked:
                row = lax.broadcasted_iota(jnp.int32, (t, t), 0)
                col = lax.broadcasted_iota(jnp.int32, (t, t), 1)
                sc = jnp.where(col <= row, sc, -jnp.inf)
            p = jnp.exp2(sc - jnp.tile(lse_ref[rows, :], (1, rep)))
            dsb = (p * (dp_buf[...] - jnp.tile(dl_ref[rows, :], (1, rep)))).astype(BF16)
            dv_acc[...] += lax.dot_general(p.astype(BF16), do_ref[rows, :], (_DIMS["tn"], ((), ())),
                                           preferred_element_type=F32)
            dk_acc[...] += lax.dot_general(dsb, jnp.concatenate([q_ref[rows, :], q_ones], axis=1),
                                           (_DIMS["tn"], ((), ())), preferred_element_type=F32)
            dq_acc[rows, :] += jnp.dot(dsb, k_ones, preferred_element_type=F32)

        n_rest = nq - 1 - kb
        inputs(kb, s_a, dp_a)

        @pl.when(n_rest > 0)
        def _():
            inputs(kb + 1, s_b, dp_b)

        consume(kb, s_a, dp_a, True)
        diag = rows_of(kb)
        dq_ref[diag, :] = (dq_acc[diag, 0:128] * scale).astype(BF16)
        drow_ref[:, diag] = dq_acc[diag, 128:256].T[0:8, :]

        def pair(j, carry):
            qb = kb + 1 + 2 * j
            inputs(qb + 1, s_a, dp_a)
            consume(qb, s_b, dp_b, False)
            inputs(qb + 2, s_b, dp_b)
            consume(qb + 1, s_a, dp_a, False)
            return carry

        n_pairs = jnp.maximum(n_rest - 1, 0) // 2
        lax.fori_loop(0, n_pairs, pair, 0)
        left = kb + 1 + 2 * n_pairs

        @pl.when(n_rest - 2 * n_pairs == 1)
        def _():
            consume(left, s_b, dp_b, False)

        @pl.when(n_rest - 2 * n_pairs == 2)
        def _():
            inputs(left + 1, s_a, dp_a)
            consume(left, s_b, dp_b, False)
            consume(left + 1, s_a, dp_a, False)

        dk_ref[...] = (dk_acc[:, 0:128] * scale).astype(BF16)
        dcol_ref[...] = dk_acc[:, 128:256].T[0:8, :]
        dv_ref[...] = dv_acc[...].astype(BF16)

    kside = lambda off: pl.BlockSpec((t, 128), lambda h, kb: (kb, off + h))
    whole = pl.BlockSpec((s, 128), lambda h, kb: (0, h))
    hw = n_heads * 128
    return _pcall(
        body, name="attn_bwd", grid=(n_heads, nq),
        in_specs=[whole, kside(n_heads), kside(2 * n_heads), whole, whole, whole,
                  pl.BlockSpec((None, None, 1, t), lambda h, kb: (h, kb, 0, 0))],
        out_specs=(whole, kside(0), kside(0), pl.BlockSpec((None, 8, s), lambda h, kb: (h, 0, 0)),
                   pl.BlockSpec((None, 8, t), lambda h, kb: (h, 0, kb))),
        out_shape=(jax.ShapeDtypeStruct((s, hw), BF16), jax.ShapeDtypeStruct((s, hw), BF16),
                   jax.ShapeDtypeStruct((s, hw), BF16), jax.ShapeDtypeStruct((n_heads, 8, s), F32),
                   jax.ShapeDtypeStruct((n_heads, 8, s), F32)),
        scratch_shapes=[pltpu.VMEM((s, 256), F32), pltpu.VMEM((t, 256), F32), pltpu.VMEM((t, 128), F32)]
        + [pltpu.VMEM((t, t), F32)] * 4,
        semantics=("parallel", "arbitrary"), args=(p_main, p_main, p_main, do, lse_rep, delta_rep, cum4), side=side)


def _conv_fwd(u_ext, cw_ref):
    r1 = pltpu.roll(u_ext, 1, 0)
    r2 = pltpu.roll(u_ext, 2, 0)
    conv = cw_ref[2:3, :] * u_ext + cw_ref[1:2, :] * r1 + cw_ref[0:1, :] * r2
    return conv[HALO:], r1[HALO:], r2[HALO:]


def _pool_z(pu, pu_halo, row0, tm, pg):
    ext = jnp.concatenate([pu_halo, pu], axis=0)
    t1 = (row0 + lax.broadcasted_iota(jnp.int32, (tm, pg), 0) + 1).astype(F32)
    zs = []
    for g, w in enumerate(POOL_WINDOWS):
        sm = ext[:, g * pg:(g + 1) * pg]
        sh = 1
        while sh < w:
            sm = sm + pltpu.roll(sm, sh, 0)
            sh *= 2
        mean = sm[HALO:] / jnp.minimum(t1, float(w))
        zs.append(mean - pu[:, g * pg:(g + 1) * pg])
    return zs


def _mix_specs(tm, s):
    per = tm // HALO
    last = s // HALO - 1
    cur = lambda w, j: pl.BlockSpec((tm, w), lambda i: (i, j))
    prev = lambda j: pl.BlockSpec((HALO, 512), lambda i: (jnp.maximum(i * per - 1, 0), j))
    nxt = lambda j: pl.BlockSpec((HALO, 512), lambda i: (jnp.minimum((i + 1) * per, last), j))
    full = lambda shape: pl.BlockSpec(shape, lambda i: (0,) * len(shape))
    return cur, prev, nxt, full


def _mix_fwd(p_main, o, conv_w, pool_w, pool_scale, tm):
    s = p_main.shape[0]
    d = 2048
    pg = 128
    cur, prev, nxt, full = _mix_specs(tm, s)

    def body(ga_ref, cb_ref, cc_ref, ch_ref, gc_ref, pu_ref, gp_ref, cch_ref, chh_ref, puh_ref,
             o_ref, cw_ref, pw_ref, ps_ref, y_ref):
        i = pl.program_id(0)
        first = i == 0
        ga = ga_ref[...].astype(F32)
        y_ref[:, 0:1024] = (o_ref[...].astype(F32) * ga * _sigmoid(ga)).astype(BF16)

        u = cc_ref[...].astype(F32) * ch_ref[...].astype(F32)
        uh = jnp.where(first, 0.0, cch_ref[...].astype(F32) * chh_ref[...].astype(F32))
        conv, _, _ = _conv_fwd(jnp.concatenate([uh, u], axis=0), cw_ref)
        gc = gc_ref[...].astype(F32)
        y_ref[:, 1024:1536] = (cb_ref[...].astype(F32) * conv * gc * _sigmoid(gc)).astype(BF16)

        pu = pu_ref[...].astype(F32)
        puh = jnp.where(first, 0.0, puh_ref[...].astype(F32))
        zs = _pool_z(pu, puh, i * tm, tm, pg)
        gp = gp_ref[...].astype(F32)
        gate = gp * _sigmoid(gp) * ps_ref[...]
        for g in range(4):
            r = jnp.dot(zs[g].astype(BF16), pw_ref[g], preferred_element_type=F32)
            y_ref[:, 1536 + g * pg:1536 + (g + 1) * pg] = (r * gate[:, g * pg:(g + 1) * pg]).astype(BF16)

    return pl.pallas_call(
        body, name="mix_fwd", grid=(s // tm,),
        in_specs=[cur(1024, 3), cur(512, 8), cur(512, 9), cur(512, 10), cur(512, 11), cur(512, 12), cur(512, 13),
                  prev(9), prev(10), prev(12),
                  cur(1024, 0), full((3, 512)), full((4, pg, pg)), full((1, 512))],
        out_specs=pl.BlockSpec((tm, d), lambda i: (i, 0)),
        out_shape=jax.ShapeDtypeStruct((s, d), BF16),
        compiler_params=_params(("parallel",)),
    )(*([p_main] * 10), o, conv_w, pool_w, pool_scale)


def _mix_bwd(p_main, o, dy, conv_w, pool_w, pool_scale, tm):
    s = p_main.shape[0]
    pg = 128
    n_heads = 8
    cur, prev, nxt, full = _mix_specs(tm, s)
    nblk = s // tm
    n_ext = tm + HALO

    def silu_and_grad(x):
        sg = _sigmoid(x)
        return x * sg, sg * (1.0 + x * (1.0 - sg))

    def body(ga_ref, cb_ref, cc_ref, ch_ref, gc_ref, pu_ref, gp_ref, cch_ref, chh_ref, puh_ref,
             cbn_ref, gcn_ref, gpn_ref, o_ref, dy_ref, dycn_ref, dypn_ref, cw_ref, pw_ref, ps_ref,
             do_ref, dl_ref, dp_ref, dsm_ref, dpw_ref):
        i = pl.program_id(0)
        first = i == 0
        last = i == nblk - 1

        @pl.when(first)
        def _():
            dsm_ref[...] = jnp.zeros_like(dsm_ref)
            dpw_ref[...] = jnp.zeros_like(dpw_ref)

        ga = ga_ref[...].astype(F32)
        of = o_ref[...].astype(F32)
        dya = dy_ref[:, 0:1024]
        sa, dsa = silu_and_grad(ga)
        dout = dya * sa
        do_ref[...] = dout.astype(BF16)
        dp_ref[:, 0:1024] = (dya * of * dsa).astype(BF16)
        prod = dout * of
        for h in range(n_heads):
            dsum = jnp.sum(prod[:, h * 128:(h + 1) * 128], axis=1, keepdims=True)
            dl_ref[:, h * 128:(h + 1) * 128] = jnp.broadcast_to(dsum, (tm, 128))

        cb = cb_ref[...].astype(F32)
        cc = cc_ref[...].astype(F32)
        ch = ch_ref[...].astype(F32)
        gc = gc_ref[...].astype(F32)
        u = cc * ch
        uh = jnp.where(first, 0.0, cch_ref[...].astype(F32) * chh_ref[...].astype(F32))
        conv, u1, u2 = _conv_fwd(jnp.concatenate([uh, u], axis=0), cw_ref)
        sc_, dsc = silu_and_grad(gc)
        dyc = dy_ref[:, 1024:1536]
        dp_ref[:, 1024:1536] = (dyc * conv * sc_).astype(BF16)
        dp_ref[:, 2560:3072] = (dyc * cb * conv * dsc).astype(BF16)
        dconv = dyc * cb * sc_
        gcn = gcn_ref[...].astype(F32)
        dconv_n = jnp.where(last, 0.0, dycn_ref[...] * cbn_ref[...].astype(F32) * gcn * _sigmoid(gcn))
        dext = jnp.concatenate([dconv, dconv_n], axis=0)
        du = (cw_ref[2:3, :] * dext + cw_ref[1:2, :] * pltpu.roll(dext, n_ext - 1, 0)
              + cw_ref[0:1, :] * pltpu.roll(dext, n_ext - 2, 0))[:tm]
        dp_ref[:, 1536:2048] = (du * ch).astype(BF16)
        dp_ref[:, 2048:2560] = (du * cc).astype(BF16)
        dsm_ref[0:1, :] += jnp.sum(dconv * u2, axis=0, keepdims=True)
        dsm_ref[1:2, :] += jnp.sum(dconv * u1, axis=0, keepdims=True)
        dsm_ref[2:3, :] += jnp.sum(dconv * u, axis=0, keepdims=True)

        pu = pu_ref[...].astype(F32)
        puh = jnp.where(first, 0.0, puh_ref[...].astype(F32))
        zs = _pool_z(pu, puh, i * tm, tm, pg)
        gp = gp_ref[...].astype(F32)
        sp, dsp = silu_and_grad(gp)
        dyp = dy_ref[:, 1536:2048]
        gpn = gpn_ref[...].astype(F32)
        dr_n = jnp.where(last, 0.0, dypn_ref[...] * gpn * _sigmoid(gpn) * ps_ref[...])
        drs = dyp * sp
        dr = drs * ps_ref[...]
        t1 = (i * tm + lax.broadcasted_iota(jnp.int32, (tm, pg), 0) + 1).astype(F32)
        r_parts, dpu_parts = [], []
        for g, w in enumerate(POOL_WINDOWS):
            cols = slice(g * pg, (g + 1) * pg)
            zb = zs[g].astype(BF16)
            r_parts.append(jnp.dot(zb, pw_ref[g], preferred_element_type=F32))
            drb = dr[:, cols].astype(BF16)
            dpw_ref[g] += lax.dot_general(zb, drb, (_DIMS["tn"], ((), ())), preferred_element_type=F32)
            dz = lax.dot_general(drb, pw_ref[g], (_DIMS["nt"], ((), ())), preferred_element_type=F32)
            dz_n = lax.dot_general(dr_n[:, cols].astype(BF16), pw_ref[g], (_DIMS["nt"], ((), ())),
                                   preferred_element_type=F32)
            sm = jnp.concatenate([dz / jnp.minimum(t1, float(w)), dz_n / float(w)], axis=0)
            sh = 1
            while sh < w:
                sm = sm + pltpu.roll(sm, n_ext - sh, 0)
                sh *= 2
            dpu_parts.append(sm[:tm] - dz)
        r = jnp.concatenate(r_parts, axis=1)
        dp_ref[:, 3072:3584] = jnp.concatenate(dpu_parts, axis=1).astype(BF16)
        dp_ref[:, 3584:4096] = (dyp * r * ps_ref[...] * dsp).astype(BF16)
        dsm_ref[3:4, :] += jnp.sum(drs * r, axis=0, keepdims=True)

    dy_next = lambda j: pl.BlockSpec((HALO, 512), lambda i: (jnp.minimum((i + 1) * (tm // HALO), s // HALO - 1), j))
    return pl.pallas_call(
        body, name="mix_bwd", grid=(nblk,),
        in_specs=[cur(1024, 3), cur(512, 8), cur(512, 9), cur(512, 10), cur(512, 11), cur(512, 12), cur(512, 13),
                  prev(9), prev(10), prev(12), nxt(8), nxt(11), nxt(13),
                  cur(1024, 0), pl.BlockSpec((tm, 2048), lambda i: (i, 0)), dy_next(2), dy_next(3),
                  full((3, 512)), full((4, pg, pg)), full((1, 512))],
        out_specs=(pl.BlockSpec((tm, 1024), lambda i: (i, 0)), pl.BlockSpec((tm, 1024), lambda i: (i, 0)),
                   pl.BlockSpec((tm, 4096), lambda i: (i, 0)), full((8, 512)), full((4, pg, pg))),
        out_shape=(jax.ShapeDtypeStruct((s, 1024), BF16), jax.ShapeDtypeStruct((s, 1024), F32),
                   jax.ShapeDtypeStruct((s, 4096), BF16), jax.ShapeDtypeStruct((8, 512), F32),
                   jax.ShapeDtypeStruct((4, pg, pg), F32)),
        compiler_params=_params(("arbitrary",)),
    )(*([p_main] * 13), o, dy, dy, dy, conv_w, pool_w, pool_scale)


def _outproj_ln(y, w_out, x, ln_g, ln_b, alpha, tm, side=None):
    s, d = x.shape

    def body(y_ref, w_ref, x_ref, g_ref, b_ref, xn_ref, xb_ref, pre_ref):
        pre = alpha * x_ref[...] + jnp.dot(y_ref[...], w_ref[...], preferred_element_type=F32)
        mu = jnp.mean(pre, axis=1, keepdims=True)
        cen = pre - mu
        var = jnp.mean(cen * cen, axis=1, keepdims=True)
        xn = cen * lax.rsqrt(var + LN_EPS) * g_ref[...] + b_ref[...]
        pre_ref[...] = pre
        xn_ref[...] = xn
        xb_ref[...] = xn.astype(BF16)

    row = pl.BlockSpec((tm, d), lambda i: (i, 0))
    vec = pl.BlockSpec((1, d), lambda i: (0, 0))
    return _pcall(
        body, name="outproj_ln", grid=(s // tm,),
        in_specs=[row, pl.BlockSpec((d, d), lambda i: (0, 0)), row, vec, vec],
        out_specs=(row, row, row),
        out_shape=(jax.ShapeDtypeStruct((s, d), F32), jax.ShapeDtypeStruct((s, d), BF16),
                   jax.ShapeDtypeStruct((s, d), F32)),
        scratch_shapes=[], semantics=("parallel",), args=(y, w_out, x, ln_g, ln_b), side=side)


def _ln_bwd(dxn, pre, ln_g, tm, side=None):
    s, d = pre.shape

    def body(dx_ref, pre_ref, g_ref, dpre_ref, dpb_ref, dgb_ref):
        @pl.when(pl.program_id(0) == 0)
        def _():
            dgb_ref[...] = jnp.zeros_like(dgb_ref)

        pre_ = pre_ref[...]
        dx = dx_ref[...]
        mu = jnp.mean(pre_, axis=1, keepdims=True)
        cen = pre_ - mu
        var = jnp.mean(cen * cen, axis=1, keepdims=True)
        rstd = lax.rsqrt(var + LN_EPS)
        xhat = cen * rstd
        dxh = dx * g_ref[...]
        dpre = rstd * (dxh - jnp.mean(dxh, axis=1, keepdims=True)
                       - xhat * jnp.mean(dxh * xhat, axis=1, keepdims=True))
        dpre_ref[...] = dpre
        dpb_ref[...] = dpre.astype(BF16)
        dgb_ref[0:1, :] += jnp.sum(dx * xhat, axis=0, keepdims=True)
        dgb_ref[1:2, :] += jnp.sum(dx, axis=0, keepdims=True)

    row = pl.BlockSpec((tm, d), lambda i: (i, 0))
    return _pcall(
        body, name="ln_bwd", grid=(s // tm,),
        in_specs=[row, row, pl.BlockSpec((1, d), lambda i: (0, 0))],
        out_specs=(row, row, pl.BlockSpec((8, d), lambda i: (0, 0))),
        out_shape=(jax.ShapeDtypeStruct((s, d), F32), jax.ShapeDtypeStruct((s, d), BF16),
                   jax.ShapeDtypeStruct((8, d), F32)),
        scratch_shapes=[], semantics=("arbitrary",), args=(dxn, pre, ln_g), side=side)


def _loss_grad(y, target, tm):
    s, d = y.shape

    def body(y_ref, t_ref, dy_ref, loss_ref):
        @pl.when(pl.program_id(0) == 0)
        def _():
            loss_ref[...] = jnp.zeros_like(loss_ref)

        diff = y_ref[...] - t_ref[...]
        dy_ref[...] = diff / d
        loss_ref[...] += 0.5 * jnp.sum(jnp.sum(diff * diff, axis=1, keepdims=True) / d)

    row = pl.BlockSpec((tm, d), lambda i: (i, 0))
    return pl.pallas_call(
        body, name="loss_grad", grid=(s // tm,),
        in_specs=[row, row],
        out_specs=(row, pl.BlockSpec((8, 128), lambda i: (0, 0))),
        out_shape=(jax.ShapeDtypeStruct((s, d), F32), jax.ShapeDtypeStruct((8, 128), F32)),
        compiler_params=_params(("arbitrary",)),
    )(y, target)


def _adamw(w, g, m, v, name, lead_block=None, emit_grad=False):
    a, r, c = w.shape
    tr = r if lead_block else _tile(r, (256, 128, 64, 32, 16, 8))
    c1 = 1.0 - ADAM_B1 ** ADAM_STEP
    c2 = 1.0 - ADAM_B2 ** ADAM_STEP

    def body(w_ref, g_ref, m_ref, v_ref, d_ref, mo_ref, vo_ref, *go_ref):
        g_ = g_ref[...]
        if emit_grad:
            go_ref[0][...] = g_
        m_new = ADAM_B1 * m_ref[...] + (1.0 - ADAM_B1) * g_
        v_new = ADAM_B2 * v_ref[...] + (1.0 - ADAM_B2) * (g_ * g_)
        d_ref[...] = -ADAM_LR * ((m_new / c1) / (jnp.sqrt(v_new / c2) + ADAM_EPS) + ADAM_WD * w_ref[...])
        mo_ref[...] = m_new
        vo_ref[...] = v_new

    ta = lead_block or 1
    spec = pl.BlockSpec((ta, tr, c), lambda i, j: (i, j, 0))
    shp = jax.ShapeDtypeStruct(w.shape, F32)
    n_out = 4 if emit_grad else 3
    return pl.pallas_call(
        body, name=name, grid=(a // ta, r // tr), in_specs=[spec] * 4, out_specs=(spec,) * n_out,
        out_shape=(shp,) * n_out, compiler_params=_params(("parallel", "parallel")),
    )(w, g, m, v)


def _place():
    x, y, c = lax.axis_index("x"), lax.axis_index("y"), lax.axis_index("c")
    return x, y, c, [(1 - x, y), (x, 1 - y), (1 - x, 1 - y)]


def _remote(src, dst, send, recv, k, to):
    return pltpu.make_async_remote_copy(src_ref=src, dst_ref=dst, send_sem=send.at[k], recv_sem=recv.at[k],
                                        device_id=to, device_id_type=MESH)


def _slot(ref, px, py, pc):
    return ref.at[pl.ds(2 * px + py, 1), pl.ds(pc, 1)]


def _side_gather_ici(shard, rows=None, into=None):
    _, two, r, w = shard.shape
    lo, hi = rows or (0, r)

    def _slot(ref, px, py, pc):
        return ref.at[pl.ds(2 * px + py, 1), pl.ds(pc, 1), pl.ds(lo, hi - lo)]

    def sends(sins, souts, send, recv):
        x, y, c, chips = _place()
        return [_remote(sins[0].at[:, pl.ds(c, 1), pl.ds(lo, hi - lo)], _slot(souts[0], x, y, c), send, recv, k,
                        (*chip, c)) for k, chip in enumerate(chips)]

    def start(sins, souts, send, recv):
        for cp in sends(sins, souts, send, recv):
            cp.start()

    def wait(sins, souts, send, recv):
        x, y, c, chips = _place()
        for cp in sends(sins, souts, send, recv):
            cp.wait_send()
        for k, chip in enumerate(chips):
            _remote(_slot(souts[0], *chip, c), _slot(souts[0], *chip, c), send, recv, k, (x, y, c)).wait_recv()

    more = [] if into is None else [into]
    return _Side([shard] + more, [jax.ShapeDtypeStruct((4, two, r, w), shard.dtype)], 3, start, wait,
                 aliases={} if into is None else {1: 0})


def _side_gather_d2d(gathered):
    def sends(souts, send, recv):
        x, y, c, chips = _place()
        return [_remote(_slot(souts[0], *chip, c), _slot(souts[0], *chip, c), send, recv, k, (x, y, 1 - c))
                for k, chip in enumerate(chips)]

    def start(sins, souts, send, recv):
        for cp in sends(souts, send, recv):
            cp.start()

    def wait(sins, souts, send, recv):
        x, y, c, chips = _place()
        for cp in sends(souts, send, recv):
            cp.wait_send()
        for k, chip in enumerate(chips):
            _remote(_slot(souts[0], *chip, 1 - c), _slot(souts[0], *chip, 1 - c), send, recv, k, (x, y, c)).wait_recv()

    return _Side([gathered], [jax.ShapeDtypeStruct(gathered.shape, gathered.dtype)], 3, start, wait, aliases={0: 0})


def _side_swap(g):
    nchip, _, r, w = g.shape

    def copy(sins, souts, send, recv):
        x, y, c, _ = _place()
        return _remote(sins[0].at[:, pl.ds(1 - c, 1)], souts[0], send, recv, 0, (x, y, 1 - c))

    return _Side([g], [jax.ShapeDtypeStruct((nchip, 1, r, w), g.dtype)], 1,
                 lambda *a: copy(*a).start(), lambda *a: copy(*a).wait())


def _side_exchange(hsum):
    _, _, r, w = hsum.shape

    def copies(sins, souts, send, recv):
        x, y, c, chips = _place()
        return [_remote(sins[0].at[pl.ds(2 * px + py, 1)], souts[0].at[pl.ds(k, 1)], send, recv, k, (px, py, c))
                for k, (px, py) in enumerate(chips)]

    def start(*a):
        for cp in copies(*a):
            cp.start()

    def wait(*a):
        for cp in copies(*a):
            cp.wait()

    return _Side([hsum], [jax.ShapeDtypeStruct((3, 1, r, w), hsum.dtype)], 3, start, wait)


def _side_share(both):
    def copy(sins, souts, send, recv):
        x, y, c, _ = _place()
        return _remote(souts[0].at[pl.ds(c, 1)], souts[0].at[pl.ds(c, 1)], send, recv, 0, (x, y, 1 - c))

    def wait(sins, souts, send, recv):
        x, y, c, _ = _place()
        copy(sins, souts, send, recv).wait_send()
        _remote(souts[0].at[pl.ds(1 - c, 1)], souts[0].at[pl.ds(1 - c, 1)], send, recv, 0, (x, y, c)).wait_recv()

    return _Side([both], [jax.ShapeDtypeStruct(both.shape, both.dtype)], 1,
                 lambda *a: copy(*a).start(), wait, aliases={0: 0})


def _own_slot_filled(gathered, shard):
    chip = 2 * lax.axis_index("x") + lax.axis_index("y")
    return lax.dynamic_update_slice(gathered, shard, (chip, 0, 0, 0))


def _plane_tile(r, cdim):
    tr = _tile(r, (256, 128))
    if tr != r:
        return tr, cdim
    return r, _tile(cdim, (256, 128))


def _add_halves(g, recv, core, name):
    nchip, nl, r, cdim = g.shape
    half = nl // 2
    tr, tc = _plane_tile(r, cdim)

    def body(c_ref, g_ref, r_ref, o_ref):
        o_ref[...] = (g_ref[...].astype(F32) + r_ref[...].astype(F32)).astype(BF16)

    blk = (1, 1, tr, tc)
    return pl.pallas_call(
        body, name=name,
        grid_spec=pltpu.PrefetchScalarGridSpec(
            num_scalar_prefetch=1, grid=(nchip, half, r // tr, cdim // tc),
            in_specs=[pl.BlockSpec(blk, lambda j, l, i, k, c_ref: (j, c_ref[0] * half + l, i, k)),
                      pl.BlockSpec(blk, lambda j, l, i, k, c_ref: (j, l, i, k))],
            out_specs=pl.BlockSpec(blk, lambda j, l, i, k, c_ref: (j, l, i, k))),
        out_shape=jax.ShapeDtypeStruct((nchip, half, r, cdim), BF16),
        compiler_params=_params(("parallel",) * 4),
    )(core, g, recv)


def _add_partials(hsum, others, chip, core, name):
    nchip, _, r, cdim = hsum.shape
    tr, tc = _plane_tile(r, cdim)

    def body(i_ref, c_ref, h_ref, a_ref, b_ref, d_ref, o_ref):
        o_ref[...] = ((h_ref[...].astype(F32) + a_ref[...].astype(F32))
                      + (b_ref[...].astype(F32) + d_ref[...].astype(F32)))[0]

    blk = (1, 1, tr, tc)
    other = lambda n: pl.BlockSpec(blk, lambda i, k, i_ref, c_ref: (n, 0, i, k))
    return pl.pallas_call(
        body, name=name,
        grid_spec=pltpu.PrefetchScalarGridSpec(
            num_scalar_prefetch=2, grid=(r // tr, cdim // tc),
            in_specs=[pl.BlockSpec(blk, lambda i, k, i_ref, c_ref: (i_ref[0], 0, i, k)), other(0), other(1), other(2)],
            out_specs=pl.BlockSpec((1, tr, tc), lambda i, k, i_ref, c_ref: (c_ref[0], i, k))),
        out_shape=jax.ShapeDtypeStruct((2, r, cdim), F32),
        compiler_params=_params(("parallel",) * 2),
    )(chip, core, hsum, others, others, others)


def _all_gather_small(v, name):
    m_per, n = v.shape

    def body(x_ref, out_ref, send_sems, recv_sems, local_sem):
        x, y, c, chips = _place()
        me, sib = (x, y, c), (x, y, 1 - c)

        def rows(px, py, pc):
            return out_ref.at[pl.ds((4 * px + 2 * py + pc) * m_per, m_per), :]

        def copy(k, block, to, src=None):
            return pltpu.make_async_remote_copy(
                src_ref=rows(*block) if src is None else src, dst_ref=rows(*block), send_sem=send_sems.at[k],
                recv_sem=recv_sems.at[k], device_id=to, device_id_type=MESH)

        mine = pltpu.make_async_copy(x_ref, rows(*me), local_sem)
        mine.start()
        first = [copy(0, me, sib, src=x_ref)]
        first += [copy(1 + j, me, (*chip, c), src=x_ref) for j, chip in enumerate(chips)]
        for cp in first:
            cp.start()
        passed = [copy(4 + j, (*chip, c), sib) for j, chip in enumerate(chips)]
        for j, chip in enumerate(chips):
            copy(1 + j, (*chip, c), me).wait_recv()
            passed[j].start()
        copy(0, sib, me).wait_recv()
        for j, chip in enumerate(chips):
            copy(4 + j, (*chip, 1 - c), me).wait_recv()
        for cp in first + passed:
            cp.wait_send()
        mine.wait()

    return pl.pallas_call(
        body, name=name, out_shape=jax.ShapeDtypeStruct((8 * m_per, n), v.dtype),
        in_specs=[pl.BlockSpec(memory_space=pltpu.VMEM)], out_specs=pl.BlockSpec(memory_space=pltpu.VMEM),
        scratch_shapes=[pltpu.SemaphoreType.DMA((7,)), pltpu.SemaphoreType.DMA((7,)), pltpu.SemaphoreType.DMA],
        compiler_params=pltpu.CompilerParams(vmem_limit_bytes=VMEM_LIMIT),
    )(v)


def _sum_blocks(v, nblk, name):
    m_per = v.shape[0] // nblk

    def body(v_ref, o_ref):
        acc = v_ref[0:m_per, :]
        for j in range(1, nblk):
            acc = acc + v_ref[j * m_per:(j + 1) * m_per, :]
        o_ref[...] = acc

    return pl.pallas_call(
        body, name=name, out_shape=jax.ShapeDtypeStruct((m_per, 128), F32),
        in_specs=[pl.BlockSpec(memory_space=pltpu.VMEM)], out_specs=pl.BlockSpec(memory_space=pltpu.VMEM),
        compiler_params=pltpu.CompilerParams(vmem_limit_bytes=VMEM_LIMIT),
    )(v)


def _rows128(a, rows=None):
    flat = a.reshape(-1).astype(F32)
    need = -(-flat.shape[0] // 128)
    rows = rows or -(-need // 8) * 8
    return jnp.pad(flat, (0, rows * 128 - flat.shape[0])).reshape(rows, 128)


def kernel(x, w_in, b_f, conv_w, pool_w, pool_scale, w_out, ln_g, ln_b, loss_target, m_w_in, m_b_f, m_conv_w, m_pool_w, m_pool_scale, m_w_out, m_ln_g, m_ln_b, v_w_in, v_b_f, v_conv_w, v_pool_w, v_pool_scale, v_w_out, v_ln_g, v_ln_b):
    depth, d, in_shard = w_in.shape
    s = x.shape[1]
    att = d // 2
    n_heads = att // HEAD_DIM
    cw = d // 4
    pw = d - att - cw
    in_w = 4 * att + n_heads + 4 * cw + 2 * pw
    assert in_shard * 4 == in_w and (d, n_heads, cw, pw) == (2048, 8, 512, 512) and s % 512 == 0
    main_w = in_w - n_heads
    fg0 = 4 * att
    alpha = (2 * depth) ** 0.25
    t_att = 512
    tm_mix = 256

    cx, cy, cc = lax.axis_index("x"), lax.axis_index("y"), lax.axis_index("c")
    chip = 2 * cx + cy
    core_arr = jnp.reshape(cc, (1,)).astype(jnp.int32)
    chip_arr = jnp.reshape(chip, (1,)).astype(jnp.int32)
    out_rows = w_out.shape[1]

    starts = [j * in_shard for j in range(4)]
    main_start = [g if g < fg0 else g - n_heads for g in starts]
    base = [m // 16 * 16 for m in main_start]
    lead = [m - b for m, b in zip(main_start, base)]
    fg_chip = max(j for j in range(4) if starts[j] <= fg0)
    hole = fg0 - starts[fg_chip]
    blk = -(-(max(lead) + in_shard) // 16) * 16
    assert hole + n_heads <= in_shard and all(b + blk <= main_w for b in base) and base[3] + blk == main_w
    r_comb = out_rows + blk
    pad_rows = -r_comb % 32
    r_half = (r_comb + pad_rows) // 2
    lead_dyn = jnp.take(jnp.array(lead, jnp.int32), chip)
    has_fg = chip == fg_chip

    to_rows = lambda a: jnp.transpose(a, (2, 0, 1))
    from_rows = lambda a: jnp.transpose(a, (1, 2, 0))
    wt_b = jnp.transpose(to_rows(w_in).astype(BF16), (1, 0, 2))
    padded = jnp.pad(wt_b, ((0, 0), (16, 32), (0, 0)))
    plain = lax.dynamic_slice_in_dim(padded, 16 - lead_dyn, blk, axis=1)
    skipping = lax.dynamic_slice_in_dim(padded, 16 - lead_dyn + n_heads, blk, axis=1)
    pos = lax.broadcasted_iota(jnp.int32, (1, blk, 1), 1)
    block = jnp.where(jnp.logical_and(has_fg, pos >= lead_dyn + hole), skipping, plain)
    comb = jnp.concatenate([w_out.astype(BF16), block, jnp.zeros((depth, pad_rows, d), BF16)], axis=1)
    shards = comb.reshape(depth, 2, r_half, d)

    def weights_of(gathered, wt_fg):
        full = gathered.reshape(4, 2 * r_half, d)
        wt_all = sum(jnp.pad(full[j, out_rows:r_comb], ((base[j], main_w + 128 - base[j] - blk), (0, 0)))
                     for j in range(4)) + jnp.pad(wt_fg, ((main_w, 0), (0, 0)))
        return wt_all, full[:, :out_rows].reshape(d, d)

    small_w = jnp.concatenate([_rows128(conv_w), _rows128(wt_b[:, hole:hole + n_heads])], axis=0)
    n_cw = _rows128(conv_w).shape[0] * 128
    sw_all = _all_gather_small(small_w, "gather_small_weights").reshape(4, 2, small_w.shape[0] * 128)[:, 0]
    conv_full = sw_all[:, :depth * 3 * (cw // 4)].reshape(4, depth, 3, cw // 4)
    conv_full = jnp.moveaxis(conv_full, 0, 2).reshape(depth, 3, cw)
    wt_fg_all = sw_all[fg_chip, n_cw:n_cw + depth * n_heads * d].reshape(depth, n_heads, d).astype(BF16)
    wt_fg_all = jnp.pad(wt_fg_all, ((0, 0), (0, 128 - n_heads), (0, 0)))

    first = shards[0][None]
    arrived = _run_side(_side_gather_ici(first), "gather_first_ici")[0]
    arrived = _run_side(_side_gather_d2d(arrived), "gather_first_d2d")[0]
    weights = weights_of(_own_slot_filled(arrived, first), wt_fg_all[0])
    pool_b = pool_w.astype(BF16)

    split = r_half * 2 // 5 // 16 * 16
    xf = x[0]
    xb = xf.astype(BF16)
    saved = []
    for l in range(depth):
        wt_all, w_out_l = weights
        wt_fg = wt_fg_all[l]
        nxt = shards[l + 1][None] if l + 1 < depth else None
        if nxt is None:
            p_main = _matmul(xb, wt_all, "nt", BF16, "proj_main", n_used=main_w)
        else:
            p_main, (arrived,) = _matmul(xb, wt_all, "nt", BF16, "proj_main", n_used=main_w,
                                         side=_side_gather_ici(nxt, rows=(0, split)))
        fg = _matmul(xb, wt_fg, "nt", F32, "proj_fg")
        fg_t = fg[:, :n_heads].T
        b_col = b_f[l].reshape(n_heads, 1)
        cum4 = _gate_fwd(fg_t, b_col).reshape(n_heads, s // t_att, 1, t_att)
        if nxt is None:
            o, lse = _attn_fwd(p_main, cum4, n_heads, t_att)
        else:
            (o, lse), (arrived,) = _attn_fwd(p_main, cum4, n_heads, t_att,
                                             side=_side_gather_ici(nxt, rows=(split, r_half), into=arrived))
        y = _mix_fwd(p_main, o, conv_full[l], pool_b[l], pool_scale[l].reshape(1, pw), tm_mix)
        ln_args = (y, w_out_l, xf, ln_g[l].reshape(1, d), ln_b[l].reshape(1, d), alpha, tm_mix)
        if nxt is None:
            x_new, xb_new, pre = _outproj_ln(*ln_args)
        else:
            (x_new, xb_new, pre), (arrived,) = _outproj_ln(*ln_args, side=_side_gather_d2d(arrived))
            weights = weights_of(_own_slot_filled(arrived, nxt), wt_fg_all[l + 1])
        saved.append((xb, p_main, fg_t, b_col, cum4, o, lse, y, pre, wt_all, w_out_l))
        xf, xb = x_new, xb_new

    dx, loss_blk = _loss_grad(xf, loss_target[0], tm_mix)

    g_bf, g_sm, g_pw, g_ln, g_fgw, finished = ([None] * depth for _ in range(6))
    pending = None
    for l in reversed(range(depth)):
        xb, p_main, fg_t, b_col, cum4, o, lse, y, pre, wt_all, w_out_l = saved[l]
        if pending is None:
            dpre, dpre_b, g_ln[l] = _ln_bwd(dx, pre, ln_g[l].reshape(1, d), tm_mix)
        else:
            (dpre, dpre_b, g_ln[l]), (recv,) = _ln_bwd(dx, pre, ln_g[l].reshape(1, d), tm_mix,
                                                       side=_side_swap(pending))
            hsum = _add_halves(pending, recv, core_arr, "rs_add_halves")
        dy = _matmul(dpre_b, w_out_l, "nt", F32, "d_y")
        gw_out_l = _matmul(y, dpre_b, "tn", BF16, "d_w_out")
        do, delta, dp_rest, g_sm[l], g_pw[l] = _mix_bwd(p_main, o, dy, conv_full[l], pool_b[l],
                                                        pool_scale[l].reshape(1, pw), tm_mix)
        if pending is None:
            dq, dk, dv, drow, dcol = _attn_bwd(p_main, do, lse, delta, cum4, n_heads, t_att)
        else:
            (dq, dk, dv, drow, dcol), (others,) = _attn_bwd(p_main, do, lse, delta, cum4, n_heads, t_att,
                                                            side=_side_exchange(hsum))
            mine = _add_partials(hsum, others, chip_arr, core_arr, "rs_add_partials")
        dfg_t, g_bf[l] = _gate_bwd(drow, dcol, fg_t, b_col)
        dfg_b = jnp.pad(dfg_t.T, ((0, 0), (0, 128 - n_heads))).astype(BF16)
        dp_all = jnp.concatenate([dq, dk, dv, dp_rest, dfg_b], axis=1)
        if pending is None:
            gw_main_l = _matmul(dp_all, xb, "tn", BF16, "d_w_main", m_used=main_w)
        else:
            gw_main_l, (finished[l + 1],) = _matmul(dp_all, xb, "tn", BF16, "d_w_main", m_used=main_w,
                                                    side=_side_share(mine))
        g_fgw[l] = _matmul(dfg_b, xb, "tn", F32, "d_w_fg")[:n_heads]
        by_chip = jnp.concatenate([gw_out_l.reshape(4, out_rows, d), jnp.stack([gw_main_l[b:b + blk] for b in base]),
                                   jnp.zeros((4, pad_rows, d), BF16)], axis=1)
        pending = by_chip.reshape(4, 2, r_half, d)
        if l > 0:
            dx = _matmul(dp_all, wt_all, "nn", F32, "d_x", add=dpre, add_scale=alpha)
        else:
            recv = _run_side(_side_swap(pending), "rs_swap_last")[0]
            hsum = _add_halves(pending, recv, core_arr, "rs_add_halves")
            dx, (others,) = _matmul(dp_all, wt_all, "nn", F32, "d_x", add=dpre, add_scale=alpha,
                                    side=_side_exchange(hsum))

    mine = _add_partials(hsum, others, chip_arr, core_arr, "rs_add_partials")
    finished[0] = _run_side(_side_share(mine), "rs_share_last")[0]
    grads = [f.reshape(2 * r_half, d) for f in finished]
    grad_w_out = jnp.stack([g[:out_rows] for g in grads])

    g_sm = jnp.stack(g_sm)
    g_ln = jnp.stack(g_ln)
    parts = [jnp.stack(g_pw), g_sm[:, 0:3, :], g_sm[:, 3, :], g_ln[:, 0, :], g_ln[:, 1, :],
             jnp.stack(g_bf)[:, :, 0], loss_blk[0, 0], jnp.stack(g_fgw)]
    packed = jnp.concatenate([_rows128(p) for p in parts], axis=0)
    summed = _sum_blocks(_all_gather_small(packed, "gather_small_grads"), 8, "sum_small_grads")
    outs, off = [], 0
    for p in parts:
        nr = _rows128(p).shape[0]
        outs.append(summed[off:off + nr].reshape(-1)[:p.size].reshape(p.shape))
        off += nr
    grad_pool_w, gconv_full, grad_pool_scale, grad_ln_g, grad_ln_b, grad_b_f, loss, grad_fg = outs
    grad_conv_w = lax.dynamic_slice_in_dim(gconv_full, chip * (cw // 4), cw // 4, axis=2)

    r_idx = lax.broadcasted_iota(jnp.int32, (in_shard, 1), 0)

    def shard_rows(g, fg_rows):
        gblk = jnp.pad(g[out_rows:r_comb], ((16, 16), (0, 0)))
        plain = lax.dynamic_slice_in_dim(gblk, 16 + lead_dyn, in_shard, axis=0)
        after_fg = lax.dynamic_slice_in_dim(gblk, 16 + lead_dyn - n_heads, in_shard, axis=0)
        fg_placed = jnp.pad(fg_rows, ((hole, in_shard - hole - n_heads), (0, 0)))
        return jnp.where(jnp.logical_and(has_fg, r_idx >= hole + n_heads), after_fg,
                         jnp.where(jnp.logical_and(has_fg, r_idx >= hole), fg_placed, plain))

    grad_rows = jnp.stack([shard_rows(g, grad_fg[l]) for l, g in enumerate(grads)], axis=1)

    d_w_in, nm_w_in, nv_w_in, grad_w_in = (from_rows(a) for a in _adamw(
        to_rows(w_in), grad_rows, to_rows(m_w_in), to_rows(v_w_in), "adamw_w_in", lead_block=46, emit_grad=True))
    d_w_out, nm_w_out, nv_w_out = _adamw(w_out, grad_w_out, m_w_out, v_w_out, "adamw_w_out")
    small = [(b_f, grad_b_f, m_b_f, v_b_f), (conv_w, grad_conv_w, m_conv_w, v_conv_w),
             (pool_w, grad_pool_w, m_pool_w, v_pool_w), (pool_scale, grad_pool_scale, m_pool_scale, v_pool_scale),
             (ln_g, grad_ln_g, m_ln_g, v_ln_g), (ln_b, grad_ln_b, m_ln_b, v_ln_b)]
    pk = [jnp.concatenate([_rows128(t[j]) for t in small], axis=0)[None] for j in range(4)]
    sm_out = _adamw(pk[0], pk[1], pk[2], pk[3], "adamw_small", lead_block=1)
    res = {j: [] for j in range(3)}
    off = 0
    for t in small:
        nr = _rows128(t[0]).shape[0]
        for j in range(3):
            res[j].append(sm_out[j][0, off:off + nr].reshape(-1)[:t[0].size].reshape(t[0].shape))
        off += nr
    (d_b_f, d_conv_w, d_pool_w, d_pool_scale, d_ln_g, d_ln_b) = res[0]
    (nm_b_f, nm_conv_w, nm_pool_w, nm_pool_scale, nm_ln_g, nm_ln_b) = res[1]
    (nv_b_f, nv_conv_w, nv_pool_w, nv_pool_scale, nv_ln_g, nv_ln_b) = res[2]

    return (loss, dx[None], grad_w_in, grad_b_f, grad_conv_w, grad_pool_w, grad_pool_scale, grad_w_out,
            grad_ln_g, grad_ln_b,
            d_w_in, d_b_f, d_conv_w, d_pool_w, d_pool_scale, d_w_out, d_ln_g, d_ln_b,
            nm_w_in, nm_b_f, nm_conv_w, nm_pool_w, nm_pool_scale, nm_w_out, nm_ln_g, nm_ln_b,
            nv_w_in, nv_b_f, nv_conv_w, nv_pool_w, nv_pool_scale, nv_w_out, nv_ln_g, nv_ln_b)
```

```python
import jax
import jax.numpy as jnp
from jax import lax
from jax.experimental import pallas as pl
from jax.experimental.pallas import tpu as pltpu

F32 = jnp.float32
BF16 = jnp.bfloat16
MESH = pl.DeviceIdType.MESH

HEAD_DIM = 128
POOL_WINDOWS = (2, 4, 8, 16)
HALO = 16
BF16_TILE_ROWS = 16
LN_EPS = 1e-5
ADAM_LR = 0.001
ADAM_B1 = 0.9
ADAM_B2 = 0.999
ADAM_EPS = 1e-08
ADAM_WD = 0.01
ADAM_STEP = 10
VMEM_LIMIT = 56 * 1024 * 1024


def _params(sem, vmem=VMEM_LIMIT):
    return pltpu.CompilerParams(dimension_semantics=sem, vmem_limit_bytes=vmem)


def _tile(n, prefs):
    for t in prefs:
        if n % t == 0:
            return t
    return n


def _sigmoid(x):
    return 1.0 / (1.0 + jnp.exp(-x))


def _hbm_spec():
    return pl.BlockSpec(memory_space=pltpu.HBM)


class _Side:
    def __init__(self, inputs, out_shapes, n_sems, start, wait, aliases=None):
        self.inputs, self.out_shapes, self.n_sems = list(inputs), list(out_shapes), n_sems
        self.start, self.wait, self.aliases = start, wait, dict(aliases or {})


def _pcall(body, *, name, grid, in_specs, out_specs, out_shape, scratch_shapes, semantics, args, side=None):
    if side is None:
        return pl.pallas_call(
            body, name=name, grid=grid, in_specs=in_specs, out_specs=out_specs, out_shape=out_shape,
            scratch_shapes=scratch_shapes, compiler_params=_params(semantics))(*args)
    single = not isinstance(out_shape, (tuple, list))
    out_specs_l = [out_specs] if single else list(out_specs)
    out_shape_l = [out_shape] if single else list(out_shape)
    n_i, n_si, n_o, n_so, n_s = len(in_specs), len(side.inputs), len(out_shape_l), len(side.out_shapes), len(scratch_shapes)

    def wrapped(*refs):
        ins, sins = refs[:n_i], refs[n_i:n_i + n_si]
        o0 = n_i + n_si
        outs, souts = refs[o0:o0 + n_o], refs[o0 + n_o:o0 + n_o + n_so]
        s0 = o0 + n_o + n_so
        scr, (send, recv) = refs[s0:s0 + n_s], refs[s0 + n_s:]
        first = pl.program_id(0) == 0
        last = pl.program_id(0) == grid[0] - 1
        for ax in range(1, len(grid)):
            first = jnp.logical_and(first, pl.program_id(ax) == 0)
            last = jnp.logical_and(last, pl.program_id(ax) == grid[ax] - 1)

        @pl.when(first)
        def _():
            side.start(sins, souts, send, recv)

        body(*ins, *outs, *scr)

        @pl.when(last)
        def _():
            side.wait(sins, souts, send, recv)

    res = pl.pallas_call(
        wrapped, name=name, grid=grid,
        in_specs=list(in_specs) + [_hbm_spec()] * n_si,
        out_specs=tuple(out_specs_l + [_hbm_spec()] * n_so),
        out_shape=tuple(out_shape_l + side.out_shapes),
        scratch_shapes=list(scratch_shapes) + [pltpu.SemaphoreType.DMA((side.n_sems,))] * 2,
        input_output_aliases={n_i + i: n_o + o for i, o in side.aliases.items()},
        compiler_params=_params(("arbitrary",) * len(grid)),
    )(*args, *side.inputs)
    main = res[:n_o]
    return (main[0] if single else tuple(main)), tuple(res[n_o:])


def _run_side(side, name):
    n_si, n_so = len(side.inputs), len(side.out_shapes)

    def body(*refs):
        sins, souts = refs[:n_si], refs[n_si:n_si + n_so]
        send, recv = refs[n_si + n_so:]
        side.start(sins, souts, send, recv)
        side.wait(sins, souts, send, recv)

    return pl.pallas_call(
        body, name=name, out_shape=tuple(side.out_shapes),
        in_specs=[_hbm_spec()] * n_si, out_specs=tuple([_hbm_spec()] * n_so),
        scratch_shapes=[pltpu.SemaphoreType.DMA((side.n_sems,))] * 2,
        input_output_aliases=side.aliases,
    )(*side.inputs)


_DIMS = {"nn": ((1,), (0,)), "nt": ((1,), (1,)), "tn": ((0,), (0,))}


def _matmul(a, b, form, out_dtype, name, add=None, add_scale=1.0, tm=None, tn=None, tk=None, side=None,
            m_used=None, n_used=None):
    if form == "nn":
        (m, k), (_, n) = a.shape, b.shape
    elif form == "nt":
        (m, k), (n, _) = a.shape, b.shape
    else:
        (k, m), (_, n) = a.shape, b.shape
    m, n = m_used or m, n_used or n
    tm = tm or _tile(m, (1024, 512, 256, 128))
    tn = tn or _tile(n, (1024, 512, 256, 128))
    tk = tk or _tile(k, (2432, 2048, 1792, 1024, 512, 256, 128))
    nk = k // tk
    if form == "tn":
        a_spec = pl.BlockSpec((tk, tm), lambda i, j, kk: (kk, i))
    else:
        a_spec = pl.BlockSpec((tm, tk), lambda i, j, kk: (i, kk))
    if form == "nt":
        b_spec = pl.BlockSpec((tn, tk), lambda i, j, kk: (j, kk))
    else:
        b_spec = pl.BlockSpec((tk, tn), lambda i, j, kk: (kk, j))
    o_spec = pl.BlockSpec((tm, tn), lambda i, j, kk: (i, j))
    dims = (_DIMS[form], ((), ()))
    has_add = add is not None

    def finish(r, add_ref, o_ref):
        if has_add:
            r = r + add_scale * add_ref[...]
        o_ref[...] = r.astype(out_dtype)

    def body_one_pass(a_ref, b_ref, *rest):
        add_ref, o_ref = rest if has_add else (None,) + rest
        finish(lax.dot_general(a_ref[...], b_ref[...], dims, preferred_element_type=F32), add_ref, o_ref)

    def body_accumulate(a_ref, b_ref, *rest):
        add_ref, o_ref, acc = rest if has_add else (None,) + rest
        kk = pl.program_id(2)
        part = lax.dot_general(a_ref[...], b_ref[...], dims, preferred_element_type=F32)

        @pl.when(kk == 0)
        def _():
            acc[...] = part

        @pl.when(jnp.logical_and(kk > 0, kk < nk - 1))
        def _():
            acc[...] += part

        @pl.when(kk == nk - 1)
        def _():
            finish(acc[...] + part, add_ref, o_ref)

    in_specs = [a_spec, b_spec] + ([o_spec] if has_add else [])
    args = (a, b) + ((add,) if has_add else ())
    return _pcall(
        body_one_pass if nk == 1 else body_accumulate, name=name, grid=(m // tm, n // tn, nk),
        in_specs=in_specs, out_specs=o_spec,
        out_shape=jax.ShapeDtypeStruct((m, n), out_dtype),
        scratch_shapes=[] if nk == 1 else [pltpu.VMEM((tm, tn), F32)],
        semantics=("parallel", "parallel", "arbitrary"), args=args, side=side)


def _tri(n, upper):
    r = lax.broadcasted_iota(jnp.int32, (n, n), 0)
    c = lax.broadcasted_iota(jnp.int32, (n, n), 1)
    return jnp.where((r <= c) if upper else (r >= c), 1.0, 0.0).astype(F32)


def _gate_fwd(fg_t, b_col):
    h, s = fg_t.shape
    nb = s // 128

    def body(fg_ref, b_ref, cum_ref):
        u = _tri(128, True)
        carry = jnp.zeros((h, 128), F32)
        for j in range(nb):
            z = fg_ref[:, j * 128:(j + 1) * 128] + b_ref[...]
            logf = -(jnp.maximum(-z, 0.0) + jnp.log(1.0 + jnp.exp(-jnp.abs(z))))
            c = jnp.dot(logf, u, precision=lax.Precision.HIGHEST, preferred_element_type=F32) + carry
            cum_ref[:, j * 128:(j + 1) * 128] = c
            carry = jnp.broadcast_to(c[:, 127:128], (h, 128))

    return pl.pallas_call(
        body, name="gate_fwd", out_shape=jax.ShapeDtypeStruct((h, s), F32),
        in_specs=[pl.BlockSpec(memory_space=pltpu.VMEM)] * 2,
        out_specs=pl.BlockSpec(memory_space=pltpu.VMEM),
    )(fg_t, b_col)


def _gate_bwd(drow_t, dcol_t, fg_t, b_col):
    h, s = fg_t.shape
    nb = s // 128

    def body(dr_ref, dc_ref, fg_ref, b_ref, dfg_ref, db_ref):
        low = _tri(128, False)
        carry = jnp.zeros((h, 128), F32)
        db = jnp.zeros((h, 128), F32)
        for j in reversed(range(nb)):
            sl = slice(j * 128, (j + 1) * 128)
            r = jnp.dot(dr_ref[:, 0, sl] - dc_ref[:, 0, sl], low, precision=lax.Precision.HIGHEST,
                        preferred_element_type=F32) + carry
            carry = jnp.broadcast_to(r[:, 0:1], (h, 128))
            z = fg_ref[:, sl] + b_ref[...]
            dfg = r / (1.0 + jnp.exp(z))
            dfg_ref[:, sl] = dfg
            db = db + dfg
        db_ref[...] = jnp.broadcast_to(jnp.sum(db, axis=1, keepdims=True), (h, 128))

    return pl.pallas_call(
        body, name="gate_bwd",
        out_shape=(jax.ShapeDtypeStruct((h, s), F32), jax.ShapeDtypeStruct((h, 128), F32)),
        in_specs=[pl.BlockSpec(memory_space=pltpu.VMEM)] * 4,
        out_specs=(pl.BlockSpec(memory_space=pltpu.VMEM),) * 2,
    )(drow_t, dcol_t, fg_t, b_col)


LOG2E = 1.4426950408889634


def _attn_fwd(p_main, cum4, n_heads, t, side=None):
    s = p_main.shape[0]
    nq = s // t
    scale2 = HEAD_DIM ** -0.5 * LOG2E
    rep = t // 128

    def body(q_ref, k_ref, v_ref, c_ref, o_ref, lse_ref, m_s, l_s, acc_s, s_a, s_b):
        qi = pl.program_id(1)
        m_s[...] = jnp.full_like(m_s, -jnp.inf)
        l_s[...] = jnp.zeros_like(l_s)
        acc_s[...] = jnp.zeros_like(acc_s)

        def keys(ki):
            return pl.ds(pl.multiple_of(ki * t, t), t)

        def logits(ki, buf):
            buf[...] = lax.dot_general(q_ref[...], k_ref[keys(ki), :], (_DIMS["nt"], ((), ())),
                                       preferred_element_type=F32)

        def consume(ki, buf, masked):
            sc = buf[...] * scale2 - c_ref[ki] * LOG2E
            if masked:
                row = lax.broadcasted_iota(jnp.int32, (t, t), 0)
                col = lax.broadcasted_iota(jnp.int32, (t, t), 1)
                sc = jnp.where(col <= row, sc, -jnp.inf)
            m_prev = m_s[...]
            m_new = jnp.maximum(m_prev, jnp.max(sc, axis=1, keepdims=True))
            alpha = jnp.exp2(m_prev - m_new)
            p = jnp.exp2(sc - jnp.tile(m_new, (1, rep)))
            l_s[...] = alpha * l_s[...] + jnp.sum(p, axis=1, keepdims=True)
            acc_s[...] = alpha * acc_s[...] + jnp.dot(p.astype(BF16), v_ref[keys(ki), :], preferred_element_type=F32)
            m_s[...] = m_new

        def pair(j, carry):
            logits(2 * j + 1, s_b)
            consume(2 * j, s_a, False)
            logits(2 * j + 2, s_a)
            consume(2 * j + 1, s_b, False)
            return carry

        logits(0, s_a)
        lax.fori_loop(0, qi // 2, pair, 0)

        @pl.when(qi % 2 == 0)
        def _():
            consume(qi, s_a, True)

        @pl.when(qi % 2 == 1)
        def _():
            logits(qi, s_b)
            consume(qi - 1, s_a, False)
            consume(qi, s_b, True)

        o_ref[...] = (acc_s[...] / l_s[...]).astype(BF16)
        lse_ref[...] = m_s[...] + jnp.log(l_s[...]) * LOG2E

    head = lambda off: pl.BlockSpec((s, 128), lambda h, qi: (0, off + h))
    return _pcall(
        body, name="attn_fwd", grid=(n_heads, nq),
        in_specs=[pl.BlockSpec((t, 128), lambda h, qi: (qi, h)), head(n_heads), head(2 * n_heads),
                  pl.BlockSpec((None, nq, 1, t), lambda h, qi: (h, 0, 0, 0))],
        out_specs=(pl.BlockSpec((t, 128), lambda h, qi: (qi, h)),) * 2,
        out_shape=(jax.ShapeDtypeStruct((s, n_heads * 128), BF16), jax.ShapeDtypeStruct((s, n_heads * 128), F32)),
        scratch_shapes=[pltpu.VMEM((t, 128), F32)] * 3 + [pltpu.VMEM((t, t), F32)] * 2,
        semantics=("parallel", "arbitrary"), args=(p_main, p_main, p_main, cum4), side=side)


def _attn_bwd(p_main, do, lse_rep, delta_rep, cum4, n_heads, t, side=None):
    s = p_main.shape[0]
    nq = s // t
    scale = HEAD_DIM ** -0.5
    rep = t // 128

    def body(q_ref, k_ref, v_ref, do_ref, lse_ref, dl_ref, c_ref,
             dq_ref, dk_ref, dv_ref, drow_ref, dcol_ref, dq_acc, dk_acc, dv_acc, s_a, dp_a, s_b, dp_b):
        kb = pl.program_id(1)

        @pl.when(kb == 0)
        def _():
            dq_acc[...] = jnp.zeros_like(dq_acc)

        dk_acc[...] = jnp.zeros_like(dk_acc)
        dv_acc[...] = jnp.zeros_like(dv_acc)
        k = k_ref[...]
        v = v_ref[...]
        c2 = c_ref[...] * LOG2E
        k_ones = jnp.concatenate([k, jnp.ones((t, 128), BF16)], axis=1)
        q_ones = jnp.ones((t, 128), BF16)

        def rows_of(qb):
            return pl.ds(pl.multiple_of(qb * t, t), t)

        def inputs(qb, s_buf, dp_buf):
            s_buf[...] = lax.dot_general(q_ref[rows_of(qb), :], k, (_DIMS["nt"], ((), ())), preferred_element_type=F32)
            dp_buf[...] = lax.dot_general(do_ref[rows_of(qb), :], v, (_DIMS["nt"], ((), ())), preferred_element_type=F32)

        def consume(qb, s_buf, dp_buf, masked):
            rows = rows_of(qb)
            sc = s_buf[...] * (scale * LOG2E) - c2
            if masked:
                row = lax.broadcasted_iota(jnp.int32, (t, t), 0)
                col = lax.broadcasted_iota(jnp.int32, (t, t), 1)
                sc = jnp.where(col <= row, sc, -jnp.inf)
            p = jnp.exp2(sc - jnp.tile(lse_ref[rows, :], (1, rep)))
            dsb = (p * (dp_buf[...] - jnp.tile(dl_ref[rows, :], (1, rep)))).astype(BF16)
            dv_acc[...] += lax.dot_general(p.astype(BF16), do_ref[rows, :], (_DIMS["tn"], ((), ())),
                                           preferred_element_type=F32)
            dk_acc[...] += lax.dot_general(dsb, jnp.concatenate([q_ref[rows, :], q_ones], axis=1),
                                           (_DIMS["tn"], ((), ())), preferred_element_type=F32)
            dq_acc[rows, :] += jnp.dot(dsb, k_ones, preferred_element_type=F32)

        n_rest = nq - 1 - kb
        inputs(kb, s_a, dp_a)

        @pl.when(n_rest > 0)
        def _():
            inputs(kb + 1, s_b, dp_b)

        consume(kb, s_a, dp_a, True)
        diag = rows_of(kb)
        dq_ref[diag, :] = (dq_acc[diag, 0:128] * scale).astype(BF16)
        drow_ref[:, diag] = dq_acc[diag, 128:256].T[0:8, :]

        def pair(j, carry):
            qb = kb + 1 + 2 * j
            inputs(qb + 1, s_a, dp_a)
            consume(qb, s_b, dp_b, False)
            inputs(qb + 2, s_b, dp_b)
            consume(qb + 1, s_a, dp_a, False)
            return carry

        n_pairs = jnp.maximum(n_rest - 1, 0) // 2
        lax.fori_loop(0, n_pairs, pair, 0)
        left = kb + 1 + 2 * n_pairs

        @pl.when(n_rest - 2 * n_pairs == 1)
        def _():
            consume(left, s_b, dp_b, False)

        @pl.when(n_rest - 2 * n_pairs == 2)
        def _():
            inputs(left + 1, s_a, dp_a)
            consume(left, s_b, dp_b, False)
            consume(left + 1, s_a, dp_a, False)

        dk_ref[...] = (dk_acc[:, 0:128] * scale).astype(BF16)
        dcol_ref[...] = dk_acc[:, 128:256].T[0:8, :]
        dv_ref[...] = dv_acc[...].astype(BF16)

    kside = lambda off: pl.BlockSpec((t, 128), lambda h, kb: (kb, off + h))
    whole = pl.BlockSpec((s, 128), lambda h, kb: (0, h))
    hw = n_heads * 128
    return _pcall(
        body, name="attn_bwd", grid=(n_heads, nq),
        in_specs=[whole, kside(n_heads), kside(2 * n_heads), whole, whole, whole,
                  pl.BlockSpec((None, None, 1, t), lambda h, kb: (h, kb, 0, 0))],
        out_specs=(whole, kside(0), kside(0), pl.BlockSpec((None, 8, s), lambda h, kb: (h, 0, 0)),
                   pl.BlockSpec((None, 8, t), lambda h, kb: (h, 0, kb))),
        out_shape=(jax.ShapeDtypeStruct((s, hw), BF16), jax.ShapeDtypeStruct((s, hw), BF16),
                   jax.ShapeDtypeStruct((s, hw), BF16), jax.ShapeDtypeStruct((n_heads, 8, s), F32),
                   jax.ShapeDtypeStruct((n_heads, 8, s), F32)),
        scratch_shapes=[pltpu.VMEM((s, 256), F32), pltpu.VMEM((t, 256), F32), pltpu.VMEM((t, 128), F32)]
        + [pltpu.VMEM((t, t), F32)] * 4,
        semantics=("parallel", "arbitrary"), args=(p_main, p_main, p_main, do, lse_rep, delta_rep, cum4), side=side)


def _conv_fwd(u_ext, cw_ref):
    r1 = pltpu.roll(u_ext, 1, 0)
    r2 = pltpu.roll(u_ext, 2, 0)
    conv = cw_ref[2:3, :] * u_ext + cw_ref[1:2, :] * r1 + cw_ref[0:1, :] * r2
    return conv[HALO:], r1[HALO:], r2[HALO:]


def _pool_z(pu, pu_halo, row0, tm, pg):
    ext = jnp.concatenate([pu_halo, pu], axis=0)
    t1 = (row0 + lax.broadcasted_iota(jnp.int32, (tm, pg), 0) + 1).astype(F32)
    zs = []
    for g, w in enumerate(POOL_WINDOWS):
        sm = ext[:, g * pg:(g + 1) * pg]
        sh = 1
        while sh < w:
            sm = sm + pltpu.roll(sm, sh, 0)
            sh *= 2
        mean = sm[HALO:] / jnp.minimum(t1, float(w))
        zs.append(mean - pu[:, g * pg:(g + 1) * pg])
    return zs


def _mix_specs(tm, s):
    per = tm // HALO
    last = s // HALO - 1
    cur = lambda w, j: pl.BlockSpec((tm, w), lambda i: (i, j))
    prev = lambda j: pl.BlockSpec((HALO, 512), lambda i: (jnp.maximum(i * per - 1, 0), j))
    nxt = lambda j: pl.BlockSpec((HALO, 512), lambda i: (jnp.minimum((i + 1) * per, last), j))
    full = lambda shape: pl.BlockSpec(shape, lambda i: (0,) * len(shape))
    return cur, prev, nxt, full


def _mix_fwd(p_main, o, conv_w, pool_w, pool_scale, tm):
    s = p_main.shape[0]
    d = 2048
    pg = 128
    cur, prev, nxt, full = _mix_specs(tm, s)

    def body(ga_ref, cb_ref, cc_ref, ch_ref, gc_ref, pu_ref, gp_ref, cch_ref, chh_ref, puh_ref,
             o_ref, cw_ref, pw_ref, ps_ref, y_ref):
        i = pl.program_id(0)
        first = i == 0
        ga = ga_ref[...].astype(F32)
        y_ref[:, 0:1024] = (o_ref[...].astype(F32) * ga * _sigmoid(ga)).astype(BF16)

        u = cc_ref[...].astype(F32) * ch_ref[...].astype(F32)
        uh = jnp.where(first, 0.0, cch_ref[...].astype(F32) * chh_ref[...].astype(F32))
        conv, _, _ = _conv_fwd(jnp.concatenate([uh, u], axis=0), cw_ref)
        gc = gc_ref[...].astype(F32)
        y_ref[:, 1024:1536] = (cb_ref[...].astype(F32) * conv * gc * _sigmoid(gc)).astype(BF16)

        pu = pu_ref[...].astype(F32)
        puh = jnp.where(first, 0.0, puh_ref[...].astype(F32))
        zs = _pool_z(pu, puh, i * tm, tm, pg)
        gp = gp_ref[...].astype(F32)
        gate = gp * _sigmoid(gp) * ps_ref[...]
        for g in range(4):
            r = jnp.dot(zs[g].astype(BF16), pw_ref[g], preferred_element_type=F32)
            y_ref[:, 1536 + g * pg:1536 + (g + 1) * pg] = (r * gate[:, g * pg:(g + 1) * pg]).astype(BF16)

    return pl.pallas_call(
        body, name="mix_fwd", grid=(s // tm,),
        in_specs=[cur(1024, 3), cur(512, 8), cur(512, 9), cur(512, 10), cur(512, 11), cur(512, 12), cur(512, 13),
                  prev(9), prev(10), prev(12),
                  cur(1024, 0), full((3, 512)), full((4, pg, pg)), full((1, 512))],
        out_specs=pl.BlockSpec((tm, d), lambda i: (i, 0)),
        out_shape=jax.ShapeDtypeStruct((s, d), BF16),
        compiler_params=_params(("parallel",)),
    )(*([p_main] * 10), o, conv_w, pool_w, pool_scale)


def _mix_bwd(p_main, o, dy, conv_w, pool_w, pool_scale, tm):
    s = p_main.shape[0]
    pg = 128
    n_heads = 8
    cur, prev, nxt, full = _mix_specs(tm, s)
    nblk = s // tm
    n_ext = tm + HALO

    def silu_and_grad(x):
        sg = _sigmoid(x)
        return x * sg, sg * (1.0 + x * (1.0 - sg))

    def body(ga_ref, cb_ref, cc_ref, ch_ref, gc_ref, pu_ref, gp_ref, cch_ref, chh_ref, puh_ref,
             cbn_ref, gcn_ref, gpn_ref, o_ref, dy_ref, dycn_ref, dypn_ref, cw_ref, pw_ref, ps_ref,
             do_ref, dl_ref, dp_ref, dsm_ref, dpw_ref):
        i = pl.program_id(0)
        first = i == 0
        last = i == nblk - 1

        @pl.when(first)
        def _():
            dsm_ref[...] = jnp.zeros_like(dsm_ref)
            dpw_ref[...] = jnp.zeros_like(dpw_ref)

        ga = ga_ref[...].astype(F32)
        of = o_ref[...].astype(F32)
        dya = dy_ref[:, 0:1024]
        sa, dsa = silu_and_grad(ga)
        dout = dya * sa
        do_ref[...] = dout.astype(BF16)
        dp_ref[:, 0:1024] = (dya * of * dsa).astype(BF16)
        prod = dout * of
        for h in range(n_heads):
            dsum = jnp.sum(prod[:, h * 128:(h + 1) * 128], axis=1, keepdims=True)
            dl_ref[:, h * 128:(h + 1) * 128] = jnp.broadcast_to(dsum, (tm, 128))

        cb = cb_ref[...].astype(F32)
        cc = cc_ref[...].astype(F32)
        ch = ch_ref[...].astype(F32)
        gc = gc_ref[...].astype(F32)
        u = cc * ch
        uh = jnp.where(first, 0.0, cch_ref[...].astype(F32) * chh_ref[...].astype(F32))
        conv, u1, u2 = _conv_fwd(jnp.concatenate([uh, u], axis=0), cw_ref)
        sc_, dsc = silu_and_grad(gc)
        dyc = dy_ref[:, 1024:1536]
        dp_ref[:, 1024:1536] = (dyc * conv * sc_).astype(BF16)
        dp_ref[:, 2560:3072] = (dyc * cb * conv * dsc).astype(BF16)
        dconv = dyc * cb * sc_
        gcn = gcn_ref[...].astype(F32)
        dconv_n = jnp.where(last, 0.0, dycn_ref[...] * cbn_ref[...].astype(F32) * gcn * _sigmoid(gcn))
        dext = jnp.concatenate([dconv, dconv_n], axis=0)
        du = (cw_ref[2:3, :] * dext + cw_ref[1:2, :] * pltpu.roll(dext, n_ext - 1, 0)
              + cw_ref[0:1, :] * pltpu.roll(dext, n_ext - 2, 0))[:tm]
        dp_ref[:, 1536:2048] = (du * ch).astype(BF16)
        dp_ref[:, 2048:2560] = (du * cc).astype(BF16)
        dsm_ref[0:1, :] += jnp.sum(dconv * u2, axis=0, keepdims=True)
        dsm_ref[1:2, :] += jnp.sum(dconv * u1, axis=0, keepdims=True)
        dsm_ref[2:3, :] += jnp.sum(dconv * u, axis=0, keepdims=True)

        pu = pu_ref[...].astype(F32)
        puh = jnp.where(first, 0.0, puh_ref[...].astype(F32))
        zs = _pool_z(pu, puh, i * tm, tm, pg)
        gp = gp_ref[...].astype(F32)
        sp, dsp = silu_and_grad(gp)
        dyp = dy_ref[:, 1536:2048]
        gpn = gpn_ref[...].astype(F32)
        dr_n = jnp.where(last, 0.0, dypn_ref[...] * gpn * _sigmoid(gpn) * ps_ref[...])
        drs = dyp * sp
        dr = drs * ps_ref[...]
        t1 = (i * tm + lax.broadcasted_iota(jnp.int32, (tm, pg), 0) + 1).astype(F32)
        r_parts, dpu_parts = [], []
        for g, w in enumerate(POOL_WINDOWS):
            cols = slice(g * pg, (g + 1) * pg)
            zb = zs[g].astype(BF16)
            r_parts.append(jnp.dot(zb, pw_ref[g], preferred_element_type=F32))
            drb = dr[:, cols].astype(BF16)
            dpw_ref[g] += lax.dot_general(zb, drb, (_DIMS["tn"], ((), ())), preferred_element_type=F32)
            dz = lax.dot_general(drb, pw_ref[g], (_DIMS["nt"], ((), ())), preferred_element_type=F32)
            dz_n = lax.dot_general(dr_n[:, cols].astype(BF16), pw_ref[g], (_DIMS["nt"], ((), ())),
                                   preferred_element_type=F32)
            sm = jnp.concatenate([dz / jnp.minimum(t1, float(w)), dz_n / float(w)], axis=0)
            sh = 1
            while sh < w:
                sm = sm + pltpu.roll(sm, n_ext - sh, 0)
                sh *= 2
            dpu_parts.append(sm[:tm] - dz)
        r = jnp.concatenate(r_parts, axis=1)
        dp_ref[:, 3072:3584] = jnp.concatenate(dpu_parts, axis=1).astype(BF16)
        dp_ref[:, 3584:4096] = (dyp * r * ps_ref[...] * dsp).astype(BF16)
        dsm_ref[3:4, :] += jnp.sum(drs * r, axis=0, keepdims=True)

    dy_next = lambda j: pl.BlockSpec((HALO, 512), lambda i: (jnp.minimum((i + 1) * (tm // HALO), s // HALO - 1), j))
    return pl.pallas_call(
        body, name="mix_bwd", grid=(nblk,),
        in_specs=[cur(1024, 3), cur(512, 8), cur(512, 9), cur(512, 10), cur(512, 11), cur(512, 12), cur(512, 13),
                  prev(9), prev(10), prev(12), nxt(8), nxt(11), nxt(13),
                  cur(1024, 0), pl.BlockSpec((tm, 2048), lambda i: (i, 0)), dy_next(2), dy_next(3),
                  full((3, 512)), full((4, pg, pg)), full((1, 512))],
        out_specs=(pl.BlockSpec((tm, 1024), lambda i: (i, 0)), pl.BlockSpec((tm, 1024), lambda i: (i, 0)),
                   pl.BlockSpec((tm, 4096), lambda i: (i, 0)), full((8, 512)), full((4, pg, pg))),
        out_shape=(jax.ShapeDtypeStruct((s, 1024), BF16), jax.ShapeDtypeStruct((s, 1024), F32),
                   jax.ShapeDtypeStruct((s, 4096), BF16), jax.ShapeDtypeStruct((8, 512), F32),
                   jax.ShapeDtypeStruct((4, pg, pg), F32)),
        compiler_params=_params(("arbitrary",)),
    )(*([p_main] * 13), o, dy, dy, dy, conv_w, pool_w, pool_scale)


def _outproj_ln(y, w_out, x, ln_g, ln_b, alpha, tm, side=None):
    s, d = x.shape

    def body(y_ref, w_ref, x_ref, g_ref, b_ref, xn_ref, xb_ref, pre_ref):
        pre = alpha * x_ref[...] + jnp.dot(y_ref[...], w_ref[...], preferred_element_type=F32)
        mu = jnp.mean(pre, axis=1, keepdims=True)
        cen = pre - mu
        var = jnp.mean(cen * cen, axis=1, keepdims=True)
        xn = cen * lax.rsqrt(var + LN_EPS) * g_ref[...] + b_ref[...]
        pre_ref[...] = pre
        xn_ref[...] = xn
        xb_ref[...] = xn.astype(BF16)

    row = pl.BlockSpec((tm, d), lambda i: (i, 0))
    vec = pl.BlockSpec((1, d), lambda i: (0, 0))
    return _pcall(
        body, name="outproj_ln", grid=(s // tm,),
        in_specs=[row, pl.BlockSpec((d, d), lambda i: (0, 0)), row, vec, vec],
        out_specs=(row, row, row),
        out_shape=(jax.ShapeDtypeStruct((s, d), F32), jax.ShapeDtypeStruct((s, d), BF16),
                   jax.ShapeDtypeStruct((s, d), F32)),
        scratch_shapes=[], semantics=("parallel",), args=(y, w_out, x, ln_g, ln_b), side=side)


def _ln_bwd(dxn, pre, ln_g, tm, side=None):
    s, d = pre.shape

    def body(dx_ref, pre_ref, g_ref, dpre_ref, dpb_ref, dgb_ref):
        @pl.when(pl.program_id(0) == 0)
        def _():
            dgb_ref[...] = jnp.zeros_like(dgb_ref)

        pre_ = pre_ref[...]
        dx = dx_ref[...]
        mu = jnp.mean(pre_, axis=1, keepdims=True)
        cen = pre_ - mu
        var = jnp.mean(cen * cen, axis=1, keepdims=True)
        rstd = lax.rsqrt(var + LN_EPS)
        xhat = cen * rstd
        dxh = dx * g_ref[...]
        dpre = rstd * (dxh - jnp.mean(dxh, axis=1, keepdims=True)
                       - xhat * jnp.mean(dxh * xhat, axis=1, keepdims=True))
        dpre_ref[...] = dpre
        dpb_ref[...] = dpre.astype(BF16)
        dgb_ref[0:1, :] += jnp.sum(dx * xhat, axis=0, keepdims=True)
        dgb_ref[1:2, :] += jnp.sum(dx, axis=0, keepdims=True)

    row = pl.BlockSpec((tm, d), lambda i: (i, 0))
    return _pcall(
        body, name="ln_bwd", grid=(s // tm,),
        in_specs=[row, row, pl.BlockSpec((1, d), lambda i: (0, 0))],
        out_specs=(row, row, pl.BlockSpec((8, d), lambda i: (0, 0))),
        out_shape=(jax.ShapeDtypeStruct((s, d), F32), jax.ShapeDtypeStruct((s, d), BF16),
                   jax.ShapeDtypeStruct((8, d), F32)),
        scratch_shapes=[], semantics=("arbitrary",), args=(dxn, pre, ln_g), side=side)


def _loss_grad(y, target, tm):
    s, d = y.shape

    def body(y_ref, t_ref, dy_ref, loss_ref):
        @pl.when(pl.program_id(0) == 0)
        def _():
            loss_ref[...] = jnp.zeros_like(loss_ref)

        diff = y_ref[...] - t_ref[...]
        dy_ref[...] = diff / d
        loss_ref[...] += 0.5 * jnp.sum(jnp.sum(diff * diff, axis=1, keepdims=True) / d)

    row = pl.BlockSpec((tm, d), lambda i: (i, 0))
    return pl.pallas_call(
        body, name="loss_grad", grid=(s // tm,),
        in_specs=[row, row],
        out_specs=(row, pl.BlockSpec((8, 128), lambda i: (0, 0))),
        out_shape=(jax.ShapeDtypeStruct((s, d), F32), jax.ShapeDtypeStruct((8, 128), F32)),
        compiler_params=_params(("arbitrary",)),
    )(y, target)


def _adamw(w, g, m, v, name, lead_block=None, emit_grad=False):
    a, r, c = w.shape
    tr = r if lead_block else _tile(r, (256, 128, 64, 32, 16, 8))
    c1 = 1.0 - ADAM_B1 ** ADAM_STEP
    c2 = 1.0 - ADAM_B2 ** ADAM_STEP

    def body(w_ref, g_ref, m_ref, v_ref, d_ref, mo_ref, vo_ref, *go_ref):
        g_ = g_ref[...]
        if emit_grad:
            go_ref[0][...] = g_
        m_new = ADAM_B1 * m_ref[...] + (1.0 - ADAM_B1) * g_
        v_new = ADAM_B2 * v_ref[...] + (1.0 - ADAM_B2) * (g_ * g_)
        d_ref[...] = -ADAM_LR * ((m_new / c1) / (jnp.sqrt(v_new / c2) + ADAM_EPS) + ADAM_WD * w_ref[...])
        mo_ref[...] = m_new
        vo_ref[...] = v_new

    ta = lead_block or 1
    spec = pl.BlockSpec((ta, tr, c), lambda i, j: (i, j, 0))
    shp = jax.ShapeDtypeStruct(w.shape, F32)
    n_out = 4 if emit_grad else 3
    return pl.pallas_call(
        body, name=name, grid=(a // ta, r // tr), in_specs=[spec] * 4, out_specs=(spec,) * n_out,
        out_shape=(shp,) * n_out, compiler_params=_params(("parallel", "parallel")),
    )(w, g, m, v)


def _place():
    x, y, c = lax.axis_index("x"), lax.axis_index("y"), lax.axis_index("c")
    return x, y, c, [(1 - x, y), (x, 1 - y), (1 - x, 1 - y)]


def _remote(src, dst, send, recv, k, to):
    return pltpu.make_async_remote_copy(src_ref=src, dst_ref=dst, send_sem=send.at[k], recv_sem=recv.at[k],
                                        device_id=to, device_id_type=MESH)


def _slot(ref, px, py, pc):
    return ref.at[pl.ds(2 * px + py, 1), pl.ds(pc, 1)]


def _side_gather_ici(shard, rows=None, into=None):
    _, two, r, w = shard.shape
    lo, hi = rows or (0, r)

    def _slot(ref, px, py, pc):
        return ref.at[pl.ds(2 * px + py, 1), pl.ds(pc, 1), pl.ds(lo, hi - lo)]

    def sends(sins, souts, send, recv):
        x, y, c, chips = _place()
        return [_remote(sins[0].at[:, pl.ds(c, 1), pl.ds(lo, hi - lo)], _slot(souts[0], x, y, c), send, recv, k,
                        (*chip, c)) for k, chip in enumerate(chips)]

    def start(sins, souts, send, recv):
        for cp in sends(sins, souts, send, recv):
            cp.start()

    def wait(sins, souts, send, recv):
        x, y, c, chips = _place()
        for cp in sends(sins, souts, send, recv):
            cp.wait_send()
        for k, chip in enumerate(chips):
            _remote(_slot(souts[0], *chip, c), _slot(souts[0], *chip, c), send, recv, k, (x, y, c)).wait_recv()

    more = [] if into is None else [into]
    return _Side([shard] + more, [jax.ShapeDtypeStruct((4, two, r, w), shard.dtype)], 3, start, wait,
                 aliases={} if into is None else {1: 0})


def _side_gather_d2d(gathered):
    def sends(souts, send, recv):
        x, y, c, chips = _place()
        return [_remote(_slot(souts[0], *chip, c), _slot(souts[0], *chip, c), send, recv, k, (x, y, 1 - c))
                for k, chip in enumerate(chips)]

    def start(sins, souts, send, recv):
        for cp in sends(souts, send, recv):
            cp.start()

    def wait(sins, souts, send, recv):
        x, y, c, chips = _place()
        for cp in sends(souts, send, recv):
            cp.wait_send()
        for k, chip in enumerate(chips):
            _remote(_slot(souts[0], *chip, 1 - c), _slot(souts[0], *chip, 1 - c), send, recv, k, (x, y, c)).wait_recv()

    return _Side([gathered], [jax.ShapeDtypeStruct(gathered.shape, gathered.dtype)], 3, start, wait, aliases={0: 0})


def _side_swap(g):
    nchip, _, r, w = g.shape

    def copy(sins, souts, send, recv):
        x, y, c, _ = _place()
        return _remote(sins[0].at[:, pl.ds(1 - c, 1)], souts[0], send, recv, 0, (x, y, 1 - c))

    return _Side([g], [jax.ShapeDtypeStruct((nchip, 1, r, w), g.dtype)], 1,
                 lambda *a: copy(*a).start(), lambda *a: copy(*a).wait())


def _side_exchange(hsum):
    _, _, r, w = hsum.shape

    def copies(sins, souts, send, recv):
        x, y, c, chips = _place()
        return [_remote(sins[0].at[pl.ds(2 * px + py, 1)], souts[0].at[pl.ds(k, 1)], send, recv, k, (px, py, c))
                for k, (px, py) in enumerate(chips)]

    def start(*a):
        for cp in copies(*a):
            cp.start()

    def wait(*a):
        for cp in copies(*a):
            cp.wait()

    return _Side([hsum], [jax.ShapeDtypeStruct((3, 1, r, w), hsum.dtype)], 3, start, wait)


def _side_share(both):
    def copy(sins, souts, send, recv):
        x, y, c, _ = _place()
        return _remote(souts[0].at[pl.ds(c, 1)], souts[0].at[pl.ds(c, 1)], send, recv, 0, (x, y, 1 - c))

    def wait(sins, souts, send, recv):
        x, y, c, _ = _place()
        copy(sins, souts, send, recv).wait_send()
        _remote(souts[0].at[pl.ds(1 - c, 1)], souts[0].at[pl.ds(1 - c, 1)], send, recv, 0, (x, y, c)).wait_recv()

    return _Side([both], [jax.ShapeDtypeStruct(both.shape, both.dtype)], 1,
                 lambda *a: copy(*a).start(), wait, aliases={0: 0})


def _own_slot_filled(gathered, shard):
    chip = 2 * lax.axis_index("x") + lax.axis_index("y")
    return lax.dynamic_update_slice(gathered, shard, (chip, 0, 0, 0))


def _plane_tile(r, cdim):
    tr = _tile(r, (256, 128))
    if tr != r:
        return tr, cdim
    return r, _tile(cdim, (512, 256, 128))


def _add_halves(g, recv, core, name):
    nchip, nl, r, cdim = g.shape
    half = nl // 2
    tr, tc = _plane_tile(r, cdim)

    def body(c_ref, g_ref, r_ref, o_ref):
        o_ref[...] = (g_ref[...].astype(F32) + r_ref[...].astype(F32)).astype(BF16)

    blk = (1, 1, tr, tc)
    return pl.pallas_call(
        body, name=name,
        grid_spec=pltpu.PrefetchScalarGridSpec(
            num_scalar_prefetch=1, grid=(nchip, half, r // tr, cdim // tc),
            in_specs=[pl.BlockSpec(blk, lambda j, l, i, k, c_ref: (j, c_ref[0] * half + l, i, k)),
                      pl.BlockSpec(blk, lambda j, l, i, k, c_ref: (j, l, i, k))],
            out_specs=pl.BlockSpec(blk, lambda j, l, i, k, c_ref: (j, l, i, k))),
        out_shape=jax.ShapeDtypeStruct((nchip, half, r, cdim), BF16),
        compiler_params=_params(("parallel",) * 4),
    )(core, g, recv)


def _add_partials(hsum, others, chip, core, name):
    nchip, _, r, cdim = hsum.shape
    tr, tc = _plane_tile(r, cdim)

    def body(i_ref, c_ref, h_ref, a_ref, b_ref, d_ref, o_ref):
        o_ref[...] = ((h_ref[...].astype(F32) + a_ref[...].astype(F32))
                      + (b_ref[...].astype(F32) + d_ref[...].astype(F32)))[0]

    blk = (1, 1, tr, tc)
    other = lambda n: pl.BlockSpec(blk, lambda i, k, i_ref, c_ref: (n, 0, i, k))
    return pl.pallas_call(
        body, name=name,
        grid_spec=pltpu.PrefetchScalarGridSpec(
            num_scalar_prefetch=2, grid=(r // tr, cdim // tc),
            in_specs=[pl.BlockSpec(blk, lambda i, k, i_ref, c_ref: (i_ref[0], 0, i, k)), other(0), other(1), other(2)],
            out_specs=pl.BlockSpec((1, tr, tc), lambda i, k, i_ref, c_ref: (c_ref[0], i, k))),
        out_shape=jax.ShapeDtypeStruct((2, r, cdim), F32),
        compiler_params=_params(("parallel",) * 2),
    )(chip, core, hsum, others, others, others)


def _all_gather_small(v, name):
    m_per, n = v.shape

    def body(x_ref, out_ref, send_sems, recv_sems, local_sem):
        x, y, c, chips = _place()
        me, sib = (x, y, c), (x, y, 1 - c)

        def rows(px, py, pc):
            return out_ref.at[pl.ds((4 * px + 2 * py + pc) * m_per, m_per), :]

        def copy(k, block, to, src=None):
            return pltpu.make_async_remote_copy(
                src_ref=rows(*block) if src is None else src, dst_ref=rows(*block), send_sem=send_sems.at[k],
                recv_sem=recv_sems.at[k], device_id=to, device_id_type=MESH)

        mine = pltpu.make_async_copy(x_ref, rows(*me), local_sem)
        mine.start()
        first = [copy(0, me, sib, src=x_ref)]
        first += [copy(1 + j, me, (*chip, c), src=x_ref) for j, chip in enumerate(chips)]
        for cp in first:
            cp.start()
        passed = [copy(4 + j, (*chip, c), sib) for j, chip in enumerate(chips)]
        for j, chip in enumerate(chips):
            copy(1 + j, (*chip, c), me).wait_recv()
            passed[j].start()
        copy(0, sib, me).wait_recv()
        for j, chip in enumerate(chips):
            copy(4 + j, (*chip, 1 - c), me).wait_recv()
        for cp in first + passed:
            cp.wait_send()
        mine.wait()

    return pl.pallas_call(
        body, name=name, out_shape=jax.ShapeDtypeStruct((8 * m_per, n), v.dtype),
        in_specs=[pl.BlockSpec(memory_space=pltpu.VMEM)], out_specs=pl.BlockSpec(memory_space=pltpu.VMEM),
        scratch_shapes=[pltpu.SemaphoreType.DMA((7,)), pltpu.SemaphoreType.DMA((7,)), pltpu.SemaphoreType.DMA],
        compiler_params=pltpu.CompilerParams(vmem_limit_bytes=VMEM_LIMIT),
    )(v)


def _sum_blocks(v, nblk, name):
    m_per = v.shape[0] // nblk

    def body(v_ref, o_ref):
        acc = v_ref[0:m_per, :]
        for j in range(1, nblk):
            acc = acc + v_ref[j * m_per:(j + 1) * m_per, :]
        o_ref[...] = acc

    return pl.pallas_call(
        body, name=name, out_shape=jax.ShapeDtypeStruct((m_per, 128), F32),
        in_specs=[pl.BlockSpec(memory_space=pltpu.VMEM)], out_specs=pl.BlockSpec(memory_space=pltpu.VMEM),
        compiler_params=pltpu.CompilerParams(vmem_limit_bytes=VMEM_LIMIT),
    )(v)


def _rows128(a, rows=None):
    flat = a.reshape(-1).astype(F32)
    need = -(-flat.shape[0] // 128)
    rows = rows or -(-need // 8) * 8
    return jnp.pad(flat, (0, rows * 128 - flat.shape[0])).reshape(rows, 128)


def kernel(x, w_in, b_f, conv_w, pool_w, pool_scale, w_out, ln_g, ln_b, loss_target, m_w_in, m_b_f, m_conv_w, m_pool_w, m_pool_scale, m_w_out, m_ln_g, m_ln_b, v_w_in, v_b_f, v_conv_w, v_pool_w, v_pool_scale, v_w_out, v_ln_g, v_ln_b):
    depth, d, in_shard = w_in.shape
    s = x.shape[1]
    att = d // 2
    n_heads = att // HEAD_DIM
    cw = d // 4
    pw = d - att - cw
    in_w = 4 * att + n_heads + 4 * cw + 2 * pw
    assert in_shard * 4 == in_w and (d, n_heads, cw, pw) == (2048, 8, 512, 512) and s % 512 == 0
    main_w = in_w - n_heads
    fg0 = 4 * att
    alpha = (2 * depth) ** 0.25
    t_att = 512
    tm_mix = 256

    cx, cy, cc = lax.axis_index("x"), lax.axis_index("y"), lax.axis_index("c")
    chip = 2 * cx + cy
    core_arr = jnp.reshape(cc, (1,)).astype(jnp.int32)
    chip_arr = jnp.reshape(chip, (1,)).astype(jnp.int32)
    out_rows = w_out.shape[1]

    starts = [j * in_shard for j in range(4)]
    main_start = [g if g < fg0 else g - n_heads for g in starts]
    al = BF16_TILE_ROWS
    base = [m // al * al for m in main_start]
    lead = [m - b for m, b in zip(main_start, base)]
    fg_chip = max(j for j in range(4) if starts[j] <= fg0)
    hole = fg0 - starts[fg_chip]
    blk = -(-(max(lead) + in_shard) // al) * al
    assert hole + n_heads <= in_shard and all(b + blk <= main_w for b in base) and base[3] + blk == main_w
    r_comb = out_rows + blk
    pad_rows = -r_comb % (2 * al)
    r_half = (r_comb + pad_rows) // 2
    lead_dyn = jnp.take(jnp.array(lead, jnp.int32), chip)
    has_fg = chip == fg_chip

    to_rows = lambda a: jnp.transpose(a, (2, 0, 1))
    from_rows = lambda a: jnp.transpose(a, (1, 2, 0))
    wt_b = jnp.transpose(to_rows(w_in).astype(BF16), (1, 0, 2))
    padded = jnp.pad(wt_b, ((0, 0), (al, 2 * al), (0, 0)))
    plain = lax.dynamic_slice_in_dim(padded, al - lead_dyn, blk, axis=1)
    skipping = lax.dynamic_slice_in_dim(padded, al - lead_dyn + n_heads, blk, axis=1)
    pos = lax.broadcasted_iota(jnp.int32, (1, blk, 1), 1)
    block = jnp.where(jnp.logical_and(has_fg, pos >= lead_dyn + hole), skipping, plain)
    comb = jnp.concatenate([w_out.astype(BF16), block, jnp.zeros((depth, pad_rows, d), BF16)], axis=1)
    shards = comb.reshape(depth, 2, r_half, d)

    def weights_of(gathered, wt_fg):
        full = gathered.reshape(4, 2 * r_half, d)
        wt_all = sum(jnp.pad(full[j, out_rows:r_comb], ((base[j], main_w + 128 - base[j] - blk), (0, 0)))
                     for j in range(4)) + jnp.pad(wt_fg, ((main_w, 0), (0, 0)))
        return wt_all, full[:, :out_rows].reshape(d, d)

    small_w = jnp.concatenate([_rows128(conv_w), _rows128(wt_b[:, hole:hole + n_heads])], axis=0)
    n_cw = _rows128(conv_w).shape[0] * 128
    sw_all = _all_gather_small(small_w, "gather_small_weights").reshape(4, 2, small_w.shape[0] * 128)[:, 0]
    conv_full = sw_all[:, :depth * 3 * (cw // 4)].reshape(4, depth, 3, cw // 4)
    conv_full = jnp.moveaxis(conv_full, 0, 2).reshape(depth, 3, cw)
    wt_fg_all = sw_all[fg_chip, n_cw:n_cw + depth * n_heads * d].reshape(depth, n_heads, d).astype(BF16)
    wt_fg_all = jnp.pad(wt_fg_all, ((0, 0), (0, 128 - n_heads), (0, 0)))

    first = shards[0][None]
    arrived = _run_side(_side_gather_ici(first), "gather_first_ici")[0]
    arrived = _run_side(_side_gather_d2d(arrived), "gather_first_d2d")[0]
    weights = weights_of(_own_slot_filled(arrived, first), wt_fg_all[0])
    pool_b = pool_w.astype(BF16)

    split = r_half * 2 // 5 // al * al
    xf = x[0]
    xb = xf.astype(BF16)
    saved = []
    for l in range(depth):
        wt_all, w_out_l = weights
        wt_fg = wt_fg_all[l]
        nxt = shards[l + 1][None] if l + 1 < depth else None
        if nxt is None:
            p_main = _matmul(xb, wt_all, "nt", BF16, "proj_main", n_used=main_w)
        else:
            p_main, (arrived,) = _matmul(xb, wt_all, "nt", BF16, "proj_main", n_used=main_w,
                                         side=_side_gather_ici(nxt, rows=(0, split)))
        fg = _matmul(xb, wt_fg, "nt", F32, "proj_fg")
        fg_t = fg[:, :n_heads].T
        b_col = b_f[l].reshape(n_heads, 1)
        cum4 = _gate_fwd(fg_t, b_col).reshape(n_heads, s // t_att, 1, t_att)
        if nxt is None:
            o, lse = _attn_fwd(p_main, cum4, n_heads, t_att)
        else:
            (o, lse), (arrived,) = _attn_fwd(p_main, cum4, n_heads, t_att,
                                             side=_side_gather_ici(nxt, rows=(split, r_half), into=arrived))
        y = _mix_fwd(p_main, o, conv_full[l], pool_b[l], pool_scale[l].reshape(1, pw), tm_mix)
        ln_args = (y, w_out_l, xf, ln_g[l].reshape(1, d), ln_b[l].reshape(1, d), alpha, tm_mix)
        if nxt is None:
            x_new, xb_new, pre = _outproj_ln(*ln_args)
        else:
            (x_new, xb_new, pre), (arrived,) = _outproj_ln(*ln_args, side=_side_gather_d2d(arrived))
            weights = weights_of(_own_slot_filled(arrived, nxt), wt_fg_all[l + 1])
        saved.append((xb, p_main, fg_t, b_col, cum4, o, lse, y, pre, wt_all, w_out_l))
        xf, xb = x_new, xb_new

    dx, loss_blk = _loss_grad(xf, loss_target[0], tm_mix)

    g_bf, g_sm, g_pw, g_ln, g_fgw, finished = ([None] * depth for _ in range(6))
    pending = None
    for l in reversed(range(depth)):
        xb, p_main, fg_t, b_col, cum4, o, lse, y, pre, wt_all, w_out_l = saved[l]
        if pending is None:
            dpre, dpre_b, g_ln[l] = _ln_bwd(dx, pre, ln_g[l].reshape(1, d), tm_mix)
        else:
            (dpre, dpre_b, g_ln[l]), (recv,) = _ln_bwd(dx, pre, ln_g[l].reshape(1, d), tm_mix,
                                                       side=_side_swap(pending))
            hsum = _add_halves(pending, recv, core_arr, "rs_add_halves")
        dy = _matmul(dpre_b, w_out_l, "nt", F32, "d_y")
        gw_out_l = _matmul(y, dpre_b, "tn", BF16, "d_w_out")
        do, delta, dp_rest, g_sm[l], g_pw[l] = _mix_bwd(p_main, o, dy, conv_full[l], pool_b[l],
                                                        pool_scale[l].reshape(1, pw), tm_mix)
        if pending is None:
            dq, dk, dv, drow, dcol = _attn_bwd(p_main, do, lse, delta, cum4, n_heads, t_att)
        else:
            (dq, dk, dv, drow, dcol), (others,) = _attn_bwd(p_main, do, lse, delta, cum4, n_heads, t_att,
                                                            side=_side_exchange(hsum))
            mine = _add_partials(hsum, others, chip_arr, core_arr, "rs_add_partials")
        dfg_t, g_bf[l] = _gate_bwd(drow, dcol, fg_t, b_col)
        dfg_b = jnp.pad(dfg_t.T, ((0, 0), (0, 128 - n_heads))).astype(BF16)
        dp_all = jnp.concatenate([dq, dk, dv, dp_rest, dfg_b], axis=1)
        if pending is None:
            gw_main_l = _matmul(dp_all, xb, "tn", BF16, "d_w_main", m_used=main_w)
        else:
            gw_main_l, (finished[l + 1],) = _matmul(dp_all, xb, "tn", BF16, "d_w_main", m_used=main_w,
                                                    side=_side_share(mine))
        g_fgw[l] = _matmul(dfg_b, xb, "tn", F32, "d_w_fg")[:n_heads]
        by_chip = jnp.concatenate([gw_out_l.reshape(4, out_rows, d), jnp.stack([gw_main_l[b:b + blk] for b in base]),
                                   jnp.zeros((4, pad_rows, d), BF16)], axis=1)
        pending = by_chip.reshape(4, 2, r_half, d)
        if l > 0:
            dx = _matmul(dp_all, wt_all, "nn", F32, "d_x", add=dpre, add_scale=alpha)
        else:
            recv = _run_side(_side_swap(pending), "rs_swap_last")[0]
            hsum = _add_halves(pending, recv, core_arr, "rs_add_halves")
            dx, (others,) = _matmul(dp_all, wt_all, "nn", F32, "d_x", add=dpre, add_scale=alpha,
                                    side=_side_exchange(hsum))

    mine = _add_partials(hsum, others, chip_arr, core_arr, "rs_add_partials")
    finished[0] = _run_side(_side_share(mine), "rs_share_last")[0]
    grads = [f.reshape(2 * r_half, d) for f in finished]
    grad_w_out = jnp.stack([g[:out_rows] for g in grads])

    g_sm = jnp.stack(g_sm)
    g_ln = jnp.stack(g_ln)
    parts = [jnp.stack(g_pw), g_sm[:, 0:3, :], g_sm[:, 3, :], g_ln[:, 0, :], g_ln[:, 1, :],
             jnp.stack(g_bf)[:, :, 0], loss_blk[0, 0], jnp.stack(g_fgw)]
    packed = jnp.concatenate([_rows128(p) for p in parts], axis=0)
    summed = _sum_blocks(_all_gather_small(packed, "gather_small_grads"), 8, "sum_small_grads")
    outs, off = [], 0
    for p in parts:
        nr = _rows128(p).shape[0]
        outs.append(summed[off:off + nr].reshape(-1)[:p.size].reshape(p.shape))
        off += nr
    grad_pool_w, gconv_full, grad_pool_scale, grad_ln_g, grad_ln_b, grad_b_f, loss, grad_fg = outs
    grad_conv_w = lax.dynamic_slice_in_dim(gconv_full, chip * (cw // 4), cw // 4, axis=2)

    r_idx = lax.broadcasted_iota(jnp.int32, (in_shard, 1), 0)

    def shard_rows(g, fg_rows):
        gblk = jnp.pad(g[out_rows:r_comb], ((al, al), (0, 0)))
        plain = lax.dynamic_slice_in_dim(gblk, al + lead_dyn, in_shard, axis=0)
        after_fg = lax.dynamic_slice_in_dim(gblk, al + lead_dyn - n_heads, in_shard, axis=0)
        fg_placed = jnp.pad(fg_rows, ((hole, in_shard - hole - n_heads), (0, 0)))
        return jnp.where(jnp.logical_and(has_fg, r_idx >= hole + n_heads), after_fg,
                         jnp.where(jnp.logical_and(has_fg, r_idx >= hole), fg_placed, plain))

    grad_rows = jnp.stack([shard_rows(g, grad_fg[l]) for l, g in enumerate(grads)], axis=1)

    d_w_in, nm_w_in, nv_w_in, grad_w_in = (from_rows(a) for a in _adamw(
        to_rows(w_in), grad_rows, to_rows(m_w_in), to_rows(v_w_in), "adamw_w_in", lead_block=46, emit_grad=True))
    d_w_out, nm_w_out, nv_w_out = _adamw(w_out, grad_w_out, m_w_out, v_w_out, "adamw_w_out")
    small = [(b_f, grad_b_f, m_b_f, v_b_f), (conv_w, grad_conv_w, m_conv_w, v_conv_w),
             (pool_w, grad_pool_w, m_pool_w, v_pool_w), (pool_scale, grad_pool_scale, m_pool_scale, v_pool_scale),
             (ln_g, grad_ln_g, m_ln_g, v_ln_g), (ln_b, grad_ln_b, m_ln_b, v_ln_b)]
    pk = [jnp.concatenate([_rows128(t[j]) for t in small], axis=0)[None] for j in range(4)]
    sm_out = _adamw(pk[0], pk[1], pk[2], pk[3], "adamw_small", lead_block=1)
    res = {j: [] for j in range(3)}
    off = 0
    for t in small:
        nr = _rows128(t[0]).shape[0]
        for j in range(3):
            res[j].append(sm_out[j][0, off:off + nr].reshape(-1)[:t[0].size].reshape(t[0].shape))
        off += nr
    (d_b_f, d_conv_w, d_pool_w, d_pool_scale, d_ln_g, d_ln_b) = res[0]
    (nm_b_f, nm_conv_w, nm_pool_w, nm_pool_scale, nm_ln_g, nm_ln_b) = res[1]
    (nv_b_f, nv_conv_w, nv_pool_w, nv_pool_scale, nv_ln_g, nv_ln_b) = res[2]

    return (loss, dx[None], grad_w_in, grad_b_f, grad_conv_w, grad_pool_w, grad_pool_scale, grad_w_out,
            grad_ln_g, grad_ln_b,
            d_w_in, d_b_f, d_conv_w, d_pool_w, d_pool_scale, d_w_out, d_ln_g, d_ln_b,
            nm_w_in, nm_b_f, nm_conv_w, nm_pool_w, nm_pool_scale, nm_w_out, nm_ln_g, nm_ln_b,
            nv_w_in, nv_b_f, nv_conv_w, nv_pool_w, nv_pool_scale, nv_w_out, nv_ln_g, nv_ln_b)
```

```python
import jax
import jax.numpy as jnp
from jax import lax
from jax.experimental import pallas as pl
from jax.experimental.pallas import tpu as pltpu

F32 = jnp.float32
BF16 = jnp.bfloat16
MESH = pl.DeviceIdType.MESH

HEAD_DIM = 128
POOL_WINDOWS = (2, 4, 8, 16)
HALO = 16
BF16_TILE_ROWS = 16
LN_EPS = 1e-5
ADAM_LR = 0.001
ADAM_B1 = 0.9
ADAM_B2 = 0.999
ADAM_EPS = 1e-08
ADAM_WD = 0.01
ADAM_STEP = 10
VMEM_LIMIT = 56 * 1024 * 1024


def _params(sem, vmem=VMEM_LIMIT):
    return pltpu.CompilerParams(dimension_semantics=sem, vmem_limit_bytes=vmem)


def _tile(n, prefs):
    for t in prefs:
        if n % t == 0:
            return t
    return n


def _sigmoid(x):
    return 1.0 / (1.0 + jnp.exp(-x))


def _hbm_spec():
    return pl.BlockSpec(memory_space=pltpu.HBM)


class _Side:
    def __init__(self, inputs, out_shapes, n_sems, start, wait, aliases=None):
        self.inputs, self.out_shapes, self.n_sems = list(inputs), list(out_shapes), n_sems
        self.start, self.wait, self.aliases = start, wait, dict(aliases or {})


def _pcall(body, *, name, grid, in_specs, out_specs, out_shape, scratch_shapes, semantics, args, side=None):
    if side is None:
        return pl.pallas_call(
            body, name=name, grid=grid, in_specs=in_specs, out_specs=out_specs, out_shape=out_shape,
            scratch_shapes=scratch_shapes, compiler_params=_params(semantics))(*args)
    single = not isinstance(out_shape, (tuple, list))
    out_specs_l = [out_specs] if single else list(out_specs)
    out_shape_l = [out_shape] if single else list(out_shape)
    n_i, n_si, n_o, n_so, n_s = len(in_specs), len(side.inputs), len(out_shape_l), len(side.out_shapes), len(scratch_shapes)

    def wrapped(*refs):
        ins, sins = refs[:n_i], refs[n_i:n_i + n_si]
        o0 = n_i + n_si
        outs, souts = refs[o0:o0 + n_o], refs[o0 + n_o:o0 + n_o + n_so]
        s0 = o0 + n_o + n_so
        scr, (send, recv) = refs[s0:s0 + n_s], refs[s0 + n_s:]
        first = pl.program_id(0) == 0
        last = pl.program_id(0) == grid[0] - 1
        for ax in range(1, len(grid)):
            first = jnp.logical_and(first, pl.program_id(ax) == 0)
            last = jnp.logical_and(last, pl.program_id(ax) == grid[ax] - 1)

        @pl.when(first)
        def _():
            side.start(sins, souts, send, recv)

        body(*ins, *outs, *scr)

        @pl.when(last)
        def _():
            side.wait(sins, souts, send, recv)

    res = pl.pallas_call(
        wrapped, name=name, grid=grid,
        in_specs=list(in_specs) + [_hbm_spec()] * n_si,
        out_specs=tuple(out_specs_l + [_hbm_spec()] * n_so),
        out_shape=tuple(out_shape_l + side.out_shapes),
        scratch_shapes=list(scratch_shapes) + [pltpu.SemaphoreType.DMA((side.n_sems,))] * 2,
        input_output_aliases={n_i + i: n_o + o for i, o in side.aliases.items()},
        compiler_params=_params(("arbitrary",) * len(grid)),
    )(*args, *side.inputs)
    main = res[:n_o]
    return (main[0] if single else tuple(main)), tuple(res[n_o:])


def _run_side(side, name):
    n_si, n_so = len(side.inputs), len(side.out_shapes)

    def body(*refs):
        sins, souts = refs[:n_si], refs[n_si:n_si + n_so]
        send, recv = refs[n_si + n_so:]
        side.start(sins, souts, send, recv)
        side.wait(sins, souts, send, recv)

    return pl.pallas_call(
        body, name=name, out_shape=tuple(side.out_shapes),
        in_specs=[_hbm_spec()] * n_si, out_specs=tuple([_hbm_spec()] * n_so),
        scratch_shapes=[pltpu.SemaphoreType.DMA((side.n_sems,))] * 2,
        input_output_aliases=side.aliases,
    )(*side.inputs)


_DIMS = {"nn": ((1,), (0,)), "nt": ((1,), (1,)), "tn": ((0,), (0,))}


def _matmul(a, b, form, out_dtype, name, add=None, add_scale=1.0, tm=None, tn=None, tk=None, side=None,
            m_used=None, n_used=None):
    if form == "nn":
        (m, k), (_, n) = a.shape, b.shape
    elif form == "nt":
        (m, k), (n, _) = a.shape, b.shape
    else:
        (k, m), (_, n) = a.shape, b.shape
    m, n = m_used or m, n_used or n
    tm = tm or _tile(m, (1024, 512, 256, 128))
    tn = tn or _tile(n, (1024, 512, 256, 128))
    tk = tk or _tile(k, (2432, 2048, 1792, 1024, 512, 256, 128))
    nk = k // tk
    if form == "tn":
        a_spec = pl.BlockSpec((tk, tm), lambda i, j, kk: (kk, i))
    else:
        a_spec = pl.BlockSpec((tm, tk), lambda i, j, kk: (i, kk))
    if form == "nt":
        b_spec = pl.BlockSpec((tn, tk), lambda i, j, kk: (j, kk))
    else:
        b_spec = pl.BlockSpec((tk, tn), lambda i, j, kk: (kk, j))
    o_spec = pl.BlockSpec((tm, tn), lambda i, j, kk: (i, j))
    dims = (_DIMS[form], ((), ()))
    has_add = add is not None

    def finish(r, add_ref, o_ref):
        if has_add:
            r = r + add_scale * add_ref[...]
        o_ref[...] = r.astype(out_dtype)

    def body_one_pass(a_ref, b_ref, *rest):
        add_ref, o_ref = rest if has_add else (None,) + rest
        finish(lax.dot_general(a_ref[...], b_ref[...], dims, preferred_element_type=F32), add_ref, o_ref)

    def body_accumulate(a_ref, b_ref, *rest):
        add_ref, o_ref, acc = rest if has_add else (None,) + rest
        kk = pl.program_id(2)
        part = lax.dot_general(a_ref[...], b_ref[...], dims, preferred_element_type=F32)

        @pl.when(kk == 0)
        def _():
            acc[...] = part

        @pl.when(jnp.logical_and(kk > 0, kk < nk - 1))
        def _():
            acc[...] += part

        @pl.when(kk == nk - 1)
        def _():
            finish(acc[...] + part, add_ref, o_ref)

    in_specs = [a_spec, b_spec] + ([o_spec] if has_add else [])
    args = (a, b) + ((add,) if has_add else ())
    return _pcall(
        body_one_pass if nk == 1 else body_accumulate, name=name, grid=(m // tm, n // tn, nk),
        in_specs=in_specs, out_specs=o_spec,
        out_shape=jax.ShapeDtypeStruct((m, n), out_dtype),
        scratch_shapes=[] if nk == 1 else [pltpu.VMEM((tm, tn), F32)],
        semantics=("parallel", "parallel", "arbitrary"), args=args, side=side)


def _tri(n, upper):
    r = lax.broadcasted_iota(jnp.int32, (n, n), 0)
    c = lax.broadcasted_iota(jnp.int32, (n, n), 1)
    return jnp.where((r <= c) if upper else (r >= c), 1.0, 0.0).astype(F32)


def _gate_fwd(fg_t, b_col):
    h, s = fg_t.shape
    nb = s // 128

    def body(fg_ref, b_ref, cum_ref):
        u = _tri(128, True)
        carry = jnp.zeros((h, 128), F32)
        for j in range(nb):
            z = fg_ref[:, j * 128:(j + 1) * 128] + b_ref[...]
            logf = -(jnp.maximum(-z, 0.0) + jnp.log(1.0 + jnp.exp(-jnp.abs(z))))
            c = jnp.dot(logf, u, precision=lax.Precision.HIGHEST, preferred_element_type=F32) + carry
            cum_ref[:, j * 128:(j + 1) * 128] = c
            carry = jnp.broadcast_to(c[:, 127:128], (h, 128))

    return pl.pallas_call(
        body, name="gate_fwd", out_shape=jax.ShapeDtypeStruct((h, s), F32),
        in_specs=[pl.BlockSpec(memory_space=pltpu.VMEM)] * 2,
        out_specs=pl.BlockSpec(memory_space=pltpu.VMEM),
    )(fg_t, b_col)


def _gate_bwd(drow_t, dcol_t, fg_t, b_col):
    h, s = fg_t.shape
    nb = s // 128

    def body(dr_ref, dc_ref, fg_ref, b_ref, dfg_ref, db_ref):
        low = _tri(128, False)
        carry = jnp.zeros((h, 128), F32)
        db = jnp.zeros((h, 128), F32)
        for j in reversed(range(nb)):
            sl = slice(j * 128, (j + 1) * 128)
            r = jnp.dot(dr_ref[:, 0, sl] - dc_ref[:, 0, sl], low, precision=lax.Precision.HIGHEST,
                        preferred_element_type=F32) + carry
            carry = jnp.broadcast_to(r[:, 0:1], (h, 128))
            z = fg_ref[:, sl] + b_ref[...]
            dfg = r / (1.0 + jnp.exp(z))
            dfg_ref[:, sl] = dfg
            db = db + dfg
        db_ref[...] = jnp.broadcast_to(jnp.sum(db, axis=1, keepdims=True), (h, 128))

    return pl.pallas_call(
        body, name="gate_bwd",
        out_shape=(jax.ShapeDtypeStruct((h, s), F32), jax.ShapeDtypeStruct((h, 128), F32)),
        in_specs=[pl.BlockSpec(memory_space=pltpu.VMEM)] * 4,
        out_specs=(pl.BlockSpec(memory_space=pltpu.VMEM),) * 2,
    )(drow_t, dcol_t, fg_t, b_col)


LOG2E = 1.4426950408889634


def _attn_fwd(p_main, cum4, n_heads, t, side=None):
    s = p_main.shape[0]
    nq = s // t
    scale2 = HEAD_DIM ** -0.5 * LOG2E
    rep = t // 128

    def body(q_ref, k_ref, v_ref, c_ref, o_ref, lse_ref, m_s, l_s, acc_s, s_a, s_b):
        qi = pl.program_id(1)
        m_s[...] = jnp.full_like(m_s, -jnp.inf)
        l_s[...] = jnp.zeros_like(l_s)
        acc_s[...] = jnp.zeros_like(acc_s)

        def keys(ki):
            return pl.ds(pl.multiple_of(ki * t, t), t)

        def logits(ki, buf):
            buf[...] = lax.dot_general(q_ref[...], k_ref[keys(ki), :], (_DIMS["nt"], ((), ())),
                                       preferred_element_type=F32)

        def consume(ki, buf, masked):
            sc = buf[...] * scale2 - c_ref[ki] * LOG2E
            if masked:
                row = lax.broadcasted_iota(jnp.int32, (t, t), 0)
                col = lax.broadcasted_iota(jnp.int32, (t, t), 1)
                sc = jnp.where(col <= row, sc, -jnp.inf)
            m_prev = m_s[...]
            m_new = jnp.maximum(m_prev, jnp.max(sc, axis=1, keepdims=True))
            alpha = jnp.exp2(m_prev - m_new)
            p = jnp.exp2(sc - jnp.tile(m_new, (1, rep)))
            l_s[...] = alpha * l_s[...] + jnp.sum(p, axis=1, keepdims=True)
            acc_s[...] = alpha * acc_s[...] + jnp.dot(p.astype(BF16), v_ref[keys(ki), :], preferred_element_type=F32)
            m_s[...] = m_new

        def pair(j, carry):
            logits(2 * j + 1, s_b)
            consume(2 * j, s_a, False)
            logits(2 * j + 2, s_a)
            consume(2 * j + 1, s_b, False)
            return carry

        logits(0, s_a)
        lax.fori_loop(0, qi // 2, pair, 0)

        @pl.when(qi % 2 == 0)
        def _():
            consume(qi, s_a, True)

        @pl.when(qi % 2 == 1)
        def _():
            logits(qi, s_b)
            consume(qi - 1, s_a, False)
            consume(qi, s_b, True)

        o_ref[...] = (acc_s[...] / l_s[...]).astype(BF16)
        lse_ref[...] = m_s[...] + jnp.log(l_s[...]) * LOG2E

    head = lambda off: pl.BlockSpec((s, 128), lambda h, qi: (0, off + h))
    return _pcall(
        body, name="attn_fwd", grid=(n_heads, nq),
        in_specs=[pl.BlockSpec((t, 128), lambda h, qi: (qi, h)), head(n_heads), head(2 * n_heads),
                  pl.BlockSpec((None, nq, 1, t), lambda h, qi: (h, 0, 0, 0))],
        out_specs=(pl.BlockSpec((t, 128), lambda h, qi: (qi, h)),) * 2,
        out_shape=(jax.ShapeDtypeStruct((s, n_heads * 128), BF16), jax.ShapeDtypeStruct((s, n_heads * 128), F32)),
        scratch_shapes=[pltpu.VMEM((t, 128), F32)] * 3 + [pltpu.VMEM((t, t), F32)] * 2,
        semantics=("parallel", "arbitrary"), args=(p_main, p_main, p_main, cum4), side=side)


def _attn_bwd(p_main, do, lse_rep, delta_rep, cum4, n_heads, t, side=None):
    s = p_main.shape[0]
    nq = s // t
    scale = HEAD_DIM ** -0.5
    rep = t // 128

    def body(q_ref, k_ref, v_ref, do_ref, lse_ref, dl_ref, c_ref,
             dq_ref, dk_ref, dv_ref, drow_ref, dcol_ref, dq_acc, dk_acc, dv_acc, s_a, dp_a, s_b, dp_b):
        kb = pl.program_id(1)

        @pl.when(kb == 0)
        def _():
            dq_acc[...] = jnp.zeros_like(dq_acc)

        dk_acc[...] = jnp.zeros_like(dk_acc)
        dv_acc[...] = jnp.zeros_like(dv_acc)
        k = k_ref[...]
        v = v_ref[...]
        c2 = c_ref[...] * LOG2E
        k_ones = jnp.concatenate([k, jnp.ones((t, 128), BF16)], axis=1)
        q_ones = jnp.ones((t, 128), BF16)

        def rows_of(qb):
            return pl.ds(pl.multiple_of(qb * t, t), t)

        def inputs(qb, s_buf, dp_buf):
            s_buf[...] = lax.dot_general(q_ref[rows_of(qb), :], k, (_DIMS["nt"], ((), ())), preferred_element_type=F32)
            dp_buf[...] = lax.dot_general(do_ref[rows_of(qb), :], v, (_DIMS["nt"], ((), ())), preferred_element_type=F32)

        def consume(qb, s_buf, dp_buf, masked):
            rows = rows_of(qb)
            sc = s_buf[...] * (scale * LOG2E) - c2
            if masked:
                row = lax.broadcasted_iota(jnp.int32, (t, t), 0)
                col = lax.broadcasted_iota(jnp.int32, (t, t), 1)
                sc = jnp.where(col <= row, sc, -jnp.inf)
            p = jnp.exp2(sc - jnp.tile(lse_ref[rows, :], (1, rep)))
            dsb = (p * (dp_buf[...] - jnp.tile(dl_ref[rows, :], (1, rep)))).astype(BF16)
            dv_acc[...] += lax.dot_general(p.astype(BF16), do_ref[rows, :], (_DIMS["tn"], ((), ())),
                                           preferred_element_type=F32)
            dk_acc[...] += lax.dot_general(dsb, jnp.concatenate([q_ref[rows, :], q_ones], axis=1),
                                           (_DIMS["tn"], ((), ())), preferred_element_type=F32)
            dq_acc[rows, :] += jnp.dot(dsb, k_ones, preferred_element_type=F32)

        n_rest = nq - 1 - kb
        inputs(kb, s_a, dp_a)

        @pl.when(n_rest > 0)
        def _():
            inputs(kb + 1, s_b, dp_b)

        consume(kb, s_a, dp_a, True)
        diag = rows_of(kb)
        dq_ref[diag, :] = (dq_acc[diag, 0:128] * scale).astype(BF16)
        drow_ref[:, diag] = dq_acc[diag, 128:256].T[0:8, :]

        def pair(j, carry):
            qb = kb + 1 + 2 * j
            inputs(qb + 1, s_a, dp_a)
            consume(qb, s_b, dp_b, False)
            inputs(qb + 2, s_b, dp_b)
            consume(qb + 1, s_a, dp_a, False)
            return carry

        n_pairs = jnp.maximum(n_rest - 1, 0) // 2
        lax.fori_loop(0, n_pairs, pair, 0)
        left = kb + 1 + 2 * n_pairs

        @pl.when(n_rest - 2 * n_pairs == 1)
        def _():
            consume(left, s_b, dp_b, False)

        @pl.when(n_rest - 2 * n_pairs == 2)
        def _():
            inputs(left + 1, s_a, dp_a)
            consume(left, s_b, dp_b, False)
            consume(left + 1, s_a, dp_a, False)

        dk_ref[...] = (dk_acc[:, 0:128] * scale).astype(BF16)
        dcol_ref[...] = dk_acc[:, 128:256].T[0:8, :]
        dv_ref[...] = dv_acc[...].astype(BF16)

    kside = lambda off: pl.BlockSpec((t, 128), lambda h, kb: (kb, off + h))
    whole = pl.BlockSpec((s, 128), lambda h, kb: (0, h))
    hw = n_heads * 128
    return _pcall(
        body, name="attn_bwd", grid=(n_heads, nq),
        in_specs=[whole, kside(n_heads), kside(2 * n_heads), whole, whole, whole,
                  pl.BlockSpec((None, None, 1, t), lambda h, kb: (h, kb, 0, 0))],
        out_specs=(whole, kside(0), kside(0), pl.BlockSpec((None, 8, s), lambda h, kb: (h, 0, 0)),
                   pl.BlockSpec((None, 8, t), lambda h, kb: (h, 0, kb))),
        out_shape=(jax.ShapeDtypeStruct((s, hw), BF16), jax.ShapeDtypeStruct((s, hw), BF16),
                   jax.ShapeDtypeStruct((s, hw), BF16), jax.ShapeDtypeStruct((n_heads, 8, s), F32),
                   jax.ShapeDtypeStruct((n_heads, 8, s), F32)),
        scratch_shapes=[pltpu.VMEM((s, 256), F32), pltpu.VMEM((t, 256), F32), pltpu.VMEM((t, 128), F32)]
        + [pltpu.VMEM((t, t), F32)] * 4,
        semantics=("parallel", "arbitrary"), args=(p_main, p_main, p_main, do, lse_rep, delta_rep, cum4), side=side)


def _conv_fwd(u_ext, cw_ref):
    r1 = pltpu.roll(u_ext, 1, 0)
    r2 = pltpu.roll(u_ext, 2, 0)
    conv = cw_ref[2:3, :] * u_ext + cw_ref[1:2, :] * r1 + cw_ref[0:1, :] * r2
    return conv[HALO:], r1[HALO:], r2[HALO:]


def _pool_z(pu, pu_halo, row0, tm, pg):
    ext = jnp.concatenate([pu_halo, pu], axis=0)
    t1 = (row0 + lax.broadcasted_iota(jnp.int32, (tm, pg), 0) + 1).astype(F32)
    zs = []
    for g, w in enumerate(POOL_WINDOWS):
        sm = ext[:, g * pg:(g + 1) * pg]
        sh = 1
        while sh < w:
            sm = sm + pltpu.roll(sm, sh, 0)
            sh *= 2
        mean = sm[HALO:] / jnp.minimum(t1, float(w))
        zs.append(mean - pu[:, g * pg:(g + 1) * pg])
    return zs


def _mix_specs(tm, s):
    per = tm // HALO
    last = s // HALO - 1
    cur = lambda w, j: pl.BlockSpec((tm, w), lambda i: (i, j))
    prev = lambda j: pl.BlockSpec((HALO, 512), lambda i: (jnp.maximum(i * per - 1, 0), j))
    nxt = lambda j: pl.BlockSpec((HALO, 512), lambda i: (jnp.minimum((i + 1) * per, last), j))
    full = lambda shape: pl.BlockSpec(shape, lambda i: (0,) * len(shape))
    return cur, prev, nxt, full


def _mix_fwd(p_main, o, conv_w, pool_w, pool_scale, tm):
    s = p_main.shape[0]
    d = 2048
    pg = 128
    cur, prev, nxt, full = _mix_specs(tm, s)

    def body(ga_ref, cb_ref, cc_ref, ch_ref, gc_ref, pu_ref, gp_ref, cch_ref, chh_ref, puh_ref,
             o_ref, cw_ref, pw_ref, ps_ref, y_ref):
        i = pl.program_id(0)
        first = i == 0
        ga = ga_ref[...].astype(F32)
        y_ref[:, 0:1024] = (o_ref[...].astype(F32) * ga * _sigmoid(ga)).astype(BF16)

        u = cc_ref[...].astype(F32) * ch_ref[...].astype(F32)
        uh = jnp.where(first, 0.0, cch_ref[...].astype(F32) * chh_ref[...].astype(F32))
        conv, _, _ = _conv_fwd(jnp.concatenate([uh, u], axis=0), cw_ref)
        gc = gc_ref[...].astype(F32)
        y_ref[:, 1024:1536] = (cb_ref[...].astype(F32) * conv * gc * _sigmoid(gc)).astype(BF16)

        pu = pu_ref[...].astype(F32)
        puh = jnp.where(first, 0.0, puh_ref[...].astype(F32))
        zs = _pool_z(pu, puh, i * tm, tm, pg)
        gp = gp_ref[...].astype(F32)
        gate = gp * _sigmoid(gp) * ps_ref[...]
        for g in range(4):
            r = jnp.dot(zs[g].astype(BF16), pw_ref[g], preferred_element_type=F32)
            y_ref[:, 1536 + g * pg:1536 + (g + 1) * pg] = (r * gate[:, g * pg:(g + 1) * pg]).astype(BF16)

    return pl.pallas_call(
        body, name="mix_fwd", grid=(s // tm,),
        in_specs=[cur(1024, 3), cur(512, 8), cur(512, 9), cur(512, 10), cur(512, 11), cur(512, 12), cur(512, 13),
                  prev(9), prev(10), prev(12),
                  cur(1024, 0), full((3, 512)), full((4, pg, pg)), full((1, 512))],
        out_specs=pl.BlockSpec((tm, d), lambda i: (i, 0)),
        out_shape=jax.ShapeDtypeStruct((s, d), BF16),
        compiler_params=_params(("parallel",)),
    )(*([p_main] * 10), o, conv_w, pool_w, pool_scale)


def _mix_bwd(p_main, o, dy, conv_w, pool_w, pool_scale, tm):
    s = p_main.shape[0]
    pg = 128
    n_heads = 8
    cur, prev, nxt, full = _mix_specs(tm, s)
    nblk = s // tm
    n_ext = tm + HALO

    def silu_and_grad(x):
        sg = _sigmoid(x)
        return x * sg, sg * (1.0 + x * (1.0 - sg))

    def body(ga_ref, cb_ref, cc_ref, ch_ref, gc_ref, pu_ref, gp_ref, cch_ref, chh_ref, puh_ref,
             cbn_ref, gcn_ref, gpn_ref, o_ref, dy_ref, dycn_ref, dypn_ref, cw_ref, pw_ref, ps_ref,
             do_ref, dl_ref, dp_ref, dsm_ref, dpw_ref):
        i = pl.program_id(0)
        first = i == 0
        last = i == nblk - 1

        @pl.when(first)
        def _():
            dsm_ref[...] = jnp.zeros_like(dsm_ref)
            dpw_ref[...] = jnp.zeros_like(dpw_ref)

        ga = ga_ref[...].astype(F32)
        of = o_ref[...].astype(F32)
        dya = dy_ref[:, 0:1024]
        sa, dsa = silu_and_grad(ga)
        dout = dya * sa
        do_ref[...] = dout.astype(BF16)
        dp_ref[:, 0:1024] = (dya * of * dsa).astype(BF16)
        prod = dout * of
        for h in range(n_heads):
            dsum = jnp.sum(prod[:, h * 128:(h + 1) * 128], axis=1, keepdims=True)
            dl_ref[:, h * 128:(h + 1) * 128] = jnp.broadcast_to(dsum, (tm, 128))

        cb = cb_ref[...].astype(F32)
        cc = cc_ref[...].astype(F32)
        ch = ch_ref[...].astype(F32)
        gc = gc_ref[...].astype(F32)
        u = cc * ch
        uh = jnp.where(first, 0.0, cch_ref[...].astype(F32) * chh_ref[...].astype(F32))
        conv, u1, u2 = _conv_fwd(jnp.concatenate([uh, u], axis=0), cw_ref)
        sc_, dsc = silu_and_grad(gc)
        dyc = dy_ref[:, 1024:1536]
        dp_ref[:, 1024:1536] = (dyc * conv * sc_).astype(BF16)
        dp_ref[:, 2560:3072] = (dyc * cb * conv * dsc).astype(BF16)
        dconv = dyc * cb * sc_
        gcn = gcn_ref[...].astype(F32)
        dconv_n = jnp.where(last, 0.0, dycn_ref[...] * cbn_ref[...].astype(F32) * gcn * _sigmoid(gcn))
        dext = jnp.concatenate([dconv, dconv_n], axis=0)
        du = (cw_ref[2:3, :] * dext + cw_ref[1:2, :] * pltpu.roll(dext, n_ext - 1, 0)
              + cw_ref[0:1, :] * pltpu.roll(dext, n_ext - 2, 0))[:tm]
        dp_ref[:, 1536:2048] = (du * ch).astype(BF16)
        dp_ref[:, 2048:2560] = (du * cc).astype(BF16)
        dsm_ref[0:1, :] += jnp.sum(dconv * u2, axis=0, keepdims=True)
        dsm_ref[1:2, :] += jnp.sum(dconv * u1, axis=0, keepdims=True)
        dsm_ref[2:3, :] += jnp.sum(dconv * u, axis=0, keepdims=True)

        pu = pu_ref[...].astype(F32)
        puh = jnp.where(first, 0.0, puh_ref[...].astype(F32))
        zs = _pool_z(pu, puh, i * tm, tm, pg)
        gp = gp_ref[...].astype(F32)
        sp, dsp = silu_and_grad(gp)
        dyp = dy_ref[:, 1536:2048]
        gpn = gpn_ref[...].astype(F32)
        dr_n = jnp.where(last, 0.0, dypn_ref[...] * gpn * _sigmoid(gpn) * ps_ref[...])
        drs = dyp * sp
        dr = drs * ps_ref[...]
        t1 = (i * tm + lax.broadcasted_iota(jnp.int32, (tm, pg), 0) + 1).astype(F32)
        r_parts, dpu_parts = [], []
        for g, w in enumerate(POOL_WINDOWS):
            cols = slice(g * pg, (g + 1) * pg)
            zb = zs[g].astype(BF16)
            r_parts.append(jnp.dot(zb, pw_ref[g], preferred_element_type=F32))
            drb = dr[:, cols].astype(BF16)
            dpw_ref[g] += lax.dot_general(zb, drb, (_DIMS["tn"], ((), ())), preferred_element_type=F32)
            dz = lax.dot_general(drb, pw_ref[g], (_DIMS["nt"], ((), ())), preferred_element_type=F32)
            dz_n = lax.dot_general(dr_n[:, cols].astype(BF16), pw_ref[g], (_DIMS["nt"], ((), ())),
                                   preferred_element_type=F32)
            sm = jnp.concatenate([dz / jnp.minimum(t1, float(w)), dz_n / float(w)], axis=0)
            sh = 1
            while sh < w:
                sm = sm + pltpu.roll(sm, n_ext - sh, 0)
                sh *= 2
            dpu_parts.append(sm[:tm] - dz)
        r = jnp.concatenate(r_parts, axis=1)
        dp_ref[:, 3072:3584] = jnp.concatenate(dpu_parts, axis=1).astype(BF16)
        dp_ref[:, 3584:4096] = (dyp * r * ps_ref[...] * dsp).astype(BF16)
        dsm_ref[3:4, :] += jnp.sum(drs * r, axis=0, keepdims=True)

    dy_next = lambda j: pl.BlockSpec((HALO, 512), lambda i: (jnp.minimum((i + 1) * (tm // HALO), s // HALO - 1), j))
    return pl.pallas_call(
        body, name="mix_bwd", grid=(nblk,),
        in_specs=[cur(1024, 3), cur(512, 8), cur(512, 9), cur(512, 10), cur(512, 11), cur(512, 12), cur(512, 13),
                  prev(9), prev(10), prev(12), nxt(8), nxt(11), nxt(13),
                  cur(1024, 0), pl.BlockSpec((tm, 2048), lambda i: (i, 0)), dy_next(2), dy_next(3),
                  full((3, 512)), full((4, pg, pg)), full((1, 512))],
        out_specs=(pl.BlockSpec((tm, 1024), lambda i: (i, 0)), pl.BlockSpec((tm, 1024), lambda i: (i, 0)),
                   pl.BlockSpec((tm, 4096), lambda i: (i, 0)), full((8, 512)), full((4, pg, pg))),
        out_shape=(jax.ShapeDtypeStruct((s, 1024), BF16), jax.ShapeDtypeStruct((s, 1024), F32),
                   jax.ShapeDtypeStruct((s, 4096), BF16), jax.ShapeDtypeStruct((8, 512), F32),
                   jax.ShapeDtypeStruct((4, pg, pg), F32)),
        compiler_params=_params(("arbitrary",)),
    )(*([p_main] * 13), o, dy, dy, dy, conv_w, pool_w, pool_scale)


def _outproj_ln(y, w_out, x, ln_g, ln_b, alpha, tm, side=None):
    s, d = x.shape

    def body(y_ref, w_ref, x_ref, g_ref, b_ref, xn_ref, xb_ref, pre_ref):
        pre = alpha * x_ref[...] + jnp.dot(y_ref[...], w_ref[...], preferred_element_type=F32)
        mu = jnp.mean(pre, axis=1, keepdims=True)
        cen = pre - mu
        var = jnp.mean(cen * cen, axis=1, keepdims=True)
        xn = cen * lax.rsqrt(var + LN_EPS) * g_ref[...] + b_ref[...]
        pre_ref[...] = pre
        xn_ref[...] = xn
        xb_ref[...] = xn.astype(BF16)

    row = pl.BlockSpec((tm, d), lambda i: (i, 0))
    vec = pl.BlockSpec((1, d), lambda i: (0, 0))
    return _pcall(
        body, name="outproj_ln", grid=(s // tm,),
        in_specs=[row, pl.BlockSpec((d, d), lambda i: (0, 0)), row, vec, vec],
        out_specs=(row, row, row),
        out_shape=(jax.ShapeDtypeStruct((s, d), F32), jax.ShapeDtypeStruct((s, d), BF16),
                   jax.ShapeDtypeStruct((s, d), F32)),
        scratch_shapes=[], semantics=("parallel",), args=(y, w_out, x, ln_g, ln_b), side=side)


def _ln_bwd(dxn, pre, ln_g, tm, side=None):
    s, d = pre.shape

    def body(dx_ref, pre_ref, g_ref, dpre_ref, dpb_ref, dgb_ref):
        @pl.when(pl.program_id(0) == 0)
        def _():
            dgb_ref[...] = jnp.zeros_like(dgb_ref)

        pre_ = pre_ref[...]
        dx = dx_ref[...]
        mu = jnp.mean(pre_, axis=1, keepdims=True)
        cen = pre_ - mu
        var = jnp.mean(cen * cen, axis=1, keepdims=True)
        rstd = lax.rsqrt(var + LN_EPS)
        xhat = cen * rstd
        dxh = dx * g_ref[...]
        dpre = rstd * (dxh - jnp.mean(dxh, axis=1, keepdims=True)
                       - xhat * jnp.mean(dxh * xhat, axis=1, keepdims=True))
        dpre_ref[...] = dpre
        dpb_ref[...] = dpre.astype(BF16)
        dgb_ref[0:1, :] += jnp.sum(dx * xhat, axis=0, keepdims=True)
        dgb_ref[1:2, :] += jnp.sum(dx, axis=0, keepdims=True)

    row = pl.BlockSpec((tm, d), lambda i: (i, 0))
    return _pcall(
        body, name="ln_bwd", grid=(s // tm,),
        in_specs=[row, row, pl.BlockSpec((1, d), lambda i: (0, 0))],
        out_specs=(row, row, pl.BlockSpec((8, d), lambda i: (0, 0))),
        out_shape=(jax.ShapeDtypeStruct((s, d), F32), jax.ShapeDtypeStruct((s, d), BF16),
                   jax.ShapeDtypeStruct((8, d), F32)),
        scratch_shapes=[], semantics=("arbitrary",), args=(dxn, pre, ln_g), side=side)


def _loss_grad(y, target, tm):
    s, d = y.shape

    def body(y_ref, t_ref, dy_ref, loss_ref):
        @pl.when(pl.program_id(0) == 0)
        def _():
            loss_ref[...] = jnp.zeros_like(loss_ref)

        diff = y_ref[...] - t_ref[...]
        dy_ref[...] = diff / d
        loss_ref[...] += 0.5 * jnp.sum(jnp.sum(diff * diff, axis=1, keepdims=True) / d)

    row = pl.BlockSpec((tm, d), lambda i: (i, 0))
    return pl.pallas_call(
        body, name="loss_grad", grid=(s // tm,),
        in_specs=[row, row],
        out_specs=(row, pl.BlockSpec((8, 128), lambda i: (0, 0))),
        out_shape=(jax.ShapeDtypeStruct((s, d), F32), jax.ShapeDtypeStruct((8, 128), F32)),
        compiler_params=_params(("arbitrary",)),
    )(y, target)


def _adamw(w, g, m, v, name, lead_block=None, emit_grad=False):
    a, r, c = w.shape
    tr = r if lead_block else _tile(r, (256, 128, 64, 32, 16, 8))
    c1 = 1.0 - ADAM_B1 ** ADAM_STEP
    c2 = 1.0 - ADAM_B2 ** ADAM_STEP

    def body(w_ref, g_ref, m_ref, v_ref, d_ref, mo_ref, vo_ref, *go_ref):
        g_ = g_ref[...]
        if emit_grad:
            go_ref[0][...] = g_
        m_new = ADAM_B1 * m_ref[...] + (1.0 - ADAM_B1) * g_
        v_new = ADAM_B2 * v_ref[...] + (1.0 - ADAM_B2) * (g_ * g_)
        d_ref[...] = -ADAM_LR * ((m_new / c1) / (jnp.sqrt(v_new / c2) + ADAM_EPS) + ADAM_WD * w_ref[...])
        mo_ref[...] = m_new
        vo_ref[...] = v_new

    ta = lead_block or 1
    spec = pl.BlockSpec((ta, tr, c), lambda i, j: (i, j, 0))
    shp = jax.ShapeDtypeStruct(w.shape, F32)
    n_out = 4 if emit_grad else 3
    return pl.pallas_call(
        body, name=name, grid=(a // ta, r // tr), in_specs=[spec] * 4, out_specs=(spec,) * n_out,
        out_shape=(shp,) * n_out, compiler_params=_params(("parallel", "parallel")),
    )(w, g, m, v)


def _place():
    x, y, c = lax.axis_index("x"), lax.axis_index("y"), lax.axis_index("c")
    return x, y, c, [(1 - x, y), (x, 1 - y), (1 - x, 1 - y)]


def _remote(src, dst, send, recv, k, to):
    return pltpu.make_async_remote_copy(src_ref=src, dst_ref=dst, send_sem=send.at[k], recv_sem=recv.at[k],
                                        device_id=to, device_id_type=MESH)


def _slot(ref, px, py, pc):
    return ref.at[pl.ds(2 * px + py, 1), pl.ds(pc, 1)]


def _side_gather_ici(shard, rows=None, into=None):
    _, two, r, w = shard.shape
    lo, hi = rows or (0, r)

    def _slot(ref, px, py, pc):
        return ref.at[pl.ds(2 * px + py, 1), pl.ds(pc, 1), pl.ds(lo, hi - lo)]

    def sends(sins, souts, send, recv):
        x, y, c, chips = _place()
        return [_remote(sins[0].at[:, pl.ds(c, 1), pl.ds(lo, hi - lo)], _slot(souts[0], x, y, c), send, recv, k,
                        (*chip, c)) for k, chip in enumerate(chips)]

    def start(sins, souts, send, recv):
        for cp in sends(sins, souts, send, recv):
            cp.start()

    def wait(sins, souts, send, recv):
        x, y, c, chips = _place()
        for cp in sends(sins, souts, send, recv):
            cp.wait_send()
        for k, chip in enumerate(chips):
            _remote(_slot(souts[0], *chip, c), _slot(souts[0], *chip, c), send, recv, k, (x, y, c)).wait_recv()

    more = [] if into is None else [into]
    return _Side([shard] + more, [jax.ShapeDtypeStruct((4, two, r, w), shard.dtype)], 3, start, wait,
                 aliases={} if into is None else {1: 0})


def _side_gather_d2d(gathered):
    def sends(souts, send, recv):
        x, y, c, chips = _place()
        return [_remote(_slot(souts[0], *chip, c), _slot(souts[0], *chip, c), send, recv, k, (x, y, 1 - c))
                for k, chip in enumerate(chips)]

    def start(sins, souts, send, recv):
        for cp in sends(souts, send, recv):
            cp.start()

    def wait(sins, souts, send, recv):
        x, y, c, chips = _place()
        for cp in sends(souts, send, recv):
            cp.wait_send()
        for k, chip in enumerate(chips):
            _remote(_slot(souts[0], *chip, 1 - c), _slot(souts[0], *chip, 1 - c), send, recv, k, (x, y, c)).wait_recv()

    return _Side([gathered], [jax.ShapeDtypeStruct(gathered.shape, gathered.dtype)], 3, start, wait, aliases={0: 0})


def _side_swap(g):
    nchip, _, r, w = g.shape

    def copy(sins, souts, send, recv):
        x, y, c, _ = _place()
        return _remote(sins[0].at[:, pl.ds(1 - c, 1)], souts[0], send, recv, 0, (x, y, 1 - c))

    return _Side([g], [jax.ShapeDtypeStruct((nchip, 1, r, w), g.dtype)], 1,
                 lambda *a: copy(*a).start(), lambda *a: copy(*a).wait())


def _side_exchange(hsum):
    _, _, r, w = hsum.shape

    def copies(sins, souts, send, recv):
        x, y, c, chips = _place()
        return [_remote(sins[0].at[pl.ds(2 * px + py, 1)], souts[0].at[pl.ds(k, 1)], send, recv, k, (px, py, c))
                for k, (px, py) in enumerate(chips)]

    def start(*a):
        for cp in copies(*a):
            cp.start()

    def wait(*a):
        for cp in copies(*a):
            cp.wait()

    return _Side([hsum], [jax.ShapeDtypeStruct((3, 1, r, w), hsum.dtype)], 3, start, wait)


def _side_share(both):
    def copy(sins, souts, send, recv):
        x, y, c, _ = _place()
        return _remote(souts[0].at[pl.ds(c, 1)], souts[0].at[pl.ds(c, 1)], send, recv, 0, (x, y, 1 - c))

    def wait(sins, souts, send, recv):
        x, y, c, _ = _place()
        copy(sins, souts, send, recv).wait_send()
        _remote(souts[0].at[pl.ds(1 - c, 1)], souts[0].at[pl.ds(1 - c, 1)], send, recv, 0, (x, y, c)).wait_recv()

    return _Side([both], [jax.ShapeDtypeStruct(both.shape, both.dtype)], 1,
                 lambda *a: copy(*a).start(), wait, aliases={0: 0})


def _own_slot_filled(gathered, shard):
    chip = 2 * lax.axis_index("x") + lax.axis_index("y")
    return lax.dynamic_update_slice(gathered, shard, (chip, 0, 0, 0))


def _plane_tile(r, cdim):
    tr = _tile(r, (256, 128))
    if tr != r:
        return tr, cdim
    return r, _tile(cdim, (512, 256, 128))


def _add_halves(g, recv, core, name):
    nchip, nl, r, cdim = g.shape
    half = nl // 2
    tr, tc = _plane_tile(r, cdim)

    def body(c_ref, g_ref, r_ref, o_ref):
        o_ref[...] = (g_ref[...].astype(F32) + r_ref[...].astype(F32)).astype(BF16)

    blk = (1, 1, tr, tc)
    return pl.pallas_call(
        body, name=name,
        grid_spec=pltpu.PrefetchScalarGridSpec(
            num_scalar_prefetch=1, grid=(nchip, half, r // tr, cdim // tc),
            in_specs=[pl.BlockSpec(blk, lambda j, l, i, k, c_ref: (j, c_ref[0] * half + l, i, k)),
                      pl.BlockSpec(blk, lambda j, l, i, k, c_ref: (j, l, i, k))],
            out_specs=pl.BlockSpec(blk, lambda j, l, i, k, c_ref: (j, l, i, k))),
        out_shape=jax.ShapeDtypeStruct((nchip, half, r, cdim), BF16),
        compiler_params=_params(("parallel",) * 4),
    )(core, g, recv)


def _add_partials(hsum, others, chip, core, name):
    nchip, _, r, cdim = hsum.shape
    tr, tc = _plane_tile(r, cdim)

    def body(i_ref, c_ref, h_ref, a_ref, b_ref, d_ref, o_ref):
        o_ref[...] = ((h_ref[...].astype(F32) + a_ref[...].astype(F32))
                      + (b_ref[...].astype(F32) + d_ref[...].astype(F32)))[0]

    blk = (1, 1, tr, tc)
    other = lambda n: pl.BlockSpec(blk, lambda i, k, i_ref, c_ref: (n, 0, i, k))
    return pl.pallas_call(
        body, name=name,
        grid_spec=pltpu.PrefetchScalarGridSpec(
            num_scalar_prefetch=2, grid=(r // tr, cdim // tc),
            in_specs=[pl.BlockSpec(blk, lambda i, k, i_ref, c_ref: (i_ref[0], 0, i, k)), other(0), other(1), other(2)],
            out_specs=pl.BlockSpec((1, tr, tc), lambda i, k, i_ref, c_ref: (c_ref[0], i, k))),
        out_shape=jax.ShapeDtypeStruct((2, r, cdim), F32),
        compiler_params=_params(("parallel",) * 2),
    )(chip, core, hsum, others, others, others)


def _all_gather_small(v, name):
    m_per, n = v.shape

    def body(x_ref, out_ref, send_sems, recv_sems, local_sem):
        x, y, c, chips = _place()
        me, sib = (x, y, c), (x, y, 1 - c)

        def rows(px, py, pc):
            return out_ref.at[pl.ds((4 * px + 2 * py + pc) * m_per, m_per), :]

        def copy(k, block, to, src=None):
            return pltpu.make_async_remote_copy(
                src_ref=rows(*block) if src is None else src, dst_ref=rows(*block), send_sem=send_sems.at[k],
                recv_sem=recv_sems.at[k], device_id=to, device_id_type=MESH)

        mine = pltpu.make_async_copy(x_ref, rows(*me), local_sem)
        mine.start()
        first = [copy(0, me, sib, src=x_ref)]
        first += [copy(1 + j, me, (*chip, c), src=x_ref) for j, chip in enumerate(chips)]
        for cp in first:
            cp.start()
        passed = [copy(4 + j, (*chip, c), sib) for j, chip in enumerate(chips)]
        for j, chip in enumerate(chips):
            copy(1 + j, (*chip, c), me).wait_recv()
            passed[j].start()
        copy(0, sib, me).wait_recv()
        for j, chip in enumerate(chips):
            copy(4 + j, (*chip, 1 - c), me).wait_recv()
        for cp in first + passed:
            cp.wait_send()
        mine.wait()

    return pl.pallas_call(
        body, name=name, out_shape=jax.ShapeDtypeStruct((8 * m_per, n), v.dtype),
        in_specs=[pl.BlockSpec(memory_space=pltpu.VMEM)], out_specs=pl.BlockSpec(memory_space=pltpu.VMEM),
        scratch_shapes=[pltpu.SemaphoreType.DMA((7,)), pltpu.SemaphoreType.DMA((7,)), pltpu.SemaphoreType.DMA],
        compiler_params=pltpu.CompilerParams(vmem_limit_bytes=VMEM_LIMIT),
    )(v)


def _sum_blocks(v, nblk, name):
    m_per = v.shape[0] // nblk

    def body(v_ref, o_ref):
        acc = v_ref[0:m_per, :]
        for j in range(1, nblk):
            acc = acc + v_ref[j * m_per:(j + 1) * m_per, :]
        o_ref[...] = acc

    return pl.pallas_call(
        body, name=name, out_shape=jax.ShapeDtypeStruct((m_per, 128), F32),
        in_specs=[pl.BlockSpec(memory_space=pltpu.VMEM)], out_specs=pl.BlockSpec(memory_space=pltpu.VMEM),
        compiler_params=pltpu.CompilerParams(vmem_limit_bytes=VMEM_LIMIT),
    )(v)


def _rows128(a, rows=None):
    flat = a.reshape(-1).astype(F32)
    need = -(-flat.shape[0] // 128)
    rows = rows or -(-need // 8) * 8
    return jnp.pad(flat, (0, rows * 128 - flat.shape[0])).reshape(rows, 128)


def kernel(x, w_in, b_f, conv_w, pool_w, pool_scale, w_out, ln_g, ln_b, loss_target, m_w_in, m_b_f, m_conv_w, m_pool_w, m_pool_scale, m_w_out, m_ln_g, m_ln_b, v_w_in, v_b_f, v_conv_w, v_pool_w, v_pool_scale, v_w_out, v_ln_g, v_ln_b):
    depth, d, in_shard = w_in.shape
    s = x.shape[1]
    att = d // 2
    n_heads = att // HEAD_DIM
    cw = d // 4
    pw = d - att - cw
    in_w = 4 * att + n_heads + 4 * cw + 2 * pw
    assert in_shard * 4 == in_w and (d, n_heads, cw, pw) == (2048, 8, 512, 512) and s % 512 == 0
    main_w = in_w - n_heads
    fg0 = 4 * att
    alpha = (2 * depth) ** 0.25
    t_att = 512
    tm_mix = 256

    cx, cy, cc = lax.axis_index("x"), lax.axis_index("y"), lax.axis_index("c")
    chip = 2 * cx + cy
    core_arr = jnp.reshape(cc, (1,)).astype(jnp.int32)
    chip_arr = jnp.reshape(chip, (1,)).astype(jnp.int32)
    out_rows = w_out.shape[1]

    starts = [j * in_shard for j in range(4)]
    main_start = [g if g < fg0 else g - n_heads for g in starts]
    al = BF16_TILE_ROWS
    base = [m // al * al for m in main_start]
    lead = [m - b for m, b in zip(main_start, base)]
    fg_chip = max(j for j in range(4) if starts[j] <= fg0)
    hole = fg0 - starts[fg_chip]
    blk = -(-(max(lead) + in_shard) // al) * al
    assert hole + n_heads <= in_shard and all(b + blk <= main_w for b in base) and base[3] + blk == main_w
    r_comb = out_rows + blk
    pad_rows = -r_comb % (2 * al)
    r_half = (r_comb + pad_rows) // 2
    lead_dyn = jnp.take(jnp.array(lead, jnp.int32), chip)
    has_fg = chip == fg_chip

    to_rows = lambda a: jnp.transpose(a, (2, 0, 1))
    from_rows = lambda a: jnp.transpose(a, (1, 2, 0))
    wt_b = jnp.transpose(to_rows(w_in).astype(BF16), (1, 0, 2))
    padded = jnp.pad(wt_b, ((0, 0), (al, 2 * al), (0, 0)))
    plain = lax.dynamic_slice_in_dim(padded, al - lead_dyn, blk, axis=1)
    skipping = lax.dynamic_slice_in_dim(padded, al - lead_dyn + n_heads, blk, axis=1)
    pos = lax.broadcasted_iota(jnp.int32, (1, blk, 1), 1)
    block = jnp.where(jnp.logical_and(has_fg, pos >= lead_dyn + hole), skipping, plain)
    comb = jnp.concatenate([w_out.astype(BF16), block, jnp.zeros((depth, pad_rows, d), BF16)], axis=1)
    shards = comb.reshape(depth, 2, r_half, d)

    def weights_of(gathered, wt_fg):
        full = gathered.reshape(4, 2 * r_half, d)
        wt_all = sum(jnp.pad(full[j, out_rows:r_comb], ((base[j], main_w + 128 - base[j] - blk), (0, 0)))
                     for j in range(4)) + jnp.pad(wt_fg, ((main_w, 0), (0, 0)))
        return wt_all, full[:, :out_rows].reshape(d, d)

    small_w = jnp.concatenate([_rows128(conv_w), _rows128(wt_b[:, hole:hole + n_heads])], axis=0)
    n_cw = _rows128(conv_w).shape[0] * 128
    sw_all = _all_gather_small(small_w, "gather_small_weights").reshape(4, 2, small_w.shape[0] * 128)[:, 0]
    conv_full = sw_all[:, :depth * 3 * (cw // 4)].reshape(4, depth, 3, cw // 4)
    conv_full = jnp.moveaxis(conv_full, 0, 2).reshape(depth, 3, cw)
    wt_fg_all = sw_all[fg_chip, n_cw:n_cw + depth * n_heads * d].reshape(depth, n_heads, d).astype(BF16)
    wt_fg_all = jnp.pad(wt_fg_all, ((0, 0), (0, 128 - n_heads), (0, 0)))

    first = shards[0][None]
    arrived = _run_side(_side_gather_ici(first), "gather_first_ici")[0]
    arrived = _run_side(_side_gather_d2d(arrived), "gather_first_d2d")[0]
    weights = weights_of(_own_slot_filled(arrived, first), wt_fg_all[0])
    pool_b = pool_w.astype(BF16)

    split = r_half * 2 // 5 // al * al
    xf = x[0]
    xb = xf.astype(BF16)
    saved = []
    for l in range(depth):
        wt_all, w_out_l = weights
        wt_fg = wt_fg_all[l]
        nxt = shards[l + 1][None] if l + 1 < depth else None
        if nxt is None:
            p_main = _matmul(xb, wt_all, "nt", BF16, "proj_main", n_used=main_w)
        else:
            p_main, (arrived,) = _matmul(xb, wt_all, "nt", BF16, "proj_main", n_used=main_w,
                                         side=_side_gather_ici(nxt, rows=(0, split)))
        fg = _matmul(xb, wt_fg, "nt", F32, "proj_fg")
        fg_t = fg[:, :n_heads].T
        b_col = b_f[l].reshape(n_heads, 1)
        cum4 = _gate_fwd(fg_t, b_col).reshape(n_heads, s // t_att, 1, t_att)
        if nxt is None:
            o, lse = _attn_fwd(p_main, cum4, n_heads, t_att)
        else:
            (o, lse), (arrived,) = _attn_fwd(p_main, cum4, n_heads, t_att,
                                             side=_side_gather_ici(nxt, rows=(split, r_half), into=arrived))
        y = _mix_fwd(p_main, o, conv_full[l], pool_b[l], pool_scale[l].reshape(1, pw), tm_mix)
        ln_args = (y, w_out_l, xf, ln_g[l].reshape(1, d), ln_b[l].reshape(1, d), alpha, tm_mix)
        if nxt is None:
            x_new, xb_new, pre = _outproj_ln(*ln_args)
        else:
            (x_new, xb_new, pre), (arrived,) = _outproj_ln(*ln_args, side=_side_gather_d2d(arrived))
            weights = weights_of(_own_slot_filled(arrived, nxt), wt_fg_all[l + 1])
        saved.append((xb, p_main, fg_t, b_col, cum4, o, lse, y, pre, wt_all, w_out_l))
        xf, xb = x_new, xb_new

    dx, loss_blk = _loss_grad(xf, loss_target[0], tm_mix)

    g_bf, g_sm, g_pw, g_ln, g_fgw, finished = ([None] * depth for _ in range(6))
    pending = None
    for l in reversed(range(depth)):
        xb, p_main, fg_t, b_col, cum4, o, lse, y, pre, wt_all, w_out_l = saved[l]
        if pending is None:
            dpre, dpre_b, g_ln[l] = _ln_bwd(dx, pre, ln_g[l].reshape(1, d), tm_mix)
        else:
            (dpre, dpre_b, g_ln[l]), (recv,) = _ln_bwd(dx, pre, ln_g[l].reshape(1, d), tm_mix,
                                                       side=_side_swap(pending))
            hsum = _add_halves(pending, recv, core_arr, "rs_add_halves")
        dy = _matmul(dpre_b, w_out_l, "nt", BF16, "d_y")
        gw_out_l = _matmul(y, dpre_b, "tn", BF16, "d_w_out")
        do, delta, dp_rest, g_sm[l], g_pw[l] = _mix_bwd(p_main, o, dy, conv_full[l], pool_b[l],
                                                        pool_scale[l].reshape(1, pw), tm_mix)
        if pending is None:
            dq, dk, dv, drow, dcol = _attn_bwd(p_main, do, lse, delta, cum4, n_heads, t_att)
        else:
            (dq, dk, dv, drow, dcol), (others,) = _attn_bwd(p_main, do, lse, delta, cum4, n_heads, t_att,
                                                            side=_side_exchange(hsum))
            mine = _add_partials(hsum, others, chip_arr, core_arr, "rs_add_partials")
        dfg_t, g_bf[l] = _gate_bwd(drow, dcol, fg_t, b_col)
        dfg_b = jnp.pad(dfg_t.T, ((0, 0), (0, 128 - n_heads))).astype(BF16)
        dp_all = jnp.concatenate([dq, dk, dv, dp_rest, dfg_b], axis=1)
        if pending is None:
            gw_main_l = _matmul(dp_all, xb, "tn", BF16, "d_w_main", m_used=main_w)
        else:
            gw_main_l, (finished[l + 1],) = _matmul(dp_all, xb, "tn", BF16, "d_w_main", m_used=main_w,
                                                    side=_side_share(mine))
        g_fgw[l] = _matmul(dfg_b, xb, "tn", F32, "d_w_fg")[:n_heads]
        by_chip = jnp.concatenate([gw_out_l.reshape(4, out_rows, d), jnp.stack([gw_main_l[b:b + blk] for b in base]),
                                   jnp.zeros((4, pad_rows, d), BF16)], axis=1)
        pending = by_chip.reshape(4, 2, r_half, d)
        if l > 0:
            dx = _matmul(dp_all, wt_all, "nn", F32, "d_x", add=dpre, add_scale=alpha)
        else:
            recv = _run_side(_side_swap(pending), "rs_swap_last")[0]
            hsum = _add_halves(pending, recv, core_arr, "rs_add_halves")
            dx, (others,) = _matmul(dp_all, wt_all, "nn", F32, "d_x", add=dpre, add_scale=alpha,
                                    side=_side_exchange(hsum))

    mine = _add_partials(hsum, others, chip_arr, core_arr, "rs_add_partials")
    finished[0] = _run_side(_side_share(mine), "rs_share_last")[0]
    grads = [f.reshape(2 * r_half, d) for f in finished]
    grad_w_out = jnp.stack([g[:out_rows] for g in grads])

    g_sm = jnp.stack(g_sm)
    g_ln = jnp.stack(g_ln)
    parts = [jnp.stack(g_pw), g_sm[:, 0:3, :], g_sm[:, 3, :], g_ln[:, 0, :], g_ln[:, 1, :],
             jnp.stack(g_bf)[:, :, 0], loss_blk[0, 0], jnp.stack(g_fgw)]
    packed = jnp.concatenate([_rows128(p) for p in parts], axis=0)
    summed = _sum_blocks(_all_gather_small(packed, "gather_small_grads"), 8, "sum_small_grads")
    outs, off = [], 0
    for p in parts:
        nr = _rows128(p).shape[0]
        outs.append(summed[off:off + nr].reshape(-1)[:p.size].reshape(p.shape))
        off += nr
    grad_pool_w, gconv_full, grad_pool_scale, grad_ln_g, grad_ln_b, grad_b_f, loss, grad_fg = outs
    grad_conv_w = lax.dynamic_slice_in_dim(gconv_full, chip * (cw // 4), cw // 4, axis=2)

    r_idx = lax.broadcasted_iota(jnp.int32, (in_shard, 1), 0)

    def shard_rows(g, fg_rows):
        gblk = jnp.pad(g[out_rows:r_comb], ((al, al), (0, 0)))
        plain = lax.dynamic_slice_in_dim(gblk, al + lead_dyn, in_shard, axis=0)
        after_fg = lax.dynamic_slice_in_dim(gblk, al + lead_dyn - n_heads, in_shard, axis=0)
        fg_placed = jnp.pad(fg_rows, ((hole, in_shard - hole - n_heads), (0, 0)))
        return jnp.where(jnp.logical_and(has_fg, r_idx >= hole + n_heads), after_fg,
                         jnp.where(jnp.logical_and(has_fg, r_idx >= hole), fg_placed, plain))

    grad_rows = jnp.stack([shard_rows(g, grad_fg[l]) for l, g in enumerate(grads)], axis=1)

    d_w_in, nm_w_in, nv_w_in, grad_w_in = (from_rows(a) for a in _adamw(
        to_rows(w_in), grad_rows, to_rows(m_w_in), to_rows(v_w_in), "adamw_w_in", lead_block=46, emit_grad=True))
    d_w_out, nm_w_out, nv_w_out = _adamw(w_out, grad_w_out, m_w_out, v_w_out, "adamw_w_out")
    small = [(b_f, grad_b_f, m_b_f, v_b_f), (conv_w, grad_conv_w, m_conv_w, v_conv_w),
             (pool_w, grad_pool_w, m_pool_w, v_pool_w), (pool_scale, grad_pool_scale, m_pool_scale, v_pool_scale),
             (ln_g, grad_ln_g, m_ln_g, v_ln_g), (ln_b, grad_ln_b, m_ln_b, v_ln_b)]
    pk = [jnp.concatenate([_rows128(t[j]) for t in small], axis=0)[None] for j in range(4)]
    sm_out = _adamw(pk[0], pk[1], pk[2], pk[3], "adamw_small", lead_block=1)
    res = {j: [] for j in range(3)}
    off = 0
    for t in small:
        nr = _rows128(t[0]).shape[0]
        for j in range(3):
            res[j].append(sm_out[j][0, off:off + nr].reshape(-1)[:t[0].size].reshape(t[0].shape))
        off += nr
    (d_b_f, d_conv_w, d_pool_w, d_pool_scale, d_ln_g, d_ln_b) = res[0]
    (nm_b_f, nm_conv_w, nm_pool_w, nm_pool_scale, nm_ln_g, nm_ln_b) = res[1]
    (nv_b_f, nv_conv_w, nv_pool_w, nv_pool_scale, nv_ln_g, nv_ln_b) = res[2]

    return (loss, dx[None], grad_w_in, grad_b_f, grad_conv_w, grad_pool_w, grad_pool_scale, grad_w_out,
            grad_ln_g, grad_ln_b,
            d_w_in, d_b_f, d_conv_w, d_pool_w, d_pool_scale, d_w_out, d_ln_g, d_ln_b,
            nm_w_in, nm_b_f, nm_conv_w, nm_pool_w, nm_pool_scale, nm_w_out, nm_ln_g, nm_ln_b,
            nv_w_in, nv_b_f, nv_conv_w, nv_pool_w, nv_pool_scale, nv_w_out, nv_ln_g, nv_ln_b)
```
